```python
import jax
import jax.numpy as jnp
from jax import lax
import numpy as np

D_MODEL = 1024
BATCH = 4
SEQ = 4096
DEPTH = 4
DEC_BATCH = 32
DEC_SEQ = 1
PAST_LEN = 8192
PAGE_SIZE = 128

N_MIXERS = 2
N_GLA_LAYERS = (DEPTH + 1) // 2
N_NSA_LAYERS = DEPTH // 2
D_FF = 4 * D_MODEL
EPS = 1e-6

GLA_HEADS = 4
GLA_DK_TOT = D_MODEL // 2
GLA_DV_TOT = D_MODEL
GLA_DK = GLA_DK_TOT // GLA_HEADS
GLA_DV = GLA_DV_TOT // GLA_HEADS
GLA_GATE_RANK = 16
GLA_TAU = 16.0
GLA_CHUNK = 64
GLA_IN = 2 * GLA_DK_TOT + 2 * GLA_DV_TOT + GLA_GATE_RANK

NSA_HEADS = 16
NSA_KV_HEADS = 4
NSA_GROUP = NSA_HEADS // NSA_KV_HEADS
HEAD_DIM = D_MODEL // NSA_HEADS
NSA_QD = NSA_HEADS * HEAD_DIM
NSA_KVD = NSA_KV_HEADS * HEAD_DIM
NSA_IN = NSA_QD + 6 * NSA_KVD + 3 * NSA_HEADS
L_CMP = 32
CMP_STRIDE = 16
CMP_HIDDEN = 4 * HEAD_DIM
L_SLC = 64
SLC_RATIO = L_SLC // CMP_STRIDE
N_SEL = 16
WINDOW = 512
NSA_QBLOCK = 64
ROT_DIM = HEAD_DIM // 4
ROPE_THETA = 500000.0

kernel_name = 'gla_nsa_hybrid_step'


def rms_norm(x, g):
    xf = x.astype(jnp.float32)
    y = xf * lax.rsqrt(jnp.mean(xf * xf, axis=-1, keepdims=True) + EPS)
    return (y * g.astype(jnp.float32)).astype(x.dtype)


def partial_rope(x, pos):
    half = ROT_DIM // 2
    inv_freq = ROPE_THETA ** (-jnp.arange(half, dtype=jnp.float32) * 2.0 / ROT_DIM)
    ang = pos.astype(jnp.float32)[:, None] * inv_freq[None, :]
    cos = jnp.cos(ang)[None, :, None, :]
    sin = jnp.sin(ang)[None, :, None, :]
    x1 = x[..., :half].astype(jnp.float32)
    x2 = x[..., half:ROT_DIM].astype(jnp.float32)
    rot = jnp.concatenate([x1 * cos - x2 * sin, x1 * sin + x2 * cos], axis=-1).astype(x.dtype)
    return jnp.concatenate([rot, x[..., ROT_DIM:]], axis=-1)


def masked_softmax(s, mask):
    s = jnp.where(mask, s, -jnp.inf)
    m = jnp.max(s, axis=-1, keepdims=True)
    m = jnp.where(jnp.isfinite(m), m, 0.0)
    e = jnp.where(mask, jnp.exp(s - m), 0.0)
    return e / jnp.maximum(jnp.sum(e, axis=-1, keepdims=True), 1e-30)


def sq_relu_mlp(h, w_up, w_down):
    u = jax.nn.relu(h @ w_up)
    return (u * u) @ w_down


def gla_scan(q, k, v, g, s0):
    B, T, H, DK = q.shape
    DV = v.shape[-1]
    C = GLA_CHUNK if T % GLA_CHUNK == 0 else T
    n = T // C

    def chunks(a):
        return a.astype(jnp.float32).reshape(B, n, C, H, a.shape[-1]).transpose(1, 0, 3, 2, 4)

    causal = jnp.tril(jnp.ones((C, C), dtype=bool))[None, None, :, :, None]

    def step(S, xs):
        qc, kc, vc, gc = xs
        b = jnp.cumsum(gc, axis=2)
        o_inter = jnp.einsum('bhck,bhkv->bhcv', qc * jnp.exp(b), S)
        diff = b[:, :, :, None, :] - b[:, :, None, :, :]
        decay = jnp.exp(jnp.where(causal, diff, -jnp.inf))
        a = jnp.einsum('bhik,bhjk,bhijk->bhij', qc, kc, decay)
        o = o_inter + jnp.einsum('bhij,bhjv->bhiv', a, vc)
        b_last = b[:, :, -1:, :]
        S_new = jnp.exp(b_last[:, :, 0, :, None]) * S + jnp.einsum('bhck,bhcv->bhkv', kc * jnp.exp(b_last - b), vc)
        return S_new, o

    S, o = lax.scan(step, s0.astype(jnp.float32), (chunks(q), chunks(k), chunks(v), chunks(g)))
    o = o.transpose(1, 0, 3, 2, 4).reshape(B, T, H, DV)
    return o, S


def gla_mixer(h, s0, w_in, w_gate2, b_gate, g_out, w_out):
    B, T, _ = h.shape
    z = h @ w_in
    q, k, v, r, gr = jnp.split(z, [GLA_DK_TOT, 2 * GLA_DK_TOT, 2 * GLA_DK_TOT + GLA_DV_TOT, 2 * GLA_DK_TOT + 2 * GLA_DV_TOT], axis=-1)
    log_alpha = jax.nn.log_sigmoid((gr @ w_gate2 + b_gate).astype(jnp.float32)) / GLA_TAU
    hk = lambda a: a.reshape(B, T, GLA_HEADS, GLA_DK)
    o, s = gla_scan(hk(q) * (GLA_DK ** -0.5), hk(k), v.reshape(B, T, GLA_HEADS, GLA_DV), hk(log_alpha), s0)
    o = rms_norm(o, g_out).reshape(B, T, GLA_DV_TOT).astype(h.dtype)
    return (o * jax.nn.silu(r)) @ w_out, s


def nsa_project(h, pos, w_in, b_gate, g_q, g_k):
    B, T, _ = h.shape
    z = h @ w_in
    q = z[..., :NSA_QD].reshape(B, T, NSA_HEADS, HEAD_DIM)
    kv = z[..., NSA_QD:NSA_QD + 6 * NSA_KVD].reshape(B, T, 6, NSA_KV_HEADS, HEAD_DIM)
    gates = jax.nn.sigmoid((z[..., NSA_QD + 6 * NSA_KVD:] + b_gate).astype(jnp.float32))
    gates = gates.reshape(B, T, NSA_HEADS, 3)
    q = partial_rope(rms_norm(q, g_q), pos)

    def key(i, j):
        return partial_rope(rms_norm(kv[:, :, i], g_k[j]), pos)

    rows = (key(0, 0), kv[:, :, 1], key(2, 1), kv[:, :, 3], key(4, 2), kv[:, :, 5])
    return q, rows, gates


def nsa_compress(x, pe, w1, b1, w2):
    B, L = x.shape[:2]
    nh = L // CMP_STRIDE
    halves = x[:, :nh * CMP_STRIDE].reshape(B, nh, CMP_STRIDE, NSA_KV_HEADS, HEAD_DIM)
    first = jnp.einsum('bnskd,sdh->bnkh', halves + pe[:CMP_STRIDE, None, :], w1[:CMP_STRIDE])
    second = jnp.einsum('bnskd,sdh->bnkh', halves + pe[CMP_STRIDE:, None, :], w1[CMP_STRIDE:])
    hid = jax.nn.gelu(first[:, :-1] + second[:, 1:] + b1)
    return hid @ w2


def slc_blocks(x):
    B, L = x.shape[:2]
    ns = -(-L // L_SLC)
    x = jnp.pad(x, ((0, 0), (0, ns * L_SLC - L), (0, 0), (0, 0)))
    return x.reshape(B, ns, L_SLC, NSA_KV_HEADS, HEAD_DIM).transpose(0, 3, 1, 2, 4)


def nsa_keys(k_cmp, v_cmp, k_slc, v_slc, g_kcmp, cmp_pe, cmp_w1, cmp_b1, cmp_w2):
    kc = rms_norm(nsa_compress(k_cmp, cmp_pe[0], cmp_w1[0], cmp_b1[0], cmp_w2[0]), g_kcmp)
    vc = nsa_compress(v_cmp, cmp_pe[1], cmp_w1[1], cmp_b1[1], cmp_w2[1])
    return kc, vc, slc_blocks(k_slc), slc_blocks(v_slc)


def nsa_attend(q, q_pos, kc, vc, ks_b, vs_b, kw, vw, kw_pos, gates):
    B, Q = q.shape[:2]
    qg = q.reshape(B, Q, NSA_KV_HEADS, NSA_GROUP, HEAD_DIM)
    scale = HEAD_DIM ** -0.5
    nc = kc.shape[1]
    blk_end = jnp.arange(nc) * CMP_STRIDE + (L_CMP - 1)
    mask_c = blk_end[None, :] <= q_pos[:, None]
    s_c = jnp.einsum('bqkgd,bnkd->bkgqn', qg, kc, preferred_element_type=jnp.float32) * scale
    p_c = masked_softmax(s_c, mask_c)
    o_c = jnp.einsum('bkgqn,bnkd->bqkgd', p_c.astype(vc.dtype), vc)
    ns = ks_b.shape[2]
    imp = jnp.pad(jnp.sum(p_c, axis=2), ((0, 0), (0, 0), (0, 0), (1, SLC_RATIO * ns - nc)))
    pair = imp[..., 1:] + imp[..., :-1]
    p_slc = pair.reshape(B, NSA_KV_HEADS, Q, ns, SLC_RATIO).sum(-1)
    blk = jnp.arange(ns)[None, :]
    cur = (q_pos // L_SLC)[:, None]
    valid = blk <= cur
    forced = (blk == 0) | (blk == cur) | (blk == cur - 1)
    score = jnp.where(valid & forced, jnp.inf, jnp.where(valid, p_slc, -jnp.inf))
    _, sel = lax.top_k(score, min(N_SEL, ns))
    bi = jnp.arange(B)[:, None, None, None]
    hi = jnp.arange(NSA_KV_HEADS)[None, :, None, None]
    kg = ks_b[bi, hi, sel]
    vg = vs_b[bi, hi, sel]
    tok_pos = sel[..., None] * L_SLC + jnp.arange(L_SLC)
    mask_s = (tok_pos <= q_pos[None, None, :, None, None]).reshape(B, NSA_KV_HEADS, 1, Q, -1)
    s_s = jnp.einsum('bqkgd,bkqnld->bkgqnl', qg, kg, preferred_element_type=jnp.float32) * scale
    p_s = masked_softmax(s_s.reshape(B, NSA_KV_HEADS, NSA_GROUP, Q, -1), mask_s)
    p_s = p_s.reshape(s_s.shape).astype(vg.dtype)
    o_s = jnp.einsum('bkgqnl,bkqnld->bqkgd', p_s, vg)
    rel = q_pos[:, None] - kw_pos[None, :]
    mask_w = (rel >= 0) & (rel <= WINDOW) & (kw_pos[None, :] >= 0)
    s_w = jnp.einsum('bqkgd,bwkd->bkgqw', qg, kw, preferred_element_type=jnp.float32) * scale
    p_w = masked_softmax(s_w, mask_w)
    o_w = jnp.einsum('bkgqw,bwkd->bqkgd', p_w.astype(vw.dtype), vw)
    g = gates.reshape(B, Q, NSA_KV_HEADS, NSA_GROUP, 3, 1)
    o = g[..., 0, :] * o_c + g[..., 1, :] * o_s + g[..., 2, :] * o_w
    return o.reshape(B, Q, NSA_QD).astype(q.dtype)


def nsa_prompt(h, w):
    (w_in, b_gate, g_q, g_k, g_kcmp, cmp_pe, cmp_w1, cmp_b1, cmp_w2, w_out) = w
    B, T, _ = h.shape
    q, rows, gates = nsa_project(h, jnp.arange(T), w_in, b_gate, g_q, g_k)
    k_cmp, v_cmp, k_slc, v_slc, k_win, v_win = rows
    kc, vc, ks_b, vs_b = nsa_keys(k_cmp, v_cmp, k_slc, v_slc, g_kcmp, cmp_pe, cmp_w1, cmp_b1, cmp_w2)
    pad = ((0, 0), (WINDOW, 0), (0, 0), (0, 0))
    kw_pad = jnp.pad(k_win, pad)
    vw_pad = jnp.pad(v_win, pad)
    band = WINDOW + NSA_QBLOCK

    def query_block(i):
        s = i * NSA_QBLOCK
        sl = lambda a, n: lax.dynamic_slice_in_dim(a, s, n, axis=1)
        return nsa_attend(sl(q, NSA_QBLOCK), s + jnp.arange(NSA_QBLOCK), kc, vc, ks_b, vs_b,
                          sl(kw_pad, band), sl(vw_pad, band), s - WINDOW + jnp.arange(band),
                          sl(gates, NSA_QBLOCK))

    o = lax.map(query_block, jnp.arange(T // NSA_QBLOCK))
    o = o.transpose(1, 0, 2, 3).reshape(B, T, NSA_QD)
    n_win = min(WINDOW, T)
    return o @ w_out, (k_cmp, v_cmp, k_slc, v_slc, k_win[:, T - n_win:], v_win[:, T - n_win:])


def nsa_sample(h, past_len, pools, win_bufs, page_table, w):
    (w_in, b_gate, g_q, g_k, g_kcmp, cmp_pe, cmp_w1, cmp_b1, cmp_w2, w_out) = w
    B, T, _ = h.shape
    pos = past_len + jnp.arange(T)
    q, rows, gates = nsa_project(h, pos, w_in, b_gate, g_q, g_k)
    k_cmp, v_cmp, k_slc, v_slc, k_win, v_win = rows

    def with_past(pool, new):
        past = pool[page_table].reshape(B, -1, NSA_KV_HEADS, HEAD_DIM)
        return jnp.concatenate([past, new], axis=1)

    pk_cmp, pv_cmp, pk_slc, pv_slc = pools
    kc, vc, ks_b, vs_b = nsa_keys(with_past(pk_cmp, k_cmp), with_past(pv_cmp, v_cmp),
                                  with_past(pk_slc, k_slc), with_past(pv_slc, v_slc),
                                  g_kcmp, cmp_pe, cmp_w1, cmp_b1, cmp_w2)
    buf_k, buf_v = win_bufs
    n_buf = buf_k.shape[1]
    kw = jnp.concatenate([buf_k, k_win], axis=1)
    vw = jnp.concatenate([buf_v, v_win], axis=1)
    kw_pos = past_len - n_buf + jnp.arange(n_buf + T)
    o = nsa_attend(q, pos, kc, vc, ks_b, vs_b, kw, vw, kw_pos, gates)
    return o @ w_out, rows


def setup_inputs(seed: int = 0) -> dict:
    key = jax.random.key(seed)
    keys = iter(list(jax.random.split(key, 40)))

    def nrm(shape, scale):
        return scale * jax.random.normal(next(keys), shape, jnp.float32)

    def gain(shape):
        return 1.0 + nrm(shape, 0.02)

    n_pages = PAST_LEN // PAGE_SIZE
    n_used = DEC_BATCH * n_pages
    n_phys = n_used + n_used // 4
    n_buf = min(WINDOW, PAST_LEN)
    pool = (N_NSA_LAYERS, n_phys, PAGE_SIZE, NSA_KV_HEADS, HEAD_DIM)
    buf = (N_NSA_LAYERS, DEC_BATCH, n_buf, NSA_KV_HEADS, HEAD_DIM)
    page_table = jax.random.permutation(next(keys), n_phys)[:n_used].reshape(DEC_BATCH, n_pages).astype(jnp.int32)
    return {
        'x_prompt': nrm((BATCH, SEQ, D_MODEL), 1.0),
        'x_sample': nrm((DEC_BATCH, DEC_SEQ, D_MODEL), 1.0),
        'state_gla': nrm((N_GLA_LAYERS, DEC_BATCH, GLA_HEADS, GLA_DK, GLA_DV), 0.5),
        'cache_k_cmp': nrm(pool, 1.0),
        'cache_v_cmp': nrm(pool, 1.0),
        'cache_k_slc': nrm(pool, 1.0),
        'cache_v_slc': nrm(pool, 1.0),
        'cache_k_win': nrm(buf, 1.0),
        'cache_v_win': nrm(buf, 1.0),
        'page_table': page_table,
        'norm_mix': gain((DEPTH, D_MODEL)),
        'norm_mlp': gain((DEPTH, D_MODEL)),
        'mlp_up': nrm((DEPTH, D_MODEL, D_FF), D_MODEL ** -0.5),
        'mlp_down': nrm((DEPTH, D_FF, D_MODEL), D_FF ** -0.5),
        'gla_w_in': nrm((N_GLA_LAYERS, D_MODEL, GLA_IN), D_MODEL ** -0.5),
        'gla_w_gate2': nrm((N_GLA_LAYERS, GLA_GATE_RANK, GLA_DK_TOT), GLA_GATE_RANK ** -0.5),
        'gla_b_gate': nrm((N_GLA_LAYERS, GLA_DK_TOT), 0.1),
        'gla_g_out': gain((N_GLA_LAYERS, GLA_DV)),
        'gla_w_out': nrm((N_GLA_LAYERS, GLA_DV_TOT, D_MODEL), GLA_DV_TOT ** -0.5),
        'nsa_w_in': nrm((N_NSA_LAYERS, D_MODEL, NSA_IN), D_MODEL ** -0.5),
        'nsa_b_gate': nrm((N_NSA_LAYERS, 3 * NSA_HEADS), 0.1),
        'nsa_g_q': gain((N_NSA_LAYERS, HEAD_DIM)),
        'nsa_g_k': gain((N_NSA_LAYERS, 3, HEAD_DIM)),
        'nsa_g_kcmp': gain((N_NSA_LAYERS, HEAD_DIM)),
        'nsa_cmp_pe': nrm((N_NSA_LAYERS, 2, L_CMP, HEAD_DIM), 0.1),
        'nsa_cmp_w1': nrm((N_NSA_LAYERS, 2, L_CMP, HEAD_DIM, CMP_HIDDEN), (L_CMP * HEAD_DIM) ** -0.5),
        'nsa_cmp_b1': nrm((N_NSA_LAYERS, 2, CMP_HIDDEN), 0.1),
        'nsa_cmp_w2': nrm((N_NSA_LAYERS, 2, CMP_HIDDEN, HEAD_DIM), CMP_HIDDEN ** -0.5),
        'nsa_w_out': nrm((N_NSA_LAYERS, NSA_QD, D_MODEL), NSA_QD ** -0.5),
    }


def reference(x_prompt, x_sample, state_gla, cache_k_cmp, cache_v_cmp, cache_k_slc, cache_v_slc,
              cache_k_win, cache_v_win, page_table, norm_mix, norm_mlp, mlp_up, mlp_down,
              gla_w_in, gla_w_gate2, gla_b_gate, gla_g_out, gla_w_out,
              nsa_w_in, nsa_b_gate, nsa_g_q, nsa_g_k, nsa_g_kcmp, nsa_cmp_pe, nsa_cmp_w1,
              nsa_cmp_b1, nsa_cmp_w2, nsa_w_out):
    past_len = page_table.shape[1] * PAGE_SIZE
    xp, xs = x_prompt, x_sample
    gla_p, gla_s, nsa_p, nsa_s = [], [], [], []
    for i in range(DEPTH):
        j = i // N_MIXERS
        hp = rms_norm(xp, norm_mix[i])
        hs = rms_norm(xs, norm_mix[i])
        if i % N_MIXERS == 0:
            gw = (gla_w_in[j], gla_w_gate2[j], gla_b_gate[j], gla_g_out[j], gla_w_out[j])
            s0 = jnp.zeros((xp.shape[0], GLA_HEADS, GLA_DK, GLA_DV), jnp.float32)
            op, sp = gla_mixer(hp, s0, *gw)
            os_, ss = gla_mixer(hs, state_gla[j], *gw)
            gla_p.append(sp)
            gla_s.append(ss)
        else:
            nw = (nsa_w_in[j], nsa_b_gate[j], nsa_g_q[j], nsa_g_k[j], nsa_g_kcmp[j], nsa_cmp_pe[j],
                  nsa_cmp_w1[j], nsa_cmp_b1[j], nsa_cmp_w2[j], nsa_w_out[j])
            op, rp = nsa_prompt(hp, nw)
            os_, rs = nsa_sample(hs, past_len,
                                 (cache_k_cmp[j], cache_v_cmp[j], cache_k_slc[j], cache_v_slc[j]),
                                 (cache_k_win[j], cache_v_win[j]), page_table, nw)
            nsa_p.append(rp)
            nsa_s.append(rs)
        xp = xp + op
        xs = xs + os_
        xp = xp + sq_relu_mlp(rms_norm(xp, norm_mlp[i]), mlp_up[i], mlp_down[i])
        xs = xs + sq_relu_mlp(rms_norm(xs, norm_mlp[i]), mlp_up[i], mlp_down[i])

    def stack(lst, r):
        return jnp.stack([e[r] for e in lst])

    return (xp, xs, jnp.stack(gla_p), jnp.stack(gla_s),
            stack(nsa_p, 0), stack(nsa_s, 0), stack(nsa_p, 1), stack(nsa_s, 1),
            stack(nsa_p, 2), stack(nsa_s, 2), stack(nsa_p, 3), stack(nsa_s, 3),
            stack(nsa_p, 4), stack(nsa_s, 4), stack(nsa_p, 5), stack(nsa_s, 5))
```

```python
import functools

import jax
import jax.numpy as jnp
from jax import lax
from jax.experimental import pallas as pl
from jax.experimental.pallas import tpu as pltpu

F32 = jnp.float32
BF16 = jnp.bfloat16
EPS = 1e-6
NEG = -1e30
VMEM_LIMIT_BYTES = 48 * 1024 * 1024
LANES = 128
PAGE = 128
HD = 64
KVH = 4
GRP = 4
L_CMP, CMP_STRIDE, L_SLC, N_SEL, WINDOW = 32, 16, 64, 16, 512
GLA_H, GLA_DK, GLA_DV, GLA_CHUNK = 4, 128, 256, 64
ROPE_THETA, ROT_HALF = 500000.0, 8


def _params(*sem):
    return pltpu.CompilerParams(dimension_semantics=sem, vmem_limit_bytes=VMEM_LIMIT_BYTES)


def _dot(a, b):
    return jnp.dot(a, b, preferred_element_type=F32)


def _dot_nt(a, b):
    return lax.dot_general(a, b, (((1,), (1,)), ((), ())), preferred_element_type=F32)


def _dot_tn(a, b):
    return lax.dot_general(a, b, (((0,), (0,)), ((), ())), preferred_element_type=F32)


def _rms_rows(x, g):
    ms = jnp.mean(x * x, axis=-1, keepdims=True)
    return x * lax.rsqrt(ms + EPS) * g


def _mlp_kernel(x_ref, g_ref, wu_ref, wd_ref, o_ref, h_scr, acc_scr):
    f = pl.program_id(1)

    @pl.when(f == 0)
    def _():
        h_scr[...] = _rms_rows(x_ref[...], g_ref[...]).astype(BF16)
        acc_scr[...] = jnp.zeros_like(acc_scr)

    u = jnp.maximum(_dot(h_scr[...], wu_ref[...]), 0.0)
    acc_scr[...] += _dot((u * u).astype(BF16), wd_ref[...])

    @pl.when(f == pl.num_programs(1) - 1)
    def _():
        o_ref[...] = x_ref[...] + acc_scr[...]


def _mlp(x, g, wu, wd, tm, tf=512):
    m, d = x.shape
    ff = wu.shape[1]
    return pl.pallas_call(
        _mlp_kernel,
        grid=(m // tm, ff // tf),
        in_specs=[pl.BlockSpec((tm, d), lambda i, f: (i, 0)),
                  pl.BlockSpec((1, d), lambda i, f: (0, 0)),
                  pl.BlockSpec((d, tf), lambda i, f: (0, f)),
                  pl.BlockSpec((tf, d), lambda i, f: (f, 0))],
        out_specs=pl.BlockSpec((tm, d), lambda i, f: (i, 0)),
        out_shape=jax.ShapeDtypeStruct((m, d), F32),
        scratch_shapes=[pltpu.VMEM((tm, d), BF16), pltpu.VMEM((tm, d), F32)],
        compiler_params=_params("parallel", "arbitrary"),
        name="mlp_block",
    )(x, g, wu, wd)


def _gla_proj_kernel(x_ref, g_ref, w_ref, wg2_ref, bg_ref, q_ref, k_ref, v_ref, r_ref, gl_ref):
    h = _rms_rows(x_ref[...], g_ref[...]).astype(BF16)
    dk, dv = GLA_H * GLA_DK, GLA_H * GLA_DV
    q_ref[...] = _dot(h, w_ref[:, 0:dk]) * (GLA_DK ** -0.5)
    k_ref[...] = _dot(h, w_ref[:, dk:2 * dk])
    v_ref[...] = _dot(h, w_ref[:, 2 * dk:2 * dk + dv])
    r_ref[...] = _dot(h, w_ref[:, 2 * dk + dv:2 * dk + 2 * dv])
    gr = _dot(h, w_ref[:, 2 * dk + 2 * dv:])
    xg = _dot(gr.astype(BF16), wg2_ref[...]) + bg_ref[...]
    gl_ref[...] = jax.nn.log_sigmoid(xg) * (1.0 / 16.0)


def _gla_proj(x, g, w, wg2, bg, tm):
    m, d = x.shape
    dk, dv = GLA_H * GLA_DK, GLA_H * GLA_DV
    row = lambda n: pl.BlockSpec((tm, n), lambda i: (i, 0))
    full = lambda a: pl.BlockSpec(a.shape, lambda i: (0, 0))
    return pl.pallas_call(
        _gla_proj_kernel,
        grid=(m // tm,),
        in_specs=[row(d), full(g), full(w), full(wg2), full(bg)],
        out_specs=[row(dk), row(dk), row(dv), row(dv), row(dk)],
        out_shape=[jax.ShapeDtypeStruct((m, n), F32) for n in (dk, dk, dv, dv, dk)],
        compiler_params=_params("parallel"),
        name="gla_proj",
    )(x, g, w, wg2, bg)


def _gla_scan_kernel(q_ref, k_ref, g_ref, v_ref, o_ref, s_ref, st_scr, *, n_chunks):
    c_len = GLA_CHUNK
    st_scr[...] = jnp.zeros_like(st_scr)
    rowi = lax.broadcasted_iota(jnp.int32, (c_len, GLA_DK), 0)
    causal = (lax.broadcasted_iota(jnp.int32, (c_len, c_len), 0)
              >= lax.broadcasted_iota(jnp.int32, (c_len, c_len), 1))

    def body(c, carry):
        sl = pl.ds(pl.multiple_of(c * c_len, c_len), c_len)
        b = g_ref[sl, :]
        sh = 1
        while sh < c_len:
            b = b + jnp.where(rowi >= sh, pltpu.roll(b, sh, 0), 0.0)
            sh *= 2
        b_last = b[c_len - 1:c_len, :]
        b_mid = b[c_len // 2 - 1:c_len // 2, :]
        q = q_ref[sl, :]
        k = k_ref[sl, :]
        v = v_ref[sl, :].astype(BF16)
        qe = (q * jnp.exp(b)).astype(BF16)
        qa = (q * jnp.exp(b - b_mid)).astype(BF16)
        ka = (k * jnp.exp(b_mid - b)).astype(BF16)
        kd = (k * jnp.exp(b_last - b)).astype(BF16)
        a = jnp.where(causal, _dot_nt(qa, ka), 0.0)
        st = st_scr[...]
        o_ref[sl, :] = _dot_nt(qe, st.astype(BF16)) + _dot(a.astype(BF16), v)
        st_scr[...] = st * jnp.exp(b_last) + _dot_tn(v, kd)
        return carry

    lax.fori_loop(0, n_chunks, body, 0)
    s_ref[...] = st_scr[...].T


def _gla_scan(q, k, gl, v):
    b, t, _ = q.shape
    kq = pl.BlockSpec((None, t, GLA_DK), lambda i, h: (i, 0, h))
    vv = pl.BlockSpec((None, t, GLA_DV), lambda i, h: (i, 0, h))
    return pl.pallas_call(
        functools.partial(_gla_scan_kernel, n_chunks=t // GLA_CHUNK),
        grid=(b, GLA_H),
        in_specs=[kq, kq, kq, vv],
        out_specs=[vv, pl.BlockSpec((None, None, GLA_DK, GLA_DV), lambda i, h: (i, h, 0, 0))],
        out_shape=[jax.ShapeDtypeStruct((b, t, GLA_H * GLA_DV), F32),
                   jax.ShapeDtypeStruct((b, GLA_H, GLA_DK, GLA_DV), F32)],
        scratch_shapes=[pltpu.VMEM((GLA_DV, GLA_DK), F32)],
        compiler_params=_params("parallel", "parallel"),
        name="gla_scan",
    )(q, k, gl, v)


def _gla_step_kernel(q_ref, k_ref, g_ref, v_ref, s0_ref, o_ref, s_ref):
    def col(x):
        return jnp.broadcast_to(x, (LANES, LANES)).T

    for h in range(GLA_H):
        ks = slice(h * GLA_DK, (h + 1) * GLA_DK)
        qc, kc, ec = col(q_ref[:, ks]), col(k_ref[:, ks]), col(jnp.exp(g_ref[:, ks]))
        for half in range(GLA_DV // LANES):
            vs = slice(h * GLA_DV + half * LANES, h * GLA_DV + (half + 1) * LANES)
            ss = slice(half * LANES, (half + 1) * LANES)
            sn = ec * s0_ref[h, :, ss] + kc * v_ref[:, vs]
            s_ref[h, :, ss] = sn
            o_ref[:, vs] = jnp.sum(qc * sn, axis=0, keepdims=True)


def _gla_step(q, k, gl, v, s0):
    b = q.shape[0]
    kq = pl.BlockSpec((None, 1, GLA_H * GLA_DK), lambda i: (i, 0, 0))
    vv = pl.BlockSpec((None, 1, GLA_H * GLA_DV), lambda i: (i, 0, 0))
    st = pl.BlockSpec((None, GLA_H, GLA_DK, GLA_DV), lambda i: (i, 0, 0, 0))
    return pl.pallas_call(
        _gla_step_kernel,
        grid=(b,),
        in_specs=[kq, kq, kq, vv, st],
        out_specs=[vv, st],
        out_shape=[jax.ShapeDtypeStruct((b, 1, GLA_H * GLA_DV), F32),
                   jax.ShapeDtypeStruct((b, GLA_H, GLA_DK, GLA_DV), F32)],
        compiler_params=_params("parallel"),
        name="gla_step",
    )(q, k, gl, v, s0)


def _gla_out_kernel(o_ref, r_ref, x_ref, go_ref, w_ref, y_ref):
    parts = []
    for h in range(GLA_H):
        sl = slice(h * GLA_DV, (h + 1) * GLA_DV)
        r = r_ref[:, sl]
        parts.append((_rms_rows(o_ref[:, sl], go_ref[...]) * (r * jax.nn.sigmoid(r))).astype(BF16))
    y_ref[...] = x_ref[...] + _dot(jnp.concatenate(parts, axis=1), w_ref[...])


def _gla_out(o, r, x, go, w, tm):
    m, d = x.shape
    row = pl.BlockSpec((tm, d), lambda i: (i, 0))
    full = lambda a: pl.BlockSpec(a.shape, lambda i: (0, 0))
    return pl.pallas_call(
        _gla_out_kernel,
        grid=(m // tm,),
        in_specs=[row, row, row, full(go), full(w)],
        out_specs=row,
        out_shape=jax.ShapeDtypeStruct((m, d), F32),
        compiler_params=_params("parallel"),
        name="gla_out",
    )(o, r, x, go, w)


def _gla_layer(xp, xs, s0, norm_g, w_in, w_gate2, b_gate, g_out, w_out):
    b, t, d = xp.shape
    bs = xs.shape[0]
    dk, dv = GLA_H * GLA_DK, GLA_H * GLA_DV
    rank = w_gate2.shape[0]
    w = jnp.pad(w_in, ((0, 0), (0, LANES - rank))).astype(BF16)
    wg2 = jnp.pad(w_gate2, ((0, LANES - rank), (0, 0))).astype(BF16)
    g = norm_g.reshape(1, d)
    bg = b_gate.reshape(1, dk)
    go = g_out.reshape(1, GLA_DV)
    wo = w_out.astype(BF16)
    x2 = xp.reshape(b * t, d)
    q, k, v, r, gl = _gla_proj(x2, g, w, wg2, bg, tm=512)
    o, sp = _gla_scan(q.reshape(b, t, dk), k.reshape(b, t, dk), gl.reshape(b, t, dk), v.reshape(b, t, dv))
    yp = _gla_out(o.reshape(b * t, dv), r, x2, go, wo, tm=512).reshape(b, t, d)
    xs2 = xs.reshape(bs, d)
    q, k, v, r, gl = _gla_proj(xs2, g, w, wg2, bg, tm=bs)
    o, ss = _gla_step(q.reshape(bs, 1, dk), k.reshape(bs, 1, dk), gl.reshape(bs, 1, dk), v.reshape(bs, 1, dv), s0)
    ys = _gla_out(o.reshape(bs, dv), r, xs2, go, wo, tm=bs).reshape(bs, 1, d)
    return yp, ys, sp, ss


def _nsa_proj_kernel(x_ref, gn_ref, wt_ref, gq_ref, gk_ref, bg_ref, cos_ref, sin_ref,
                     q_ref, kc_ref, vc_ref, ks_ref, vs_ref, kw_ref, vw_ref, gt_ref):
    tm = x_ref.shape[0]
    nq = GRP * KVH * HD
    nkv = KVH * HD
    h = _rms_rows(x_ref[...], gn_ref[...]).astype(BF16)
    cos = cos_ref[...][None]
    sin = sin_ref[...][None]

    def norm_rope(z, g, nh):
        z3 = z.reshape(nh, HD, tm)
        y = z3 * lax.rsqrt(jnp.mean(z3 * z3, axis=1, keepdims=True) + EPS) * g[None]
        x1 = y[:, 0:ROT_HALF, :]
        x2 = y[:, ROT_HALF:2 * ROT_HALF, :]
        return jnp.concatenate([x1 * cos - x2 * sin, x1 * sin + x2 * cos, y[:, 2 * ROT_HALF:, :]], axis=1)

    zq = _dot_nt(wt_ref[0:nq, :], h)
    q_ref[...] = (norm_rope(zq, gq_ref[...], GRP * KVH) * (HD ** -0.5)).astype(BF16)
    zkv = _dot_nt(wt_ref[nq:nq + 6 * nkv, :], h)
    outs = (kc_ref, vc_ref, ks_ref, vs_ref, kw_ref, vw_ref)
    for i in range(6):
        z = zkv[i * nkv:(i + 1) * nkv, :]
        outs[i][...] = norm_rope(z, gk_ref[i // 2], KVH) if i % 2 == 0 else z.reshape(KVH, HD, tm)
    zg = _dot_nt(wt_ref[nq + 6 * nkv:, :], h) + bg_ref[...]
    gt_ref[...] = jax.nn.sigmoid(zg).reshape(KVH, 16, tm)


def _nsa_proj(x, gn, wt, gq, gk, bg, cos, sin, tm):
    b, t, d = x.shape
    full = lambda a: pl.BlockSpec(a.shape, lambda i, j: (0,) * a.ndim)
    kv_spec = pl.BlockSpec((None, KVH, HD, tm), lambda i, j: (i, 0, 0, j))
    kv_shape = jax.ShapeDtypeStruct((b, KVH, HD, t), F32)
    return pl.pallas_call(
        _nsa_proj_kernel,
        grid=(b, t // tm),
        in_specs=[pl.BlockSpec((None, tm, d), lambda i, j: (i, j, 0)), full(gn), full(wt), full(gq), full(gk),
                  full(bg), pl.BlockSpec((ROT_HALF, tm), lambda i, j: (0, j)),
                  pl.BlockSpec((ROT_HALF, tm), lambda i, j: (0, j))],
        out_specs=[pl.BlockSpec((None, GRP * KVH, HD, tm), lambda i, j: (i, 0, 0, j))] + [kv_spec] * 6
        + [pl.BlockSpec((None, KVH, 16, tm), lambda i, j: (i, 0, 0, j))],
        out_shape=[jax.ShapeDtypeStruct((b, GRP * KVH, HD, t), BF16)] + [kv_shape] * 6
        + [jax.ShapeDtypeStruct((b, KVH, 16, t), F32)],
        compiler_params=_params("parallel", "parallel"),
        name="nsa_proj",
    )(x, gn, wt, gq, gk, bg, cos, sin)


def _nsa_out_kernel(ot_ref, x_ref, w_ref, y_ref):
    ot = ot_ref[...].reshape(GRP * KVH * HD, ot_ref.shape[-1])
    y_ref[...] = x_ref[...] + _dot_tn(ot, w_ref[...])


def _nsa_out(ot, x, w, tm):
    b, t, d = x.shape
    return pl.pallas_call(
        _nsa_out_kernel,
        grid=(b, t // tm),
        in_specs=[pl.BlockSpec((None, GRP * KVH, HD, tm), lambda i, j: (i, 0, 0, j)),
                  pl.BlockSpec((None, tm, d), lambda i, j: (i, j, 0)),
                  pl.BlockSpec(w.shape, lambda i, j: (0, 0))],
        out_specs=pl.BlockSpec((None, tm, d), lambda i, j: (i, j, 0)),
        out_shape=jax.ShapeDtypeStruct((b, t, d), F32),
        compiler_params=_params("parallel", "parallel"),
        name="nsa_out",
    )(ot, x, w)


def _cmp_fs_kernel(*refs, n_pages, paged):
    refs = refs[1:] if paged else refs
    pages = refs[:n_pages]
    w_ref, o_ref, xt_scr, lhs_scr = refs[n_pages:]
    p = pl.program_id(1)
    half = CMP_STRIDE
    for pair in range(KVH // 2):
        for j in range(n_pages):
            xt_scr[pair * n_pages + j] = pages[j][2 * pair:2 * pair + 2].reshape(2 * HD, PAGE).T
            row0 = pl.multiple_of((p * n_pages + j) * (PAGE // half), PAGE // half)
            for s in range(half):
                lhs_scr[pair, pl.ds(row0, PAGE // half), s * LANES:(s + 1) * LANES] = (
                    xt_scr[pair * n_pages + j, pl.ds(s, PAGE // half, stride=half), :])

    @pl.when(p == pl.num_programs(1) - 1)
    def _():
        n = w_ref.shape[1]
        for pair in range(KVH // 2):
            o_ref[:, pair * n:(pair + 1) * n] = _dot(lhs_scr[pair].astype(BF16), w_ref[...])


def _cmp_fs(src, w, n_pages_total, pages_per_step, page_table=None, layer=0):
    paged = page_table is not None
    b = page_table.shape[0] if paged else src.shape[0]
    nh = n_pages_total * (PAGE // CMP_STRIDE)
    steps = n_pages_total // pages_per_step
    if paged:
        page_spec = lambda j: pl.BlockSpec(
            (None, None, KVH, HD, PAGE), lambda i, p, pt: (layer, pt[i, p * pages_per_step + j], 0, 0, 0))
        w_spec = pl.BlockSpec(w.shape, lambda i, p, pt: (0, 0))
        o_spec = pl.BlockSpec((None, nh, KVH * 512), lambda i, p, pt: (i, 0, 0))
    else:
        page_spec = lambda j: pl.BlockSpec((None, KVH, HD, PAGE), lambda i, p: (i, 0, 0, p * pages_per_step + j))
        w_spec = pl.BlockSpec(w.shape, lambda i, p: (0, 0))
        o_spec = pl.BlockSpec((None, nh, KVH * 512), lambda i, p: (i, 0, 0))
    grid_spec = pltpu.PrefetchScalarGridSpec(
        num_scalar_prefetch=1 if paged else 0,
        grid=(b, steps),
        in_specs=[page_spec(j) for j in range(pages_per_step)] + [w_spec],
        out_specs=o_spec,
        scratch_shapes=[pltpu.VMEM((2 * pages_per_step, PAGE, 2 * HD), F32),
                        pltpu.VMEM((KVH // 2, nh, CMP_STRIDE * LANES), F32)],
    )
    args = ([page_table] if paged else []) + [src] * pages_per_step + [w]
    return pl.pallas_call(
        functools.partial(_cmp_fs_kernel, n_pages=pages_per_step, paged=paged),
        grid_spec=grid_spec,
        out_shape=jax.ShapeDtypeStruct((b, nh, KVH * 512), F32),
        compiler_params=_params("parallel", "arbitrary"),
        name="cmp_fs",
    )(*args)


def _cmp_out_kernel(fs_ref, b1_ref, pe_ref, w1_ref, w2t_ref, gk_ref, o_ref, *, is_key):
    nh = fs_ref.shape[0]
    hidden = b1_ref.shape[1]
    c = _dot(pe_ref[0].astype(BF16), w1_ref[0]) + _dot(pe_ref[1].astype(BF16), w1_ref[1])
    bias = c[0:1, :] + b1_ref[...]
    for kvh in range(KVH):
        first = fs_ref[:, kvh * 2 * hidden:kvh * 2 * hidden + hidden]
        second = fs_ref[:, kvh * 2 * hidden + hidden:(kvh + 1) * 2 * hidden]
        hid = jax.nn.gelu(first + pltpu.roll(second, nh - 1, 0) + bias)
        yt = _dot_nt(w2t_ref[...], hid.astype(BF16))
        if is_key:
            yt = yt * lax.rsqrt(jnp.mean(yt * yt, axis=0, keepdims=True) + EPS) * gk_ref[...]
        o_ref[kvh] = yt.astype(BF16)


def _cmp_out(fs, b1, pe, w1, w2t, gk, is_key):
    b, nh, _ = fs.shape
    full = lambda a: pl.BlockSpec(a.shape, lambda i: (0,) * a.ndim)
    return pl.pallas_call(
        functools.partial(_cmp_out_kernel, is_key=is_key),
        grid=(b,),
        in_specs=[pl.BlockSpec((None, nh, fs.shape[2]), lambda i: (i, 0, 0)), full(b1), full(pe), full(w1),
                  full(w2t), full(gk)],
        out_specs=pl.BlockSpec((None, KVH, HD, nh), lambda i: (i, 0, 0, 0)),
        out_shape=jax.ShapeDtypeStruct((b, KVH, HD, nh), BF16),
        compiler_params=_params("parallel"),
        name="cmp_out",
    )(fs, b1, pe, w1, w2t, gk)


def _cmp_weights(pe, w1, b1, w2, g_kcmp, nh):
    hidden = w1.shape[-1]
    wfs = jnp.concatenate([w1[:CMP_STRIDE], w1[CMP_STRIDE:]], axis=-1)
    w_pair = jnp.einsum("sdc,ab->sadbc", wfs, jnp.eye(2, dtype=F32))
    w_pair = w_pair.reshape(CMP_STRIDE * 2 * HD, 2 * 2 * hidden).astype(BF16)
    pe2 = jnp.zeros((2, 8, CMP_STRIDE * HD), F32).at[:, 0, :].set(pe.reshape(2, CMP_STRIDE * HD))
    w1f = w1.reshape(2, CMP_STRIDE * HD, hidden).astype(BF16)
    gk = jnp.broadcast_to(g_kcmp.reshape(HD, 1), (HD, nh))
    return w_pair, b1.reshape(1, hidden), pe2, w1f, w2.T.astype(BF16), gk


def _softmax_update(s, mask, vt, m, l, acc):
    m_new = jnp.maximum(m, jnp.max(jnp.where(mask, s, NEG), axis=0, keepdims=True))
    alpha = jnp.exp(m - m_new)
    p = jnp.where(mask, jnp.exp(s - m_new), 0.0)
    l_new = l * alpha + jnp.sum(p, axis=0, keepdims=True)
    acc_new = acc * alpha + _dot(vt, p.astype(BF16))
    return m_new, l_new, acc_new, p


def _softmax_init(lanes):
    return jnp.full((1, lanes), NEG, F32), jnp.zeros((1, lanes), F32), jnp.zeros((HD, lanes), F32)


def _finish(l, acc):
    return acc / jnp.maximum(l, 1e-30)


def _split_dot(mt, x):
    hi = x.astype(BF16)
    lo = (x - hi.astype(F32)).astype(BF16)
    return _dot(mt, hi) + _dot(mt, lo)


def _attn_prompt_kernel(q_ref, kc_ref, vc_ref, ks_ref, vs_ref, kw_ref, vw_ref, gt_ref, mt_ref, o_ref, sel_scr,
                        *, tq, n_cmp, n_sel):
    i = pl.program_id(2)
    lanes = GRP * tq
    ncp = kc_ref.shape[1]
    nsp = mt_ref.shape[0]
    qt = jnp.concatenate([q_ref[h] for h in range(GRP)], axis=1)
    tpos1 = i * tq + lax.broadcasted_iota(jnp.int32, (1, tq), 1)
    tpos = jnp.concatenate([tpos1] * GRP, axis=1)

    s = _dot_tn(kc_ref[...], qt)
    n_idx = lax.broadcasted_iota(jnp.int32, (ncp, lanes), 0)
    mask = (n_idx * CMP_STRIDE + (L_CMP - 1) <= tpos) & (n_idx < n_cmp)
    m, l, acc, p = _softmax_update(s, mask, vc_ref[...], *_softmax_init(lanes))
    o_c = _finish(l, acc)
    p = p / jnp.maximum(l, 1e-30)
    imp = p[:, 0:tq]
    for h in range(1, GRP):
        imp = imp + p[:, h * tq:(h + 1) * tq]
    p_slc = _split_dot(mt_ref[...], imp)

    j_idx = lax.broadcasted_iota(jnp.int32, (nsp, tq), 0)
    cur = tpos1 // L_SLC
    valid = j_idx <= cur
    forced = (j_idx == 0) | (j_idx == cur) | (j_idx == cur - 1)
    score = jnp.where(valid & forced, jnp.inf, jnp.where(valid, p_slc, -jnp.inf))
    rank = jnp.zeros((nsp, tq), F32)
    for jp in range(nsp):
        row = jnp.broadcast_to(score[jp:jp + 1, :], (nsp, tq))
        ahead = (row > score) | ((row == score) & (j_idx > jp))
        rank = rank + jnp.where(ahead, 1.0, 0.0)
    sel = jnp.where(rank < n_sel, 1.0, 0.0)
    for j in range(nsp):
        sel_scr[j] = jnp.broadcast_to(sel[j:j + 1, :], (8, tq))

    kt_len = 2 * LANES
    blocks_per_tile = kt_len // L_SLC
    row_k = lax.broadcasted_iota(jnp.int32, (kt_len, lanes), 0)

    def slc_body(kt, carry):
        start = pl.multiple_of(kt * kt_len, kt_len)
        s = _dot_tn(ks_ref[:, pl.ds(start, kt_len)].astype(BF16), qt)
        rows = []
        for r in range(blocks_per_tile):
            e = sel_scr[kt * blocks_per_tile + r]
            e = jnp.broadcast_to(e[None], (L_SLC // 8, 8, tq)).reshape(L_SLC, tq)
            rows.append(jnp.concatenate([e] * GRP, axis=1))
        mask = (jnp.concatenate(rows, axis=0) > 0.5) & (kt * kt_len + row_k <= tpos)
        return _softmax_update(s, mask, vs_ref[:, pl.ds(start, kt_len)].astype(BF16), *carry)[:3]

    n_kt = (i * tq + tq + kt_len - 1) // kt_len
    _, l, acc = lax.fori_loop(0, n_kt, slc_body, _softmax_init(lanes))
    o_s = _finish(l, acc)

    row_w = lax.broadcasted_iota(jnp.int32, (tq, lanes), 0)

    def win_body(kt, carry):
        start = pl.multiple_of(kt * tq, tq)
        s = _dot_tn(kw_ref[:, pl.ds(start, tq)].astype(BF16), qt)
        rel = tpos - (kt * tq + row_w)
        mask = (rel >= 0) & (rel <= WINDOW)
        return _softmax_update(s, mask, vw_ref[:, pl.ds(start, tq)].astype(BF16), *carry)[:3]

    lo = jnp.maximum(i - (WINDOW + tq - 1) // tq, 0)
    _, l, acc = lax.fori_loop(lo, i + 1, win_body, _softmax_init(lanes))
    o_w = _finish(l, acc)

    for h in range(GRP):
        ls = slice(h * tq, (h + 1) * tq)
        o = (gt_ref[3 * h:3 * h + 1, :] * o_c[:, ls] + gt_ref[3 * h + 1:3 * h + 2, :] * o_s[:, ls]
             + gt_ref[3 * h + 2:3 * h + 3, :] * o_w[:, ls])
        o_ref[h] = o.astype(BF16)


def _slc_matrix(nsp, ncp, n_cmp):
    ratio = L_SLC // CMP_STRIDE
    j = jnp.arange(nsp)[:, None]
    n = jnp.arange(ncp)[None, :]
    m = ((n >= ratio * j) & (n <= ratio * j + ratio - 1)).astype(F32)
    m = m + ((n >= ratio * j - 1) & (n <= ratio * j + ratio - 2)).astype(F32)
    return jnp.where(n < n_cmp, m, 0.0).astype(BF16)


def _attn_prompt(qt, kct, vct, kst, vst, kwt, vwt, gt, tq=LANES):
    b, _, _, t = qt.shape
    ncp = kct.shape[-1]
    n_cmp = t // CMP_STRIDE - 1
    ns = -(-t // L_SLC)
    nsp = -(-ns // 8) * 8
    mt = _slc_matrix(nsp, ncp, n_cmp)
    res = lambda n: pl.BlockSpec((None, None, HD, n), lambda i, g, j: (i, g, 0, 0))
    return pl.pallas_call(
        functools.partial(_attn_prompt_kernel, tq=tq, n_cmp=n_cmp, n_sel=min(N_SEL, ns)),
        grid=(b, KVH, t // tq),
        in_specs=[pl.BlockSpec((None, GRP, HD, tq), lambda i, g, j: (i, g, 0, j)),
                  res(ncp), res(ncp), res(t), res(t), res(t), res(t),
                  pl.BlockSpec((None, None, 16, tq), lambda i, g, j: (i, g, 0, j)),
                  pl.BlockSpec(mt.shape, lambda i, g, j: (0, 0))],
        out_specs=pl.BlockSpec((None, GRP, HD, tq), lambda i, g, j: (i, g, 0, j)),
        out_shape=jax.ShapeDtypeStruct(qt.shape, BF16),
        scratch_shapes=[pltpu.VMEM((nsp, 8, tq), F32)],
        compiler_params=_params("parallel", "parallel", "arbitrary"),
        name="nsa_attn_prompt",
    )(qt, kct, vct, kst, vst, kwt, vwt, gt, mt)


def _attn_sample_kernel(*refs, n_pages, n_cmp, n_blocks, n_sel):
    refs = refs[1:]
    k_pages = refs[:n_pages]
    v_pages = refs[n_pages:2 * n_pages]
    (q_ref, kc_ref, vc_ref, kw_ref, vw_ref, kns_ref, vns_ref, knw_ref, vnw_ref, gt_ref, mt_ref,
     o_ref, sel_scr, oc_scr, ow_scr, m_scr, l_scr, acc_scr) = refs[2 * n_pages:]
    p = pl.program_id(1)
    ncp = kc_ref.shape[-1]
    nsp = mt_ref.shape[0]
    first_row = lax.broadcasted_iota(jnp.int32, (LANES, LANES), 0) == 0

    @pl.when(p == 0)
    def _():
        lane = lax.broadcasted_iota(jnp.int32, (ncp, LANES), 1)
        n_idx = lax.broadcasted_iota(jnp.int32, (ncp, LANES), 0)
        j_col = lax.broadcasted_iota(jnp.int32, (nsp, LANES), 0)
        jp_idx = lax.broadcasted_iota(jnp.int32, (nsp, nsp), 0)
        j_idx = lax.broadcasted_iota(jnp.int32, (nsp, nsp), 1)
        cur = n_blocks - 1
        for g in range(KVH):
            qt = q_ref[g]
            s = _dot_tn(kc_ref[g], qt)
            m, l, acc, pr = _softmax_update(s, n_idx < n_cmp, vc_ref[g], *_softmax_init(LANES))
            oc_scr[g] = _finish(l, acc)
            pr = pr / jnp.maximum(l, 1e-30)
            imp = jnp.sum(jnp.where(lane < GRP, pr, 0.0), axis=1, keepdims=True)
            p_slc = _split_dot(mt_ref[...], jnp.broadcast_to(imp, (ncp, LANES)))
            valid = j_col <= cur
            forced = (j_col == 0) | (j_col == cur) | (j_col == cur - 1)
            score = jnp.where(valid & forced, jnp.inf, jnp.where(valid, p_slc, -jnp.inf))
            col = jnp.concatenate([score] * (nsp // LANES), axis=1)
            row = col.T
            ahead = (col > row) | ((col == row) & (jp_idx < j_idx))
            rank = jnp.sum(jnp.where(ahead, 1.0, 0.0), axis=0, keepdims=True)
            sel = jnp.where(rank < n_sel, 1.0, 0.0)
            sel_scr[g] = jnp.broadcast_to(sel, (LANES, nsp)).T
            st = _softmax_init(LANES)
            s = _dot_tn(kw_ref[g].astype(BF16), qt)
            st = _softmax_update(s, jnp.full(s.shape, True), vw_ref[g].astype(BF16), *st)[:3]
            s = _dot_tn(knw_ref[g].astype(BF16), qt)
            _, l, acc = _softmax_update(s, first_row, vnw_ref[g].astype(BF16), *st)[:3]
            ow_scr[g] = _finish(l, acc)
            m0, l0, a0 = _softmax_init(LANES)
            m_scr[g], l_scr[g], acc_scr[g] = m0, l0, a0

    blocks_per_page = PAGE // L_SLC
    for g in range(KVH):
        qt = q_ref[g]
        kt = jnp.concatenate([k_pages[j][g] for j in range(n_pages)], axis=1).astype(BF16)
        vt = jnp.concatenate([v_pages[j][g] for j in range(n_pages)], axis=1).astype(BF16)
        s = _dot_tn(kt, qt)
        rows = []
        for r in range(n_pages * blocks_per_page):
            e = sel_scr[g, pl.ds(p * n_pages * blocks_per_page + r, 1), :]
            rows.append(jnp.broadcast_to(e, (L_SLC, LANES)))
        mask = jnp.concatenate(rows, axis=0) > 0.5
        m_scr[g], l_scr[g], acc_scr[g] = _softmax_update(s, mask, vt, m_scr[g], l_scr[g], acc_scr[g])[:3]

    @pl.when(p == pl.num_programs(1) - 1)
    def _():
        for g in range(KVH):
            s = _dot_tn(kns_ref[g].astype(BF16), q_ref[g])
            _, l, acc = _softmax_update(s, first_row, vns_ref[g].astype(BF16), m_scr[g], l_scr[g], acc_scr[g])[:3]
            o_s = _finish(l, acc)
            o_ref[g] = gt_ref[g, 0:1, :] * oc_scr[g] + gt_ref[g, 1:2, :] * o_s + gt_ref[g, 2:3, :] * ow_scr[g]


def _attn_sample(qs, kct, vct, pool_k, pool_v, page_table, layer, kwt, vwt, kns, vns, knw, vnw, gs,
                 pages_per_step=8):
    b, n_pages_total = page_table.shape
    past = n_pages_total * PAGE
    n_cmp = past // CMP_STRIDE - 1
    ncp = kct.shape[-1]
    n_blocks = past // L_SLC + 1
    nsp = -(-n_blocks // LANES) * LANES
    mt = _slc_matrix(nsp, ncp, n_cmp)
    steps = n_pages_total // pages_per_step
    page_spec = lambda j: pl.BlockSpec((None, None, KVH, HD, PAGE),
                                       lambda i, p, pt: (layer, pt[i, p * pages_per_step + j], 0, 0, 0))
    per_b = lambda a: pl.BlockSpec((None,) + a.shape[1:], lambda i, p, pt: (i,) + (0,) * (a.ndim - 1))
    small = [qs, kct, vct, kwt, vwt, kns, vns, knw, vnw, gs]
    grid_spec = pltpu.PrefetchScalarGridSpec(
        num_scalar_prefetch=1,
        grid=(b, steps),
        in_specs=[page_spec(j) for j in range(pages_per_step)] * 2 + [per_b(a) for a in small]
        + [pl.BlockSpec(mt.shape, lambda i, p, pt: (0, 0))],
        out_specs=pl.BlockSpec((None, KVH, HD, LANES), lambda i, p, pt: (i, 0, 0, 0)),
        scratch_shapes=[pltpu.VMEM((KVH, nsp, LANES), F32), pltpu.VMEM((KVH, HD, LANES), F32),
                        pltpu.VMEM((KVH, HD, LANES), F32), pltpu.VMEM((KVH, 1, LANES), F32),
                        pltpu.VMEM((KVH, 1, LANES), F32), pltpu.VMEM((KVH, HD, LANES), F32)],
    )
    return pl.pallas_call(
        functools.partial(_attn_sample_kernel, n_pages=pages_per_step, n_cmp=n_cmp, n_blocks=n_blocks,
                          n_sel=min(N_SEL, n_blocks)),
        grid_spec=grid_spec,
        out_shape=jax.ShapeDtypeStruct((b, KVH, HD, LANES), F32),
        compiler_params=_params("parallel", "arbitrary"),
        name="nsa_attn_sample",
    )(page_table, *([pool_k] * pages_per_step), *([pool_v] * pages_per_step), *small, mt)


def _rope_tables(pos):
    inv_freq = ROPE_THETA ** (-jnp.arange(ROT_HALF, dtype=F32) * 2.0 / (2 * ROT_HALF))
    ang = pos.astype(F32)[:, None] * inv_freq[None, :]
    return jnp.cos(ang).T, jnp.sin(ang).T


def _nsa_layer(xp, xs, pools, win_bufs, page_table, layer, norm_g, w_in, b_gate, g_q, g_k, g_kcmp,
               cmp_pe, cmp_w1, cmp_b1, cmp_w2, w_out):
    b, t, d = xp.shape
    bs = xs.shape[0]
    past = page_table.shape[1] * PAGE
    nq, nkv = GRP * KVH * HD, KVH * HD
    n_gate = 3 * GRP * KVH

    wt_g = jnp.pad(w_in[:, nq + 6 * nkv:].T.reshape(KVH, n_gate // KVH, d), ((0, 0), (0, 16 - n_gate // KVH), (0, 0)))
    wt = jnp.concatenate([w_in[:, :nq + 6 * nkv].T, wt_g.reshape(KVH * 16, d)], axis=0).astype(BF16)
    bg = jnp.pad(b_gate.reshape(KVH, n_gate // KVH), ((0, 0), (0, 16 - n_gate // KVH))).reshape(KVH * 16, 1)
    gn = norm_g.reshape(1, d)
    wo = w_out.astype(BF16)

    def project(x, pos, tm):
        cos, sin = _rope_tables(pos)
        col = lambda v: jnp.broadcast_to(v[..., None], v.shape + (tm,))
        return _nsa_proj(x, gn, wt, col(g_q), col(g_k), col(bg[:, 0]), cos, sin, tm)

    tm = min(512, t)
    qt, kct_rows, vct_rows, kst, vst, kwt, vwt, gt = project(xp, jnp.arange(t), tm)
    pages = t // PAGE
    nh = pages * (PAGE // CMP_STRIDE)
    cmp_w = [_cmp_weights(cmp_pe[i], cmp_w1[i], cmp_b1[i], cmp_w2[i], g_kcmp, nh) for i in range(2)]
    pps = min(16, pages)
    kct = _cmp_out(_cmp_fs(kct_rows, cmp_w[0][0], pages, pps), *cmp_w[0][1:], is_key=True)
    vct = _cmp_out(_cmp_fs(vct_rows, cmp_w[1][0], pages, pps), *cmp_w[1][1:], is_key=False)
    ot = _attn_prompt(qt, kct, vct, kst, vst, kwt, vwt, gt)
    yp = _nsa_out(ot, xp, wo, tm)
    n_win = min(WINDOW, t)
    rows_p = [kct_rows, vct_rows, kst, vst, kwt[..., t - n_win:], vwt[..., t - n_win:]]

    xs_pad = jnp.pad(xs.reshape(1, bs, d), ((0, 0), (0, LANES - bs), (0, 0)))
    outs = project(xs_pad, jnp.full((LANES,), past), LANES)
    qt_s, rows_s, gt_s = outs[0], outs[1:7], outs[7]
    pages_s = page_table.shape[1]
    nh_s = pages_s * (PAGE // CMP_STRIDE)
    cmp_ws = [_cmp_weights(cmp_pe[i], cmp_w1[i], cmp_b1[i], cmp_w2[i], g_kcmp, nh_s) for i in range(2)]
    pool5 = [jnp.transpose(pl_, (0, 1, 3, 4, 2)) for pl_ in pools]
    pps = min(16, pages_s)
    kct_s = _cmp_out(_cmp_fs(pool5[0], cmp_ws[0][0], pages_s, pps, page_table, layer), *cmp_ws[0][1:], is_key=True)
    vct_s = _cmp_out(_cmp_fs(pool5[1], cmp_ws[1][0], pages_s, pps, page_table, layer), *cmp_ws[1][1:], is_key=False)
    qs = jnp.transpose(qt_s[0, :, :, :bs].reshape(KVH, GRP, HD, bs), (3, 0, 2, 1))
    qs = jnp.pad(qs, ((0, 0), (0, 0), (0, 0), (0, LANES - GRP)))
    new = lambda a: jnp.pad(jnp.transpose(a[0, :, :, :bs], (2, 0, 1))[..., None],
                            ((0, 0), (0, 0), (0, 0), (0, LANES - 1)))
    gs = jnp.transpose(gt_s[0, :, :n_gate // KVH, :bs].reshape(KVH, GRP, 3, bs), (3, 0, 2, 1))
    gs = jnp.pad(gs, ((0, 0), (0, 0), (0, 8 - 3), (0, LANES - GRP)))
    win5 = [jnp.transpose(wb[layer], (0, 2, 3, 1)) for wb in win_bufs]
    ot_s = _attn_sample(qs, kct_s, vct_s, pool5[2], pool5[3], page_table, layer, win5[0], win5[1],
                        new(rows_s[2]), new(rows_s[3]), new(rows_s[4]), new(rows_s[5]), gs)
    ot_s = jnp.transpose(ot_s[..., :GRP], (1, 3, 2, 0)).reshape(1, GRP * KVH, HD, bs)
    ot_s = jnp.pad(ot_s, ((0, 0), (0, 0), (0, 0), (0, LANES - bs))).astype(BF16)
    ys = _nsa_out(ot_s, xs_pad, wo, LANES)[0, :bs].reshape(bs, 1, d)

    to_rows = lambda a: jnp.transpose(a, (0, 3, 1, 2))
    rows_p = [to_rows(a) for a in rows_p]
    rows_s = [to_rows(a[:, :, :, :bs]).reshape(bs, 1, KVH, HD) for a in rows_s]
    return yp, ys, rows_p, rows_s


def kernel(x_prompt, x_sample, state_gla, cache_k_cmp, cache_v_cmp, cache_k_slc, cache_v_slc, cache_k_win, cache_v_win, page_table, norm_mix, norm_mlp, mlp_up, mlp_down, gla_w_in, gla_w_gate2, gla_b_gate, gla_g_out, gla_w_out, nsa_w_in, nsa_b_gate, nsa_g_q, nsa_g_k, nsa_g_kcmp, nsa_cmp_pe, nsa_cmp_w1, nsa_cmp_b1, nsa_cmp_w2, nsa_w_out):
    depth = norm_mix.shape[0]
    b, t, d = x_prompt.shape
    bs = x_sample.shape[0]
    xp, xs = x_prompt, x_sample
    gla_p, gla_s, nsa_p, nsa_s = [], [], [], []
    pools = (cache_k_cmp, cache_v_cmp, cache_k_slc, cache_v_slc)
    for i in range(depth):
        j = i // 2
        if i % 2 == 0:
            xp, xs, sp, ss = _gla_layer(xp, xs, state_gla[j], norm_mix[i], gla_w_in[j], gla_w_gate2[j],
                                        gla_b_gate[j], gla_g_out[j], gla_w_out[j])
            gla_p.append(sp)
            gla_s.append(ss)
        else:
            xp, xs, rp, rs = _nsa_layer(xp, xs, pools, (cache_k_win, cache_v_win), page_table, j, norm_mix[i],
                                        nsa_w_in[j], nsa_b_gate[j], nsa_g_q[j], nsa_g_k[j], nsa_g_kcmp[j],
                                        nsa_cmp_pe[j], nsa_cmp_w1[j], nsa_cmp_b1[j], nsa_cmp_w2[j], nsa_w_out[j])
            nsa_p.append(rp)
            nsa_s.append(rs)
        g = norm_mlp[i].reshape(1, d)
        wu, wd = mlp_up[i].astype(BF16), mlp_down[i].astype(BF16)
        xp = _mlp(xp.reshape(b * t, d), g, wu, wd, tm=512).reshape(b, t, d)
        xs = _mlp(xs.reshape(bs, d), g, wu, wd, tm=bs).reshape(bs, 1, d)
    stack = lambda lst, r: jnp.stack([e[r] for e in lst])
    return (xp, xs, jnp.stack(gla_p), jnp.stack(gla_s),
            stack(nsa_p, 0), stack(nsa_s, 0), stack(nsa_p, 1), stack(nsa_s, 1),
            stack(nsa_p, 2), stack(nsa_s, 2), stack(nsa_p, 3), stack(nsa_s, 3),
            stack(nsa_p, 4), stack(nsa_s, 4), stack(nsa_p, 5), stack(nsa_s, 5))
```

```python
import functools

import jax
import jax.numpy as jnp
from jax import lax
from jax.experimental import pallas as pl
from jax.experimental.pallas import tpu as pltpu

F32 = jnp.float32
BF16 = jnp.bfloat16
EPS = 1e-6
NEG = -1e30
VMEM_LIMIT_BYTES = 48 * 1024 * 1024
LANES = 128
PAGE = 128
HD = 64
KVH = 4
GRP = 4
L_CMP, CMP_STRIDE, L_SLC, N_SEL, WINDOW = 32, 16, 64, 16, 512
GLA_H, GLA_DK, GLA_DV, GLA_CHUNK = 4, 128, 256, 64
ROPE_THETA, ROT_HALF = 500000.0, 8
LOG2E = 1.4426950408889634
V_PAD = 16
KT = 512


def _params(*sem):
    return pltpu.CompilerParams(dimension_semantics=sem, vmem_limit_bytes=VMEM_LIMIT_BYTES)


def _dot(a, b):
    return jnp.dot(a, b, preferred_element_type=F32)


def _dot_nt(a, b):
    return lax.dot_general(a, b, (((1,), (1,)), ((), ())), preferred_element_type=F32)


def _dot_tn(a, b):
    return lax.dot_general(a, b, (((0,), (0,)), ((), ())), preferred_element_type=F32)


def _rms_rows(x, g):
    ms = jnp.mean(x * x, axis=-1, keepdims=True)
    return x * lax.rsqrt(ms + EPS) * g


def _mlp_kernel(x_ref, g_ref, wu_ref, wd_ref, o_ref, h_scr, acc_scr):
    f = pl.program_id(1)

    @pl.when(f == 0)
    def _():
        h_scr[...] = _rms_rows(x_ref[...], g_ref[...]).astype(BF16)
        acc_scr[...] = jnp.zeros_like(acc_scr)

    u = jnp.maximum(_dot(h_scr[...], wu_ref[...]), 0.0)
    acc_scr[...] += _dot((u * u).astype(BF16), wd_ref[...])

    @pl.when(f == pl.num_programs(1) - 1)
    def _():
        o_ref[...] = x_ref[...] + acc_scr[...]


def _mlp(x, g, wu, wd, tm, tf=512):
    m, d = x.shape
    ff = wu.shape[1]
    return pl.pallas_call(
        _mlp_kernel,
        grid=(m // tm, ff // tf),
        in_specs=[pl.BlockSpec((tm, d), lambda i, f: (i, 0)),
                  pl.BlockSpec((1, d), lambda i, f: (0, 0)),
                  pl.BlockSpec((d, tf), lambda i, f: (0, f)),
                  pl.BlockSpec((tf, d), lambda i, f: (f, 0))],
        out_specs=pl.BlockSpec((tm, d), lambda i, f: (i, 0)),
        out_shape=jax.ShapeDtypeStruct((m, d), F32),
        scratch_shapes=[pltpu.VMEM((tm, d), BF16), pltpu.VMEM((tm, d), F32)],
        compiler_params=_params("parallel", "arbitrary"),
        name="mlp_block",
    )(x, g, wu, wd)


def _gla_proj_kernel(x_ref, g_ref, w_ref, wg2_ref, bg_ref, q_ref, k_ref, v_ref, r_ref, gl_ref):
    h = _rms_rows(x_ref[...], g_ref[...]).astype(BF16)
    dk, dv = GLA_H * GLA_DK, GLA_H * GLA_DV
    q_ref[...] = _dot(h, w_ref[:, 0:dk]) * (GLA_DK ** -0.5)
    k_ref[...] = _dot(h, w_ref[:, dk:2 * dk])
    v_ref[...] = _dot(h, w_ref[:, 2 * dk:2 * dk + dv])
    r_ref[...] = _dot(h, w_ref[:, 2 * dk + dv:2 * dk + 2 * dv])
    gr = _dot(h, w_ref[:, 2 * dk + 2 * dv:])
    xg = _dot(gr.astype(BF16), wg2_ref[...]) + bg_ref[...]
    gl_ref[...] = jax.nn.log_sigmoid(xg) * (1.0 / 16.0)


def _gla_proj(x, g, w, wg2, bg, tm):
    m, d = x.shape
    dk, dv = GLA_H * GLA_DK, GLA_H * GLA_DV
    row = lambda n: pl.BlockSpec((tm, n), lambda i: (i, 0))
    full = lambda a: pl.BlockSpec(a.shape, lambda i: (0, 0))
    return pl.pallas_call(
        _gla_proj_kernel,
        grid=(m // tm,),
        in_specs=[row(d), full(g), full(w), full(wg2), full(bg)],
        out_specs=[row(dk), row(dk), row(dv), row(dv), row(dk)],
        out_shape=[jax.ShapeDtypeStruct((m, n), F32) for n in (dk, dk, dv, dv, dk)],
        compiler_params=_params("parallel"),
        name="gla_proj",
    )(x, g, w, wg2, bg)


def _gla_scan_kernel(q_ref, k_ref, g_ref, v_ref, o_ref, s_ref, st_scr, *, n_chunks):
    c_len = GLA_CHUNK
    st_scr[...] = jnp.zeros_like(st_scr)
    rowi = lax.broadcasted_iota(jnp.int32, (c_len, GLA_DK), 0)
    causal = (lax.broadcasted_iota(jnp.int32, (c_len, c_len), 0)
              >= lax.broadcasted_iota(jnp.int32, (c_len, c_len), 1))

    def body(c, carry):
        sl = pl.ds(pl.multiple_of(c * c_len, c_len), c_len)
        b = g_ref[sl, :]
        sh = 1
        while sh < c_len:
            b = b + jnp.where(rowi >= sh, pltpu.roll(b, sh, 0), 0.0)
            sh *= 2
        b_last = b[c_len - 1:c_len, :]
        b_mid = b[c_len // 2 - 1:c_len // 2, :]
        q = q_ref[sl, :]
        k = k_ref[sl, :]
        v = v_ref[sl, :].astype(BF16)
        qe = (q * jnp.exp(b)).astype(BF16)
        qa = (q * jnp.exp(b - b_mid)).astype(BF16)
        ka = (k * jnp.exp(b_mid - b)).astype(BF16)
        kd = (k * jnp.exp(b_last - b)).astype(BF16)
        a = jnp.where(causal, _dot_nt(qa, ka), 0.0)
        st = st_scr[...]
        o_ref[sl, :] = _dot_nt(qe, st.astype(BF16)) + _dot(a.astype(BF16), v)
        st_scr[...] = st * jnp.exp(b_last) + _dot_tn(v, kd)
        return carry

    lax.fori_loop(0, n_chunks, body, 0)
    s_ref[...] = st_scr[...].T


def _gla_scan(q, k, gl, v):
    b, t, _ = q.shape
    kq = pl.BlockSpec((None, t, GLA_DK), lambda i, h: (i, 0, h))
    vv = pl.BlockSpec((None, t, GLA_DV), lambda i, h: (i, 0, h))
    return pl.pallas_call(
        functools.partial(_gla_scan_kernel, n_chunks=t // GLA_CHUNK),
        grid=(b, GLA_H),
        in_specs=[kq, kq, kq, vv],
        out_specs=[vv, pl.BlockSpec((None, None, GLA_DK, GLA_DV), lambda i, h: (i, h, 0, 0))],
        out_shape=[jax.ShapeDtypeStruct((b, t, GLA_H * GLA_DV), F32),
                   jax.ShapeDtypeStruct((b, GLA_H, GLA_DK, GLA_DV), F32)],
        scratch_shapes=[pltpu.VMEM((GLA_DV, GLA_DK), F32)],
        compiler_params=_params("parallel", "parallel"),
        name="gla_scan",
    )(q, k, gl, v)


def _gla_step_kernel(q_ref, k_ref, g_ref, v_ref, s0_ref, o_ref, s_ref):
    def col(x):
        return jnp.broadcast_to(x, (LANES, LANES)).T

    for h in range(GLA_H):
        ks = slice(h * GLA_DK, (h + 1) * GLA_DK)
        qc, kc, ec = col(q_ref[:, ks]), col(k_ref[:, ks]), col(jnp.exp(g_ref[:, ks]))
        for half in range(GLA_DV // LANES):
            vs = slice(h * GLA_DV + half * LANES, h * GLA_DV + (half + 1) * LANES)
            ss = slice(half * LANES, (half + 1) * LANES)
            sn = ec * s0_ref[h, :, ss] + kc * v_ref[:, vs]
            s_ref[h, :, ss] = sn
            o_ref[:, vs] = jnp.sum(qc * sn, axis=0, keepdims=True)


def _gla_step(q, k, gl, v, s0):
    b = q.shape[0]
    kq = pl.BlockSpec((None, 1, GLA_H * GLA_DK), lambda i: (i, 0, 0))
    vv = pl.BlockSpec((None, 1, GLA_H * GLA_DV), lambda i: (i, 0, 0))
    st = pl.BlockSpec((None, GLA_H, GLA_DK, GLA_DV), lambda i: (i, 0, 0, 0))
    return pl.pallas_call(
        _gla_step_kernel,
        grid=(b,),
        in_specs=[kq, kq, kq, vv, st],
        out_specs=[vv, st],
        out_shape=[jax.ShapeDtypeStruct((b, 1, GLA_H * GLA_DV), F32),
                   jax.ShapeDtypeStruct((b, GLA_H, GLA_DK, GLA_DV), F32)],
        compiler_params=_params("parallel"),
        name="gla_step",
    )(q, k, gl, v, s0)


def _gla_out_kernel(o_ref, r_ref, x_ref, go_ref, w_ref, y_ref):
    parts = []
    for h in range(GLA_H):
        sl = slice(h * GLA_DV, (h + 1) * GLA_DV)
        r = r_ref[:, sl]
        parts.append((_rms_rows(o_ref[:, sl], go_ref[...]) * (r * jax.nn.sigmoid(r))).astype(BF16))
    y_ref[...] = x_ref[...] + _dot(jnp.concatenate(parts, axis=1), w_ref[...])


def _gla_out(o, r, x, go, w, tm):
    m, d = x.shape
    row = pl.BlockSpec((tm, d), lambda i: (i, 0))
    full = lambda a: pl.BlockSpec(a.shape, lambda i: (0, 0))
    return pl.pallas_call(
        _gla_out_kernel,
        grid=(m // tm,),
        in_specs=[row, row, row, full(go), full(w)],
        out_specs=row,
        out_shape=jax.ShapeDtypeStruct((m, d), F32),
        compiler_params=_params("parallel"),
        name="gla_out",
    )(o, r, x, go, w)


def _gla_layer(xp, xs, s0, norm_g, w_in, w_gate2, b_gate, g_out, w_out):
    b, t, d = xp.shape
    bs = xs.shape[0]
    dk, dv = GLA_H * GLA_DK, GLA_H * GLA_DV
    rank = w_gate2.shape[0]
    w = jnp.pad(w_in, ((0, 0), (0, LANES - rank))).astype(BF16)
    wg2 = jnp.pad(w_gate2, ((0, LANES - rank), (0, 0))).astype(BF16)
    g = norm_g.reshape(1, d)
    bg = b_gate.reshape(1, dk)
    go = g_out.reshape(1, GLA_DV)
    wo = w_out.astype(BF16)
    x2 = xp.reshape(b * t, d)
    q, k, v, r, gl = _gla_proj(x2, g, w, wg2, bg, tm=512)
    o, sp = _gla_scan(q.reshape(b, t, dk), k.reshape(b, t, dk), gl.reshape(b, t, dk), v.reshape(b, t, dv))
    yp = _gla_out(o.reshape(b * t, dv), r, x2, go, wo, tm=512).reshape(b, t, d)
    xs2 = xs.reshape(bs, d)
    q, k, v, r, gl = _gla_proj(xs2, g, w, wg2, bg, tm=bs)
    o, ss = _gla_step(q.reshape(bs, 1, dk), k.reshape(bs, 1, dk), gl.reshape(bs, 1, dk), v.reshape(bs, 1, dv), s0)
    ys = _gla_out(o.reshape(bs, dv), r, xs2, go, wo, tm=bs).reshape(bs, 1, d)
    return yp, ys, sp, ss


def _nsa_proj_kernel(x_ref, gn_ref, wt_ref, gq_ref, gk_ref, bg_ref, cos_ref, sin_ref,
                     q_ref, kc_ref, vc_ref, ks_ref, vs_ref, kw_ref, vw_ref, gt_ref,
                     ksa_ref, vsa_ref, kwb_ref, vwa_ref):
    tm = x_ref.shape[0]
    nq = GRP * KVH * HD
    nkv = KVH * HD
    h = _rms_rows(x_ref[...], gn_ref[...]).astype(BF16)
    cos = cos_ref[...][None]
    sin = sin_ref[...][None]

    def norm_rope(z, g, nh):
        z3 = z.reshape(nh, HD, tm)
        y = z3 * lax.rsqrt(jnp.mean(z3 * z3, axis=1, keepdims=True) + EPS) * g[None]
        x1 = y[:, 0:ROT_HALF, :]
        x2 = y[:, ROT_HALF:2 * ROT_HALF, :]
        return jnp.concatenate([x1 * cos - x2 * sin, x1 * sin + x2 * cos, y[:, 2 * ROT_HALF:, :]], axis=1)

    zq = _dot_nt(wt_ref[0:nq, :], h)
    q_ref[...] = (norm_rope(zq, gq_ref[...], GRP * KVH) * (HD ** -0.5 * LOG2E)).astype(BF16)
    zkv = _dot_nt(wt_ref[nq:nq + 6 * nkv, :], h)
    outs = (kc_ref, vc_ref, ks_ref, vs_ref, kw_ref, vw_ref)
    rows = []
    for i in range(6):
        z = zkv[i * nkv:(i + 1) * nkv, :]
        rows.append(norm_rope(z, gk_ref[i // 2], KVH) if i % 2 == 0 else z.reshape(KVH, HD, tm))
        outs[i][...] = rows[i]
    zg = _dot_nt(wt_ref[nq + 6 * nkv:, :], h) + bg_ref[...]
    gt_ref[...] = jax.nn.sigmoid(zg).reshape(KVH, 16, tm)

    nsp = ksa_ref.shape[1] - HD
    blk = lax.broadcasted_iota(jnp.int32, (KVH, nsp, tm), 1)
    tok = pl.program_id(1) * tm + lax.broadcasted_iota(jnp.int32, (KVH, nsp, tm), 2)
    onehot = jnp.where(tok // L_SLC == blk, 1.0, 0.0)
    ksa_ref[...] = jnp.concatenate([rows[2], onehot], axis=1).astype(BF16)
    kwb_ref[...] = rows[4].astype(BF16)
    ones = jnp.where(lax.broadcasted_iota(jnp.int32, (KVH, V_PAD, tm), 1) == 0, 1.0, 0.0)
    vsa_ref[...] = jnp.concatenate([rows[3], ones], axis=1).astype(BF16)
    vwa_ref[...] = jnp.concatenate([rows[5], ones], axis=1).astype(BF16)


def _sel_rows(t):
    return -(-(-(-t // L_SLC)) // 16) * 16


def _nsa_proj(x, gn, wt, gq, gk, bg, cos, sin, tm):
    b, t, d = x.shape
    full = lambda a: pl.BlockSpec(a.shape, lambda i, j: (0,) * a.ndim)
    rows_spec = lambda n: pl.BlockSpec((None, KVH, n, tm), lambda i, j: (i, 0, 0, j))
    rows_shape = lambda n, dt: jax.ShapeDtypeStruct((b, KVH, n, t), dt)
    aug = [HD + _sel_rows(t), HD + V_PAD, HD, HD + V_PAD]
    return pl.pallas_call(
        _nsa_proj_kernel,
        grid=(b, t // tm),
        in_specs=[pl.BlockSpec((None, tm, d), lambda i, j: (i, j, 0)), full(gn), full(wt), full(gq), full(gk),
                  full(bg), pl.BlockSpec((ROT_HALF, tm), lambda i, j: (0, j)),
                  pl.BlockSpec((ROT_HALF, tm), lambda i, j: (0, j))],
        out_specs=[pl.BlockSpec((None, GRP * KVH, HD, tm), lambda i, j: (i, 0, 0, j))] + [rows_spec(HD)] * 6
        + [rows_spec(16)] + [rows_spec(n) for n in aug],
        out_shape=[jax.ShapeDtypeStruct((b, GRP * KVH, HD, t), BF16)] + [rows_shape(HD, F32)] * 6
        + [rows_shape(16, F32)] + [rows_shape(n, BF16) for n in aug],
        compiler_params=_params("parallel", "parallel"),
        name="nsa_proj",
    )(x, gn, wt, gq, gk, bg, cos, sin)


def _nsa_out_kernel(ot_ref, x_ref, w_ref, y_ref):
    ot = ot_ref[...].reshape(GRP * KVH * HD, ot_ref.shape[-1])
    y_ref[...] = x_ref[...] + _dot_tn(ot, w_ref[...])


def _nsa_out(ot, x, w, tm):
    b, t, d = x.shape
    return pl.pallas_call(
        _nsa_out_kernel,
        grid=(b, t // tm),
        in_specs=[pl.BlockSpec((None, GRP * KVH, HD, tm), lambda i, j: (i, 0, 0, j)),
                  pl.BlockSpec((None, tm, d), lambda i, j: (i, j, 0)),
                  pl.BlockSpec(w.shape, lambda i, j: (0, 0))],
        out_specs=pl.BlockSpec((None, tm, d), lambda i, j: (i, j, 0)),
        out_shape=jax.ShapeDtypeStruct((b, t, d), F32),
        compiler_params=_params("parallel", "parallel"),
        name="nsa_out",
    )(ot, x, w)


def _cmp_fs_kernel(*refs, n_pages, paged):
    refs = refs[1:] if paged else refs
    pages = refs[:n_pages]
    w_ref, o_ref, xt_scr, lhs_scr = refs[n_pages:]
    p = pl.program_id(1)
    half = CMP_STRIDE
    for pair in range(KVH // 2):
        for j in range(n_pages):
            xt_scr[pair * n_pages + j] = pages[j][2 * pair:2 * pair + 2].reshape(2 * HD, PAGE).T
            row0 = pl.multiple_of((p * n_pages + j) * (PAGE // half), PAGE // half)
            for s in range(half):
                lhs_scr[pair, pl.ds(row0, PAGE // half), s * LANES:(s + 1) * LANES] = (
                    xt_scr[pair * n_pages + j, pl.ds(s, PAGE // half, stride=half), :])

    @pl.when(p == pl.num_programs(1) - 1)
    def _():
        n = w_ref.shape[1]
        for pair in range(KVH // 2):
            o_ref[:, pair * n:(pair + 1) * n] = _dot(lhs_scr[pair].astype(BF16), w_ref[...])


def _cmp_fs(src, w, n_pages_total, pages_per_step, page_table=None, layer=0):
    paged = page_table is not None
    b = page_table.shape[0] if paged else src.shape[0]
    nh = n_pages_total * (PAGE // CMP_STRIDE)
    steps = n_pages_total // pages_per_step
    if paged:
        page_spec = lambda j: pl.BlockSpec(
            (None, None, KVH, HD, PAGE), lambda i, p, pt: (layer, pt[i, p * pages_per_step + j], 0, 0, 0))
        w_spec = pl.BlockSpec(w.shape, lambda i, p, pt: (0, 0))
        o_spec = pl.BlockSpec((None, nh, KVH * 512), lambda i, p, pt: (i, 0, 0))
    else:
        page_spec = lambda j: pl.BlockSpec((None, KVH, HD, PAGE), lambda i, p: (i, 0, 0, p * pages_per_step + j))
        w_spec = pl.BlockSpec(w.shape, lambda i, p: (0, 0))
        o_spec = pl.BlockSpec((None, nh, KVH * 512), lambda i, p: (i, 0, 0))
    grid_spec = pltpu.PrefetchScalarGridSpec(
        num_scalar_prefetch=1 if paged else 0,
        grid=(b, steps),
        in_specs=[page_spec(j) for j in range(pages_per_step)] + [w_spec],
        out_specs=o_spec,
        scratch_shapes=[pltpu.VMEM((2 * pages_per_step, PAGE, 2 * HD), F32),
                        pltpu.VMEM((KVH // 2, nh, CMP_STRIDE * LANES), F32)],
    )
    args = ([page_table] if paged else []) + [src] * pages_per_step + [w]
    return pl.pallas_call(
        functools.partial(_cmp_fs_kernel, n_pages=pages_per_step, paged=paged),
        grid_spec=grid_spec,
        out_shape=jax.ShapeDtypeStruct((b, nh, KVH * 512), F32),
        compiler_params=_params("parallel", "arbitrary"),
        name="cmp_fs",
    )(*args)


def _cmp_out_kernel(fs_ref, b1_ref, pe_ref, w1_ref, w2t_ref, gk_ref, o_ref, *, is_key):
    nh = fs_ref.shape[0]
    hidden = b1_ref.shape[1]
    c = _dot(pe_ref[0].astype(BF16), w1_ref[0]) + _dot(pe_ref[1].astype(BF16), w1_ref[1])
    bias = c[0:1, :] + b1_ref[...]
    for kvh in range(KVH):
        first = fs_ref[:, kvh * 2 * hidden:kvh * 2 * hidden + hidden]
        second = fs_ref[:, kvh * 2 * hidden + hidden:(kvh + 1) * 2 * hidden]
        hid = jax.nn.gelu(first + pltpu.roll(second, nh - 1, 0) + bias)
        yt = _dot_nt(w2t_ref[...], hid.astype(BF16))
        if is_key:
            yt = yt * lax.rsqrt(jnp.mean(yt * yt, axis=0, keepdims=True) + EPS) * gk_ref[...]
        o_ref[kvh] = yt.astype(BF16)


def _cmp_out(fs, b1, pe, w1, w2t, gk, is_key):
    b, nh, _ = fs.shape
    full = lambda a: pl.BlockSpec(a.shape, lambda i: (0,) * a.ndim)
    return pl.pallas_call(
        functools.partial(_cmp_out_kernel, is_key=is_key),
        grid=(b,),
        in_specs=[pl.BlockSpec((None, nh, fs.shape[2]), lambda i: (i, 0, 0)), full(b1), full(pe), full(w1),
                  full(w2t), full(gk)],
        out_specs=pl.BlockSpec((None, KVH, HD, nh), lambda i: (i, 0, 0, 0)),
        out_shape=jax.ShapeDtypeStruct((b, KVH, HD, nh), BF16),
        compiler_params=_params("parallel"),
        name="cmp_out",
    )(fs, b1, pe, w1, w2t, gk)


def _cmp_weights(pe, w1, b1, w2, g_kcmp, nh):
    hidden = w1.shape[-1]
    wfs = jnp.concatenate([w1[:CMP_STRIDE], w1[CMP_STRIDE:]], axis=-1)
    w_pair = jnp.einsum("sdc,ab->sadbc", wfs, jnp.eye(2, dtype=F32))
    w_pair = w_pair.reshape(CMP_STRIDE * 2 * HD, 2 * 2 * hidden).astype(BF16)
    pe2 = jnp.zeros((2, 8, CMP_STRIDE * HD), F32).at[:, 0, :].set(pe.reshape(2, CMP_STRIDE * HD))
    w1f = w1.reshape(2, CMP_STRIDE * HD, hidden).astype(BF16)
    gk = jnp.broadcast_to(g_kcmp.reshape(HD, 1), (HD, nh))
    return w_pair, b1.reshape(1, hidden), pe2, w1f, w2.T.astype(BF16), gk


def _softmax_update(s, mask, vt, m, l, acc):
    m_new = jnp.maximum(m, jnp.max(jnp.where(mask, s, NEG), axis=0, keepdims=True))
    alpha = jnp.exp2(m - m_new)
    p = jnp.where(mask, jnp.exp2(s - m_new), 0.0)
    l_new = l * alpha + jnp.sum(p, axis=0, keepdims=True)
    acc_new = acc * alpha + _dot(vt, p.astype(BF16))
    return m_new, l_new, acc_new, p


def _softmax_init(lanes):
    return jnp.full((1, lanes), NEG, F32), jnp.zeros((1, lanes), F32), jnp.zeros((HD, lanes), F32)


def _finish(l, acc):
    return acc / jnp.maximum(l, 1e-30)


def _split_dot(mt, x):
    hi = x.astype(BF16)
    lo = (x - hi.astype(F32)).astype(BF16)
    return _dot(mt, hi) + _dot(mt, lo)


def _attn_prompt_kernel(q_ref, kc_ref, vc_ref, ksa_ref, vsa_ref, kwb_ref, vwa_ref, gt_ref, mt_ref, o_ref, s_scr,
                        *, tq, n_cmp, n_sel):
    i = pl.program_id(2)
    lanes = GRP * tq
    ncp = kc_ref.shape[1]
    nsp = mt_ref.shape[0]
    qt = jnp.concatenate([q_ref[h] for h in range(GRP)], axis=1)

    def col_max(mx, s):
        return jnp.maximum(mx, jnp.max(s.reshape(s.shape[0] // 8, 8, lanes), axis=0))

    def finish(acc):
        return acc[0:HD] / jnp.maximum(acc[HD:HD + 1], 1e-30)

    tpos1 = i * tq + lax.broadcasted_iota(jnp.int32, (1, tq), 1)
    tpos = jnp.concatenate([tpos1] * GRP, axis=1)

    s = _dot_tn(kc_ref[...], qt)
    n_idx = lax.broadcasted_iota(jnp.int32, (ncp, lanes), 0)
    mask = (n_idx * CMP_STRIDE + (L_CMP - 1) <= tpos) & (n_idx < n_cmp)
    m, l, acc, p = _softmax_update(s, mask, vc_ref[...], *_softmax_init(lanes))
    o_c = _finish(l, acc)
    p = p / jnp.maximum(l, 1e-30)
    imp = p[:, 0:tq]
    for h in range(1, GRP):
        imp = imp + p[:, h * tq:(h + 1) * tq]
    p_slc = _split_dot(mt_ref[...], imp)

    j_idx = lax.broadcasted_iota(jnp.int32, (nsp, tq), 0)
    cur = tpos1 // L_SLC
    valid = j_idx <= cur
    forced = (j_idx == 0) | (j_idx == cur) | (j_idx == cur - 1)
    score = jnp.where(valid & forced, jnp.inf, jnp.where(valid, p_slc, -jnp.inf))
    bits = lax.bitcast_convert_type(score, jnp.int32)
    key = jnp.where(bits >= 0, bits, bits ^ 0x7FFFFFFF)
    n_grp = nsp // 8
    keys = [key[8 * r:8 * r + 8] for r in range(n_grp)]
    keys_m1 = [k - 1 for k in keys]
    sub = lax.broadcasted_iota(jnp.int32, (8, tq), 0)

    def count_group(grp, ranks):
        ranks = list(ranks)
        for u in range(8):
            row = jnp.broadcast_to(keys[grp][u:u + 1, :], (8, tq))
            for r in range(n_grp):
                thr = keys[r] if r < grp else keys_m1[r] if r > grp else jnp.where(sub > u, keys_m1[r], keys[r])
                ranks[r] = ranks[r] + jnp.where(row > thr, 1.0, 0.0)
        return tuple(ranks)

    ranks = tuple(jnp.zeros((8, tq), F32) for _ in range(n_grp))
    last_valid = (i * tq + tq - 1) // L_SLC
    for grp in range(n_grp):
        ranks = lax.cond(8 * grp <= last_valid, functools.partial(count_group, grp), lambda r: r, ranks)
    bias = jnp.where(jnp.concatenate(ranks, axis=0) < n_sel, 0.0, NEG)
    qa = jnp.concatenate([qt, jnp.concatenate([bias] * GRP, axis=1).astype(BF16)], axis=0)

    def slc_scores(st):
        start = pl.multiple_of(st * KT, KT)
        return start, _dot_tn(ksa_ref[:, pl.ds(start, KT)], qa)

    def slc_pass1(st, mx):
        start, s = slc_scores(st)
        s_scr[pl.ds(start, KT), :] = s
        return col_max(mx, s)

    n_full = (i * tq) // KT
    mx = lax.fori_loop(0, n_full, slc_pass1, jnp.full((8, lanes), NEG, F32))
    start, s = slc_scores(n_full)
    key = start + lax.broadcasted_iota(jnp.int32, (KT, lanes), 0)
    s = jnp.where(key <= tpos, s, NEG)
    s_scr[pl.ds(start, KT), :] = s
    m = jnp.max(col_max(mx, s), axis=0, keepdims=True)

    def slc_pass2(st, acc):
        start = pl.multiple_of(st * KT, KT)
        p = jnp.exp2(s_scr[pl.ds(start, KT), :] - m).astype(BF16)
        return acc + _dot(vsa_ref[:, pl.ds(start, KT)], p)

    o_s = finish(lax.fori_loop(0, n_full + 1, slc_pass2, jnp.zeros((HD + V_PAD, lanes), F32)))

    n_wt = WINDOW // tq + 1
    w0 = jnp.maximum(i - (n_wt - 1), 0) * tq
    row_w = lax.broadcasted_iota(jnp.int32, (tq, lanes), 0)
    mx = jnp.full((8, lanes), NEG, F32)
    for r in range(n_wt):
        start = pl.multiple_of(w0 + r * tq, tq)
        s = _dot_tn(kwb_ref[:, pl.ds(start, tq)], qt)
        rel = tpos - (start + row_w)
        s = jnp.where((rel >= 0) & (rel <= WINDOW), s, NEG)
        s_scr[r * tq:(r + 1) * tq, :] = s
        mx = col_max(mx, s)
    m = jnp.max(mx, axis=0, keepdims=True)
    acc = jnp.zeros((HD + V_PAD, lanes), F32)
    for r in range(n_wt):
        start = pl.multiple_of(w0 + r * tq, tq)
        p = jnp.exp2(s_scr[r * tq:(r + 1) * tq, :] - m).astype(BF16)
        acc = acc + _dot(vwa_ref[:, pl.ds(start, tq)], p)
    o_w = finish(acc)

    for h in range(GRP):
        ls = slice(h * tq, (h + 1) * tq)
        o = (gt_ref[3 * h:3 * h + 1, :] * o_c[:, ls] + gt_ref[3 * h + 1:3 * h + 2, :] * o_s[:, ls]
             + gt_ref[3 * h + 2:3 * h + 3, :] * o_w[:, ls])
        o_ref[h] = o.astype(BF16)


def _slc_matrix(nsp, ncp, n_cmp):
    ratio = L_SLC // CMP_STRIDE
    j = jnp.arange(nsp)[:, None]
    n = jnp.arange(ncp)[None, :]
    m = ((n >= ratio * j) & (n <= ratio * j + ratio - 1)).astype(F32)
    m = m + ((n >= ratio * j - 1) & (n <= ratio * j + ratio - 2)).astype(F32)
    return jnp.where(n < n_cmp, m, 0.0).astype(BF16)


def _attn_prompt(qt, kct, vct, ksa, vsa, kwb, vwa, gt, tq=LANES):
    b, _, _, t = qt.shape
    assert t % KT == 0 and WINDOW % tq == 0 and t >= WINDOW + tq
    ncp = kct.shape[-1]
    n_cmp = t // CMP_STRIDE - 1
    ns = -(-t // L_SLC)
    mt = _slc_matrix(_sel_rows(t), ncp, n_cmp)
    res = lambda a: pl.BlockSpec((None, None) + a.shape[2:], lambda i, g, j: (i, g, 0, 0))
    return pl.pallas_call(
        functools.partial(_attn_prompt_kernel, tq=tq, n_cmp=n_cmp, n_sel=min(N_SEL, ns)),
        grid=(b, KVH, t // tq),
        in_specs=[pl.BlockSpec((None, GRP, HD, tq), lambda i, g, j: (i, g, 0, j)),
                  res(kct), res(vct), res(ksa), res(vsa), res(kwb), res(vwa),
                  pl.BlockSpec((None, None, 16, tq), lambda i, g, j: (i, g, 0, j)),
                  pl.BlockSpec(mt.shape, lambda i, g, j: (0, 0))],
        out_specs=pl.BlockSpec((None, GRP, HD, tq), lambda i, g, j: (i, g, 0, j)),
        out_shape=jax.ShapeDtypeStruct(qt.shape, BF16),
        scratch_shapes=[pltpu.VMEM((t, GRP * tq), F32)],
        compiler_params=_params("parallel", "parallel", "arbitrary"),
        name="nsa_attn_prompt",
    )(qt, kct, vct, ksa, vsa, kwb, vwa, gt, mt)


def _attn_sample_kernel(*refs, n_pages, n_cmp, n_blocks, n_sel):
    refs = refs[1:]
    k_pages = refs[:n_pages]
    v_pages = refs[n_pages:2 * n_pages]
    (q_ref, kc_ref, vc_ref, kw_ref, vw_ref, kns_ref, vns_ref, knw_ref, vnw_ref, gt_ref, mt_ref,
     o_ref, sel_scr, oc_scr, ow_scr, m_scr, l_scr, acc_scr) = refs[2 * n_pages:]
    p = pl.program_id(1)
    ncp = kc_ref.shape[-1]
    nsp = mt_ref.shape[0]
    first_row = lax.broadcasted_iota(jnp.int32, (LANES, LANES), 0) == 0

    @pl.when(p == 0)
    def _():
        lane = lax.broadcasted_iota(jnp.int32, (ncp, LANES), 1)
        n_idx = lax.broadcasted_iota(jnp.int32, (ncp, LANES), 0)
        j_col = lax.broadcasted_iota(jnp.int32, (nsp, LANES), 0)
        jp_idx = lax.broadcasted_iota(jnp.int32, (nsp, nsp), 0)
        j_idx = lax.broadcasted_iota(jnp.int32, (nsp, nsp), 1)
        cur = n_blocks - 1
        for g in range(KVH):
            qt = q_ref[g]
            s = _dot_tn(kc_ref[g], qt)
            m, l, acc, pr = _softmax_update(s, n_idx < n_cmp, vc_ref[g], *_softmax_init(LANES))
            oc_scr[g] = _finish(l, acc)
            pr = pr / jnp.maximum(l, 1e-30)
            imp = jnp.sum(jnp.where(lane < GRP, pr, 0.0), axis=1, keepdims=True)
            p_slc = _split_dot(mt_ref[...], jnp.broadcast_to(imp, (ncp, LANES)))
            valid = j_col <= cur
            forced = (j_col == 0) | (j_col == cur) | (j_col == cur - 1)
            score = jnp.where(valid & forced, jnp.inf, jnp.where(valid, p_slc, -jnp.inf))
            col = jnp.concatenate([score] * (nsp // LANES), axis=1)
            row = col.T
            ahead = (col > row) | ((col == row) & (jp_idx < j_idx))
            rank = jnp.sum(jnp.where(ahead, 1.0, 0.0), axis=0, keepdims=True)
            sel = jnp.where(rank < n_sel, 1.0, 0.0)
            sel_scr[g] = jnp.broadcast_to(sel, (LANES, nsp)).T
            st = _softmax_init(LANES)
            s = _dot_tn(kw_ref[g].astype(BF16), qt)
            st = _softmax_update(s, jnp.full(s.shape, True), vw_ref[g].astype(BF16), *st)[:3]
            s = _dot_tn(knw_ref[g].astype(BF16), qt)
            _, l, acc = _softmax_update(s, first_row, vnw_ref[g].astype(BF16), *st)[:3]
            ow_scr[g] = _finish(l, acc)
            m0, l0, a0 = _softmax_init(LANES)
            m_scr[g], l_scr[g], acc_scr[g] = m0, l0, a0

    blocks_per_page = PAGE // L_SLC
    for g in range(KVH):
        qt = q_ref[g]
        kt = jnp.concatenate([k_pages[j][g] for j in range(n_pages)], axis=1).astype(BF16)
        vt = jnp.concatenate([v_pages[j][g] for j in range(n_pages)], axis=1).astype(BF16)
        s = _dot_tn(kt, qt)
        rows = []
        for r in range(n_pages * blocks_per_page):
            e = sel_scr[g, pl.ds(p * n_pages * blocks_per_page + r, 1), :]
            rows.append(jnp.broadcast_to(e, (L_SLC, LANES)))
        mask = jnp.concatenate(rows, axis=0) > 0.5
        m_scr[g], l_scr[g], acc_scr[g] = _softmax_update(s, mask, vt, m_scr[g], l_scr[g], acc_scr[g])[:3]

    @pl.when(p == pl.num_programs(1) - 1)
    def _():
        for g in range(KVH):
            s = _dot_tn(kns_ref[g].astype(BF16), q_ref[g])
            _, l, acc = _softmax_update(s, first_row, vns_ref[g].astype(BF16), m_scr[g], l_scr[g], acc_scr[g])[:3]
            o_s = _finish(l, acc)
            o_ref[g] = gt_ref[g, 0:1, :] * oc_scr[g] + gt_ref[g, 1:2, :] * o_s + gt_ref[g, 2:3, :] * ow_scr[g]


def _attn_sample(qs, kct, vct, pool_k, pool_v, page_table, layer, kwt, vwt, kns, vns, knw, vnw, gs,
                 pages_per_step=8):
    b, n_pages_total = page_table.shape
    past = n_pages_total * PAGE
    n_cmp = past // CMP_STRIDE - 1
    ncp = kct.shape[-1]
    n_blocks = past // L_SLC + 1
    nsp = -(-n_blocks // LANES) * LANES
    mt = _slc_matrix(nsp, ncp, n_cmp)
    steps = n_pages_total // pages_per_step
    page_spec = lambda j: pl.BlockSpec((None, None, KVH, HD, PAGE),
                                       lambda i, p, pt: (layer, pt[i, p * pages_per_step + j], 0, 0, 0))
    per_b = lambda a: pl.BlockSpec((None,) + a.shape[1:], lambda i, p, pt: (i,) + (0,) * (a.ndim - 1))
    small = [qs, kct, vct, kwt, vwt, kns, vns, knw, vnw, gs]
    grid_spec = pltpu.PrefetchScalarGridSpec(
        num_scalar_prefetch=1,
        grid=(b, steps),
        in_specs=[page_spec(j) for j in range(pages_per_step)] * 2 + [per_b(a) for a in small]
        + [pl.BlockSpec(mt.shape, lambda i, p, pt: (0, 0))],
        out_specs=pl.BlockSpec((None, KVH, HD, LANES), lambda i, p, pt: (i, 0, 0, 0)),
        scratch_shapes=[pltpu.VMEM((KVH, nsp, LANES), F32), pltpu.VMEM((KVH, HD, LANES), F32),
                        pltpu.VMEM((KVH, HD, LANES), F32), pltpu.VMEM((KVH, 1, LANES), F32),
                        pltpu.VMEM((KVH, 1, LANES), F32), pltpu.VMEM((KVH, HD, LANES), F32)],
    )
    return pl.pallas_call(
        functools.partial(_attn_sample_kernel, n_pages=pages_per_step, n_cmp=n_cmp, n_blocks=n_blocks,
                          n_sel=min(N_SEL, n_blocks)),
        grid_spec=grid_spec,
        out_shape=jax.ShapeDtypeStruct((b, KVH, HD, LANES), F32),
        compiler_params=_params("parallel", "arbitrary"),
        name="nsa_attn_sample",
    )(page_table, *([pool_k] * pages_per_step), *([pool_v] * pages_per_step), *small, mt)


def _rope_tables(pos):
    inv_freq = ROPE_THETA ** (-jnp.arange(ROT_HALF, dtype=F32) * 2.0 / (2 * ROT_HALF))
    ang = pos.astype(F32)[:, None] * inv_freq[None, :]
    return jnp.cos(ang).T, jnp.sin(ang).T


def _nsa_layer(xp, xs, pools, win_bufs, page_table, layer, norm_g, w_in, b_gate, g_q, g_k, g_kcmp,
               cmp_pe, cmp_w1, cmp_b1, cmp_w2, w_out):
    b, t, d = xp.shape
    bs = xs.shape[0]
    past = page_table.shape[1] * PAGE
    nq, nkv = GRP * KVH * HD, KVH * HD
    n_gate = 3 * GRP * KVH

    wt_g = jnp.pad(w_in[:, nq + 6 * nkv:].T.reshape(KVH, n_gate // KVH, d), ((0, 0), (0, 16 - n_gate // KVH), (0, 0)))
    wt = jnp.concatenate([w_in[:, :nq + 6 * nkv].T, wt_g.reshape(KVH * 16, d)], axis=0).astype(BF16)
    bg = jnp.pad(b_gate.reshape(KVH, n_gate // KVH), ((0, 0), (0, 16 - n_gate // KVH))).reshape(KVH * 16, 1)
    gn = norm_g.reshape(1, d)
    wo = w_out.astype(BF16)

    def project(x, pos, tm):
        cos, sin = _rope_tables(pos)
        col = lambda v: jnp.broadcast_to(v[..., None], v.shape + (tm,))
        return _nsa_proj(x, gn, wt, col(g_q), col(g_k), col(bg[:, 0]), cos, sin, tm)

    tm = min(512, t)
    qt, kct_rows, vct_rows, kst, vst, kwt, vwt, gt, ksa, vsa, kwb, vwa = project(xp, jnp.arange(t), tm)
    pages = t // PAGE
    nh = pages * (PAGE // CMP_STRIDE)
    cmp_w = [_cmp_weights(cmp_pe[i], cmp_w1[i], cmp_b1[i], cmp_w2[i], g_kcmp, nh) for i in range(2)]
    pps = min(16, pages)
    kct = _cmp_out(_cmp_fs(kct_rows, cmp_w[0][0], pages, pps), *cmp_w[0][1:], is_key=True)
    vct = _cmp_out(_cmp_fs(vct_rows, cmp_w[1][0], pages, pps), *cmp_w[1][1:], is_key=False)
    ot = _attn_prompt(qt, kct, vct, ksa, vsa, kwb, vwa, gt)
    yp = _nsa_out(ot, xp, wo, tm)
    n_win = min(WINDOW, t)
    rows_p = [kct_rows, vct_rows, kst, vst, kwt[..., t - n_win:], vwt[..., t - n_win:]]

    xs_pad = jnp.pad(xs.reshape(1, bs, d), ((0, 0), (0, LANES - bs), (0, 0)))
    outs = project(xs_pad, jnp.full((LANES,), past), LANES)
    qt_s, rows_s, gt_s = outs[0], outs[1:7], outs[7]
    pages_s = page_table.shape[1]
    nh_s = pages_s * (PAGE // CMP_STRIDE)
    cmp_ws = [_cmp_weights(cmp_pe[i], cmp_w1[i], cmp_b1[i], cmp_w2[i], g_kcmp, nh_s) for i in range(2)]
    pool5 = [jnp.transpose(pl_, (0, 1, 3, 4, 2)) for pl_ in pools]
    pps = min(16, pages_s)
    kct_s = _cmp_out(_cmp_fs(pool5[0], cmp_ws[0][0], pages_s, pps, page_table, layer), *cmp_ws[0][1:], is_key=True)
    vct_s = _cmp_out(_cmp_fs(pool5[1], cmp_ws[1][0], pages_s, pps, page_table, layer), *cmp_ws[1][1:], is_key=False)
    qs = jnp.transpose(qt_s[0, :, :, :bs].reshape(KVH, GRP, HD, bs), (3, 0, 2, 1))
    qs = jnp.pad(qs, ((0, 0), (0, 0), (0, 0), (0, LANES - GRP)))
    new = lambda a: jnp.pad(jnp.transpose(a[0, :, :, :bs], (2, 0, 1))[..., None],
                            ((0, 0), (0, 0), (0, 0), (0, LANES - 1)))
    gs = jnp.transpose(gt_s[0, :, :n_gate // KVH, :bs].reshape(KVH, GRP, 3, bs), (3, 0, 2, 1))
    gs = jnp.pad(gs, ((0, 0), (0, 0), (0, 8 - 3), (0, LANES - GRP)))
    win5 = [jnp.transpose(wb[layer], (0, 2, 3, 1)) for wb in win_bufs]
    ot_s = _attn_sample(qs, kct_s, vct_s, pool5[2], pool5[3], page_table, layer, win5[0], win5[1],
                        new(rows_s[2]), new(rows_s[3]), new(rows_s[4]), new(rows_s[5]), gs)
    ot_s = jnp.transpose(ot_s[..., :GRP], (1, 3, 2, 0)).reshape(1, GRP * KVH, HD, bs)
    ot_s = jnp.pad(ot_s, ((0, 0), (0, 0), (0, 0), (0, LANES - bs))).astype(BF16)
    ys = _nsa_out(ot_s, xs_pad, wo, LANES)[0, :bs].reshape(bs, 1, d)

    to_rows = lambda a: jnp.transpose(a, (0, 3, 1, 2))
    rows_p = [to_rows(a) for a in rows_p]
    rows_s = [to_rows(a[:, :, :, :bs]).reshape(bs, 1, KVH, HD) for a in rows_s]
    return yp, ys, rows_p, rows_s


def kernel(x_prompt, x_sample, state_gla, cache_k_cmp, cache_v_cmp, cache_k_slc, cache_v_slc, cache_k_win, cache_v_win, page_table, norm_mix, norm_mlp, mlp_up, mlp_down, gla_w_in, gla_w_gate2, gla_b_gate, gla_g_out, gla_w_out, nsa_w_in, nsa_b_gate, nsa_g_q, nsa_g_k, nsa_g_kcmp, nsa_cmp_pe, nsa_cmp_w1, nsa_cmp_b1, nsa_cmp_w2, nsa_w_out):
    depth = norm_mix.shape[0]
    b, t, d = x_prompt.shape
    bs = x_sample.shape[0]
    xp, xs = x_prompt, x_sample
    gla_p, gla_s, nsa_p, nsa_s = [], [], [], []
    pools = (cache_k_cmp, cache_v_cmp, cache_k_slc, cache_v_slc)
    for i in range(depth):
        j = i // 2
        if i % 2 == 0:
            xp, xs, sp, ss = _gla_layer(xp, xs, state_gla[j], norm_mix[i], gla_w_in[j], gla_w_gate2[j],
                                        gla_b_gate[j], gla_g_out[j], gla_w_out[j])
            gla_p.append(sp)
            gla_s.append(ss)
        else:
            xp, xs, rp, rs = _nsa_layer(xp, xs, pools, (cache_k_win, cache_v_win), page_table, j, norm_mix[i],
                                        nsa_w_in[j], nsa_b_gate[j], nsa_g_q[j], nsa_g_k[j], nsa_g_kcmp[j],
                                        nsa_cmp_pe[j], nsa_cmp_w1[j], nsa_cmp_b1[j], nsa_cmp_w2[j], nsa_w_out[j])
            nsa_p.append(rp)
            nsa_s.append(rs)
        g = norm_mlp[i].reshape(1, d)
        wu, wd = mlp_up[i].astype(BF16), mlp_down[i].astype(BF16)
        xp = _mlp(xp.reshape(b * t, d), g, wu, wd, tm=512).reshape(b, t, d)
        xs = _mlp(xs.reshape(bs, d), g, wu, wd, tm=bs).reshape(bs, 1, d)
    stack = lambda lst, r: jnp.stack([e[r] for e in lst])
    return (xp, xs, jnp.stack(gla_p), jnp.stack(gla_s),
            stack(nsa_p, 0), stack(nsa_s, 0), stack(nsa_p, 1), stack(nsa_s, 1),
            stack(nsa_p, 2), stack(nsa_s, 2), stack(nsa_p, 3), stack(nsa_s, 3),
            stack(nsa_p, 4), stack(nsa_s, 4), stack(nsa_p, 5), stack(nsa_s, 5))
```

```python
import functools

import jax
import jax.numpy as jnp
from jax import lax
from jax.experimental import pallas as pl
from jax.experimental.pallas import tpu as pltpu

F32 = jnp.float32
BF16 = jnp.bfloat16
EPS = 1e-6
NEG = -1e30
VMEM_LIMIT_BYTES = 48 * 1024 * 1024
LANES = 128
PAGE = 128
HD = 64
KVH = 4
GRP = 4
L_CMP, CMP_STRIDE, L_SLC, N_SEL, WINDOW = 32, 16, 64, 16, 512
GLA_H, GLA_DK, GLA_DV, GLA_CHUNK = 4, 128, 256, 64
ROPE_THETA, ROT_HALF = 500000.0, 8
LOG2E = 1.4426950408889634
V_PAD = 16
KT = 512


def _params(*sem):
    return pltpu.CompilerParams(dimension_semantics=sem, vmem_limit_bytes=VMEM_LIMIT_BYTES)


def _dot(a, b):
    return jnp.dot(a, b, preferred_element_type=F32)


def _dot_nt(a, b):
    return lax.dot_general(a, b, (((1,), (1,)), ((), ())), preferred_element_type=F32)


def _dot_tn(a, b):
    return lax.dot_general(a, b, (((0,), (0,)), ((), ())), preferred_element_type=F32)


def _rms_rows(x, g):
    ms = jnp.mean(x * x, axis=-1, keepdims=True)
    return x * lax.rsqrt(ms + EPS) * g


def _mlp_kernel(x_ref, g_ref, wu_ref, wd_ref, o_ref, h_scr, acc_scr):
    f = pl.program_id(1)

    @pl.when(f == 0)
    def _():
        h_scr[...] = _rms_rows(x_ref[...], g_ref[...]).astype(BF16)
        acc_scr[...] = jnp.zeros_like(acc_scr)

    u = jnp.maximum(_dot(h_scr[...], wu_ref[...]), 0.0)
    acc_scr[...] += _dot((u * u).astype(BF16), wd_ref[...])

    @pl.when(f == pl.num_programs(1) - 1)
    def _():
        o_ref[...] = x_ref[...] + acc_scr[...]


def _mlp(x, g, wu, wd, tm, tf=1024):
    m, d = x.shape
    ff = wu.shape[1]
    return pl.pallas_call(
        _mlp_kernel,
        grid=(m // tm, ff // tf),
        in_specs=[pl.BlockSpec((tm, d), lambda i, f: (i, 0)),
                  pl.BlockSpec((1, d), lambda i, f: (0, 0)),
                  pl.BlockSpec((d, tf), lambda i, f: (0, f)),
                  pl.BlockSpec((tf, d), lambda i, f: (f, 0))],
        out_specs=pl.BlockSpec((tm, d), lambda i, f: (i, 0)),
        out_shape=jax.ShapeDtypeStruct((m, d), F32),
        scratch_shapes=[pltpu.VMEM((tm, d), BF16), pltpu.VMEM((tm, d), F32)],
        compiler_params=_params("parallel", "arbitrary"),
        name="mlp_block",
    )(x, g, wu, wd)


def _gla_proj_kernel(x_ref, g_ref, w_ref, wg2_ref, bg_ref, q_ref, k_ref, v_ref, r_ref, gl_ref):
    h = _rms_rows(x_ref[...], g_ref[...]).astype(BF16)
    dk, dv = GLA_H * GLA_DK, GLA_H * GLA_DV
    q_ref[...] = _dot(h, w_ref[:, 0:dk]) * (GLA_DK ** -0.5)
    k_ref[...] = _dot(h, w_ref[:, dk:2 * dk])
    v_ref[...] = _dot(h, w_ref[:, 2 * dk:2 * dk + dv])
    r_ref[...] = _dot(h, w_ref[:, 2 * dk + dv:2 * dk + 2 * dv])
    gr = _dot(h, w_ref[:, 2 * dk + 2 * dv:])
    xg = _dot(gr.astype(BF16), wg2_ref[...]) + bg_ref[...]
    gl_ref[...] = jax.nn.log_sigmoid(xg) * (1.0 / 16.0)


def _gla_proj(x, g, w, wg2, bg, tm):
    m, d = x.shape
    dk, dv = GLA_H * GLA_DK, GLA_H * GLA_DV
    row = lambda n: pl.BlockSpec((tm, n), lambda i: (i, 0))
    full = lambda a: pl.BlockSpec(a.shape, lambda i: (0, 0))
    return pl.pallas_call(
        _gla_proj_kernel,
        grid=(m // tm,),
        in_specs=[row(d), full(g), full(w), full(wg2), full(bg)],
        out_specs=[row(dk), row(dk), row(dv), row(dv), row(dk)],
        out_shape=[jax.ShapeDtypeStruct((m, n), F32) for n in (dk, dk, dv, dv, dk)],
        compiler_params=_params("parallel"),
        name="gla_proj",
    )(x, g, w, wg2, bg)


def _gla_scan_kernel(q_ref, k_ref, g_ref, v_ref, o_ref, s_ref, st_scr, *, n_chunks):
    c_len = GLA_CHUNK
    st_scr[...] = jnp.zeros_like(st_scr)
    rowi = lax.broadcasted_iota(jnp.int32, (c_len, GLA_DK), 0)
    causal = (lax.broadcasted_iota(jnp.int32, (c_len, c_len), 0)
              >= lax.broadcasted_iota(jnp.int32, (c_len, c_len), 1))

    def body(c, carry):
        sl = pl.ds(pl.multiple_of(c * c_len, c_len), c_len)
        b = g_ref[sl, :]
        sh = 1
        while sh < c_len:
            b = b + jnp.where(rowi >= sh, pltpu.roll(b, sh, 0), 0.0)
            sh *= 2
        b_last = b[c_len - 1:c_len, :]
        b_mid = b[c_len // 2 - 1:c_len // 2, :]
        q = q_ref[sl, :]
        k = k_ref[sl, :]
        v = v_ref[sl, :].astype(BF16)
        qe = (q * jnp.exp(b)).astype(BF16)
        qa = (q * jnp.exp(b - b_mid)).astype(BF16)
        ka = (k * jnp.exp(b_mid - b)).astype(BF16)
        kd = (k * jnp.exp(b_last - b)).astype(BF16)
        a = jnp.where(causal, _dot_nt(qa, ka), 0.0)
        st = st_scr[...]
        o_ref[sl, :] = _dot_nt(qe, st.astype(BF16)) + _dot(a.astype(BF16), v)
        st_scr[...] = st * jnp.exp(b_last) + _dot_tn(v, kd)
        return carry

    lax.fori_loop(0, n_chunks, body, 0)
    s_ref[...] = st_scr[...].T


def _gla_scan(q, k, gl, v):
    b, t, _ = q.shape
    kq = pl.BlockSpec((None, t, GLA_DK), lambda i, h: (i, 0, h))
    vv = pl.BlockSpec((None, t, GLA_DV), lambda i, h: (i, 0, h))
    return pl.pallas_call(
        functools.partial(_gla_scan_kernel, n_chunks=t // GLA_CHUNK),
        grid=(b, GLA_H),
        in_specs=[kq, kq, kq, vv],
        out_specs=[vv, pl.BlockSpec((None, None, GLA_DK, GLA_DV), lambda i, h: (i, h, 0, 0))],
        out_shape=[jax.ShapeDtypeStruct((b, t, GLA_H * GLA_DV), F32),
                   jax.ShapeDtypeStruct((b, GLA_H, GLA_DK, GLA_DV), F32)],
        scratch_shapes=[pltpu.VMEM((GLA_DV, GLA_DK), F32)],
        compiler_params=_params("parallel", "parallel"),
        name="gla_scan",
    )(q, k, gl, v)


def _gla_step_kernel(q_ref, k_ref, g_ref, v_ref, s0_ref, o_ref, s_ref):
    def col(x):
        return jnp.broadcast_to(x, (LANES, LANES)).T

    for h in range(GLA_H):
        ks = slice(h * GLA_DK, (h + 1) * GLA_DK)
        qc, kc, ec = col(q_ref[:, ks]), col(k_ref[:, ks]), col(jnp.exp(g_ref[:, ks]))
        for half in range(GLA_DV // LANES):
            vs = slice(h * GLA_DV + half * LANES, h * GLA_DV + (half + 1) * LANES)
            ss = slice(half * LANES, (half + 1) * LANES)
            sn = ec * s0_ref[h, :, ss] + kc * v_ref[:, vs]
            s_ref[h, :, ss] = sn
            o_ref[:, vs] = jnp.sum(qc * sn, axis=0, keepdims=True)


def _gla_step(q, k, gl, v, s0):
    b = q.shape[0]
    kq = pl.BlockSpec((None, 1, GLA_H * GLA_DK), lambda i: (i, 0, 0))
    vv = pl.BlockSpec((None, 1, GLA_H * GLA_DV), lambda i: (i, 0, 0))
    st = pl.BlockSpec((None, GLA_H, GLA_DK, GLA_DV), lambda i: (i, 0, 0, 0))
    return pl.pallas_call(
        _gla_step_kernel,
        grid=(b,),
        in_specs=[kq, kq, kq, vv, st],
        out_specs=[vv, st],
        out_shape=[jax.ShapeDtypeStruct((b, 1, GLA_H * GLA_DV), F32),
                   jax.ShapeDtypeStruct((b, GLA_H, GLA_DK, GLA_DV), F32)],
        compiler_params=_params("parallel"),
        name="gla_step",
    )(q, k, gl, v, s0)


def _gla_out_kernel(o_ref, r_ref, x_ref, go_ref, w_ref, y_ref):
    parts = []
    for h in range(GLA_H):
        sl = slice(h * GLA_DV, (h + 1) * GLA_DV)
        r = r_ref[:, sl]
        parts.append((_rms_rows(o_ref[:, sl], go_ref[...]) * (r * jax.nn.sigmoid(r))).astype(BF16))
    y_ref[...] = x_ref[...] + _dot(jnp.concatenate(parts, axis=1), w_ref[...])


def _gla_out(o, r, x, go, w, tm):
    m, d = x.shape
    row = pl.BlockSpec((tm, d), lambda i: (i, 0))
    full = lambda a: pl.BlockSpec(a.shape, lambda i: (0, 0))
    return pl.pallas_call(
        _gla_out_kernel,
        grid=(m // tm,),
        in_specs=[row, row, row, full(go), full(w)],
        out_specs=row,
        out_shape=jax.ShapeDtypeStruct((m, d), F32),
        compiler_params=_params("parallel"),
        name="gla_out",
    )(o, r, x, go, w)


def _gla_layer(xp, xs, s0, norm_g, w_in, w_gate2, b_gate, g_out, w_out):
    b, t, d = xp.shape
    bs = xs.shape[0]
    dk, dv = GLA_H * GLA_DK, GLA_H * GLA_DV
    rank = w_gate2.shape[0]
    w = jnp.pad(w_in, ((0, 0), (0, LANES - rank))).astype(BF16)
    wg2 = jnp.pad(w_gate2, ((0, LANES - rank), (0, 0))).astype(BF16)
    g = norm_g.reshape(1, d)
    bg = b_gate.reshape(1, dk)
    go = g_out.reshape(1, GLA_DV)
    wo = w_out.astype(BF16)
    x2 = xp.reshape(b * t, d)
    q, k, v, r, gl = _gla_proj(x2, g, w, wg2, bg, tm=512)
    o, sp = _gla_scan(q.reshape(b, t, dk), k.reshape(b, t, dk), gl.reshape(b, t, dk), v.reshape(b, t, dv))
    yp = _gla_out(o.reshape(b * t, dv), r, x2, go, wo, tm=512).reshape(b, t, d)
    xs2 = xs.reshape(bs, d)
    q, k, v, r, gl = _gla_proj(xs2, g, w, wg2, bg, tm=bs)
    o, ss = _gla_step(q.reshape(bs, 1, dk), k.reshape(bs, 1, dk), gl.reshape(bs, 1, dk), v.reshape(bs, 1, dv), s0)
    ys = _gla_out(o.reshape(bs, dv), r, xs2, go, wo, tm=bs).reshape(bs, 1, d)
    return yp, ys, sp, ss


def _nsa_proj_kernel(x_ref, gn_ref, wt_ref, gq_ref, gk_ref, bg_ref, cos_ref, sin_ref,
                     q_ref, kc_ref, vc_ref, ks_ref, vs_ref, kw_ref, vw_ref, gt_ref,
                     ksa_ref, vsa_ref, kwb_ref, vwa_ref):
    tm = x_ref.shape[0]
    nq = GRP * KVH * HD
    nkv = KVH * HD
    h = _rms_rows(x_ref[...], gn_ref[...]).astype(BF16)
    cos = cos_ref[...][None]
    sin = sin_ref[...][None]

    def norm_rope(z, g, nh):
        z3 = z.reshape(nh, HD, tm)
        y = z3 * lax.rsqrt(jnp.mean(z3 * z3, axis=1, keepdims=True) + EPS) * g[None]
        x1 = y[:, 0:ROT_HALF, :]
        x2 = y[:, ROT_HALF:2 * ROT_HALF, :]
        return jnp.concatenate([x1 * cos - x2 * sin, x1 * sin + x2 * cos, y[:, 2 * ROT_HALF:, :]], axis=1)

    zq = _dot_nt(wt_ref[0:nq, :], h)
    q_ref[...] = (norm_rope(zq, gq_ref[...], GRP * KVH) * (HD ** -0.5 * LOG2E)).astype(BF16)
    zkv = _dot_nt(wt_ref[nq:nq + 6 * nkv, :], h)
    outs = (kc_ref, vc_ref, ks_ref, vs_ref, kw_ref, vw_ref)
    rows = []
    for i in range(6):
        z = zkv[i * nkv:(i + 1) * nkv, :]
        rows.append(norm_rope(z, gk_ref[i // 2], KVH) if i % 2 == 0 else z.reshape(KVH, HD, tm))
        outs[i][...] = rows[i]
    zg = _dot_nt(wt_ref[nq + 6 * nkv:, :], h) + bg_ref[...]
    gt_ref[...] = jax.nn.sigmoid(zg).reshape(KVH, 16, tm)

    nsp = ksa_ref.shape[1] - HD
    blk = lax.broadcasted_iota(jnp.int32, (KVH, nsp, tm), 1)
    tok = pl.program_id(1) * tm + lax.broadcasted_iota(jnp.int32, (KVH, nsp, tm), 2)
    onehot = jnp.where(tok // L_SLC == blk, 1.0, 0.0)
    ksa_ref[...] = jnp.concatenate([rows[2], onehot], axis=1).astype(BF16)
    kwb_ref[...] = rows[4].astype(BF16)
    ones = jnp.where(lax.broadcasted_iota(jnp.int32, (KVH, V_PAD, tm), 1) == 0, 1.0, 0.0)
    vsa_ref[...] = jnp.concatenate([rows[3], ones], axis=1).astype(BF16)
    vwa_ref[...] = jnp.concatenate([rows[5], ones], axis=1).astype(BF16)


def _sel_rows(t):
    return -(-(-(-t // L_SLC)) // 16) * 16


def _nsa_proj(x, gn, wt, gq, gk, bg, cos, sin, tm):
    b, t, d = x.shape
    full = lambda a: pl.BlockSpec(a.shape, lambda i, j: (0,) * a.ndim)
    rows_spec = lambda n: pl.BlockSpec((None, KVH, n, tm), lambda i, j: (i, 0, 0, j))
    rows_shape = lambda n, dt: jax.ShapeDtypeStruct((b, KVH, n, t), dt)
    aug = [HD + _sel_rows(t), HD + V_PAD, HD, HD + V_PAD]
    return pl.pallas_call(
        _nsa_proj_kernel,
        grid=(b, t // tm),
        in_specs=[pl.BlockSpec((None, tm, d), lambda i, j: (i, j, 0)), full(gn), full(wt), full(gq), full(gk),
                  full(bg), pl.BlockSpec((ROT_HALF, tm), lambda i, j: (0, j)),
                  pl.BlockSpec((ROT_HALF, tm), lambda i, j: (0, j))],
        out_specs=[pl.BlockSpec((None, GRP * KVH, HD, tm), lambda i, j: (i, 0, 0, j))] + [rows_spec(HD)] * 6
        + [rows_spec(16)] + [rows_spec(n) for n in aug],
        out_shape=[jax.ShapeDtypeStruct((b, GRP * KVH, HD, t), BF16)] + [rows_shape(HD, F32)] * 6
        + [rows_shape(16, F32)] + [rows_shape(n, BF16) for n in aug],
        compiler_params=_params("parallel", "parallel"),
        name="nsa_proj",
    )(x, gn, wt, gq, gk, bg, cos, sin)


def _nsa_out_kernel(ot_ref, x_ref, w_ref, y_ref):
    ot = ot_ref[...].reshape(GRP * KVH * HD, ot_ref.shape[-1])
    y_ref[...] = x_ref[...] + _dot_tn(ot, w_ref[...])


def _nsa_out(ot, x, w, tm):
    b, t, d = x.shape
    return pl.pallas_call(
        _nsa_out_kernel,
        grid=(b, t // tm),
        in_specs=[pl.BlockSpec((None, GRP * KVH, HD, tm), lambda i, j: (i, 0, 0, j)),
                  pl.BlockSpec((None, tm, d), lambda i, j: (i, j, 0)),
                  pl.BlockSpec(w.shape, lambda i, j: (0, 0))],
        out_specs=pl.BlockSpec((None, tm, d), lambda i, j: (i, j, 0)),
        out_shape=jax.ShapeDtypeStruct((b, t, d), F32),
        compiler_params=_params("parallel", "parallel"),
        name="nsa_out",
    )(ot, x, w)


def _cmp_fs_kernel(*refs, n_pages, paged):
    refs = refs[1:] if paged else refs
    pages = refs[:n_pages]
    w_ref, o_ref, xt_scr, xr_scr, lhs_scr = refs[n_pages:]
    p = pl.program_id(1)
    half = CMP_STRIDE
    n_half = PAGE // half
    low = lax.broadcasted_iota(jnp.int32, (n_half, LANES), 1) < HD
    for pair in range(KVH // 2):
        for j in range(n_pages):
            idx = pair * n_pages + j
            xt = pages[j][2 * pair:2 * pair + 2].reshape(2 * HD, PAGE).T
            xt_scr[idx] = xt
            xr_scr[idx] = pltpu.roll(xt, HD, 1)
            rows = pl.ds(pl.multiple_of((p * n_pages + j) * n_half, n_half), n_half)
            for t in range(half // 2):
                ev = pl.ds(2 * t, n_half, stride=half)
                od = pl.ds(2 * t + 1, n_half, stride=half)
                lanes = slice(t * LANES, (t + 1) * LANES)
                lhs_scr[2 * pair, rows, lanes] = jnp.where(low, xt_scr[idx, ev, :], xr_scr[idx, od, :])
                lhs_scr[2 * pair + 1, rows, lanes] = jnp.where(low, xr_scr[idx, ev, :], xt_scr[idx, od, :])

    @pl.when(p == pl.num_programs(1) - 1)
    def _():
        n = w_ref.shape[1]
        for kvh in range(KVH):
            o_ref[:, kvh * n:(kvh + 1) * n] = _dot(lhs_scr[kvh].astype(BF16), w_ref[...])


def _cmp_fs(src, w, n_pages_total, pages_per_step, page_table=None, layer=0):
    paged = page_table is not None
    b = page_table.shape[0] if paged else src.shape[0]
    nh = n_pages_total * (PAGE // CMP_STRIDE)
    steps = n_pages_total // pages_per_step
    if paged:
        page_spec = lambda j: pl.BlockSpec(
            (None, None, KVH, HD, PAGE), lambda i, p, pt: (layer, pt[i, p * pages_per_step + j], 0, 0, 0))
        w_spec = pl.BlockSpec(w.shape, lambda i, p, pt: (0, 0))
        o_spec = pl.BlockSpec((None, nh, KVH * 512), lambda i, p, pt: (i, 0, 0))
    else:
        page_spec = lambda j: pl.BlockSpec((None, KVH, HD, PAGE), lambda i, p: (i, 0, 0, p * pages_per_step + j))
        w_spec = pl.BlockSpec(w.shape, lambda i, p: (0, 0))
        o_spec = pl.BlockSpec((None, nh, KVH * 512), lambda i, p: (i, 0, 0))
    grid_spec = pltpu.PrefetchScalarGridSpec(
        num_scalar_prefetch=1 if paged else 0,
        grid=(b, steps),
        in_specs=[page_spec(j) for j in range(pages_per_step)] + [w_spec],
        out_specs=o_spec,
        scratch_shapes=[pltpu.VMEM((2 * pages_per_step, PAGE, 2 * HD), F32),
                        pltpu.VMEM((2 * pages_per_step, PAGE, 2 * HD), F32),
                        pltpu.VMEM((KVH, nh, CMP_STRIDE * HD), F32)],
    )
    args = ([page_table] if paged else []) + [src] * pages_per_step + [w]
    return pl.pallas_call(
        functools.partial(_cmp_fs_kernel, n_pages=pages_per_step, paged=paged),
        grid_spec=grid_spec,
        out_shape=jax.ShapeDtypeStruct((b, nh, KVH * 512), F32),
        compiler_params=_params("parallel", "arbitrary"),
        name="cmp_fs",
    )(*args)


def _cmp_out_kernel(fs_ref, b1_ref, pe_ref, w1_ref, w2t_ref, gk_ref, o_ref, *, is_key):
    nh = fs_ref.shape[0]
    hidden = b1_ref.shape[1]
    c = _dot(pe_ref[0].astype(BF16), w1_ref[0]) + _dot(pe_ref[1].astype(BF16), w1_ref[1])
    bias = c[0:1, :] + b1_ref[...]
    for kvh in range(KVH):
        first = fs_ref[:, kvh * 2 * hidden:kvh * 2 * hidden + hidden]
        second = fs_ref[:, kvh * 2 * hidden + hidden:(kvh + 1) * 2 * hidden]
        hid = jax.nn.gelu(first + pltpu.roll(second, nh - 1, 0) + bias)
        yt = _dot_nt(w2t_ref[...], hid.astype(BF16))
        if is_key:
            yt = yt * lax.rsqrt(jnp.mean(yt * yt, axis=0, keepdims=True) + EPS) * gk_ref[...]
        o_ref[kvh] = yt.astype(BF16)


def _cmp_out(fs, b1, pe, w1, w2t, gk, is_key):
    b, nh, _ = fs.shape
    full = lambda a: pl.BlockSpec(a.shape, lambda i: (0,) * a.ndim)
    return pl.pallas_call(
        functools.partial(_cmp_out_kernel, is_key=is_key),
        grid=(b,),
        in_specs=[pl.BlockSpec((None, nh, fs.shape[2]), lambda i: (i, 0, 0)), full(b1), full(pe), full(w1),
                  full(w2t), full(gk)],
        out_specs=pl.BlockSpec((None, KVH, HD, nh), lambda i: (i, 0, 0, 0)),
        out_shape=jax.ShapeDtypeStruct((b, KVH, HD, nh), BF16),
        compiler_params=_params("parallel"),
        name="cmp_out",
    )(fs, b1, pe, w1, w2t, gk)


def _cmp_weights(pe, w1, b1, w2, g_kcmp, nh):
    hidden = w1.shape[-1]
    wfs = jnp.concatenate([w1[:CMP_STRIDE], w1[CMP_STRIDE:]], axis=-1)
    w_pair = wfs.reshape(CMP_STRIDE * HD, 2 * hidden).astype(BF16)
    pe2 = jnp.zeros((2, 8, CMP_STRIDE * HD), F32).at[:, 0, :].set(pe.reshape(2, CMP_STRIDE * HD))
    w1f = w1.reshape(2, CMP_STRIDE * HD, hidden).astype(BF16)
    gk = jnp.broadcast_to(g_kcmp.reshape(HD, 1), (HD, nh))
    return w_pair, b1.reshape(1, hidden), pe2, w1f, w2.T.astype(BF16), gk


def _softmax_update(s, mask, vt, m, l, acc):
    m_new = jnp.maximum(m, jnp.max(jnp.where(mask, s, NEG), axis=0, keepdims=True))
    alpha = jnp.exp2(m - m_new)
    p = jnp.where(mask, jnp.exp2(s - m_new), 0.0)
    l_new = l * alpha + jnp.sum(p, axis=0, keepdims=True)
    acc_new = acc * alpha + _dot(vt, p.astype(BF16))
    return m_new, l_new, acc_new, p


def _softmax_init(lanes):
    return jnp.full((1, lanes), NEG, F32), jnp.zeros((1, lanes), F32), jnp.zeros((HD, lanes), F32)


def _finish(l, acc):
    return acc / jnp.maximum(l, 1e-30)


def _split_dot(mt, x):
    hi = x.astype(BF16)
    lo = (x - hi.astype(F32)).astype(BF16)
    return _dot(mt, hi) + _dot(mt, lo)


def _attn_prompt_kernel(q_ref, kc_ref, vc_ref, ksa_ref, vsa_ref, kwb_ref, vwa_ref, gt_ref, mt_ref, o_ref,
                        s_scr, s2_scr, *, tq, n_cmp, n_sel):
    i = pl.program_id(2)
    lanes = GRP * tq
    ncp = kc_ref.shape[1]
    nsp = mt_ref.shape[0]
    qt = jnp.concatenate([q_ref[h] for h in range(GRP)], axis=1)

    def col_max(mx, s):
        return jnp.maximum(mx, jnp.max(s.reshape(s.shape[0] // 8, 8, lanes), axis=0))

    def finish(acc):
        return acc[0:HD] / jnp.maximum(acc[HD:HD + 1], 1e-30)

    tpos1 = i * tq + lax.broadcasted_iota(jnp.int32, (1, tq), 1)
    tpos = jnp.concatenate([tpos1] * GRP, axis=1)

    s = _dot_tn(kc_ref[...], qt)
    n_idx = lax.broadcasted_iota(jnp.int32, (ncp, lanes), 0)
    mask = (n_idx * CMP_STRIDE + (L_CMP - 1) <= tpos) & (n_idx < n_cmp)
    m, l, acc, p = _softmax_update(s, mask, vc_ref[...], *_softmax_init(lanes))
    o_c = _finish(l, acc)
    p = p / jnp.maximum(l, 1e-30)
    imp = p[:, 0:tq]
    for h in range(1, GRP):
        imp = imp + p[:, h * tq:(h + 1) * tq]
    p_slc = _split_dot(mt_ref[...], imp)

    j_idx = lax.broadcasted_iota(jnp.int32, (nsp, tq), 0)
    cur = tpos1 // L_SLC
    valid = j_idx <= cur
    forced = (j_idx == 0) | (j_idx == cur) | (j_idx == cur - 1)
    score = jnp.where(valid & forced, jnp.inf, jnp.where(valid, p_slc, -jnp.inf))
    bits = lax.bitcast_convert_type(score, jnp.int32)
    key = jnp.where(bits >= 0, bits, bits ^ 0x7FFFFFFF)
    n_grp = nsp // 8
    keys = [key[8 * r:8 * r + 8] for r in range(n_grp)]
    keys_m1 = [k - 1 for k in keys]
    sub = lax.broadcasted_iota(jnp.int32, (8, tq), 0)

    def count_group(grp, ranks):
        ranks = list(ranks)
        for u in range(8):
            row = jnp.broadcast_to(keys[grp][u:u + 1, :], (8, tq))
            for r in range(n_grp):
                thr = keys[r] if r < grp else keys_m1[r] if r > grp else jnp.where(sub > u, keys_m1[r], keys[r])
                ranks[r] = ranks[r] + jnp.where(row > thr, 1.0, 0.0)
        return tuple(ranks)

    ranks = tuple(jnp.zeros((8, tq), F32) for _ in range(n_grp))
    last_valid = (i * tq + tq - 1) // L_SLC
    for grp in range(n_grp):
        ranks = lax.cond(8 * grp <= last_valid, functools.partial(count_group, grp), lambda r: r, ranks)
    bias = jnp.where(jnp.concatenate(ranks, axis=0) < n_sel, 0.0, NEG)
    qa = jnp.concatenate([qt, jnp.concatenate([bias] * GRP, axis=1).astype(BF16)], axis=0)

    row_k = lax.broadcasted_iota(jnp.int32, (KT, lanes), 0)

    def slc_scores(st, buf):
        start = pl.multiple_of(st * KT, KT)
        buf[0:KT, :] = _dot_tn(ksa_ref[:, pl.ds(start, KT)], qa)

    def slc_softmax(st, buf, carry, diagonal=False):
        m, acc = carry
        s = buf[0:KT, :]
        if diagonal:
            s = jnp.where(st * KT + row_k <= tpos, s, NEG)
        m_new = jnp.maximum(m, jnp.max(col_max(jnp.full((8, lanes), NEG, F32), s), axis=0, keepdims=True))
        p = jnp.exp2(s - m_new).astype(BF16)
        pv = _dot(vsa_ref[:, pl.ds(pl.multiple_of(st * KT, KT), KT)], p)
        return m_new, acc * jnp.exp2(m - m_new) + pv

    def slc_pair(u, carry):
        slc_scores(2 * u + 1, s2_scr)
        carry = slc_softmax(2 * u, s_scr, carry)
        slc_scores(2 * u + 2, s_scr)
        return slc_softmax(2 * u + 1, s2_scr, carry)

    def slc_tail_odd(carry):
        slc_scores(n_full, s2_scr)
        carry = slc_softmax(n_full - 1, s_scr, carry)
        return slc_softmax(n_full, s2_scr, carry, diagonal=True)

    def slc_tail_even(carry):
        return slc_softmax(n_full, s_scr, carry, diagonal=True)

    n_full = (i * tq) // KT
    slc_scores(0, s_scr)
    init = (jnp.full((1, lanes), NEG, F32), jnp.zeros((HD + V_PAD, lanes), F32))
    carry = lax.fori_loop(0, n_full // 2, slc_pair, init)
    o_s = finish(lax.cond(n_full % 2 == 1, slc_tail_odd, slc_tail_even, carry)[1])

    n_wt = WINDOW // tq + 1
    w0 = jnp.maximum(i - (n_wt - 1), 0) * tq
    row_w = lax.broadcasted_iota(jnp.int32, (tq, lanes), 0)
    mx = jnp.full((8, lanes), NEG, F32)
    for r in range(n_wt):
        start = pl.multiple_of(w0 + r * tq, tq)
        s = _dot_tn(kwb_ref[:, pl.ds(start, tq)], qt)
        rel = tpos - (start + row_w)
        s = jnp.where((rel >= 0) & (rel <= WINDOW), s, NEG)
        s_scr[r * tq:(r + 1) * tq, :] = s
        mx = col_max(mx, s)
    m = jnp.max(mx, axis=0, keepdims=True)
    acc = jnp.zeros((HD + V_PAD, lanes), F32)
    for r in range(n_wt):
        start = pl.multiple_of(w0 + r * tq, tq)
        p = jnp.exp2(s_scr[r * tq:(r + 1) * tq, :] - m).astype(BF16)
        acc = acc + _dot(vwa_ref[:, pl.ds(start, tq)], p)
    o_w = finish(acc)

    for h in range(GRP):
        ls = slice(h * tq, (h + 1) * tq)
        o = (gt_ref[3 * h:3 * h + 1, :] * o_c[:, ls] + gt_ref[3 * h + 1:3 * h + 2, :] * o_s[:, ls]
             + gt_ref[3 * h + 2:3 * h + 3, :] * o_w[:, ls])
        o_ref[h] = o.astype(BF16)


def _slc_matrix(nsp, ncp, n_cmp):
    ratio = L_SLC // CMP_STRIDE
    j = jnp.arange(nsp)[:, None]
    n = jnp.arange(ncp)[None, :]
    m = ((n >= ratio * j) & (n <= ratio * j + ratio - 1)).astype(F32)
    m = m + ((n >= ratio * j - 1) & (n <= ratio * j + ratio - 2)).astype(F32)
    return jnp.where(n < n_cmp, m, 0.0).astype(BF16)


def _attn_prompt(qt, kct, vct, ksa, vsa, kwb, vwa, gt, tq=LANES):
    b, _, _, t = qt.shape
    assert t % KT == 0 and WINDOW % tq == 0 and t >= WINDOW + tq
    ncp = kct.shape[-1]
    n_cmp = t // CMP_STRIDE - 1
    ns = -(-t // L_SLC)
    mt = _slc_matrix(_sel_rows(t), ncp, n_cmp)
    res = lambda a: pl.BlockSpec((None, None) + a.shape[2:], lambda i, g, j: (i, g, 0, 0))
    return pl.pallas_call(
        functools.partial(_attn_prompt_kernel, tq=tq, n_cmp=n_cmp, n_sel=min(N_SEL, ns)),
        grid=(b, KVH, t // tq),
        in_specs=[pl.BlockSpec((None, GRP, HD, tq), lambda i, g, j: (i, g, 0, j)),
                  res(kct), res(vct), res(ksa), res(vsa), res(kwb), res(vwa),
                  pl.BlockSpec((None, None, 16, tq), lambda i, g, j: (i, g, 0, j)),
                  pl.BlockSpec(mt.shape, lambda i, g, j: (0, 0))],
        out_specs=pl.BlockSpec((None, GRP, HD, tq), lambda i, g, j: (i, g, 0, j)),
        out_shape=jax.ShapeDtypeStruct(qt.shape, BF16),
        scratch_shapes=[pltpu.VMEM((max(KT, WINDOW + tq), GRP * tq), F32), pltpu.VMEM((KT, GRP * tq), F32)],
        compiler_params=_params("parallel", "parallel", "arbitrary"),
        name="nsa_attn_prompt",
    )(qt, kct, vct, ksa, vsa, kwb, vwa, gt, mt)


def _attn_sample_kernel(*refs, n_pages, n_cmp, n_blocks, n_sel):
    refs = refs[1:]
    k_pages = refs[:n_pages]
    v_pages = refs[n_pages:2 * n_pages]
    (q_ref, kc_ref, vc_ref, kw_ref, vw_ref, kns_ref, vns_ref, knw_ref, vnw_ref, gt_ref, mt_ref,
     o_ref, sel_scr, oc_scr, ow_scr, m_scr, l_scr, acc_scr) = refs[2 * n_pages:]
    p = pl.program_id(1)
    ncp = kc_ref.shape[-1]
    nsp = mt_ref.shape[0]
    first_row = lax.broadcasted_iota(jnp.int32, (LANES, LANES), 0) == 0

    @pl.when(p == 0)
    def _():
        lane = lax.broadcasted_iota(jnp.int32, (ncp, LANES), 1)
        n_idx = lax.broadcasted_iota(jnp.int32, (ncp, LANES), 0)
        j_col = lax.broadcasted_iota(jnp.int32, (nsp, LANES), 0)
        jp_idx = lax.broadcasted_iota(jnp.int32, (nsp, nsp), 0)
        j_idx = lax.broadcasted_iota(jnp.int32, (nsp, nsp), 1)
        cur = n_blocks - 1
        for g in range(KVH):
            qt = q_ref[g]
            s = _dot_tn(kc_ref[g], qt)
            m, l, acc, pr = _softmax_update(s, n_idx < n_cmp, vc_ref[g], *_softmax_init(LANES))
            oc_scr[g] = _finish(l, acc)
            pr = pr / jnp.maximum(l, 1e-30)
            imp = jnp.sum(jnp.where(lane < GRP, pr, 0.0), axis=1, keepdims=True)
            p_slc = _split_dot(mt_ref[...], jnp.broadcast_to(imp, (ncp, LANES)))
            valid = j_col <= cur
            forced = (j_col == 0) | (j_col == cur) | (j_col == cur - 1)
            score = jnp.where(valid & forced, jnp.inf, jnp.where(valid, p_slc, -jnp.inf))
            col = jnp.concatenate([score] * (nsp // LANES), axis=1)
            row = col.T
            ahead = (col > row) | ((col == row) & (jp_idx < j_idx))
            rank = jnp.sum(jnp.where(ahead, 1.0, 0.0), axis=0, keepdims=True)
            sel = jnp.where(rank < n_sel, 1.0, 0.0)
            sel_scr[g] = jnp.broadcast_to(sel, (LANES, nsp)).T
            st = _softmax_init(LANES)
            s = _dot_tn(kw_ref[g].astype(BF16), qt)
            st = _softmax_update(s, jnp.full(s.shape, True), vw_ref[g].astype(BF16), *st)[:3]
            s = _dot_tn(knw_ref[g].astype(BF16), qt)
            _, l, acc = _softmax_update(s, first_row, vnw_ref[g].astype(BF16), *st)[:3]
            ow_scr[g] = _finish(l, acc)
            m0, l0, a0 = _softmax_init(LANES)
            m_scr[g], l_scr[g], acc_scr[g] = m0, l0, a0

    blocks_per_page = PAGE // L_SLC
    for g in range(KVH):
        qt = q_ref[g]
        kt = jnp.concatenate([k_pages[j][g] for j in range(n_pages)], axis=1).astype(BF16)
        vt = jnp.concatenate([v_pages[j][g] for j in range(n_pages)], axis=1).astype(BF16)
        s = _dot_tn(kt, qt)
        rows = []
        for r in range(n_pages * blocks_per_page):
            e = sel_scr[g, pl.ds(p * n_pages * blocks_per_page + r, 1), :]
            rows.append(jnp.broadcast_to(e, (L_SLC, LANES)))
        mask = jnp.concatenate(rows, axis=0) > 0.5
        m_scr[g], l_scr[g], acc_scr[g] = _softmax_update(s, mask, vt, m_scr[g], l_scr[g], acc_scr[g])[:3]

    @pl.when(p == pl.num_programs(1) - 1)
    def _():
        for g in range(KVH):
            s = _dot_tn(kns_ref[g].astype(BF16), q_ref[g])
            _, l, acc = _softmax_update(s, first_row, vns_ref[g].astype(BF16), m_scr[g], l_scr[g], acc_scr[g])[:3]
            o_s = _finish(l, acc)
            o_ref[g] = gt_ref[g, 0:1, :] * oc_scr[g] + gt_ref[g, 1:2, :] * o_s + gt_ref[g, 2:3, :] * ow_scr[g]


def _attn_sample(qs, kct, vct, pool_k, pool_v, page_table, layer, kwt, vwt, kns, vns, knw, vnw, gs,
                 pages_per_step=8):
    b, n_pages_total = page_table.shape
    past = n_pages_total * PAGE
    n_cmp = past // CMP_STRIDE - 1
    ncp = kct.shape[-1]
    n_blocks = past // L_SLC + 1
    nsp = -(-n_blocks // LANES) * LANES
    mt = _slc_matrix(nsp, ncp, n_cmp)
    steps = n_pages_total // pages_per_step
    page_spec = lambda j: pl.BlockSpec((None, None, KVH, HD, PAGE),
                                       lambda i, p, pt: (layer, pt[i, p * pages_per_step + j], 0, 0, 0))
    per_b = lambda a: pl.BlockSpec((None,) + a.shape[1:], lambda i, p, pt: (i,) + (0,) * (a.ndim - 1))
    small = [qs, kct, vct, kwt, vwt, kns, vns, knw, vnw, gs]
    grid_spec = pltpu.PrefetchScalarGridSpec(
        num_scalar_prefetch=1,
        grid=(b, steps),
        in_specs=[page_spec(j) for j in range(pages_per_step)] * 2 + [per_b(a) for a in small]
        + [pl.BlockSpec(mt.shape, lambda i, p, pt: (0, 0))],
        out_specs=pl.BlockSpec((None, KVH, HD, LANES), lambda i, p, pt: (i, 0, 0, 0)),
        scratch_shapes=[pltpu.VMEM((KVH, nsp, LANES), F32), pltpu.VMEM((KVH, HD, LANES), F32),
                        pltpu.VMEM((KVH, HD, LANES), F32), pltpu.VMEM((KVH, 1, LANES), F32),
                        pltpu.VMEM((KVH, 1, LANES), F32), pltpu.VMEM((KVH, HD, LANES), F32)],
    )
    return pl.pallas_call(
        functools.partial(_attn_sample_kernel, n_pages=pages_per_step, n_cmp=n_cmp, n_blocks=n_blocks,
                          n_sel=min(N_SEL, n_blocks)),
        grid_spec=grid_spec,
        out_shape=jax.ShapeDtypeStruct((b, KVH, HD, LANES), F32),
        compiler_params=_params("parallel", "arbitrary"),
        name="nsa_attn_sample",
    )(page_table, *([pool_k] * pages_per_step), *([pool_v] * pages_per_step), *small, mt)


def _rope_tables(pos):
    inv_freq = ROPE_THETA ** (-jnp.arange(ROT_HALF, dtype=F32) * 2.0 / (2 * ROT_HALF))
    ang = pos.astype(F32)[:, None] * inv_freq[None, :]
    return jnp.cos(ang).T, jnp.sin(ang).T


def _nsa_layer(xp, xs, pools, win_bufs, page_table, layer, norm_g, w_in, b_gate, g_q, g_k, g_kcmp,
               cmp_pe, cmp_w1, cmp_b1, cmp_w2, w_out):
    b, t, d = xp.shape
    bs = xs.shape[0]
    past = page_table.shape[1] * PAGE
    nq, nkv = GRP * KVH * HD, KVH * HD
    n_gate = 3 * GRP * KVH

    wt_g = jnp.pad(w_in[:, nq + 6 * nkv:].T.reshape(KVH, n_gate // KVH, d), ((0, 0), (0, 16 - n_gate // KVH), (0, 0)))
    wt = jnp.concatenate([w_in[:, :nq + 6 * nkv].T, wt_g.reshape(KVH * 16, d)], axis=0).astype(BF16)
    bg = jnp.pad(b_gate.reshape(KVH, n_gate // KVH), ((0, 0), (0, 16 - n_gate // KVH))).reshape(KVH * 16, 1)
    gn = norm_g.reshape(1, d)
    wo = w_out.astype(BF16)

    def project(x, pos, tm):
        cos, sin = _rope_tables(pos)
        col = lambda v: jnp.broadcast_to(v[..., None], v.shape + (tm,))
        return _nsa_proj(x, gn, wt, col(g_q), col(g_k), col(bg[:, 0]), cos, sin, tm)

    tm = min(512, t)
    qt, kct_rows, vct_rows, kst, vst, kwt, vwt, gt, ksa, vsa, kwb, vwa = project(xp, jnp.arange(t), tm)
    pages = t // PAGE
    nh = pages * (PAGE // CMP_STRIDE)
    cmp_w = [_cmp_weights(cmp_pe[i], cmp_w1[i], cmp_b1[i], cmp_w2[i], g_kcmp, nh) for i in range(2)]
    pps = min(16, pages)
    kct = _cmp_out(_cmp_fs(kct_rows, cmp_w[0][0], pages, pps), *cmp_w[0][1:], is_key=True)
    vct = _cmp_out(_cmp_fs(vct_rows, cmp_w[1][0], pages, pps), *cmp_w[1][1:], is_key=False)
    ot = _attn_prompt(qt, kct, vct, ksa, vsa, kwb, vwa, gt)
    yp = _nsa_out(ot, xp, wo, tm)
    n_win = min(WINDOW, t)
    rows_p = [kct_rows, vct_rows, kst, vst, kwt[..., t - n_win:], vwt[..., t - n_win:]]

    xs_pad = jnp.pad(xs.reshape(1, bs, d), ((0, 0), (0, LANES - bs), (0, 0)))
    outs = project(xs_pad, jnp.full((LANES,), past), LANES)
    qt_s, rows_s, gt_s = outs[0], outs[1:7], outs[7]
    pages_s = page_table.shape[1]
    nh_s = pages_s * (PAGE // CMP_STRIDE)
    cmp_ws = [_cmp_weights(cmp_pe[i], cmp_w1[i], cmp_b1[i], cmp_w2[i], g_kcmp, nh_s) for i in range(2)]
    pool5 = [jnp.transpose(pl_, (0, 1, 3, 4, 2)) for pl_ in pools]
    pps = min(16, pages_s)
    kct_s = _cmp_out(_cmp_fs(pool5[0], cmp_ws[0][0], pages_s, pps, page_table, layer), *cmp_ws[0][1:], is_key=True)
    vct_s = _cmp_out(_cmp_fs(pool5[1], cmp_ws[1][0], pages_s, pps, page_table, layer), *cmp_ws[1][1:], is_key=False)
    qs = jnp.transpose(qt_s[0, :, :, :bs].reshape(KVH, GRP, HD, bs), (3, 0, 2, 1))
    qs = jnp.pad(qs, ((0, 0), (0, 0), (0, 0), (0, LANES - GRP)))
    new = lambda a: jnp.pad(jnp.transpose(a[0, :, :, :bs], (2, 0, 1))[..., None],
                            ((0, 0), (0, 0), (0, 0), (0, LANES - 1)))
    gs = jnp.transpose(gt_s[0, :, :n_gate // KVH, :bs].reshape(KVH, GRP, 3, bs), (3, 0, 2, 1))
    gs = jnp.pad(gs, ((0, 0), (0, 0), (0, 8 - 3), (0, LANES - GRP)))
    win5 = [jnp.transpose(wb[layer], (0, 2, 3, 1)) for wb in win_bufs]
    ot_s = _attn_sample(qs, kct_s, vct_s, pool5[2], pool5[3], page_table, layer, win5[0], win5[1],
                        new(rows_s[2]), new(rows_s[3]), new(rows_s[4]), new(rows_s[5]), gs)
    ot_s = jnp.transpose(ot_s[..., :GRP], (1, 3, 2, 0)).reshape(1, GRP * KVH, HD, bs)
    ot_s = jnp.pad(ot_s, ((0, 0), (0, 0), (0, 0), (0, LANES - bs))).astype(BF16)
    ys = _nsa_out(ot_s, xs_pad, wo, LANES)[0, :bs].reshape(bs, 1, d)

    to_rows = lambda a: jnp.transpose(a, (0, 3, 1, 2))
    rows_p = [to_rows(a) for a in rows_p]
    rows_s = [to_rows(a[:, :, :, :bs]).reshape(bs, 1, KVH, HD) for a in rows_s]
    return yp, ys, rows_p, rows_s


def kernel(x_prompt, x_sample, state_gla, cache_k_cmp, cache_v_cmp, cache_k_slc, cache_v_slc, cache_k_win, cache_v_win, page_table, norm_mix, norm_mlp, mlp_up, mlp_down, gla_w_in, gla_w_gate2, gla_b_gate, gla_g_out, gla_w_out, nsa_w_in, nsa_b_gate, nsa_g_q, nsa_g_k, nsa_g_kcmp, nsa_cmp_pe, nsa_cmp_w1, nsa_cmp_b1, nsa_cmp_w2, nsa_w_out):
    depth = norm_mix.shape[0]
    b, t, d = x_prompt.shape
    bs = x_sample.shape[0]
    xp, xs = x_prompt, x_sample
    gla_p, gla_s, nsa_p, nsa_s = [], [], [], []
    pools = (cache_k_cmp, cache_v_cmp, cache_k_slc, cache_v_slc)
    for i in range(depth):
        j = i // 2
        if i % 2 == 0:
            xp, xs, sp, ss = _gla_layer(xp, xs, state_gla[j], norm_mix[i], gla_w_in[j], gla_w_gate2[j],
                                        gla_b_gate[j], gla_g_out[j], gla_w_out[j])
            gla_p.append(sp)
            gla_s.append(ss)
        else:
            xp, xs, rp, rs = _nsa_layer(xp, xs, pools, (cache_k_win, cache_v_win), page_table, j, norm_mix[i],
                                        nsa_w_in[j], nsa_b_gate[j], nsa_g_q[j], nsa_g_k[j], nsa_g_kcmp[j],
                                        nsa_cmp_pe[j], nsa_cmp_w1[j], nsa_cmp_b1[j], nsa_cmp_w2[j], nsa_w_out[j])
            nsa_p.append(rp)
            nsa_s.append(rs)
        g = norm_mlp[i].reshape(1, d)
        wu, wd = mlp_up[i].astype(BF16), mlp_down[i].astype(BF16)
        xp = _mlp(xp.reshape(b * t, d), g, wu, wd, tm=512).reshape(b, t, d)
        xs = _mlp(xs.reshape(bs, d), g, wu, wd, tm=bs).reshape(bs, 1, d)
    stack = lambda lst, r: jnp.stack([e[r] for e in lst])
    return (xp, xs, jnp.stack(gla_p), jnp.stack(gla_s),
            stack(nsa_p, 0), stack(nsa_s, 0), stack(nsa_p, 1), stack(nsa_s, 1),
            stack(nsa_p, 2), stack(nsa_s, 2), stack(nsa_p, 3), stack(nsa_s, 3),
            stack(nsa_p, 4), stack(nsa_s, 4), stack(nsa_p, 5), stack(nsa_s, 5))
```

```python
import functools

import jax
import jax.numpy as jnp
from jax import lax
from jax.experimental import pallas as pl
from jax.experimental.pallas import tpu as pltpu

F32 = jnp.float32
BF16 = jnp.bfloat16
EPS = 1e-6
NEG = -1e30
VMEM_LIMIT_BYTES = 48 * 1024 * 1024
LANES = 128
PAGE = 128
HD = 64
KVH = 4
GRP = 4
L_CMP, CMP_STRIDE, L_SLC, N_SEL, WINDOW = 32, 16, 64, 16, 512
GLA_H, GLA_DK, GLA_DV, GLA_CHUNK = 4, 128, 256, 64
ROPE_THETA, ROT_HALF = 500000.0, 8
LOG2E = 1.4426950408889634
V_PAD = 16
KT = 512


def _params(*sem):
    return pltpu.CompilerParams(dimension_semantics=sem, vmem_limit_bytes=VMEM_LIMIT_BYTES)


def _dot(a, b):
    return jnp.dot(a, b, preferred_element_type=F32)


def _dot_nt(a, b):
    return lax.dot_general(a, b, (((1,), (1,)), ((), ())), preferred_element_type=F32)


def _dot_tn(a, b):
    return lax.dot_general(a, b, (((0,), (0,)), ((), ())), preferred_element_type=F32)


def _rms_rows(x, g):
    ms = jnp.mean(x * x, axis=-1, keepdims=True)
    return x * lax.rsqrt(ms + EPS) * g


def _mlp_kernel(x_ref, g_ref, wu_ref, wd_ref, o_ref, h_scr, acc_scr):
    f = pl.program_id(1)

    @pl.when(f == 0)
    def _():
        h_scr[...] = _rms_rows(x_ref[...], g_ref[...]).astype(BF16)
        acc_scr[...] = jnp.zeros_like(acc_scr)

    u = jnp.maximum(_dot(h_scr[...], wu_ref[...]), 0.0)
    acc_scr[...] += _dot((u * u).astype(BF16), wd_ref[...])

    @pl.when(f == pl.num_programs(1) - 1)
    def _():
        o_ref[...] = x_ref[...] + acc_scr[...]


def _mlp(x, g, wu, wd, tm, tf=1024):
    m, d = x.shape
    ff = wu.shape[1]
    return pl.pallas_call(
        _mlp_kernel,
        grid=(m // tm, ff // tf),
        in_specs=[pl.BlockSpec((tm, d), lambda i, f: (i, 0)),
                  pl.BlockSpec((1, d), lambda i, f: (0, 0)),
                  pl.BlockSpec((d, tf), lambda i, f: (0, f)),
                  pl.BlockSpec((tf, d), lambda i, f: (f, 0))],
        out_specs=pl.BlockSpec((tm, d), lambda i, f: (i, 0)),
        out_shape=jax.ShapeDtypeStruct((m, d), F32),
        scratch_shapes=[pltpu.VMEM((tm, d), BF16), pltpu.VMEM((tm, d), F32)],
        compiler_params=_params("parallel", "arbitrary"),
        name="mlp_block",
    )(x, g, wu, wd)


def _gla_proj_kernel(x_ref, g_ref, w_ref, wg2_ref, bg_ref, q_ref, k_ref, v_ref, r_ref, gl_ref):
    h = _rms_rows(x_ref[...], g_ref[...]).astype(BF16)
    dk, dv = GLA_H * GLA_DK, GLA_H * GLA_DV
    q_ref[...] = _dot(h, w_ref[:, 0:dk]) * (GLA_DK ** -0.5)
    k_ref[...] = _dot(h, w_ref[:, dk:2 * dk])
    v_ref[...] = _dot(h, w_ref[:, 2 * dk:2 * dk + dv])
    r_ref[...] = _dot(h, w_ref[:, 2 * dk + dv:2 * dk + 2 * dv])
    gr = _dot(h, w_ref[:, 2 * dk + 2 * dv:])
    xg = _dot(gr.astype(BF16), wg2_ref[...]) + bg_ref[...]
    gl_ref[...] = jax.nn.log_sigmoid(xg) * (1.0 / 16.0)


def _gla_proj(x, g, w, wg2, bg, tm):
    m, d = x.shape
    dk, dv = GLA_H * GLA_DK, GLA_H * GLA_DV
    row = lambda n: pl.BlockSpec((tm, n), lambda i: (i, 0))
    full = lambda a: pl.BlockSpec(a.shape, lambda i: (0, 0))
    return pl.pallas_call(
        _gla_proj_kernel,
        grid=(m // tm,),
        in_specs=[row(d), full(g), full(w), full(wg2), full(bg)],
        out_specs=[row(dk), row(dk), row(dv), row(dv), row(dk)],
        out_shape=[jax.ShapeDtypeStruct((m, n), F32) for n in (dk, dk, dv, dv, dk)],
        compiler_params=_params("parallel"),
        name="gla_proj",
    )(x, g, w, wg2, bg)


def _gla_scan_kernel(q_ref, k_ref, g_ref, v_ref, o_ref, s_ref, st_scr, *, n_chunks, n_heads):
    c_len = GLA_CHUNK
    step = pl.program_id(2)

    @pl.when(step == 0)
    def _():
        st_scr[...] = jnp.zeros_like(st_scr)

    rowi = lax.broadcasted_iota(jnp.int32, (c_len, GLA_DK), 0)
    causal = (lax.broadcasted_iota(jnp.int32, (c_len, c_len), 0)
              >= lax.broadcasted_iota(jnp.int32, (c_len, c_len), 1))

    def body(c, carry):
        sl = pl.ds(pl.multiple_of(c * c_len, c_len), c_len)
        for h in range(n_heads):
            ks = slice(h * GLA_DK, (h + 1) * GLA_DK)
            vs = slice(h * GLA_DV, (h + 1) * GLA_DV)
            b = g_ref[sl, ks]
            sh = 1
            while sh < c_len:
                b = b + jnp.where(rowi >= sh, pltpu.roll(b, sh, 0), 0.0)
                sh *= 2
            b_last = b[c_len - 1:c_len, :]
            b_mid = b[c_len // 2 - 1:c_len // 2, :]
            q = q_ref[sl, ks]
            k = k_ref[sl, ks]
            v = v_ref[sl, vs].astype(BF16)
            qe = (q * jnp.exp(b)).astype(BF16)
            qa = (q * jnp.exp(b - b_mid)).astype(BF16)
            ka = (k * jnp.exp(b_mid - b)).astype(BF16)
            kd = (k * jnp.exp(b_last - b)).astype(BF16)
            a = jnp.where(causal, _dot_nt(qa, ka), 0.0)
            st = st_scr[h]
            o_ref[sl, vs] = _dot_nt(qe, st.astype(BF16)) + _dot(a.astype(BF16), v)
            st_scr[h] = st * jnp.exp(b_last) + _dot_tn(v, kd)
        return carry

    lax.fori_loop(0, n_chunks, body, 0)

    @pl.when(step == pl.num_programs(2) - 1)
    def _():
        for h in range(n_heads):
            s_ref[h] = st_scr[h].T


def _gla_scan(q, k, gl, v, n_heads=2, t_blk=1024):
    b, t, _ = q.shape
    t_blk = min(t_blk, t)
    kq = pl.BlockSpec((None, t_blk, n_heads * GLA_DK), lambda i, h, j: (i, j, h))
    vv = pl.BlockSpec((None, t_blk, n_heads * GLA_DV), lambda i, h, j: (i, j, h))
    return pl.pallas_call(
        functools.partial(_gla_scan_kernel, n_chunks=t_blk // GLA_CHUNK, n_heads=n_heads),
        grid=(b, GLA_H // n_heads, t // t_blk),
        in_specs=[kq, kq, kq, vv],
        out_specs=[vv, pl.BlockSpec((None, n_heads, GLA_DK, GLA_DV), lambda i, h, j: (i, h, 0, 0))],
        out_shape=[jax.ShapeDtypeStruct((b, t, GLA_H * GLA_DV), F32),
                   jax.ShapeDtypeStruct((b, GLA_H, GLA_DK, GLA_DV), F32)],
        scratch_shapes=[pltpu.VMEM((n_heads, GLA_DV, GLA_DK), F32)],
        compiler_params=_params("parallel", "parallel", "arbitrary"),
        name="gla_scan",
    )(q, k, gl, v)


def _gla_step_kernel(q_ref, k_ref, g_ref, v_ref, s0_ref, o_ref, s_ref):
    def col(x):
        return jnp.broadcast_to(x, (LANES, LANES)).T

    for h in range(GLA_H):
        ks = slice(h * GLA_DK, (h + 1) * GLA_DK)
        qc, kc, ec = col(q_ref[:, ks]), col(k_ref[:, ks]), col(jnp.exp(g_ref[:, ks]))
        for half in range(GLA_DV // LANES):
            vs = slice(h * GLA_DV + half * LANES, h * GLA_DV + (half + 1) * LANES)
            ss = slice(half * LANES, (half + 1) * LANES)
            sn = ec * s0_ref[h, :, ss] + kc * v_ref[:, vs]
            s_ref[h, :, ss] = sn
            o_ref[:, vs] = jnp.sum(qc * sn, axis=0, keepdims=True)


def _gla_step(q, k, gl, v, s0):
    b = q.shape[0]
    kq = pl.BlockSpec((None, 1, GLA_H * GLA_DK), lambda i: (i, 0, 0))
    vv = pl.BlockSpec((None, 1, GLA_H * GLA_DV), lambda i: (i, 0, 0))
    st = pl.BlockSpec((None, GLA_H, GLA_DK, GLA_DV), lambda i: (i, 0, 0, 0))
    return pl.pallas_call(
        _gla_step_kernel,
        grid=(b,),
        in_specs=[kq, kq, kq, vv, st],
        out_specs=[vv, st],
        out_shape=[jax.ShapeDtypeStruct((b, 1, GLA_H * GLA_DV), F32),
                   jax.ShapeDtypeStruct((b, GLA_H, GLA_DK, GLA_DV), F32)],
        compiler_params=_params("parallel"),
        name="gla_step",
    )(q, k, gl, v, s0)


def _gla_out_kernel(o_ref, r_ref, x_ref, go_ref, w_ref, y_ref):
    parts = []
    for h in range(GLA_H):
        sl = slice(h * GLA_DV, (h + 1) * GLA_DV)
        r = r_ref[:, sl]
        parts.append((_rms_rows(o_ref[:, sl], go_ref[...]) * (r * jax.nn.sigmoid(r))).astype(BF16))
    y_ref[...] = x_ref[...] + _dot(jnp.concatenate(parts, axis=1), w_ref[...])


def _gla_out(o, r, x, go, w, tm):
    m, d = x.shape
    row = pl.BlockSpec((tm, d), lambda i: (i, 0))
    full = lambda a: pl.BlockSpec(a.shape, lambda i: (0, 0))
    return pl.pallas_call(
        _gla_out_kernel,
        grid=(m // tm,),
        in_specs=[row, row, row, full(go), full(w)],
        out_specs=row,
        out_shape=jax.ShapeDtypeStruct((m, d), F32),
        compiler_params=_params("parallel"),
        name="gla_out",
    )(o, r, x, go, w)


def _gla_layer(xp, xs, s0, norm_g, w_in, w_gate2, b_gate, g_out, w_out):
    b, t, d = xp.shape
    bs = xs.shape[0]
    dk, dv = GLA_H * GLA_DK, GLA_H * GLA_DV
    rank = w_gate2.shape[0]
    w = jnp.pad(w_in, ((0, 0), (0, LANES - rank))).astype(BF16)
    wg2 = jnp.pad(w_gate2, ((0, LANES - rank), (0, 0))).astype(BF16)
    g = norm_g.reshape(1, d)
    bg = b_gate.reshape(1, dk)
    go = g_out.reshape(1, GLA_DV)
    wo = w_out.astype(BF16)
    x2 = xp.reshape(b * t, d)
    q, k, v, r, gl = _gla_proj(x2, g, w, wg2, bg, tm=512)
    o, sp = _gla_scan(q.reshape(b, t, dk), k.reshape(b, t, dk), gl.reshape(b, t, dk), v.reshape(b, t, dv))
    yp = _gla_out(o.reshape(b * t, dv), r, x2, go, wo, tm=512).reshape(b, t, d)
    xs2 = xs.reshape(bs, d)
    q, k, v, r, gl = _gla_proj(xs2, g, w, wg2, bg, tm=bs)
    o, ss = _gla_step(q.reshape(bs, 1, dk), k.reshape(bs, 1, dk), gl.reshape(bs, 1, dk), v.reshape(bs, 1, dv), s0)
    ys = _gla_out(o.reshape(bs, dv), r, xs2, go, wo, tm=bs).reshape(bs, 1, d)
    return yp, ys, sp, ss


def _nsa_proj_kernel(x_ref, gn_ref, wt_ref, gq_ref, gk_ref, bg_ref, cos_ref, sin_ref,
                     q_ref, kc_ref, vc_ref, ks_ref, vs_ref, kw_ref, vw_ref, gt_ref,
                     ksa_ref, vsa_ref, kwb_ref, vwa_ref):
    tm = x_ref.shape[0]
    nq = GRP * KVH * HD
    nkv = KVH * HD
    h = _rms_rows(x_ref[...], gn_ref[...]).astype(BF16)
    cos = cos_ref[...][None]
    sin = sin_ref[...][None]

    def norm_rope(z, g, nh):
        z3 = z.reshape(nh, HD, tm)
        y = z3 * lax.rsqrt(jnp.mean(z3 * z3, axis=1, keepdims=True) + EPS) * g[None]
        x1 = y[:, 0:ROT_HALF, :]
        x2 = y[:, ROT_HALF:2 * ROT_HALF, :]
        return jnp.concatenate([x1 * cos - x2 * sin, x1 * sin + x2 * cos, y[:, 2 * ROT_HALF:, :]], axis=1)

    zq = _dot_nt(wt_ref[0:nq, :], h)
    q_ref[...] = (norm_rope(zq, gq_ref[...], GRP * KVH) * (HD ** -0.5 * LOG2E)).astype(BF16)
    zkv = _dot_nt(wt_ref[nq:nq + 6 * nkv, :], h)
    outs = (kc_ref, vc_ref, ks_ref, vs_ref, kw_ref, vw_ref)
    rows = []
    for i in range(6):
        z = zkv[i * nkv:(i + 1) * nkv, :]
        rows.append(norm_rope(z, gk_ref[i // 2], KVH) if i % 2 == 0 else z.reshape(KVH, HD, tm))
        outs[i][...] = rows[i]
    zg = _dot_nt(wt_ref[nq + 6 * nkv:, :], h) + bg_ref[...]
    gt_ref[...] = jax.nn.sigmoid(zg).reshape(KVH, 16, tm)

    nsp = ksa_ref.shape[1] - HD
    blk = lax.broadcasted_iota(jnp.int32, (KVH, nsp, tm), 1)
    tok = pl.program_id(1) * tm + lax.broadcasted_iota(jnp.int32, (KVH, nsp, tm), 2)
    onehot = jnp.where(tok // L_SLC == blk, 1.0, 0.0)
    ksa_ref[...] = jnp.concatenate([rows[2], onehot], axis=1).astype(BF16)
    kwb_ref[...] = rows[4].astype(BF16)
    ones = jnp.where(lax.broadcasted_iota(jnp.int32, (KVH, V_PAD, tm), 1) == 0, 1.0, 0.0)
    vsa_ref[...] = jnp.concatenate([rows[3], ones], axis=1).astype(BF16)
    vwa_ref[...] = jnp.concatenate([rows[5], ones], axis=1).astype(BF16)


def _sel_rows(t):
    return -(-(-(-t // L_SLC)) // 16) * 16


def _nsa_proj(x, gn, wt, gq, gk, bg, cos, sin, tm):
    b, t, d = x.shape
    full = lambda a: pl.BlockSpec(a.shape, lambda i, j: (0,) * a.ndim)
    rows_spec = lambda n: pl.BlockSpec((None, KVH, n, tm), lambda i, j: (i, 0, 0, j))
    rows_shape = lambda n, dt: jax.ShapeDtypeStruct((b, KVH, n, t), dt)
    aug = [HD + _sel_rows(t), HD + V_PAD, HD, HD + V_PAD]
    return pl.pallas_call(
        _nsa_proj_kernel,
        grid=(b, t // tm),
        in_specs=[pl.BlockSpec((None, tm, d), lambda i, j: (i, j, 0)), full(gn), full(wt), full(gq), full(gk),
                  full(bg), pl.BlockSpec((ROT_HALF, tm), lambda i, j: (0, j)),
                  pl.BlockSpec((ROT_HALF, tm), lambda i, j: (0, j))],
        out_specs=[pl.BlockSpec((None, GRP * KVH, HD, tm), lambda i, j: (i, 0, 0, j))] + [rows_spec(HD)] * 6
        + [rows_spec(16)] + [rows_spec(n) for n in aug],
        out_shape=[jax.ShapeDtypeStruct((b, GRP * KVH, HD, t), BF16)] + [rows_shape(HD, F32)] * 6
        + [rows_shape(16, F32)] + [rows_shape(n, BF16) for n in aug],
        compiler_params=_params("parallel", "parallel"),
        name="nsa_proj",
    )(x, gn, wt, gq, gk, bg, cos, sin)


def _nsa_out_kernel(ot_ref, x_ref, w_ref, y_ref):
    ot = ot_ref[...].reshape(GRP * KVH * HD, ot_ref.shape[-1])
    y_ref[...] = x_ref[...] + _dot_tn(ot, w_ref[...])


def _nsa_out(ot, x, w, tm):
    b, t, d = x.shape
    return pl.pallas_call(
        _nsa_out_kernel,
        grid=(b, t // tm),
        in_specs=[pl.BlockSpec((None, GRP * KVH, HD, tm), lambda i, j: (i, 0, 0, j)),
                  pl.BlockSpec((None, tm, d), lambda i, j: (i, j, 0)),
                  pl.BlockSpec(w.shape, lambda i, j: (0, 0))],
        out_specs=pl.BlockSpec((None, tm, d), lambda i, j: (i, j, 0)),
        out_shape=jax.ShapeDtypeStruct((b, t, d), F32),
        compiler_params=_params("parallel", "parallel"),
        name="nsa_out",
    )(ot, x, w)


def _cmp_fs_kernel(*refs, n_pages, paged):
    refs = refs[1:] if paged else refs
    pages = refs[:n_pages]
    perm_ref, w_ref, o_ref, lhs_scr = refs[n_pages:]
    p = pl.program_id(1)
    half = CMP_STRIDE
    n_half = PAGE // half
    low = lax.broadcasted_iota(jnp.int32, (n_half, LANES), 1) < HD
    for pair in range(KVH // 2):
        for j in range(n_pages):
            a, b = pages[j][2 * pair].astype(BF16), pages[j][2 * pair + 1].astype(BF16)
            xa = _dot_nt(perm_ref[...], jnp.concatenate([a, b], axis=0))
            xb = _dot_nt(perm_ref[...], jnp.concatenate([b, a], axis=0))
            rows = pl.ds(pl.multiple_of((p * n_pages + j) * n_half, n_half), n_half)
            for t in range(half // 2):
                ev = slice(2 * t * n_half, (2 * t + 1) * n_half)
                od = slice((2 * t + 1) * n_half, (2 * t + 2) * n_half)
                lanes = slice(t * LANES, (t + 1) * LANES)
                lhs_scr[2 * pair, rows, lanes] = jnp.where(low, xa[ev], xb[od])
                lhs_scr[2 * pair + 1, rows, lanes] = jnp.where(low, xb[ev], xa[od])

    @pl.when(p == pl.num_programs(1) - 1)
    def _():
        n = w_ref.shape[1]
        for kvh in range(KVH):
            o_ref[:, kvh * n:(kvh + 1) * n] = _dot(lhs_scr[kvh].astype(BF16), w_ref[...])


def _cmp_fs(src, w, n_pages_total, pages_per_step, page_table=None, layer=0):
    paged = page_table is not None
    b = page_table.shape[0] if paged else src.shape[0]
    n_half = PAGE // CMP_STRIDE
    nh = n_pages_total * n_half
    steps = n_pages_total // pages_per_step
    tok = jnp.arange(PAGE)
    perm = (jnp.arange(PAGE)[:, None] == ((tok % CMP_STRIDE) * n_half + tok // CMP_STRIDE)[None, :]).astype(BF16)
    if paged:
        page_spec = lambda j: pl.BlockSpec(
            (None, None, KVH, HD, PAGE), lambda i, p, pt: (layer, pt[i, p * pages_per_step + j], 0, 0, 0))
        full = lambda a: pl.BlockSpec(a.shape, lambda i, p, pt: (0, 0))
        o_spec = pl.BlockSpec((None, nh, KVH * 512), lambda i, p, pt: (i, 0, 0))
    else:
        page_spec = lambda j: pl.BlockSpec((None, KVH, HD, PAGE), lambda i, p: (i, 0, 0, p * pages_per_step + j))
        full = lambda a: pl.BlockSpec(a.shape, lambda i, p: (0, 0))
        o_spec = pl.BlockSpec((None, nh, KVH * 512), lambda i, p: (i, 0, 0))
    grid_spec = pltpu.PrefetchScalarGridSpec(
        num_scalar_prefetch=1 if paged else 0,
        grid=(b, steps),
        in_specs=[page_spec(j) for j in range(pages_per_step)] + [full(perm), full(w)],
        out_specs=o_spec,
        scratch_shapes=[pltpu.VMEM((KVH, nh, CMP_STRIDE * HD), F32)],
    )
    args = ([page_table] if paged else []) + [src] * pages_per_step + [perm, w]
    return pl.pallas_call(
        functools.partial(_cmp_fs_kernel, n_pages=pages_per_step, paged=paged),
        grid_spec=grid_spec,
        out_shape=jax.ShapeDtypeStruct((b, nh, KVH * 512), F32),
        compiler_params=_params("parallel", "arbitrary"),
        name="cmp_fs",
    )(*args)


def _cmp_out_kernel(fs_ref, b1_ref, pe_ref, w1_ref, w2t_ref, gk_ref, o_ref, *, is_key):
    nh = fs_ref.shape[0]
    hidden = b1_ref.shape[1]
    c = _dot(pe_ref[0].astype(BF16), w1_ref[0]) + _dot(pe_ref[1].astype(BF16), w1_ref[1])
    bias = c[0:1, :] + b1_ref[...]
    for kvh in range(KVH):
        first = fs_ref[:, kvh * 2 * hidden:kvh * 2 * hidden + hidden]
        second = fs_ref[:, kvh * 2 * hidden + hidden:(kvh + 1) * 2 * hidden]
        hid = jax.nn.gelu(first + pltpu.roll(second, nh - 1, 0) + bias)
        yt = _dot_nt(w2t_ref[...], hid.astype(BF16))
        if is_key:
            yt = yt * lax.rsqrt(jnp.mean(yt * yt, axis=0, keepdims=True) + EPS) * gk_ref[...]
        o_ref[kvh] = yt.astype(BF16)


def _cmp_out(fs, b1, pe, w1, w2t, gk, is_key):
    b, nh, _ = fs.shape
    full = lambda a: pl.BlockSpec(a.shape, lambda i: (0,) * a.ndim)
    return pl.pallas_call(
        functools.partial(_cmp_out_kernel, is_key=is_key),
        grid=(b,),
        in_specs=[pl.BlockSpec((None, nh, fs.shape[2]), lambda i: (i, 0, 0)), full(b1), full(pe), full(w1),
                  full(w2t), full(gk)],
        out_specs=pl.BlockSpec((None, KVH, HD, nh), lambda i: (i, 0, 0, 0)),
        out_shape=jax.ShapeDtypeStruct((b, KVH, HD, nh), BF16),
        compiler_params=_params("parallel"),
        name="cmp_out",
    )(fs, b1, pe, w1, w2t, gk)


def _cmp_weights(pe, w1, b1, w2, g_kcmp, nh):
    hidden = w1.shape[-1]
    wfs = jnp.concatenate([w1[:CMP_STRIDE], w1[CMP_STRIDE:]], axis=-1)
    w_pair = wfs.reshape(CMP_STRIDE * HD, 2 * hidden).astype(BF16)
    pe2 = jnp.zeros((2, 8, CMP_STRIDE * HD), F32).at[:, 0, :].set(pe.reshape(2, CMP_STRIDE * HD))
    w1f = w1.reshape(2, CMP_STRIDE * HD, hidden).astype(BF16)
    gk = jnp.broadcast_to(g_kcmp.reshape(HD, 1), (HD, nh))
    return w_pair, b1.reshape(1, hidden), pe2, w1f, w2.T.astype(BF16), gk


def _softmax_update(s, mask, vt, m, l, acc):
    m_new = jnp.maximum(m, jnp.max(jnp.where(mask, s, NEG), axis=0, keepdims=True))
    alpha = jnp.exp2(m - m_new)
    p = jnp.where(mask, jnp.exp2(s - m_new), 0.0)
    l_new = l * alpha + jnp.sum(p, axis=0, keepdims=True)
    acc_new = acc * alpha + _dot(vt, p.astype(BF16))
    return m_new, l_new, acc_new, p


def _softmax_init(lanes):
    return jnp.full((1, lanes), NEG, F32), jnp.zeros((1, lanes), F32), jnp.zeros((HD, lanes), F32)


def _finish(l, acc):
    return acc / jnp.maximum(l, 1e-30)


def _split_dot(mt, x):
    hi = x.astype(BF16)
    lo = (x - hi.astype(F32)).astype(BF16)
    return _dot(mt, hi) + _dot(mt, lo)


def _attn_prompt_kernel(q_ref, kc_ref, vc_ref, ksa_ref, vsa_ref, kwb_ref, vwa_ref, gt_ref, mt_ref, o_ref,
                        s_scr, s2_scr, *, tq, n_cmp, n_sel):
    i = pl.program_id(2)
    lanes = GRP * tq
    ncp = kc_ref.shape[1]
    nsp = mt_ref.shape[0]
    qt = jnp.concatenate([q_ref[h] for h in range(GRP)], axis=1)

    def col_max(mx, s):
        return jnp.maximum(mx, jnp.max(s.reshape(s.shape[0] // 8, 8, lanes), axis=0))

    def finish(acc):
        return acc[0:HD] / jnp.maximum(acc[HD:HD + 1], 1e-30)

    tpos1 = i * tq + lax.broadcasted_iota(jnp.int32, (1, tq), 1)
    tpos = jnp.concatenate([tpos1] * GRP, axis=1)

    s = _dot_tn(kc_ref[...], qt)
    n_idx = lax.broadcasted_iota(jnp.int32, (ncp, lanes), 0)
    mask = (n_idx * CMP_STRIDE + (L_CMP - 1) <= tpos) & (n_idx < n_cmp)
    m, l, acc, p = _softmax_update(s, mask, vc_ref[...], *_softmax_init(lanes))
    o_c = _finish(l, acc)
    p = p / jnp.maximum(l, 1e-30)
    imp = p[:, 0:tq]
    for h in range(1, GRP):
        imp = imp + p[:, h * tq:(h + 1) * tq]
    p_slc = _split_dot(mt_ref[...], imp)

    j_idx = lax.broadcasted_iota(jnp.int32, (nsp, tq), 0)
    cur = tpos1 // L_SLC
    valid = j_idx <= cur
    forced = (j_idx == 0) | (j_idx == cur) | (j_idx == cur - 1)
    score = jnp.where(valid & forced, jnp.inf, jnp.where(valid, p_slc, -jnp.inf))
    bits = lax.bitcast_convert_type(score, jnp.int32)
    key = jnp.where(bits >= 0, bits, bits ^ 0x7FFFFFFF)
    n_grp = nsp // 8
    keys = [key[8 * r:8 * r + 8] for r in range(n_grp)]
    keys_m1 = [k - 1 for k in keys]
    sub = lax.broadcasted_iota(jnp.int32, (8, tq), 0)

    def count_group(grp, ranks):
        ranks = list(ranks)
        for u in range(8):
            row = jnp.broadcast_to(keys[grp][u:u + 1, :], (8, tq))
            for r in range(n_grp):
                thr = keys[r] if r < grp else keys_m1[r] if r > grp else jnp.where(sub > u, keys_m1[r], keys[r])
                ranks[r] = ranks[r] + jnp.where(row > thr, 1.0, 0.0)
        return tuple(ranks)

    ranks = tuple(jnp.zeros((8, tq), F32) for _ in range(n_grp))
    last_valid = (i * tq + tq - 1) // L_SLC
    for grp in range(n_grp):
        ranks = lax.cond(8 * grp <= last_valid, functools.partial(count_group, grp), lambda r: r, ranks)
    bias = jnp.where(jnp.concatenate(ranks, axis=0) < n_sel, 0.0, NEG)
    qa = jnp.concatenate([qt, jnp.concatenate([bias] * GRP, axis=1).astype(BF16)], axis=0)

    row_k = lax.broadcasted_iota(jnp.int32, (KT, lanes), 0)

    def slc_scores(st, buf):
        start = pl.multiple_of(st * KT, KT)
        buf[0:KT, :] = _dot_tn(ksa_ref[:, pl.ds(start, KT)], qa)

    def slc_softmax(st, buf, carry, diagonal=False):
        m, acc = carry
        s = buf[0:KT, :]
        if diagonal:
            s = jnp.where(st * KT + row_k <= tpos, s, NEG)
        m_new = jnp.maximum(m, jnp.max(col_max(jnp.full((8, lanes), NEG, F32), s), axis=0, keepdims=True))
        p = jnp.exp2(s - m_new).astype(BF16)
        pv = _dot(vsa_ref[:, pl.ds(pl.multiple_of(st * KT, KT), KT)], p)
        return m_new, acc * jnp.exp2(m - m_new) + pv

    def slc_pair(u, carry):
        slc_scores(2 * u + 1, s2_scr)
        carry = slc_softmax(2 * u, s_scr, carry)
        slc_scores(2 * u + 2, s_scr)
        return slc_softmax(2 * u + 1, s2_scr, carry)

    def slc_tail_odd(carry):
        slc_scores(n_full, s2_scr)
        carry = slc_softmax(n_full - 1, s_scr, carry)
        return slc_softmax(n_full, s2_scr, carry, diagonal=True)

    def slc_tail_even(carry):
        return slc_softmax(n_full, s_scr, carry, diagonal=True)

    n_full = (i * tq) // KT
    slc_scores(0, s_scr)
    init = (jnp.full((1, lanes), NEG, F32), jnp.zeros((HD + V_PAD, lanes), F32))
    carry = lax.fori_loop(0, n_full // 2, slc_pair, init)
    o_s = finish(lax.cond(n_full % 2 == 1, slc_tail_odd, slc_tail_even, carry)[1])

    n_wt = WINDOW // tq + 1
    w0 = jnp.maximum(i - (n_wt - 1), 0) * tq
    row_w = lax.broadcasted_iota(jnp.int32, (tq, lanes), 0)
    col_w = jnp.concatenate([lax.broadcasted_iota(jnp.int32, (tq, tq), 1)] * GRP, axis=1)

    def window(steady):
        mx = jnp.full((8, lanes), NEG, F32)
        for r in range(n_wt):
            start = pl.multiple_of(w0 + r * tq, tq)
            s = _dot_tn(kwb_ref[:, pl.ds(start, tq)], qt)
            if not steady:
                rel = tpos - (start + row_w)
                s = jnp.where((rel >= 0) & (rel <= WINDOW), s, NEG)
            elif r == 0:
                s = jnp.where(row_w >= col_w, s, NEG)
            elif r == n_wt - 1:
                s = jnp.where(row_w <= col_w, s, NEG)
            s_scr[r * tq:(r + 1) * tq, :] = s
            mx = col_max(mx, s)
        m = jnp.max(mx, axis=0, keepdims=True)
        acc = jnp.zeros((HD + V_PAD, lanes), F32)
        for r in range(n_wt):
            start = pl.multiple_of(w0 + r * tq, tq)
            p = jnp.exp2(s_scr[r * tq:(r + 1) * tq, :] - m).astype(BF16)
            acc = acc + _dot(vwa_ref[:, pl.ds(start, tq)], p)
        return finish(acc)

    o_w = lax.cond(i >= n_wt - 1, functools.partial(window, True), functools.partial(window, False))

    for h in range(GRP):
        ls = slice(h * tq, (h + 1) * tq)
        o = (gt_ref[3 * h:3 * h + 1, :] * o_c[:, ls] + gt_ref[3 * h + 1:3 * h + 2, :] * o_s[:, ls]
             + gt_ref[3 * h + 2:3 * h + 3, :] * o_w[:, ls])
        o_ref[h] = o.astype(BF16)


def _slc_matrix(nsp, ncp, n_cmp):
    ratio = L_SLC // CMP_STRIDE
    j = jnp.arange(nsp)[:, None]
    n = jnp.arange(ncp)[None, :]
    m = ((n >= ratio * j) & (n <= ratio * j + ratio - 1)).astype(F32)
    m = m + ((n >= ratio * j - 1) & (n <= ratio * j + ratio - 2)).astype(F32)
    return jnp.where(n < n_cmp, m, 0.0).astype(BF16)


def _attn_prompt(qt, kct, vct, ksa, vsa, kwb, vwa, gt, tq=LANES):
    b, _, _, t = qt.shape
    assert t % KT == 0 and WINDOW % tq == 0 and t >= WINDOW + tq
    ncp = kct.shape[-1]
    n_cmp = t // CMP_STRIDE - 1
    ns = -(-t // L_SLC)
    mt = _slc_matrix(_sel_rows(t), ncp, n_cmp)
    res = lambda a: pl.BlockSpec((None, None) + a.shape[2:], lambda i, g, j: (i, g, 0, 0))
    return pl.pallas_call(
        functools.partial(_attn_prompt_kernel, tq=tq, n_cmp=n_cmp, n_sel=min(N_SEL, ns)),
        grid=(b, KVH, t // tq),
        in_specs=[pl.BlockSpec((None, GRP, HD, tq), lambda i, g, j: (i, g, 0, j)),
                  res(kct), res(vct), res(ksa), res(vsa), res(kwb), res(vwa),
                  pl.BlockSpec((None, None, 16, tq), lambda i, g, j: (i, g, 0, j)),
                  pl.BlockSpec(mt.shape, lambda i, g, j: (0, 0))],
        out_specs=pl.BlockSpec((None, GRP, HD, tq), lambda i, g, j: (i, g, 0, j)),
        out_shape=jax.ShapeDtypeStruct(qt.shape, BF16),
        scratch_shapes=[pltpu.VMEM((max(KT, WINDOW + tq), GRP * tq), F32), pltpu.VMEM((KT, GRP * tq), F32)],
        compiler_params=_params("parallel", "parallel", "arbitrary"),
        name="nsa_attn_prompt",
    )(qt, kct, vct, ksa, vsa, kwb, vwa, gt, mt)


def _attn_sample_kernel(*refs, n_pages, n_cmp, n_blocks, n_sel):
    refs = refs[1:]
    k_pages = refs[:n_pages]
    v_pages = refs[n_pages:2 * n_pages]
    (q_ref, kc_ref, vc_ref, kw_ref, vw_ref, kns_ref, vns_ref, knw_ref, vnw_ref, gt_ref, mt_ref,
     o_ref, sel_scr, oc_scr, ow_scr, m_scr, l_scr, acc_scr) = refs[2 * n_pages:]
    p = pl.program_id(1)
    ncp = kc_ref.shape[-1]
    nsp = mt_ref.shape[0]
    rows_all = KVH * HD
    first_row = lax.broadcasted_iota(jnp.int32, (LANES, LANES), 0) == 0
    qt = q_ref[...]

    def stacked(ref):
        return ref[...].reshape(rows_all, ref.shape[-1]).astype(BF16)

    def finish(l, acc):
        return acc / jnp.maximum(l, 1e-30)

    def init():
        return jnp.full((1, LANES), NEG, F32), jnp.zeros((1, LANES), F32), jnp.zeros((rows_all, LANES), F32)

    @pl.when(p == 0)
    def _():
        lane = lax.broadcasted_iota(jnp.int32, (ncp, LANES), 1)
        n_idx = lax.broadcasted_iota(jnp.int32, (ncp, LANES), 0)
        j_col = lax.broadcasted_iota(jnp.int32, (nsp, LANES), 0)
        lane_s = lax.broadcasted_iota(jnp.int32, (nsp, LANES), 1)
        jp_idx = lax.broadcasted_iota(jnp.int32, (nsp, nsp), 0)
        j_idx = lax.broadcasted_iota(jnp.int32, (nsp, nsp), 1)
        cur = n_blocks - 1
        s = _dot_tn(stacked(kc_ref), qt)
        m, l, acc, pr = _softmax_update(s, n_idx < n_cmp, stacked(vc_ref), *init())
        oc_scr[...] = finish(l, acc)
        pr = pr / jnp.maximum(l, 1e-30)
        sel_all = jnp.zeros((nsp, LANES), F32)
        for g in range(KVH):
            mine = (lane >= g * GRP) & (lane < (g + 1) * GRP)
            imp = jnp.sum(jnp.where(mine, pr, 0.0), axis=1, keepdims=True)
            p_slc = _split_dot(mt_ref[...], jnp.broadcast_to(imp, (ncp, LANES)))
            valid = j_col <= cur
            forced = (j_col == 0) | (j_col == cur) | (j_col == cur - 1)
            score = jnp.where(valid & forced, jnp.inf, jnp.where(valid, p_slc, -jnp.inf))
            col = jnp.concatenate([score] * (nsp // LANES), axis=1)
            row = col.T
            ahead = (col > row) | ((col == row) & (jp_idx < j_idx))
            rank = jnp.sum(jnp.where(ahead, 1.0, 0.0), axis=0, keepdims=True)
            sel = jnp.broadcast_to(jnp.where(rank < n_sel, 1.0, 0.0), (LANES, nsp)).T
            sel_all = jnp.where((lane_s >= g * GRP) & (lane_s < (g + 1) * GRP), sel, sel_all)
        sel_scr[...] = sel_all
        s = _dot_tn(stacked(kw_ref), qt)
        st = _softmax_update(s, jnp.full(s.shape, True), stacked(vw_ref), *init())[:3]
        s = _dot_tn(stacked(knw_ref), qt)
        _, l, acc = _softmax_update(s, first_row, stacked(vnw_ref), *st)[:3]
        ow_scr[...] = finish(l, acc)
        m_scr[...], l_scr[...], acc_scr[...] = init()

    blocks_per_page = PAGE // L_SLC
    kt = jnp.concatenate([k_pages[j][...].reshape(rows_all, PAGE) for j in range(n_pages)], axis=1).astype(BF16)
    vt = jnp.concatenate([v_pages[j][...].reshape(rows_all, PAGE) for j in range(n_pages)], axis=1).astype(BF16)
    s = _dot_tn(kt, qt)
    rows = []
    for r in range(n_pages * blocks_per_page):
        e = sel_scr[pl.ds(p * n_pages * blocks_per_page + r, 1), :]
        rows.append(jnp.broadcast_to(e, (L_SLC, LANES)))
    mask = jnp.concatenate(rows, axis=0) > 0.5
    m_scr[...], l_scr[...], acc_scr[...] = _softmax_update(s, mask, vt, m_scr[...], l_scr[...], acc_scr[...])[:3]

    @pl.when(p == pl.num_programs(1) - 1)
    def _():
        s = _dot_tn(stacked(kns_ref), qt)
        _, l, acc = _softmax_update(s, first_row, stacked(vns_ref), m_scr[...], l_scr[...], acc_scr[...])[:3]
        o_ref[...] = (gt_ref[0:1, :] * oc_scr[...] + gt_ref[1:2, :] * finish(l, acc) + gt_ref[2:3, :] * ow_scr[...])


def _attn_sample(qs, kct, vct, pool_k, pool_v, page_table, layer, kwt, vwt, kns, vns, knw, vnw, gs,
                 pages_per_step=8):
    b, n_pages_total = page_table.shape
    past = n_pages_total * PAGE
    n_cmp = past // CMP_STRIDE - 1
    ncp = kct.shape[-1]
    n_blocks = past // L_SLC + 1
    nsp = -(-n_blocks // LANES) * LANES
    mt = _slc_matrix(nsp, ncp, n_cmp)
    steps = n_pages_total // pages_per_step
    page_spec = lambda j: pl.BlockSpec((None, None, KVH, HD, PAGE),
                                       lambda i, p, pt: (layer, pt[i, p * pages_per_step + j], 0, 0, 0))
    per_b = lambda a: pl.BlockSpec((None,) + a.shape[1:], lambda i, p, pt: (i,) + (0,) * (a.ndim - 1))
    small = [qs, kct, vct, kwt, vwt, kns, vns, knw, vnw, gs]
    grid_spec = pltpu.PrefetchScalarGridSpec(
        num_scalar_prefetch=1,
        grid=(b, steps),
        in_specs=[page_spec(j) for j in range(pages_per_step)] * 2 + [per_b(a) for a in small]
        + [pl.BlockSpec(mt.shape, lambda i, p, pt: (0, 0))],
        out_specs=pl.BlockSpec((None, KVH * HD, LANES), lambda i, p, pt: (i, 0, 0)),
        scratch_shapes=[pltpu.VMEM((nsp, LANES), F32), pltpu.VMEM((KVH * HD, LANES), F32),
                        pltpu.VMEM((KVH * HD, LANES), F32), pltpu.VMEM((1, LANES), F32),
                        pltpu.VMEM((1, LANES), F32), pltpu.VMEM((KVH * HD, LANES), F32)],
    )
    return pl.pallas_call(
        functools.partial(_attn_sample_kernel, n_pages=pages_per_step, n_cmp=n_cmp, n_blocks=n_blocks,
                          n_sel=min(N_SEL, n_blocks)),
        grid_spec=grid_spec,
        out_shape=jax.ShapeDtypeStruct((b, KVH * HD, LANES), F32),
        compiler_params=_params("parallel", "arbitrary"),
        name="nsa_attn_sample",
    )(page_table, *([pool_k] * pages_per_step), *([pool_v] * pages_per_step), *small, mt)


def _rope_tables(pos):
    inv_freq = ROPE_THETA ** (-jnp.arange(ROT_HALF, dtype=F32) * 2.0 / (2 * ROT_HALF))
    ang = pos.astype(F32)[:, None] * inv_freq[None, :]
    return jnp.cos(ang).T, jnp.sin(ang).T


def _nsa_layer(xp, xs, pools, win_bufs, page_table, layer, norm_g, w_in, b_gate, g_q, g_k, g_kcmp,
               cmp_pe, cmp_w1, cmp_b1, cmp_w2, w_out):
    b, t, d = xp.shape
    bs = xs.shape[0]
    past = page_table.shape[1] * PAGE
    nq, nkv = GRP * KVH * HD, KVH * HD
    n_gate = 3 * GRP * KVH

    wt_g = jnp.pad(w_in[:, nq + 6 * nkv:].T.reshape(KVH, n_gate // KVH, d), ((0, 0), (0, 16 - n_gate // KVH), (0, 0)))
    wt = jnp.concatenate([w_in[:, :nq + 6 * nkv].T, wt_g.reshape(KVH * 16, d)], axis=0).astype(BF16)
    bg = jnp.pad(b_gate.reshape(KVH, n_gate // KVH), ((0, 0), (0, 16 - n_gate // KVH))).reshape(KVH * 16, 1)
    gn = norm_g.reshape(1, d)
    wo = w_out.astype(BF16)

    def project(x, pos, tm):
        cos, sin = _rope_tables(pos)
        col = lambda v: jnp.broadcast_to(v[..., None], v.shape + (tm,))
        return _nsa_proj(x, gn, wt, col(g_q), col(g_k), col(bg[:, 0]), cos, sin, tm)

    tm = min(512, t)
    qt, kct_rows, vct_rows, kst, vst, kwt, vwt, gt, ksa, vsa, kwb, vwa = project(xp, jnp.arange(t), tm)
    pages = t // PAGE
    nh = pages * (PAGE // CMP_STRIDE)
    cmp_w = [_cmp_weights(cmp_pe[i], cmp_w1[i], cmp_b1[i], cmp_w2[i], g_kcmp, nh) for i in range(2)]
    pps = min(16, pages)
    kct = _cmp_out(_cmp_fs(kct_rows, cmp_w[0][0], pages, pps), *cmp_w[0][1:], is_key=True)
    vct = _cmp_out(_cmp_fs(vct_rows, cmp_w[1][0], pages, pps), *cmp_w[1][1:], is_key=False)
    ot = _attn_prompt(qt, kct, vct, ksa, vsa, kwb, vwa, gt)
    yp = _nsa_out(ot, xp, wo, tm)
    n_win = min(WINDOW, t)
    rows_p = [kct_rows, vct_rows, kst, vst, kwt[..., t - n_win:], vwt[..., t - n_win:]]

    xs_pad = jnp.pad(xs.reshape(1, bs, d), ((0, 0), (0, LANES - bs), (0, 0)))
    outs = project(xs_pad, jnp.full((LANES,), past), LANES)
    qt_s, rows_s, gt_s = outs[0], outs[1:7], outs[7]
    pages_s = page_table.shape[1]
    nh_s = pages_s * (PAGE // CMP_STRIDE)
    cmp_ws = [_cmp_weights(cmp_pe[i], cmp_w1[i], cmp_b1[i], cmp_w2[i], g_kcmp, nh_s) for i in range(2)]
    pool5 = [jnp.transpose(pl_, (0, 1, 3, 4, 2)) for pl_ in pools]
    pps = min(16, pages_s)
    kct_s = _cmp_out(_cmp_fs(pool5[0], cmp_ws[0][0], pages_s, pps, page_table, layer), *cmp_ws[0][1:], is_key=True)
    vct_s = _cmp_out(_cmp_fs(pool5[1], cmp_ws[1][0], pages_s, pps, page_table, layer), *cmp_ws[1][1:], is_key=False)
    qs = jnp.transpose(qt_s[0, :, :, :bs].reshape(KVH, GRP, HD, bs), (3, 0, 2, 1))
    qs = jnp.einsum("bgdh,gk->bgdkh", qs, jnp.eye(KVH, dtype=qs.dtype))
    qs = jnp.pad(qs.reshape(bs, KVH * HD, KVH * GRP), ((0, 0), (0, 0), (0, LANES - KVH * GRP)))
    new = lambda a: jnp.pad(jnp.transpose(a[0, :, :, :bs], (2, 0, 1))[..., None],
                            ((0, 0), (0, 0), (0, 0), (0, LANES - 1)))
    gs = jnp.transpose(gt_s[0, :, :n_gate // KVH, :bs].reshape(KVH, GRP, 3, bs), (3, 2, 0, 1))
    gs = jnp.pad(gs.reshape(bs, 3, KVH * GRP), ((0, 0), (0, 8 - 3), (0, LANES - KVH * GRP)))
    win5 = [jnp.transpose(wb[layer], (0, 2, 3, 1)) for wb in win_bufs]
    ot_s = _attn_sample(qs, kct_s, vct_s, pool5[2], pool5[3], page_table, layer, win5[0], win5[1],
                        new(rows_s[2]), new(rows_s[3]), new(rows_s[4]), new(rows_s[5]), gs)
    ot_s = ot_s.reshape(bs, KVH, HD, LANES)
    ot_s = jnp.stack([ot_s[:, g, :, g * GRP:(g + 1) * GRP] for g in range(KVH)])
    ot_s = jnp.transpose(ot_s, (0, 3, 2, 1)).reshape(1, GRP * KVH, HD, bs)
    ot_s = jnp.pad(ot_s, ((0, 0), (0, 0), (0, 0), (0, LANES - bs))).astype(BF16)
    ys = _nsa_out(ot_s, xs_pad, wo, LANES)[0, :bs].reshape(bs, 1, d)

    to_rows = lambda a: jnp.transpose(a, (0, 3, 1, 2))
    rows_p = [to_rows(a) for a in rows_p]
    rows_s = [to_rows(a[:, :, :, :bs]).reshape(bs, 1, KVH, HD) for a in rows_s]
    return yp, ys, rows_p, rows_s


def kernel(x_prompt, x_sample, state_gla, cache_k_cmp, cache_v_cmp, cache_k_slc, cache_v_slc, cache_k_win, cache_v_win, page_table, norm_mix, norm_mlp, mlp_up, mlp_down, gla_w_in, gla_w_gate2, gla_b_gate, gla_g_out, gla_w_out, nsa_w_in, nsa_b_gate, nsa_g_q, nsa_g_k, nsa_g_kcmp, nsa_cmp_pe, nsa_cmp_w1, nsa_cmp_b1, nsa_cmp_w2, nsa_w_out):
    depth = norm_mix.shape[0]
    b, t, d = x_prompt.shape
    bs = x_sample.shape[0]
    xp, xs = x_prompt, x_sample
    gla_p, gla_s, nsa_p, nsa_s = [], [], [], []
    pools = (cache_k_cmp, cache_v_cmp, cache_k_slc, cache_v_slc)
    for i in range(depth):
        j = i // 2
        if i % 2 == 0:
            xp, xs, sp, ss = _gla_layer(xp, xs, state_gla[j], norm_mix[i], gla_w_in[j], gla_w_gate2[j],
                                        gla_b_gate[j], gla_g_out[j], gla_w_out[j])
            gla_p.append(sp)
            gla_s.append(ss)
        else:
            xp, xs, rp, rs = _nsa_layer(xp, xs, pools, (cache_k_win, cache_v_win), page_table, j, norm_mix[i],
                                        nsa_w_in[j], nsa_b_gate[j], nsa_g_q[j], nsa_g_k[j], nsa_g_kcmp[j],
                                        nsa_cmp_pe[j], nsa_cmp_w1[j], nsa_cmp_b1[j], nsa_cmp_w2[j], nsa_w_out[j])
            nsa_p.append(rp)
            nsa_s.append(rs)
        g = norm_mlp[i].reshape(1, d)
        wu, wd = mlp_up[i].astype(BF16), mlp_down[i].astype(BF16)
        xp = _mlp(xp.reshape(b * t, d), g, wu, wd, tm=512).reshape(b, t, d)
        xs = _mlp(xs.reshape(bs, d), g, wu, wd, tm=bs).reshape(bs, 1, d)
    stack = lambda lst, r: jnp.stack([e[r] for e in lst])
    return (xp, xs, jnp.stack(gla_p), jnp.stack(gla_s),
            stack(nsa_p, 0), stack(nsa_s, 0), stack(nsa_p, 1), stack(nsa_s, 1),
            stack(nsa_p, 2), stack(nsa_s, 2), stack(nsa_p, 3), stack(nsa_s, 3),
            stack(nsa_p, 4), stack(nsa_s, 4), stack(nsa_p, 5), stack(nsa_s, 5))
```

```python
import functools

import jax
import jax.numpy as jnp
from jax import lax
from jax.experimental import pallas as pl
from jax.experimental.pallas import tpu as pltpu

F32 = jnp.float32
BF16 = jnp.bfloat16
EPS = 1e-6
NEG = -1e30
VMEM_LIMIT_BYTES = 48 * 1024 * 1024
LANES = 128
PAGE = 128
HD = 64
KVH = 4
GRP = 4
L_CMP, CMP_STRIDE, L_SLC, N_SEL, WINDOW = 32, 16, 64, 16, 512
GLA_H, GLA_DK, GLA_DV, GLA_CHUNK = 4, 128, 256, 64
ROPE_THETA, ROT_HALF = 500000.0, 8
LOG2E = 1.4426950408889634
V_PAD = 16
KT = 512


def _params(*sem):
    return pltpu.CompilerParams(dimension_semantics=sem, vmem_limit_bytes=VMEM_LIMIT_BYTES)


def _dot(a, b):
    return jnp.dot(a, b, preferred_element_type=F32)


def _dot_nt(a, b):
    return lax.dot_general(a, b, (((1,), (1,)), ((), ())), preferred_element_type=F32)


def _dot_tn(a, b):
    return lax.dot_general(a, b, (((0,), (0,)), ((), ())), preferred_element_type=F32)


def _rms_rows(x, g):
    ms = jnp.mean(x * x, axis=-1, keepdims=True)
    return x * lax.rsqrt(ms + EPS) * g


def _mlp_kernel(x_ref, g_ref, wu_ref, wd_ref, o_ref, h_scr, acc_scr):
    f = pl.program_id(1)

    @pl.when(f == 0)
    def _():
        h_scr[...] = _rms_rows(x_ref[...], g_ref[...]).astype(BF16)
        acc_scr[...] = jnp.zeros_like(acc_scr)

    u = jnp.maximum(_dot(h_scr[...], wu_ref[...]), 0.0)
    acc_scr[...] += _dot((u * u).astype(BF16), wd_ref[...])

    @pl.when(f == pl.num_programs(1) - 1)
    def _():
        o_ref[...] = x_ref[...] + acc_scr[...]


def _mlp(x, g, wu, wd, tm, tf=1024):
    m, d = x.shape
    ff = wu.shape[1]
    return pl.pallas_call(
        _mlp_kernel,
        grid=(m // tm, ff // tf),
        in_specs=[pl.BlockSpec((tm, d), lambda i, f: (i, 0)),
                  pl.BlockSpec((1, d), lambda i, f: (0, 0)),
                  pl.BlockSpec((d, tf), lambda i, f: (0, f)),
                  pl.BlockSpec((tf, d), lambda i, f: (f, 0))],
        out_specs=pl.BlockSpec((tm, d), lambda i, f: (i, 0)),
        out_shape=jax.ShapeDtypeStruct((m, d), F32),
        scratch_shapes=[pltpu.VMEM((tm, d), BF16), pltpu.VMEM((tm, d), F32)],
        compiler_params=_params("parallel", "arbitrary"),
        name="mlp_block",
    )(x, g, wu, wd)


def _gla_proj_kernel(x_ref, g_ref, w_ref, wg2_ref, bg_ref, q_ref, k_ref, v_ref, r_ref, gl_ref):
    h = _rms_rows(x_ref[...], g_ref[...]).astype(BF16)
    dk, dv = GLA_H * GLA_DK, GLA_H * GLA_DV
    q_ref[...] = _dot(h, w_ref[:, 0:dk]) * (GLA_DK ** -0.5)
    k_ref[...] = _dot(h, w_ref[:, dk:2 * dk])
    v_ref[...] = _dot(h, w_ref[:, 2 * dk:2 * dk + dv])
    r_ref[...] = _dot(h, w_ref[:, 2 * dk + dv:2 * dk + 2 * dv])
    gr = _dot(h, w_ref[:, 2 * dk + 2 * dv:])
    xg = _dot(gr.astype(BF16), wg2_ref[...]) + bg_ref[...]
    gl_ref[...] = jax.nn.log_sigmoid(xg) * (1.0 / 16.0)


def _gla_proj(x, g, w, wg2, bg, tm):
    m, d = x.shape
    dk, dv = GLA_H * GLA_DK, GLA_H * GLA_DV
    row = lambda n: pl.BlockSpec((tm, n), lambda i: (i, 0))
    full = lambda a: pl.BlockSpec(a.shape, lambda i: (0, 0))
    return pl.pallas_call(
        _gla_proj_kernel,
        grid=(m // tm,),
        in_specs=[row(d), full(g), full(w), full(wg2), full(bg)],
        out_specs=[row(dk), row(dk), row(dv), row(dv), row(dk)],
        out_shape=[jax.ShapeDtypeStruct((m, n), F32) for n in (dk, dk, dv, dv, dk)],
        compiler_params=_params("parallel"),
        name="gla_proj",
    )(x, g, w, wg2, bg)


def _gla_scan_kernel(q_ref, k_ref, g_ref, v_ref, o_ref, s_ref, st_scr, *, n_chunks, n_heads):
    c_len = GLA_CHUNK
    step = pl.program_id(2)

    @pl.when(step == 0)
    def _():
        st_scr[...] = jnp.zeros_like(st_scr)

    rowi = lax.broadcasted_iota(jnp.int32, (c_len, GLA_DK), 0)
    causal = (lax.broadcasted_iota(jnp.int32, (c_len, c_len), 0)
              >= lax.broadcasted_iota(jnp.int32, (c_len, c_len), 1))

    def body(c, carry):
        sl = pl.ds(pl.multiple_of(c * c_len, c_len), c_len)
        for h in range(n_heads):
            ks = slice(h * GLA_DK, (h + 1) * GLA_DK)
            vs = slice(h * GLA_DV, (h + 1) * GLA_DV)
            b = g_ref[sl, ks]
            sh = 1
            while sh < c_len:
                b = b + jnp.where(rowi >= sh, pltpu.roll(b, sh, 0), 0.0)
                sh *= 2
            b_last = b[c_len - 1:c_len, :]
            b_mid = b[c_len // 2 - 1:c_len // 2, :]
            q = q_ref[sl, ks]
            k = k_ref[sl, ks]
            v = v_ref[sl, vs].astype(BF16)
            qe = (q * jnp.exp(b)).astype(BF16)
            qa = (q * jnp.exp(b - b_mid)).astype(BF16)
            ka = (k * jnp.exp(b_mid - b)).astype(BF16)
            kd = (k * jnp.exp(b_last - b)).astype(BF16)
            a = jnp.where(causal, _dot_nt(qa, ka), 0.0)
            st = st_scr[h]
            o_ref[sl, vs] = _dot_nt(qe, st.astype(BF16)) + _dot(a.astype(BF16), v)
            st_scr[h] = st * jnp.exp(b_last) + _dot_tn(v, kd)
        return carry

    lax.fori_loop(0, n_chunks, body, 0)

    @pl.when(step == pl.num_programs(2) - 1)
    def _():
        for h in range(n_heads):
            s_ref[h] = st_scr[h].T


def _gla_scan(q, k, gl, v, n_heads=2, t_blk=1024):
    b, t, _ = q.shape
    t_blk = min(t_blk, t)
    kq = pl.BlockSpec((None, t_blk, n_heads * GLA_DK), lambda i, h, j: (i, j, h))
    vv = pl.BlockSpec((None, t_blk, n_heads * GLA_DV), lambda i, h, j: (i, j, h))
    return pl.pallas_call(
        functools.partial(_gla_scan_kernel, n_chunks=t_blk // GLA_CHUNK, n_heads=n_heads),
        grid=(b, GLA_H // n_heads, t // t_blk),
        in_specs=[kq, kq, kq, vv],
        out_specs=[vv, pl.BlockSpec((None, n_heads, GLA_DK, GLA_DV), lambda i, h, j: (i, h, 0, 0))],
        out_shape=[jax.ShapeDtypeStruct((b, t, GLA_H * GLA_DV), F32),
                   jax.ShapeDtypeStruct((b, GLA_H, GLA_DK, GLA_DV), F32)],
        scratch_shapes=[pltpu.VMEM((n_heads, GLA_DV, GLA_DK), F32)],
        compiler_params=_params("parallel", "parallel", "arbitrary"),
        name="gla_scan",
    )(q, k, gl, v)


def _gla_step_kernel(q_ref, k_ref, g_ref, v_ref, s0_ref, o_ref, s_ref):
    def col(x):
        return jnp.broadcast_to(x, (LANES, LANES)).T

    for h in range(GLA_H):
        ks = slice(h * GLA_DK, (h + 1) * GLA_DK)
        qc, kc, ec = col(q_ref[:, ks]), col(k_ref[:, ks]), col(jnp.exp(g_ref[:, ks]))
        for half in range(GLA_DV // LANES):
            vs = slice(h * GLA_DV + half * LANES, h * GLA_DV + (half + 1) * LANES)
            ss = slice(half * LANES, (half + 1) * LANES)
            sn = ec * s0_ref[h, :, ss] + kc * v_ref[:, vs]
            s_ref[h, :, ss] = sn
            o_ref[:, vs] = jnp.sum(qc * sn, axis=0, keepdims=True)


def _gla_step(q, k, gl, v, s0):
    b = q.shape[0]
    kq = pl.BlockSpec((None, 1, GLA_H * GLA_DK), lambda i: (i, 0, 0))
    vv = pl.BlockSpec((None, 1, GLA_H * GLA_DV), lambda i: (i, 0, 0))
    st = pl.BlockSpec((None, GLA_H, GLA_DK, GLA_DV), lambda i: (i, 0, 0, 0))
    return pl.pallas_call(
        _gla_step_kernel,
        grid=(b,),
        in_specs=[kq, kq, kq, vv, st],
        out_specs=[vv, st],
        out_shape=[jax.ShapeDtypeStruct((b, 1, GLA_H * GLA_DV), F32),
                   jax.ShapeDtypeStruct((b, GLA_H, GLA_DK, GLA_DV), F32)],
        compiler_params=_params("parallel"),
        name="gla_step",
    )(q, k, gl, v, s0)


def _gla_out_kernel(o_ref, r_ref, x_ref, go_ref, w_ref, y_ref):
    parts = []
    for h in range(GLA_H):
        sl = slice(h * GLA_DV, (h + 1) * GLA_DV)
        r = r_ref[:, sl]
        parts.append((_rms_rows(o_ref[:, sl], go_ref[...]) * (r * jax.nn.sigmoid(r))).astype(BF16))
    y_ref[...] = x_ref[...] + _dot(jnp.concatenate(parts, axis=1), w_ref[...])


def _gla_out(o, r, x, go, w, tm):
    m, d = x.shape
    row = pl.BlockSpec((tm, d), lambda i: (i, 0))
    full = lambda a: pl.BlockSpec(a.shape, lambda i: (0, 0))
    return pl.pallas_call(
        _gla_out_kernel,
        grid=(m // tm,),
        in_specs=[row, row, row, full(go), full(w)],
        out_specs=row,
        out_shape=jax.ShapeDtypeStruct((m, d), F32),
        compiler_params=_params("parallel"),
        name="gla_out",
    )(o, r, x, go, w)


def _gla_layer(xp, xs, s0, norm_g, w_in, w_gate2, b_gate, g_out, w_out):
    b, t, d = xp.shape
    bs = xs.shape[0]
    dk, dv = GLA_H * GLA_DK, GLA_H * GLA_DV
    rank = w_gate2.shape[0]
    w = jnp.pad(w_in, ((0, 0), (0, LANES - rank))).astype(BF16)
    wg2 = jnp.pad(w_gate2, ((0, LANES - rank), (0, 0))).astype(BF16)
    g = norm_g.reshape(1, d)
    bg = b_gate.reshape(1, dk)
    go = g_out.reshape(1, GLA_DV)
    wo = w_out.astype(BF16)
    x2 = xp.reshape(b * t, d)
    q, k, v, r, gl = _gla_proj(x2, g, w, wg2, bg, tm=512)
    o, sp = _gla_scan(q.reshape(b, t, dk), k.reshape(b, t, dk), gl.reshape(b, t, dk), v.reshape(b, t, dv))
    yp = _gla_out(o.reshape(b * t, dv), r, x2, go, wo, tm=512).reshape(b, t, d)
    xs2 = xs.reshape(bs, d)
    q, k, v, r, gl = _gla_proj(xs2, g, w, wg2, bg, tm=bs)
    o, ss = _gla_step(q.reshape(bs, 1, dk), k.reshape(bs, 1, dk), gl.reshape(bs, 1, dk), v.reshape(bs, 1, dv), s0)
    ys = _gla_out(o.reshape(bs, dv), r, xs2, go, wo, tm=bs).reshape(bs, 1, d)
    return yp, ys, sp, ss


def _nsa_proj_kernel(x_ref, gn_ref, wt_ref, gq_ref, gk_ref, bg_ref, cos_ref, sin_ref,
                     q_ref, kc_ref, vc_ref, ks_ref, vs_ref, kw_ref, vw_ref, gt_ref,
                     ksa_ref, vsa_ref, kwb_ref, vwa_ref):
    tm = x_ref.shape[0]
    nq = GRP * KVH * HD
    nkv = KVH * HD
    h = _rms_rows(x_ref[...], gn_ref[...]).astype(BF16)
    cos = cos_ref[...][None]
    sin = sin_ref[...][None]

    def norm_rope(z, g, nh):
        z3 = z.reshape(nh, HD, tm)
        y = z3 * lax.rsqrt(jnp.mean(z3 * z3, axis=1, keepdims=True) + EPS) * g[None]
        x1 = y[:, 0:ROT_HALF, :]
        x2 = y[:, ROT_HALF:2 * ROT_HALF, :]
        return jnp.concatenate([x1 * cos - x2 * sin, x1 * sin + x2 * cos, y[:, 2 * ROT_HALF:, :]], axis=1)

    zq = _dot_nt(wt_ref[0:nq, :], h)
    q_ref[...] = (norm_rope(zq, gq_ref[...], GRP * KVH) * (HD ** -0.5 * LOG2E)).astype(BF16)
    zkv = _dot_nt(wt_ref[nq:nq + 6 * nkv, :], h)
    outs = (kc_ref, vc_ref, ks_ref, vs_ref, kw_ref, vw_ref)
    rows = []
    for i in range(6):
        z = zkv[i * nkv:(i + 1) * nkv, :]
        rows.append(norm_rope(z, gk_ref[i // 2], KVH) if i % 2 == 0 else z.reshape(KVH, HD, tm))
        outs[i][...] = rows[i]
    zg = _dot_nt(wt_ref[nq + 6 * nkv:, :], h) + bg_ref[...]
    gt_ref[...] = jax.nn.sigmoid(zg).reshape(KVH, 16, tm)

    nsp = ksa_ref.shape[1] - HD
    blk = lax.broadcasted_iota(jnp.int32, (KVH, nsp, tm), 1)
    tok = pl.program_id(1) * tm + lax.broadcasted_iota(jnp.int32, (KVH, nsp, tm), 2)
    onehot = jnp.where(tok // L_SLC == blk, 1.0, 0.0)
    ksa_ref[...] = jnp.concatenate([rows[2], onehot], axis=1).astype(BF16)
    kwb_ref[...] = rows[4].astype(BF16)
    ones = jnp.where(lax.broadcasted_iota(jnp.int32, (KVH, V_PAD, tm), 1) == 0, 1.0, 0.0)
    vsa_ref[...] = jnp.concatenate([rows[3], ones], axis=1).astype(BF16)
    vwa_ref[...] = jnp.concatenate([rows[5], ones], axis=1).astype(BF16)


def _sel_rows(t):
    return -(-(-(-t // L_SLC)) // 16) * 16


def _nsa_proj(x, gn, wt, gq, gk, bg, cos, sin, tm):
    b, t, d = x.shape
    full = lambda a: pl.BlockSpec(a.shape, lambda i, j: (0,) * a.ndim)
    rows_spec = lambda n: pl.BlockSpec((None, KVH, n, tm), lambda i, j: (i, 0, 0, j))
    rows_shape = lambda n, dt: jax.ShapeDtypeStruct((b, KVH, n, t), dt)
    aug = [HD + _sel_rows(t), HD + V_PAD, HD, HD + V_PAD]
    return pl.pallas_call(
        _nsa_proj_kernel,
        grid=(b, t // tm),
        in_specs=[pl.BlockSpec((None, tm, d), lambda i, j: (i, j, 0)), full(gn), full(wt), full(gq), full(gk),
                  full(bg), pl.BlockSpec((ROT_HALF, tm), lambda i, j: (0, j)),
                  pl.BlockSpec((ROT_HALF, tm), lambda i, j: (0, j))],
        out_specs=[pl.BlockSpec((None, GRP * KVH, HD, tm), lambda i, j: (i, 0, 0, j))] + [rows_spec(HD)] * 6
        + [rows_spec(16)] + [rows_spec(n) for n in aug],
        out_shape=[jax.ShapeDtypeStruct((b, GRP * KVH, HD, t), BF16)] + [rows_shape(HD, F32)] * 6
        + [rows_shape(16, F32)] + [rows_shape(n, BF16) for n in aug],
        compiler_params=_params("parallel", "parallel"),
        name="nsa_proj",
    )(x, gn, wt, gq, gk, bg, cos, sin)


def _nsa_out_kernel(ot_ref, x_ref, w_ref, y_ref):
    ot = ot_ref[...].reshape(GRP * KVH * HD, ot_ref.shape[-1])
    y_ref[...] = x_ref[...] + _dot_tn(ot, w_ref[...])


def _nsa_out(ot, x, w, tm):
    b, t, d = x.shape
    return pl.pallas_call(
        _nsa_out_kernel,
        grid=(b, t // tm),
        in_specs=[pl.BlockSpec((None, GRP * KVH, HD, tm), lambda i, j: (i, 0, 0, j)),
                  pl.BlockSpec((None, tm, d), lambda i, j: (i, j, 0)),
                  pl.BlockSpec(w.shape, lambda i, j: (0, 0))],
        out_specs=pl.BlockSpec((None, tm, d), lambda i, j: (i, j, 0)),
        out_shape=jax.ShapeDtypeStruct((b, t, d), F32),
        compiler_params=_params("parallel", "parallel"),
        name="nsa_out",
    )(ot, x, w)


def _cmp_fs_kernel(*refs, n_pages, paged):
    refs = refs[1:] if paged else refs
    pages = refs[:n_pages]
    perm_ref, w_ref, o_ref, lhs_scr = refs[n_pages:]
    p = pl.program_id(1)
    half = CMP_STRIDE
    n_half = PAGE // half
    low = lax.broadcasted_iota(jnp.int32, (n_half, LANES), 1) < HD
    for pair in range(KVH // 2):
        for j in range(n_pages):
            y = _dot(pages[j][2 * pair:2 * pair + 2].reshape(2 * HD, PAGE).astype(BF16), perm_ref[...])
            xa = y.T
            xb = jnp.concatenate([y[HD:], y[:HD]], axis=0).T
            rows = pl.ds(pl.multiple_of((p * n_pages + j) * n_half, n_half), n_half)
            for t in range(half // 2):
                ev = slice(2 * t * n_half, (2 * t + 1) * n_half)
                od = slice((2 * t + 1) * n_half, (2 * t + 2) * n_half)
                lanes = slice(t * LANES, (t + 1) * LANES)
                lhs_scr[2 * pair, rows, lanes] = jnp.where(low, xa[ev], xb[od])
                lhs_scr[2 * pair + 1, rows, lanes] = jnp.where(low, xb[ev], xa[od])

    @pl.when(p == pl.num_programs(1) - 1)
    def _():
        n = w_ref.shape[1]
        for kvh in range(KVH):
            o_ref[:, kvh * n:(kvh + 1) * n] = _dot(lhs_scr[kvh].astype(BF16), w_ref[...])


def _cmp_fs(src, w, n_pages_total, pages_per_step, page_table=None, layer=0):
    paged = page_table is not None
    b = page_table.shape[0] if paged else src.shape[0]
    n_half = PAGE // CMP_STRIDE
    nh = n_pages_total * n_half
    steps = n_pages_total // pages_per_step
    tok = jnp.arange(PAGE)
    perm = (((tok % CMP_STRIDE) * n_half + tok // CMP_STRIDE)[:, None] == tok[None, :]).astype(BF16)
    if paged:
        page_spec = lambda j: pl.BlockSpec(
            (None, None, KVH, HD, PAGE), lambda i, p, pt: (layer, pt[i, p * pages_per_step + j], 0, 0, 0))
        full = lambda a: pl.BlockSpec(a.shape, lambda i, p, pt: (0, 0))
        o_spec = pl.BlockSpec((None, nh, KVH * 512), lambda i, p, pt: (i, 0, 0))
    else:
        page_spec = lambda j: pl.BlockSpec((None, KVH, HD, PAGE), lambda i, p: (i, 0, 0, p * pages_per_step + j))
        full = lambda a: pl.BlockSpec(a.shape, lambda i, p: (0, 0))
        o_spec = pl.BlockSpec((None, nh, KVH * 512), lambda i, p: (i, 0, 0))
    grid_spec = pltpu.PrefetchScalarGridSpec(
        num_scalar_prefetch=1 if paged else 0,
        grid=(b, steps),
        in_specs=[page_spec(j) for j in range(pages_per_step)] + [full(perm), full(w)],
        out_specs=o_spec,
        scratch_shapes=[pltpu.VMEM((KVH, nh, CMP_STRIDE * HD), F32)],
    )
    args = ([page_table] if paged else []) + [src] * pages_per_step + [perm, w]
    return pl.pallas_call(
        functools.partial(_cmp_fs_kernel, n_pages=pages_per_step, paged=paged),
        grid_spec=grid_spec,
        out_shape=jax.ShapeDtypeStruct((b, nh, KVH * 512), F32),
        compiler_params=_params("parallel", "arbitrary"),
        name="cmp_fs",
    )(*args)


def _cmp_out_kernel(fs_ref, b1_ref, pe_ref, w1_ref, w2t_ref, gk_ref, o_ref, *, is_key):
    nh = fs_ref.shape[0]
    hidden = b1_ref.shape[1]
    c = _dot(pe_ref[0].astype(BF16), w1_ref[0]) + _dot(pe_ref[1].astype(BF16), w1_ref[1])
    bias = c[0:1, :] + b1_ref[...]
    for kvh in range(KVH):
        first = fs_ref[:, kvh * 2 * hidden:kvh * 2 * hidden + hidden]
        second = fs_ref[:, kvh * 2 * hidden + hidden:(kvh + 1) * 2 * hidden]
        hid = jax.nn.gelu(first + pltpu.roll(second, nh - 1, 0) + bias)
        yt = _dot_nt(w2t_ref[...], hid.astype(BF16))
        if is_key:
            yt = yt * lax.rsqrt(jnp.mean(yt * yt, axis=0, keepdims=True) + EPS) * gk_ref[...]
        o_ref[kvh] = yt.astype(BF16)


def _cmp_out(fs, b1, pe, w1, w2t, gk, is_key):
    b, nh, _ = fs.shape
    full = lambda a: pl.BlockSpec(a.shape, lambda i: (0,) * a.ndim)
    return pl.pallas_call(
        functools.partial(_cmp_out_kernel, is_key=is_key),
        grid=(b,),
        in_specs=[pl.BlockSpec((None, nh, fs.shape[2]), lambda i: (i, 0, 0)), full(b1), full(pe), full(w1),
                  full(w2t), full(gk)],
        out_specs=pl.BlockSpec((None, KVH, HD, nh), lambda i: (i, 0, 0, 0)),
        out_shape=jax.ShapeDtypeStruct((b, KVH, HD, nh), BF16),
        compiler_params=_params("parallel"),
        name="cmp_out",
    )(fs, b1, pe, w1, w2t, gk)


def _cmp_weights(pe, w1, b1, w2, g_kcmp, nh):
    hidden = w1.shape[-1]
    wfs = jnp.concatenate([w1[:CMP_STRIDE], w1[CMP_STRIDE:]], axis=-1)
    w_pair = wfs.reshape(CMP_STRIDE * HD, 2 * hidden).astype(BF16)
    pe2 = jnp.zeros((2, 8, CMP_STRIDE * HD), F32).at[:, 0, :].set(pe.reshape(2, CMP_STRIDE * HD))
    w1f = w1.reshape(2, CMP_STRIDE * HD, hidden).astype(BF16)
    gk = jnp.broadcast_to(g_kcmp.reshape(HD, 1), (HD, nh))
    return w_pair, b1.reshape(1, hidden), pe2, w1f, w2.T.astype(BF16), gk


def _softmax_update(s, mask, vt, m, l, acc):
    m_new = jnp.maximum(m, jnp.max(jnp.where(mask, s, NEG), axis=0, keepdims=True))
    alpha = jnp.exp2(m - m_new)
    p = jnp.where(mask, jnp.exp2(s - m_new), 0.0)
    l_new = l * alpha + jnp.sum(p, axis=0, keepdims=True)
    acc_new = acc * alpha + _dot(vt, p.astype(BF16))
    return m_new, l_new, acc_new, p


def _softmax_init(lanes):
    return jnp.full((1, lanes), NEG, F32), jnp.zeros((1, lanes), F32), jnp.zeros((HD, lanes), F32)


def _finish(l, acc):
    return acc / jnp.maximum(l, 1e-30)


def _split_dot(mt, x):
    hi = x.astype(BF16)
    lo = (x - hi.astype(F32)).astype(BF16)
    return _dot(mt, hi) + _dot(mt, lo)


def _attn_prompt_kernel(q_ref, kc_ref, vc_ref, ksa_ref, vsa_ref, kwb_ref, vwa_ref, gt_ref, mt_ref, o_ref,
                        s_scr, s2_scr, *, tq, n_cmp, n_sel):
    i = pl.program_id(2)
    lanes = GRP * tq
    ncp = kc_ref.shape[1]
    nsp = mt_ref.shape[0]
    qt = jnp.concatenate([q_ref[h] for h in range(GRP)], axis=1)

    def col_max(mx, s):
        return jnp.maximum(mx, jnp.max(s.reshape(s.shape[0] // 8, 8, lanes), axis=0))

    def finish(acc):
        return acc[0:HD] / jnp.maximum(acc[HD:HD + 1], 1e-30)

    tpos1 = i * tq + lax.broadcasted_iota(jnp.int32, (1, tq), 1)
    tpos = jnp.concatenate([tpos1] * GRP, axis=1)

    s = _dot_tn(kc_ref[...], qt)
    n_idx = lax.broadcasted_iota(jnp.int32, (ncp, lanes), 0)
    mask = (n_idx * CMP_STRIDE + (L_CMP - 1) <= tpos) & (n_idx < n_cmp)
    m, l, acc, p = _softmax_update(s, mask, vc_ref[...], *_softmax_init(lanes))
    o_c = _finish(l, acc)
    p = p / jnp.maximum(l, 1e-30)
    imp = p[:, 0:tq]
    for h in range(1, GRP):
        imp = imp + p[:, h * tq:(h + 1) * tq]
    p_slc = _split_dot(mt_ref[...], imp)

    j_idx = lax.broadcasted_iota(jnp.int32, (nsp, tq), 0)
    cur = tpos1 // L_SLC
    valid = j_idx <= cur
    forced = (j_idx == 0) | (j_idx == cur) | (j_idx == cur - 1)
    score = jnp.where(valid & forced, jnp.inf, jnp.where(valid, p_slc, -jnp.inf))
    bits = lax.bitcast_convert_type(score, jnp.int32)
    key = jnp.where(bits >= 0, bits, bits ^ 0x7FFFFFFF)
    n_grp = nsp // 8
    keys = [key[8 * r:8 * r + 8] for r in range(n_grp)]
    keys_m1 = [k - 1 for k in keys]
    sub = lax.broadcasted_iota(jnp.int32, (8, tq), 0)

    def count_group(grp, ranks):
        ranks = list(ranks)
        for u in range(8):
            row = jnp.broadcast_to(keys[grp][u:u + 1, :], (8, tq))
            for r in range(n_grp):
                thr = keys[r] if r < grp else keys_m1[r] if r > grp else jnp.where(sub > u, keys_m1[r], keys[r])
                ranks[r] = ranks[r] + jnp.where(row > thr, 1.0, 0.0)
        return tuple(ranks)

    ranks = tuple(jnp.zeros((8, tq), F32) for _ in range(n_grp))
    last_valid = (i * tq + tq - 1) // L_SLC
    for grp in range(n_grp):
        ranks = lax.cond(8 * grp <= last_valid, functools.partial(count_group, grp), lambda r: r, ranks)
    bias = jnp.where(jnp.concatenate(ranks, axis=0) < n_sel, 0.0, NEG)
    qa = jnp.concatenate([qt, jnp.concatenate([bias] * GRP, axis=1).astype(BF16)], axis=0)

    row_k = lax.broadcasted_iota(jnp.int32, (KT, lanes), 0)

    def slc_scores(st, buf):
        start = pl.multiple_of(st * KT, KT)
        buf[0:KT, :] = _dot_tn(ksa_ref[:, pl.ds(start, KT)], qa)

    def slc_softmax(st, buf, carry, diagonal=False):
        m, acc = carry
        s = buf[0:KT, :]
        if diagonal:
            s = jnp.where(st * KT + row_k <= tpos, s, NEG)
        m_new = jnp.maximum(m, jnp.max(col_max(jnp.full((8, lanes), NEG, F32), s), axis=0, keepdims=True))
        p = jnp.exp2(s - m_new).astype(BF16)
        pv = _dot(vsa_ref[:, pl.ds(pl.multiple_of(st * KT, KT), KT)], p)
        return m_new, acc * jnp.exp2(m - m_new) + pv

    def slc_pair(u, carry):
        slc_scores(2 * u + 1, s2_scr)
        carry = slc_softmax(2 * u, s_scr, carry)
        slc_scores(2 * u + 2, s_scr)
        return slc_softmax(2 * u + 1, s2_scr, carry)

    def slc_tail_odd(carry):
        slc_scores(n_full, s2_scr)
        carry = slc_softmax(n_full - 1, s_scr, carry)
        return slc_softmax(n_full, s2_scr, carry, diagonal=True)

    def slc_tail_even(carry):
        return slc_softmax(n_full, s_scr, carry, diagonal=True)

    n_full = (i * tq) // KT
    slc_scores(0, s_scr)
    init = (jnp.full((1, lanes), NEG, F32), jnp.zeros((HD + V_PAD, lanes), F32))
    carry = lax.fori_loop(0, n_full // 2, slc_pair, init)
    o_s = finish(lax.cond(n_full % 2 == 1, slc_tail_odd, slc_tail_even, carry)[1])

    n_wt = WINDOW // tq + 1
    w0 = jnp.maximum(i - (n_wt - 1), 0) * tq
    row_w = lax.broadcasted_iota(jnp.int32, (tq, lanes), 0)
    col_w = jnp.concatenate([lax.broadcasted_iota(jnp.int32, (tq, tq), 1)] * GRP, axis=1)

    def window(steady):
        mx = jnp.full((8, lanes), NEG, F32)
        for r in range(n_wt):
            start = pl.multiple_of(w0 + r * tq, tq)
            s = _dot_tn(kwb_ref[:, pl.ds(start, tq)], qt)
            if not steady:
                rel = tpos - (start + row_w)
                s = jnp.where((rel >= 0) & (rel <= WINDOW), s, NEG)
            elif r == 0:
                s = jnp.where(row_w >= col_w, s, NEG)
            elif r == n_wt - 1:
                s = jnp.where(row_w <= col_w, s, NEG)
            s_scr[r * tq:(r + 1) * tq, :] = s
            mx = col_max(mx, s)
        m = jnp.max(mx, axis=0, keepdims=True)
        acc = jnp.zeros((HD + V_PAD, lanes), F32)
        for r in range(n_wt):
            start = pl.multiple_of(w0 + r * tq, tq)
            p = jnp.exp2(s_scr[r * tq:(r + 1) * tq, :] - m).astype(BF16)
            acc = acc + _dot(vwa_ref[:, pl.ds(start, tq)], p)
        return finish(acc)

    o_w = lax.cond(i >= n_wt - 1, functools.partial(window, True), functools.partial(window, False))

    for h in range(GRP):
        ls = slice(h * tq, (h + 1) * tq)
        o = (gt_ref[3 * h:3 * h + 1, :] * o_c[:, ls] + gt_ref[3 * h + 1:3 * h + 2, :] * o_s[:, ls]
             + gt_ref[3 * h + 2:3 * h + 3, :] * o_w[:, ls])
        o_ref[h] = o.astype(BF16)


def _slc_matrix(nsp, ncp, n_cmp):
    ratio = L_SLC // CMP_STRIDE
    j = jnp.arange(nsp)[:, None]
    n = jnp.arange(ncp)[None, :]
    m = ((n >= ratio * j) & (n <= ratio * j + ratio - 1)).astype(F32)
    m = m + ((n >= ratio * j - 1) & (n <= ratio * j + ratio - 2)).astype(F32)
    return jnp.where(n < n_cmp, m, 0.0).astype(BF16)


def _attn_prompt(qt, kct, vct, ksa, vsa, kwb, vwa, gt, tq=2 * LANES):
    b, _, _, t = qt.shape
    assert t % KT == 0 and WINDOW % tq == 0 and t >= WINDOW + tq
    ncp = kct.shape[-1]
    n_cmp = t // CMP_STRIDE - 1
    ns = -(-t // L_SLC)
    mt = _slc_matrix(_sel_rows(t), ncp, n_cmp)
    res = lambda a: pl.BlockSpec((None, None) + a.shape[2:], lambda i, g, j: (i, g, 0, 0))
    return pl.pallas_call(
        functools.partial(_attn_prompt_kernel, tq=tq, n_cmp=n_cmp, n_sel=min(N_SEL, ns)),
        grid=(b, KVH, t // tq),
        in_specs=[pl.BlockSpec((None, GRP, HD, tq), lambda i, g, j: (i, g, 0, j)),
                  res(kct), res(vct), res(ksa), res(vsa), res(kwb), res(vwa),
                  pl.BlockSpec((None, None, 16, tq), lambda i, g, j: (i, g, 0, j)),
                  pl.BlockSpec(mt.shape, lambda i, g, j: (0, 0))],
        out_specs=pl.BlockSpec((None, GRP, HD, tq), lambda i, g, j: (i, g, 0, j)),
        out_shape=jax.ShapeDtypeStruct(qt.shape, BF16),
        scratch_shapes=[pltpu.VMEM((max(KT, WINDOW + tq), GRP * tq), F32), pltpu.VMEM((KT, GRP * tq), F32)],
        compiler_params=_params("parallel", "parallel", "arbitrary"),
        name="nsa_attn_prompt",
    )(qt, kct, vct, ksa, vsa, kwb, vwa, gt, mt)


def _attn_sample_kernel(*refs, n_pages, n_cmp, n_blocks, n_sel):
    refs = refs[1:]
    k_pages = refs[:n_pages]
    v_pages = refs[n_pages:2 * n_pages]
    (q_ref, kc_ref, vc_ref, kw_ref, vw_ref, kns_ref, vns_ref, knw_ref, vnw_ref, gt_ref, mt_ref,
     o_ref, sel_scr, oc_scr, ow_scr, m_scr, l_scr, acc_scr) = refs[2 * n_pages:]
    p = pl.program_id(1)
    ncp = kc_ref.shape[-1]
    nsp = mt_ref.shape[0]
    rows_all = KVH * HD
    first_row = lax.broadcasted_iota(jnp.int32, (LANES, LANES), 0) == 0
    qt = q_ref[...]

    def stacked(ref):
        return ref[...].reshape(rows_all, ref.shape[-1]).astype(BF16)

    def finish(l, acc):
        return acc / jnp.maximum(l, 1e-30)

    def init():
        return jnp.full((1, LANES), NEG, F32), jnp.zeros((1, LANES), F32), jnp.zeros((rows_all, LANES), F32)

    @pl.when(p == 0)
    def _():
        lane = lax.broadcasted_iota(jnp.int32, (ncp, LANES), 1)
        n_idx = lax.broadcasted_iota(jnp.int32, (ncp, LANES), 0)
        j_col = lax.broadcasted_iota(jnp.int32, (nsp, LANES), 0)
        lane_s = lax.broadcasted_iota(jnp.int32, (nsp, LANES), 1)
        jp_idx = lax.broadcasted_iota(jnp.int32, (nsp, nsp), 0)
        j_idx = lax.broadcasted_iota(jnp.int32, (nsp, nsp), 1)
        cur = n_blocks - 1
        s = _dot_tn(stacked(kc_ref), qt)
        m, l, acc, pr = _softmax_update(s, n_idx < n_cmp, stacked(vc_ref), *init())
        oc_scr[...] = finish(l, acc)
        pr = pr / jnp.maximum(l, 1e-30)
        sel_all = jnp.zeros((nsp, LANES), F32)
        for g in range(KVH):
            mine = (lane >= g * GRP) & (lane < (g + 1) * GRP)
            imp = jnp.sum(jnp.where(mine, pr, 0.0), axis=1, keepdims=True)
            p_slc = _split_dot(mt_ref[...], jnp.broadcast_to(imp, (ncp, LANES)))
            valid = j_col <= cur
            forced = (j_col == 0) | (j_col == cur) | (j_col == cur - 1)
            score = jnp.where(valid & forced, jnp.inf, jnp.where(valid, p_slc, -jnp.inf))
            col = jnp.concatenate([score] * (nsp // LANES), axis=1)
            row = col.T
            ahead = (col > row) | ((col == row) & (jp_idx < j_idx))
            rank = jnp.sum(jnp.where(ahead, 1.0, 0.0), axis=0, keepdims=True)
            sel = jnp.broadcast_to(jnp.where(rank < n_sel, 1.0, 0.0), (LANES, nsp)).T
            sel_all = jnp.where((lane_s >= g * GRP) & (lane_s < (g + 1) * GRP), sel, sel_all)
        sel_scr[...] = sel_all
        s = _dot_tn(stacked(kw_ref), qt)
        st = _softmax_update(s, jnp.full(s.shape, True), stacked(vw_ref), *init())[:3]
        s = _dot_tn(stacked(knw_ref), qt)
        _, l, acc = _softmax_update(s, first_row, stacked(vnw_ref), *st)[:3]
        ow_scr[...] = finish(l, acc)
        m_scr[...], l_scr[...], acc_scr[...] = init()

    blocks_per_page = PAGE // L_SLC
    kt = jnp.concatenate([k_pages[j][...].reshape(rows_all, PAGE) for j in range(n_pages)], axis=1).astype(BF16)
    vt = jnp.concatenate([v_pages[j][...].reshape(rows_all, PAGE) for j in range(n_pages)], axis=1).astype(BF16)
    s = _dot_tn(kt, qt)
    rows = []
    for r in range(n_pages * blocks_per_page):
        e = sel_scr[pl.ds(p * n_pages * blocks_per_page + r, 1), :]
        rows.append(jnp.broadcast_to(e, (L_SLC, LANES)))
    mask = jnp.concatenate(rows, axis=0) > 0.5
    m_scr[...], l_scr[...], acc_scr[...] = _softmax_update(s, mask, vt, m_scr[...], l_scr[...], acc_scr[...])[:3]

    @pl.when(p == pl.num_programs(1) - 1)
    def _():
        s = _dot_tn(stacked(kns_ref), qt)
        _, l, acc = _softmax_update(s, first_row, stacked(vns_ref), m_scr[...], l_scr[...], acc_scr[...])[:3]
        o_ref[...] = (gt_ref[0:1, :] * oc_scr[...] + gt_ref[1:2, :] * finish(l, acc) + gt_ref[2:3, :] * ow_scr[...])


def _attn_sample(qs, kct, vct, pool_k, pool_v, page_table, layer, kwt, vwt, kns, vns, knw, vnw, gs,
                 pages_per_step=16):
    b, n_pages_total = page_table.shape
    pages_per_step = min(pages_per_step, n_pages_total)
    past = n_pages_total * PAGE
    n_cmp = past // CMP_STRIDE - 1
    ncp = kct.shape[-1]
    n_blocks = past // L_SLC + 1
    nsp = -(-n_blocks // LANES) * LANES
    mt = _slc_matrix(nsp, ncp, n_cmp)
    steps = n_pages_total // pages_per_step
    page_spec = lambda j: pl.BlockSpec((None, None, KVH, HD, PAGE),
                                       lambda i, p, pt: (layer, pt[i, p * pages_per_step + j], 0, 0, 0))
    per_b = lambda a: pl.BlockSpec((None,) + a.shape[1:], lambda i, p, pt: (i,) + (0,) * (a.ndim - 1))
    small = [qs, kct, vct, kwt, vwt, kns, vns, knw, vnw, gs]
    grid_spec = pltpu.PrefetchScalarGridSpec(
        num_scalar_prefetch=1,
        grid=(b, steps),
        in_specs=[page_spec(j) for j in range(pages_per_step)] * 2 + [per_b(a) for a in small]
        + [pl.BlockSpec(mt.shape, lambda i, p, pt: (0, 0))],
        out_specs=pl.BlockSpec((None, KVH * HD, LANES), lambda i, p, pt: (i, 0, 0)),
        scratch_shapes=[pltpu.VMEM((nsp, LANES), F32), pltpu.VMEM((KVH * HD, LANES), F32),
                        pltpu.VMEM((KVH * HD, LANES), F32), pltpu.VMEM((1, LANES), F32),
                        pltpu.VMEM((1, LANES), F32), pltpu.VMEM((KVH * HD, LANES), F32)],
    )
    return pl.pallas_call(
        functools.partial(_attn_sample_kernel, n_pages=pages_per_step, n_cmp=n_cmp, n_blocks=n_blocks,
                          n_sel=min(N_SEL, n_blocks)),
        grid_spec=grid_spec,
        out_shape=jax.ShapeDtypeStruct((b, KVH * HD, LANES), F32),
        compiler_params=_params("parallel", "arbitrary"),
        name="nsa_attn_sample",
    )(page_table, *([pool_k] * pages_per_step), *([pool_v] * pages_per_step), *small, mt)


def _rope_tables(pos):
    inv_freq = ROPE_THETA ** (-jnp.arange(ROT_HALF, dtype=F32) * 2.0 / (2 * ROT_HALF))
    ang = pos.astype(F32)[:, None] * inv_freq[None, :]
    return jnp.cos(ang).T, jnp.sin(ang).T


def _nsa_layer(xp, xs, pools, win_bufs, page_table, layer, norm_g, w_in, b_gate, g_q, g_k, g_kcmp,
               cmp_pe, cmp_w1, cmp_b1, cmp_w2, w_out):
    b, t, d = xp.shape
    bs = xs.shape[0]
    past = page_table.shape[1] * PAGE
    nq, nkv = GRP * KVH * HD, KVH * HD
    n_gate = 3 * GRP * KVH

    wt_g = jnp.pad(w_in[:, nq + 6 * nkv:].T.reshape(KVH, n_gate // KVH, d), ((0, 0), (0, 16 - n_gate // KVH), (0, 0)))
    wt = jnp.concatenate([w_in[:, :nq + 6 * nkv].T, wt_g.reshape(KVH * 16, d)], axis=0).astype(BF16)
    bg = jnp.pad(b_gate.reshape(KVH, n_gate // KVH), ((0, 0), (0, 16 - n_gate // KVH))).reshape(KVH * 16, 1)
    gn = norm_g.reshape(1, d)
    wo = w_out.astype(BF16)

    def project(x, pos, tm):
        cos, sin = _rope_tables(pos)
        col = lambda v: jnp.broadcast_to(v[..., None], v.shape + (tm,))
        return _nsa_proj(x, gn, wt, col(g_q), col(g_k), col(bg[:, 0]), cos, sin, tm)

    tm = min(512, t)
    qt, kct_rows, vct_rows, kst, vst, kwt, vwt, gt, ksa, vsa, kwb, vwa = project(xp, jnp.arange(t), tm)
    pages = t // PAGE
    nh = pages * (PAGE // CMP_STRIDE)
    cmp_w = [_cmp_weights(cmp_pe[i], cmp_w1[i], cmp_b1[i], cmp_w2[i], g_kcmp, nh) for i in range(2)]
    pps = min(16, pages)
    kct = _cmp_out(_cmp_fs(kct_rows, cmp_w[0][0], pages, pps), *cmp_w[0][1:], is_key=True)
    vct = _cmp_out(_cmp_fs(vct_rows, cmp_w[1][0], pages, pps), *cmp_w[1][1:], is_key=False)
    ot = _attn_prompt(qt, kct, vct, ksa, vsa, kwb, vwa, gt)
    yp = _nsa_out(ot, xp, wo, tm)
    n_win = min(WINDOW, t)
    rows_p = [kct_rows, vct_rows, kst, vst, kwt[..., t - n_win:], vwt[..., t - n_win:]]

    xs_pad = jnp.pad(xs.reshape(1, bs, d), ((0, 0), (0, LANES - bs), (0, 0)))
    outs = project(xs_pad, jnp.full((LANES,), past), LANES)
    qt_s, rows_s, gt_s = outs[0], outs[1:7], outs[7]
    pages_s = page_table.shape[1]
    nh_s = pages_s * (PAGE // CMP_STRIDE)
    cmp_ws = [_cmp_weights(cmp_pe[i], cmp_w1[i], cmp_b1[i], cmp_w2[i], g_kcmp, nh_s) for i in range(2)]
    pool5 = [jnp.transpose(pl_, (0, 1, 3, 4, 2)) for pl_ in pools]
    pps = min(32, pages_s)
    kct_s = _cmp_out(_cmp_fs(pool5[0], cmp_ws[0][0], pages_s, pps, page_table, layer), *cmp_ws[0][1:], is_key=True)
    vct_s = _cmp_out(_cmp_fs(pool5[1], cmp_ws[1][0], pages_s, pps, page_table, layer), *cmp_ws[1][1:], is_key=False)
    qs = jnp.transpose(qt_s[0, :, :, :bs].reshape(KVH, GRP, HD, bs), (3, 0, 2, 1))
    qs = jnp.einsum("bgdh,gk->bgdkh", qs, jnp.eye(KVH, dtype=qs.dtype))
    qs = jnp.pad(qs.reshape(bs, KVH * HD, KVH * GRP), ((0, 0), (0, 0), (0, LANES - KVH * GRP)))
    new = lambda a: jnp.pad(jnp.transpose(a[0, :, :, :bs], (2, 0, 1))[..., None],
                            ((0, 0), (0, 0), (0, 0), (0, LANES - 1)))
    gs = jnp.transpose(gt_s[0, :, :n_gate // KVH, :bs].reshape(KVH, GRP, 3, bs), (3, 2, 0, 1))
    gs = jnp.pad(gs.reshape(bs, 3, KVH * GRP), ((0, 0), (0, 8 - 3), (0, LANES - KVH * GRP)))
    win5 = [jnp.transpose(wb[layer], (0, 2, 3, 1)) for wb in win_bufs]
    ot_s = _attn_sample(qs, kct_s, vct_s, pool5[2], pool5[3], page_table, layer, win5[0], win5[1],
                        new(rows_s[2]), new(rows_s[3]), new(rows_s[4]), new(rows_s[5]), gs)
    ot_s = ot_s.reshape(bs, KVH, HD, LANES)
    ot_s = jnp.stack([ot_s[:, g, :, g * GRP:(g + 1) * GRP] for g in range(KVH)])
    ot_s = jnp.transpose(ot_s, (0, 3, 2, 1)).reshape(1, GRP * KVH, HD, bs)
    ot_s = jnp.pad(ot_s, ((0, 0), (0, 0), (0, 0), (0, LANES - bs))).astype(BF16)
    ys = _nsa_out(ot_s, xs_pad, wo, LANES)[0, :bs].reshape(bs, 1, d)

    to_rows = lambda a: jnp.transpose(a, (0, 3, 1, 2))
    rows_p = [to_rows(a) for a in rows_p]
    rows_s = [to_rows(a[:, :, :, :bs]).reshape(bs, 1, KVH, HD) for a in rows_s]
    return yp, ys, rows_p, rows_s


def kernel(x_prompt, x_sample, state_gla, cache_k_cmp, cache_v_cmp, cache_k_slc, cache_v_slc, cache_k_win, cache_v_win, page_table, norm_mix, norm_mlp, mlp_up, mlp_down, gla_w_in, gla_w_gate2, gla_b_gate, gla_g_out, gla_w_out, nsa_w_in, nsa_b_gate, nsa_g_q, nsa_g_k, nsa_g_kcmp, nsa_cmp_pe, nsa_cmp_w1, nsa_cmp_b1, nsa_cmp_w2, nsa_w_out):
    depth = norm_mix.shape[0]
    b, t, d = x_prompt.shape
    bs = x_sample.shape[0]
    xp, xs = x_prompt, x_sample
    gla_p, gla_s, nsa_p, nsa_s = [], [], [], []
    pools = (cache_k_cmp, cache_v_cmp, cache_k_slc, cache_v_slc)
    for i in range(depth):
        j = i // 2
        if i % 2 == 0:
            xp, xs, sp, ss = _gla_layer(xp, xs, state_gla[j], norm_mix[i], gla_w_in[j], gla_w_gate2[j],
                                        gla_b_gate[j], gla_g_out[j], gla_w_out[j])
            gla_p.append(sp)
            gla_s.append(ss)
        else:
            xp, xs, rp, rs = _nsa_layer(xp, xs, pools, (cache_k_win, cache_v_win), page_table, j, norm_mix[i],
                                        nsa_w_in[j], nsa_b_gate[j], nsa_g_q[j], nsa_g_k[j], nsa_g_kcmp[j],
                                        nsa_cmp_pe[j], nsa_cmp_w1[j], nsa_cmp_b1[j], nsa_cmp_w2[j], nsa_w_out[j])
            nsa_p.append(rp)
            nsa_s.append(rs)
        g = norm_mlp[i].reshape(1, d)
        wu, wd = mlp_up[i].astype(BF16), mlp_down[i].astype(BF16)
        xp = _mlp(xp.reshape(b * t, d), g, wu, wd, tm=512).reshape(b, t, d)
        xs = _mlp(xs.reshape(bs, d), g, wu, wd, tm=bs).reshape(bs, 1, d)
    stack = lambda lst, r: jnp.stack([e[r] for e in lst])
    return (xp, xs, jnp.stack(gla_p), jnp.stack(gla_s),
            stack(nsa_p, 0), stack(nsa_s, 0), stack(nsa_p, 1), stack(nsa_s, 1),
            stack(nsa_p, 2), stack(nsa_s, 2), stack(nsa_p, 3), stack(nsa_s, 3),
            stack(nsa_p, 4), stack(nsa_s, 4), stack(nsa_p, 5), stack(nsa_s, 5))
```

```python
import functools

import jax
import jax.numpy as jnp
from jax import lax
from jax.experimental import pallas as pl
from jax.experimental.pallas import tpu as pltpu

F32 = jnp.float32
BF16 = jnp.bfloat16
EPS = 1e-6
NEG = -1e30
VMEM_LIMIT_BYTES = 48 * 1024 * 1024
LANES = 128
PAGE = 128
HD = 64
KVH = 4
GRP = 4
L_CMP, CMP_STRIDE, L_SLC, N_SEL, WINDOW = 32, 16, 64, 16, 512
GLA_H, GLA_DK, GLA_DV, GLA_CHUNK = 4, 128, 256, 64
ROPE_THETA, ROT_HALF = 500000.0, 8
LOG2E = 1.4426950408889634
V_PAD = 16
KT = 512


def _params(*sem):
    return pltpu.CompilerParams(dimension_semantics=sem, vmem_limit_bytes=VMEM_LIMIT_BYTES)


def _dot(a, b):
    return jnp.dot(a, b, preferred_element_type=F32)


def _dot_nt(a, b):
    return lax.dot_general(a, b, (((1,), (1,)), ((), ())), preferred_element_type=F32)


def _dot_tn(a, b):
    return lax.dot_general(a, b, (((0,), (0,)), ((), ())), preferred_element_type=F32)


def _rms_rows(x, g):
    ms = jnp.mean(x * x, axis=-1, keepdims=True)
    return x * lax.rsqrt(ms + EPS) * g


def _mlp_kernel(x_ref, g_ref, wu_ref, wd_ref, o_ref, h_scr, acc_scr):
    f = pl.program_id(1)

    @pl.when(f == 0)
    def _():
        h_scr[...] = _rms_rows(x_ref[...], g_ref[...]).astype(BF16)
        acc_scr[...] = jnp.zeros_like(acc_scr)

    u = jnp.maximum(_dot(h_scr[...], wu_ref[...]), 0.0)
    acc_scr[...] += _dot((u * u).astype(BF16), wd_ref[...])

    @pl.when(f == pl.num_programs(1) - 1)
    def _():
        o_ref[...] = x_ref[...] + acc_scr[...]


def _mlp(x, g, wu, wd, tm, tf=1024):
    m, d = x.shape
    ff = wu.shape[1]
    return pl.pallas_call(
        _mlp_kernel,
        grid=(m // tm, ff // tf),
        in_specs=[pl.BlockSpec((tm, d), lambda i, f: (i, 0)),
                  pl.BlockSpec((1, d), lambda i, f: (0, 0)),
                  pl.BlockSpec((d, tf), lambda i, f: (0, f)),
                  pl.BlockSpec((tf, d), lambda i, f: (f, 0))],
        out_specs=pl.BlockSpec((tm, d), lambda i, f: (i, 0)),
        out_shape=jax.ShapeDtypeStruct((m, d), F32),
        scratch_shapes=[pltpu.VMEM((tm, d), BF16), pltpu.VMEM((tm, d), F32)],
        compiler_params=_params("parallel", "arbitrary"),
        name="mlp_block",
    )(x, g, wu, wd)


def _gla_proj_kernel(x_ref, g_ref, w_ref, wg2_ref, bg_ref, q_ref, k_ref, v_ref, r_ref, gl_ref):
    h = _rms_rows(x_ref[...], g_ref[...]).astype(BF16)
    dk, dv = GLA_H * GLA_DK, GLA_H * GLA_DV
    q_ref[...] = _dot(h, w_ref[:, 0:dk]) * (GLA_DK ** -0.5)
    k_ref[...] = _dot(h, w_ref[:, dk:2 * dk])
    v_ref[...] = _dot(h, w_ref[:, 2 * dk:2 * dk + dv])
    r_ref[...] = _dot(h, w_ref[:, 2 * dk + dv:2 * dk + 2 * dv])
    gr = _dot(h, w_ref[:, 2 * dk + 2 * dv:])
    xg = _dot(gr.astype(BF16), wg2_ref[...]) + bg_ref[...]
    gl_ref[...] = jax.nn.log_sigmoid(xg) * (1.0 / 16.0)


def _gla_proj(x, g, w, wg2, bg, tm):
    m, d = x.shape
    dk, dv = GLA_H * GLA_DK, GLA_H * GLA_DV
    row = lambda n: pl.BlockSpec((tm, n), lambda i: (i, 0))
    full = lambda a: pl.BlockSpec(a.shape, lambda i: (0, 0))
    return pl.pallas_call(
        _gla_proj_kernel,
        grid=(m // tm,),
        in_specs=[row(d), full(g), full(w), full(wg2), full(bg)],
        out_specs=[row(dk), row(dk), row(dv), row(dv), row(dk)],
        out_shape=[jax.ShapeDtypeStruct((m, n), F32) for n in (dk, dk, dv, dv, dk)],
        compiler_params=_params("parallel"),
        name="gla_proj",
    )(x, g, w, wg2, bg)


def _gla_scan_kernel(q_ref, k_ref, g_ref, v_ref, o_ref, s_ref, st_scr, *, n_chunks, n_heads):
    c_len = GLA_CHUNK
    step = pl.program_id(2)

    @pl.when(step == 0)
    def _():
        st_scr[...] = jnp.zeros_like(st_scr)

    rowi = lax.broadcasted_iota(jnp.int32, (c_len, GLA_DK), 0)
    causal = (lax.broadcasted_iota(jnp.int32, (c_len, c_len), 0)
              >= lax.broadcasted_iota(jnp.int32, (c_len, c_len), 1))

    def body(c, carry):
        sl = pl.ds(pl.multiple_of(c * c_len, c_len), c_len)
        for h in range(n_heads):
            ks = slice(h * GLA_DK, (h + 1) * GLA_DK)
            vs = slice(h * GLA_DV, (h + 1) * GLA_DV)
            b = g_ref[sl, ks]
            sh = 1
            while sh < c_len:
                b = b + jnp.where(rowi >= sh, pltpu.roll(b, sh, 0), 0.0)
                sh *= 2
            b_last = b[c_len - 1:c_len, :]
            b_mid = b[c_len // 2 - 1:c_len // 2, :]
            q = q_ref[sl, ks]
            k = k_ref[sl, ks]
            v = v_ref[sl, vs].astype(BF16)
            qe = (q * jnp.exp(b)).astype(BF16)
            qa = (q * jnp.exp(b - b_mid)).astype(BF16)
            ka = (k * jnp.exp(b_mid - b)).astype(BF16)
            kd = (k * jnp.exp(b_last - b)).astype(BF16)
            a = jnp.where(causal, _dot_nt(qa, ka), 0.0)
            st = st_scr[h]
            o_ref[sl, vs] = _dot_nt(qe, st.astype(BF16)) + _dot(a.astype(BF16), v)
            st_scr[h] = st * jnp.exp(b_last) + _dot_tn(v, kd)
        return carry

    lax.fori_loop(0, n_chunks, body, 0)

    @pl.when(step == pl.num_programs(2) - 1)
    def _():
        for h in range(n_heads):
            s_ref[h] = st_scr[h].T


def _gla_scan(q, k, gl, v, n_heads=4, t_blk=1024):
    b, t, _ = q.shape
    t_blk = min(t_blk, t)
    kq = pl.BlockSpec((None, t_blk, n_heads * GLA_DK), lambda i, h, j: (i, j, h))
    vv = pl.BlockSpec((None, t_blk, n_heads * GLA_DV), lambda i, h, j: (i, j, h))
    return pl.pallas_call(
        functools.partial(_gla_scan_kernel, n_chunks=t_blk // GLA_CHUNK, n_heads=n_heads),
        grid=(b, GLA_H // n_heads, t // t_blk),
        in_specs=[kq, kq, kq, vv],
        out_specs=[vv, pl.BlockSpec((None, n_heads, GLA_DK, GLA_DV), lambda i, h, j: (i, h, 0, 0))],
        out_shape=[jax.ShapeDtypeStruct((b, t, GLA_H * GLA_DV), F32),
                   jax.ShapeDtypeStruct((b, GLA_H, GLA_DK, GLA_DV), F32)],
        scratch_shapes=[pltpu.VMEM((n_heads, GLA_DV, GLA_DK), F32)],
        compiler_params=_params("parallel", "parallel", "arbitrary"),
        name="gla_scan",
    )(q, k, gl, v)


def _gla_step_kernel(q_ref, k_ref, g_ref, v_ref, s0_ref, o_ref, s_ref):
    def col(x):
        return jnp.broadcast_to(x, (LANES, LANES)).T

    for h in range(GLA_H):
        ks = slice(h * GLA_DK, (h + 1) * GLA_DK)
        qc, kc, ec = col(q_ref[:, ks]), col(k_ref[:, ks]), col(jnp.exp(g_ref[:, ks]))
        for half in range(GLA_DV // LANES):
            vs = slice(h * GLA_DV + half * LANES, h * GLA_DV + (half + 1) * LANES)
            ss = slice(half * LANES, (half + 1) * LANES)
            sn = ec * s0_ref[h, :, ss] + kc * v_ref[:, vs]
            s_ref[h, :, ss] = sn
            o_ref[:, vs] = jnp.sum(qc * sn, axis=0, keepdims=True)


def _gla_step(q, k, gl, v, s0):
    b = q.shape[0]
    kq = pl.BlockSpec((None, 1, GLA_H * GLA_DK), lambda i: (i, 0, 0))
    vv = pl.BlockSpec((None, 1, GLA_H * GLA_DV), lambda i: (i, 0, 0))
    st = pl.BlockSpec((None, GLA_H, GLA_DK, GLA_DV), lambda i: (i, 0, 0, 0))
    return pl.pallas_call(
        _gla_step_kernel,
        grid=(b,),
        in_specs=[kq, kq, kq, vv, st],
        out_specs=[vv, st],
        out_shape=[jax.ShapeDtypeStruct((b, 1, GLA_H * GLA_DV), F32),
                   jax.ShapeDtypeStruct((b, GLA_H, GLA_DK, GLA_DV), F32)],
        compiler_params=_params("parallel"),
        name="gla_step",
    )(q, k, gl, v, s0)


def _gla_out_kernel(o_ref, r_ref, x_ref, go_ref, w_ref, y_ref):
    parts = []
    for h in range(GLA_H):
        sl = slice(h * GLA_DV, (h + 1) * GLA_DV)
        r = r_ref[:, sl]
        parts.append((_rms_rows(o_ref[:, sl], go_ref[...]) * (r * jax.nn.sigmoid(r))).astype(BF16))
    y_ref[...] = x_ref[...] + _dot(jnp.concatenate(parts, axis=1), w_ref[...])


def _gla_out(o, r, x, go, w, tm):
    m, d = x.shape
    row = pl.BlockSpec((tm, d), lambda i: (i, 0))
    full = lambda a: pl.BlockSpec(a.shape, lambda i: (0, 0))
    return pl.pallas_call(
        _gla_out_kernel,
        grid=(m // tm,),
        in_specs=[row, row, row, full(go), full(w)],
        out_specs=row,
        out_shape=jax.ShapeDtypeStruct((m, d), F32),
        compiler_params=_params("parallel"),
        name="gla_out",
    )(o, r, x, go, w)


def _gla_layer(xp, xs, s0, norm_g, w_in, w_gate2, b_gate, g_out, w_out):
    b, t, d = xp.shape
    bs = xs.shape[0]
    dk, dv = GLA_H * GLA_DK, GLA_H * GLA_DV
    rank = w_gate2.shape[0]
    w = jnp.pad(w_in, ((0, 0), (0, LANES - rank))).astype(BF16)
    wg2 = jnp.pad(w_gate2, ((0, LANES - rank), (0, 0))).astype(BF16)
    g = norm_g.reshape(1, d)
    bg = b_gate.reshape(1, dk)
    go = g_out.reshape(1, GLA_DV)
    wo = w_out.astype(BF16)
    x2 = xp.reshape(b * t, d)
    q, k, v, r, gl = _gla_proj(x2, g, w, wg2, bg, tm=512)
    o, sp = _gla_scan(q.reshape(b, t, dk), k.reshape(b, t, dk), gl.reshape(b, t, dk), v.reshape(b, t, dv))
    yp = _gla_out(o.reshape(b * t, dv), r, x2, go, wo, tm=512).reshape(b, t, d)
    xs2 = xs.reshape(bs, d)
    q, k, v, r, gl = _gla_proj(xs2, g, w, wg2, bg, tm=bs)
    o, ss = _gla_step(q.reshape(bs, 1, dk), k.reshape(bs, 1, dk), gl.reshape(bs, 1, dk), v.reshape(bs, 1, dv), s0)
    ys = _gla_out(o.reshape(bs, dv), r, xs2, go, wo, tm=bs).reshape(bs, 1, d)
    return yp, ys, sp, ss


def _nsa_proj_kernel(x_ref, gn_ref, wt_ref, gq_ref, gk_ref, bg_ref, cos_ref, sin_ref,
                     q_ref, kc_ref, vc_ref, ks_ref, vs_ref, kw_ref, vw_ref, gt_ref,
                     ksa_ref, vsa_ref, kwb_ref, vwa_ref):
    tm = x_ref.shape[0]
    nq = GRP * KVH * HD
    nkv = KVH * HD
    h = _rms_rows(x_ref[...], gn_ref[...]).astype(BF16)
    cos = cos_ref[...][None]
    sin = sin_ref[...][None]

    def norm_rope(z, g, nh):
        z3 = z.reshape(nh, HD, tm)
        y = z3 * lax.rsqrt(jnp.mean(z3 * z3, axis=1, keepdims=True) + EPS) * g[None]
        x1 = y[:, 0:ROT_HALF, :]
        x2 = y[:, ROT_HALF:2 * ROT_HALF, :]
        return jnp.concatenate([x1 * cos - x2 * sin, x1 * sin + x2 * cos, y[:, 2 * ROT_HALF:, :]], axis=1)

    zq = _dot_nt(wt_ref[0:nq, :], h)
    q_ref[...] = (norm_rope(zq, gq_ref[...], GRP * KVH) * (HD ** -0.5 * LOG2E)).astype(BF16)
    zkv = _dot_nt(wt_ref[nq:nq + 6 * nkv, :], h)
    outs = (kc_ref, vc_ref, ks_ref, vs_ref, kw_ref, vw_ref)
    rows = []
    for i in range(6):
        z = zkv[i * nkv:(i + 1) * nkv, :]
        rows.append(norm_rope(z, gk_ref[i // 2], KVH) if i % 2 == 0 else z.reshape(KVH, HD, tm))
        outs[i][...] = rows[i]
    zg = _dot_nt(wt_ref[nq + 6 * nkv:, :], h) + bg_ref[...]
    gt_ref[...] = jax.nn.sigmoid(zg).reshape(KVH, 16, tm)

    nsp = ksa_ref.shape[1] - HD
    blk = lax.broadcasted_iota(jnp.int32, (KVH, nsp, tm), 1)
    tok = pl.program_id(1) * tm + lax.broadcasted_iota(jnp.int32, (KVH, nsp, tm), 2)
    onehot = jnp.where(tok // L_SLC == blk, 1.0, 0.0)
    ksa_ref[...] = jnp.concatenate([rows[2], onehot], axis=1).astype(BF16)
    kwb_ref[...] = rows[4].astype(BF16)
    ones = jnp.where(lax.broadcasted_iota(jnp.int32, (KVH, V_PAD, tm), 1) == 0, 1.0, 0.0)
    vsa_ref[...] = jnp.concatenate([rows[3], ones], axis=1).astype(BF16)
    vwa_ref[...] = jnp.concatenate([rows[5], ones], axis=1).astype(BF16)


def _sel_rows(t):
    return -(-(-(-t // L_SLC)) // 16) * 16


def _nsa_proj(x, gn, wt, gq, gk, bg, cos, sin, tm):
    b, t, d = x.shape
    full = lambda a: pl.BlockSpec(a.shape, lambda i, j: (0,) * a.ndim)
    rows_spec = lambda n: pl.BlockSpec((None, KVH, n, tm), lambda i, j: (i, 0, 0, j))
    rows_shape = lambda n, dt: jax.ShapeDtypeStruct((b, KVH, n, t), dt)
    aug = [HD + _sel_rows(t), HD + V_PAD, HD, HD + V_PAD]
    return pl.pallas_call(
        _nsa_proj_kernel,
        grid=(b, t // tm),
        in_specs=[pl.BlockSpec((None, tm, d), lambda i, j: (i, j, 0)), full(gn), full(wt), full(gq), full(gk),
                  full(bg), pl.BlockSpec((ROT_HALF, tm), lambda i, j: (0, j)),
                  pl.BlockSpec((ROT_HALF, tm), lambda i, j: (0, j))],
        out_specs=[pl.BlockSpec((None, GRP * KVH, HD, tm), lambda i, j: (i, 0, 0, j))] + [rows_spec(HD)] * 6
        + [rows_spec(16)] + [rows_spec(n) for n in aug],
        out_shape=[jax.ShapeDtypeStruct((b, GRP * KVH, HD, t), BF16)] + [rows_shape(HD, F32)] * 6
        + [rows_shape(16, F32)] + [rows_shape(n, BF16) for n in aug],
        compiler_params=_params("parallel", "parallel"),
        name="nsa_proj",
    )(x, gn, wt, gq, gk, bg, cos, sin)


def _nsa_out_kernel(ot_ref, x_ref, w_ref, y_ref):
    ot = ot_ref[...].reshape(GRP * KVH * HD, ot_ref.shape[-1])
    y_ref[...] = x_ref[...] + _dot_tn(ot, w_ref[...])


def _nsa_out(ot, x, w, tm):
    b, t, d = x.shape
    return pl.pallas_call(
        _nsa_out_kernel,
        grid=(b, t // tm),
        in_specs=[pl.BlockSpec((None, GRP * KVH, HD, tm), lambda i, j: (i, 0, 0, j)),
                  pl.BlockSpec((None, tm, d), lambda i, j: (i, j, 0)),
                  pl.BlockSpec(w.shape, lambda i, j: (0, 0))],
        out_specs=pl.BlockSpec((None, tm, d), lambda i, j: (i, j, 0)),
        out_shape=jax.ShapeDtypeStruct((b, t, d), F32),
        compiler_params=_params("parallel", "parallel"),
        name="nsa_out",
    )(ot, x, w)


def _cmp_fs_kernel(*refs, n_pages, paged):
    refs = refs[1:] if paged else refs
    pages = refs[:n_pages]
    perm_ref, w_ref, o_ref, lhs_scr = refs[n_pages:]
    p = pl.program_id(1)
    half = CMP_STRIDE
    n_half = PAGE // half
    low = lax.broadcasted_iota(jnp.int32, (n_half, LANES), 1) < HD
    for pair in range(KVH // 2):
        for j in range(n_pages):
            y = _dot(pages[j][2 * pair:2 * pair + 2].reshape(2 * HD, PAGE).astype(BF16), perm_ref[...])
            xa = y.T
            xb = jnp.concatenate([y[HD:], y[:HD]], axis=0).T
            rows = pl.ds(pl.multiple_of((p * n_pages + j) * n_half, n_half), n_half)
            for t in range(half // 2):
                ev = slice(2 * t * n_half, (2 * t + 1) * n_half)
                od = slice((2 * t + 1) * n_half, (2 * t + 2) * n_half)
                lanes = slice(t * LANES, (t + 1) * LANES)
                lhs_scr[2 * pair, rows, lanes] = jnp.where(low, xa[ev], xb[od])
                lhs_scr[2 * pair + 1, rows, lanes] = jnp.where(low, xb[ev], xa[od])

    @pl.when(p == pl.num_programs(1) - 1)
    def _():
        n = w_ref.shape[1]
        for kvh in range(KVH):
            o_ref[:, kvh * n:(kvh + 1) * n] = _dot(lhs_scr[kvh].astype(BF16), w_ref[...])


def _cmp_fs(src, w, n_pages_total, pages_per_step, page_table=None, layer=0):
    paged = page_table is not None
    b = page_table.shape[0] if paged else src.shape[0]
    n_half = PAGE // CMP_STRIDE
    nh = n_pages_total * n_half
    steps = n_pages_total // pages_per_step
    tok = jnp.arange(PAGE)
    perm = (((tok % CMP_STRIDE) * n_half + tok // CMP_STRIDE)[:, None] == tok[None, :]).astype(BF16)
    if paged:
        page_spec = lambda j: pl.BlockSpec(
            (None, None, KVH, HD, PAGE), lambda i, p, pt: (layer, pt[i, p * pages_per_step + j], 0, 0, 0))
        full = lambda a: pl.BlockSpec(a.shape, lambda i, p, pt: (0, 0))
        o_spec = pl.BlockSpec((None, nh, KVH * 512), lambda i, p, pt: (i, 0, 0))
    else:
        page_spec = lambda j: pl.BlockSpec((None, KVH, HD, PAGE), lambda i, p: (i, 0, 0, p * pages_per_step + j))
        full = lambda a: pl.BlockSpec(a.shape, lambda i, p: (0, 0))
        o_spec = pl.BlockSpec((None, nh, KVH * 512), lambda i, p: (i, 0, 0))
    grid_spec = pltpu.PrefetchScalarGridSpec(
        num_scalar_prefetch=1 if paged else 0,
        grid=(b, steps),
        in_specs=[page_spec(j) for j in range(pages_per_step)] + [full(perm), full(w)],
        out_specs=o_spec,
        scratch_shapes=[pltpu.VMEM((KVH, nh, CMP_STRIDE * HD), F32)],
    )
    args = ([page_table] if paged else []) + [src] * pages_per_step + [perm, w]
    return pl.pallas_call(
        functools.partial(_cmp_fs_kernel, n_pages=pages_per_step, paged=paged),
        grid_spec=grid_spec,
        out_shape=jax.ShapeDtypeStruct((b, nh, KVH * 512), F32),
        compiler_params=_params("parallel", "arbitrary"),
        name="cmp_fs",
    )(*args)


def _cmp_out_kernel(fs_ref, b1_ref, pe_ref, w1_ref, w2t_ref, gk_ref, o_ref, *, is_key):
    nh = fs_ref.shape[0]
    hidden = b1_ref.shape[1]
    c = _dot(pe_ref[0].astype(BF16), w1_ref[0]) + _dot(pe_ref[1].astype(BF16), w1_ref[1])
    bias = c[0:1, :] + b1_ref[...]
    for kvh in range(KVH):
        first = fs_ref[:, kvh * 2 * hidden:kvh * 2 * hidden + hidden]
        second = fs_ref[:, kvh * 2 * hidden + hidden:(kvh + 1) * 2 * hidden]
        hid = jax.nn.gelu(first + pltpu.roll(second, nh - 1, 0) + bias)
        yt = _dot_nt(w2t_ref[...], hid.astype(BF16))
        if is_key:
            yt = yt * lax.rsqrt(jnp.mean(yt * yt, axis=0, keepdims=True) + EPS) * gk_ref[...]
        o_ref[kvh] = yt.astype(BF16)


def _cmp_out(fs, b1, pe, w1, w2t, gk, is_key):
    b, nh, _ = fs.shape
    full = lambda a: pl.BlockSpec(a.shape, lambda i: (0,) * a.ndim)
    return pl.pallas_call(
        functools.partial(_cmp_out_kernel, is_key=is_key),
        grid=(b,),
        in_specs=[pl.BlockSpec((None, nh, fs.shape[2]), lambda i: (i, 0, 0)), full(b1), full(pe), full(w1),
                  full(w2t), full(gk)],
        out_specs=pl.BlockSpec((None, KVH, HD, nh), lambda i: (i, 0, 0, 0)),
        out_shape=jax.ShapeDtypeStruct((b, KVH, HD, nh), BF16),
        compiler_params=_params("parallel"),
        name="cmp_out",
    )(fs, b1, pe, w1, w2t, gk)


def _cmp_weights(pe, w1, b1, w2, g_kcmp, nh):
    hidden = w1.shape[-1]
    wfs = jnp.concatenate([w1[:CMP_STRIDE], w1[CMP_STRIDE:]], axis=-1)
    w_pair = wfs.reshape(CMP_STRIDE * HD, 2 * hidden).astype(BF16)
    pe2 = jnp.zeros((2, 8, CMP_STRIDE * HD), F32).at[:, 0, :].set(pe.reshape(2, CMP_STRIDE * HD))
    w1f = w1.reshape(2, CMP_STRIDE * HD, hidden).astype(BF16)
    gk = jnp.broadcast_to(g_kcmp.reshape(HD, 1), (HD, nh))
    return w_pair, b1.reshape(1, hidden), pe2, w1f, w2.T.astype(BF16), gk


def _softmax_update(s, mask, vt, m, l, acc):
    m_new = jnp.maximum(m, jnp.max(jnp.where(mask, s, NEG), axis=0, keepdims=True))
    alpha = jnp.exp2(m - m_new)
    p = jnp.where(mask, jnp.exp2(s - m_new), 0.0)
    l_new = l * alpha + jnp.sum(p, axis=0, keepdims=True)
    acc_new = acc * alpha + _dot(vt, p.astype(BF16))
    return m_new, l_new, acc_new, p


def _softmax_init(lanes):
    return jnp.full((1, lanes), NEG, F32), jnp.zeros((1, lanes), F32), jnp.zeros((HD, lanes), F32)


def _finish(l, acc):
    return acc / jnp.maximum(l, 1e-30)


def _split_dot(mt, x):
    hi = x.astype(BF16)
    lo = (x - hi.astype(F32)).astype(BF16)
    return _dot(mt, hi) + _dot(mt, lo)


def _attn_prompt_kernel(q_ref, kc_ref, vc_ref, ksa_ref, vsa_ref, kwb_ref, vwa_ref, gt_ref, mt_ref, o_ref,
                        s_scr, s2_scr, *, tq, n_cmp, n_sel):
    i = pl.program_id(2)
    lanes = GRP * tq
    ncp = kc_ref.shape[1]
    nsp = mt_ref.shape[0]
    qt = jnp.concatenate([q_ref[h] for h in range(GRP)], axis=1)

    def col_max(mx, s):
        return jnp.maximum(mx, jnp.max(s.reshape(s.shape[0] // 8, 8, lanes), axis=0))

    def finish(acc):
        return acc[0:HD] / jnp.maximum(acc[HD:HD + 1], 1e-30)

    tpos1 = i * tq + lax.broadcasted_iota(jnp.int32, (1, tq), 1)
    tpos = jnp.concatenate([tpos1] * GRP, axis=1)

    s = _dot_tn(kc_ref[...], qt)
    n_idx = lax.broadcasted_iota(jnp.int32, (ncp, lanes), 0)
    mask = (n_idx * CMP_STRIDE + (L_CMP - 1) <= tpos) & (n_idx < n_cmp)
    m, l, acc, p = _softmax_update(s, mask, vc_ref[...], *_softmax_init(lanes))
    o_c = _finish(l, acc)
    p = p / jnp.maximum(l, 1e-30)
    imp = p[:, 0:tq]
    for h in range(1, GRP):
        imp = imp + p[:, h * tq:(h + 1) * tq]
    p_slc = _split_dot(mt_ref[...], imp)

    j_idx = lax.broadcasted_iota(jnp.int32, (nsp, tq), 0)
    cur = tpos1 // L_SLC
    valid = j_idx <= cur
    forced = (j_idx == 0) | (j_idx == cur) | (j_idx == cur - 1)
    score = jnp.where(valid & forced, jnp.inf, jnp.where(valid, p_slc, -jnp.inf))
    bits = lax.bitcast_convert_type(score, jnp.int32)
    key = jnp.where(bits >= 0, bits, bits ^ 0x7FFFFFFF)
    n_grp = nsp // 8
    keys = [key[8 * r:8 * r + 8] for r in range(n_grp)]
    keys_m1 = [k - 1 for k in keys]
    sub = lax.broadcasted_iota(jnp.int32, (8, tq), 0)

    def count_group(grp, ranks):
        ranks = list(ranks)
        for u in range(8):
            row = jnp.broadcast_to(keys[grp][u:u + 1, :], (8, tq))
            for r in range(n_grp):
                thr = keys[r] if r < grp else keys_m1[r] if r > grp else jnp.where(sub > u, keys_m1[r], keys[r])
                ranks[r] = ranks[r] + jnp.where(row > thr, 1.0, 0.0)
        return tuple(ranks)

    ranks = tuple(jnp.zeros((8, tq), F32) for _ in range(n_grp))
    last_valid = (i * tq + tq - 1) // L_SLC
    for grp in range(n_grp):
        ranks = lax.cond(8 * grp <= last_valid, functools.partial(count_group, grp), lambda r: r, ranks)
    bias = jnp.where(jnp.concatenate(ranks, axis=0) < n_sel, 0.0, NEG)
    qa = jnp.concatenate([qt, jnp.concatenate([bias] * GRP, axis=1).astype(BF16)], axis=0)

    row_k = lax.broadcasted_iota(jnp.int32, (KT, lanes), 0)

    def slc_scores(st, buf):
        start = pl.multiple_of(st * KT, KT)
        buf[0:KT, :] = _dot_tn(ksa_ref[:, pl.ds(start, KT)], qa)

    def slc_softmax(st, buf, carry, diagonal=False):
        m, acc = carry
        s = buf[0:KT, :]
        if diagonal:
            s = jnp.where(st * KT + row_k <= tpos, s, NEG)
        m_new = jnp.maximum(m, jnp.max(col_max(jnp.full((8, lanes), NEG, F32), s), axis=0, keepdims=True))
        p = jnp.exp2(s - m_new).astype(BF16)
        pv = _dot(vsa_ref[:, pl.ds(pl.multiple_of(st * KT, KT), KT)], p)
        return m_new, acc * jnp.exp2(m - m_new) + pv

    def slc_pair(u, carry):
        slc_scores(2 * u + 1, s2_scr)
        carry = slc_softmax(2 * u, s_scr, carry)
        slc_scores(2 * u + 2, s_scr)
        return slc_softmax(2 * u + 1, s2_scr, carry)

    def slc_tail_odd(carry):
        slc_scores(n_full, s2_scr)
        carry = slc_softmax(n_full - 1, s_scr, carry)
        return slc_softmax(n_full, s2_scr, carry, diagonal=True)

    def slc_tail_even(carry):
        return slc_softmax(n_full, s_scr, carry, diagonal=True)

    n_full = (i * tq) // KT
    slc_scores(0, s_scr)
    init = (jnp.full((1, lanes), NEG, F32), jnp.zeros((HD + V_PAD, lanes), F32))
    carry = lax.fori_loop(0, n_full // 2, slc_pair, init)
    o_s = finish(lax.cond(n_full % 2 == 1, slc_tail_odd, slc_tail_even, carry)[1])

    n_wt = WINDOW // tq + 1
    w0 = jnp.maximum(i - (n_wt - 1), 0) * tq
    row_w = lax.broadcasted_iota(jnp.int32, (tq, lanes), 0)
    col_w = jnp.concatenate([lax.broadcasted_iota(jnp.int32, (tq, tq), 1)] * GRP, axis=1)

    def window(steady):
        mx = jnp.full((8, lanes), NEG, F32)
        for r in range(n_wt):
            start = pl.multiple_of(w0 + r * tq, tq)
            s = _dot_tn(kwb_ref[:, pl.ds(start, tq)], qt)
            if not steady:
                rel = tpos - (start + row_w)
                s = jnp.where((rel >= 0) & (rel <= WINDOW), s, NEG)
            elif r == 0:
                s = jnp.where(row_w >= col_w, s, NEG)
            elif r == n_wt - 1:
                s = jnp.where(row_w <= col_w, s, NEG)
            s_scr[r * tq:(r + 1) * tq, :] = s
            mx = col_max(mx, s)
        m = jnp.max(mx, axis=0, keepdims=True)
        acc = jnp.zeros((HD + V_PAD, lanes), F32)
        for r in range(n_wt):
            start = pl.multiple_of(w0 + r * tq, tq)
            p = jnp.exp2(s_scr[r * tq:(r + 1) * tq, :] - m).astype(BF16)
            acc = acc + _dot(vwa_ref[:, pl.ds(start, tq)], p)
        return finish(acc)

    o_w = lax.cond(i >= n_wt - 1, functools.partial(window, True), functools.partial(window, False))

    for h in range(GRP):
        ls = slice(h * tq, (h + 1) * tq)
        o = (gt_ref[3 * h:3 * h + 1, :] * o_c[:, ls] + gt_ref[3 * h + 1:3 * h + 2, :] * o_s[:, ls]
             + gt_ref[3 * h + 2:3 * h + 3, :] * o_w[:, ls])
        o_ref[h] = o.astype(BF16)


def _slc_matrix(nsp, ncp, n_cmp):
    ratio = L_SLC // CMP_STRIDE
    j = jnp.arange(nsp)[:, None]
    n = jnp.arange(ncp)[None, :]
    m = ((n >= ratio * j) & (n <= ratio * j + ratio - 1)).astype(F32)
    m = m + ((n >= ratio * j - 1) & (n <= ratio * j + ratio - 2)).astype(F32)
    return jnp.where(n < n_cmp, m, 0.0).astype(BF16)


def _attn_prompt(qt, kct, vct, ksa, vsa, kwb, vwa, gt, tq=4 * LANES):
    b, _, _, t = qt.shape
    assert t % KT == 0 and WINDOW % tq == 0 and t >= WINDOW + tq
    ncp = kct.shape[-1]
    n_cmp = t // CMP_STRIDE - 1
    ns = -(-t // L_SLC)
    mt = _slc_matrix(_sel_rows(t), ncp, n_cmp)
    res = lambda a: pl.BlockSpec((None, None) + a.shape[2:], lambda i, g, j: (i, g, 0, 0))
    return pl.pallas_call(
        functools.partial(_attn_prompt_kernel, tq=tq, n_cmp=n_cmp, n_sel=min(N_SEL, ns)),
        grid=(b, KVH, t // tq),
        in_specs=[pl.BlockSpec((None, GRP, HD, tq), lambda i, g, j: (i, g, 0, j)),
                  res(kct), res(vct), res(ksa), res(vsa), res(kwb), res(vwa),
                  pl.BlockSpec((None, None, 16, tq), lambda i, g, j: (i, g, 0, j)),
                  pl.BlockSpec(mt.shape, lambda i, g, j: (0, 0))],
        out_specs=pl.BlockSpec((None, GRP, HD, tq), lambda i, g, j: (i, g, 0, j)),
        out_shape=jax.ShapeDtypeStruct(qt.shape, BF16),
        scratch_shapes=[pltpu.VMEM((max(KT, WINDOW + tq), GRP * tq), F32), pltpu.VMEM((KT, GRP * tq), F32)],
        compiler_params=_params("parallel", "parallel", "arbitrary"),
        name="nsa_attn_prompt",
    )(qt, kct, vct, ksa, vsa, kwb, vwa, gt, mt)


def _attn_sample_kernel(*refs, n_pages, n_cmp, n_blocks, n_sel):
    refs = refs[1:]
    k_pages = refs[:n_pages]
    v_pages = refs[n_pages:2 * n_pages]
    (q_ref, kc_ref, vc_ref, kw_ref, vw_ref, kns_ref, vns_ref, knw_ref, vnw_ref, gt_ref, mt_ref,
     o_ref, sel_scr, oc_scr, ow_scr, m_scr, l_scr, acc_scr) = refs[2 * n_pages:]
    p = pl.program_id(1)
    ncp = kc_ref.shape[-1]
    nsp = mt_ref.shape[0]
    rows_all = KVH * HD
    first_row = lax.broadcasted_iota(jnp.int32, (LANES, LANES), 0) == 0
    qt = q_ref[...]

    def stacked(ref):
        return ref[...].reshape(rows_all, ref.shape[-1]).astype(BF16)

    def finish(l, acc):
        return acc / jnp.maximum(l, 1e-30)

    def init():
        return jnp.full((1, LANES), NEG, F32), jnp.zeros((1, LANES), F32), jnp.zeros((rows_all, LANES), F32)

    @pl.when(p == 0)
    def _():
        lane = lax.broadcasted_iota(jnp.int32, (ncp, LANES), 1)
        n_idx = lax.broadcasted_iota(jnp.int32, (ncp, LANES), 0)
        j_col = lax.broadcasted_iota(jnp.int32, (nsp, LANES), 0)
        lane_s = lax.broadcasted_iota(jnp.int32, (nsp, LANES), 1)
        jp_idx = lax.broadcasted_iota(jnp.int32, (nsp, nsp), 0)
        j_idx = lax.broadcasted_iota(jnp.int32, (nsp, nsp), 1)
        cur = n_blocks - 1
        s = _dot_tn(stacked(kc_ref), qt)
        m, l, acc, pr = _softmax_update(s, n_idx < n_cmp, stacked(vc_ref), *init())
        oc_scr[...] = finish(l, acc)
        pr = pr / jnp.maximum(l, 1e-30)
        sel_all = jnp.zeros((nsp, LANES), F32)
        for g in range(KVH):
            mine = (lane >= g * GRP) & (lane < (g + 1) * GRP)
            imp = jnp.sum(jnp.where(mine, pr, 0.0), axis=1, keepdims=True)
            p_slc = _split_dot(mt_ref[...], jnp.broadcast_to(imp, (ncp, LANES)))
            valid = j_col <= cur
            forced = (j_col == 0) | (j_col == cur) | (j_col == cur - 1)
            score = jnp.where(valid & forced, jnp.inf, jnp.where(valid, p_slc, -jnp.inf))
            col = jnp.concatenate([score] * (nsp // LANES), axis=1)
            row = col.T
            ahead = (col > row) | ((col == row) & (jp_idx < j_idx))
            rank = jnp.sum(jnp.where(ahead, 1.0, 0.0), axis=0, keepdims=True)
            sel = jnp.broadcast_to(jnp.where(rank < n_sel, 1.0, 0.0), (LANES, nsp)).T
            sel_all = jnp.where((lane_s >= g * GRP) & (lane_s < (g + 1) * GRP), sel, sel_all)
        sel_scr[...] = sel_all
        s = _dot_tn(stacked(kw_ref), qt)
        st = _softmax_update(s, jnp.full(s.shape, True), stacked(vw_ref), *init())[:3]
        s = _dot_tn(stacked(knw_ref), qt)
        _, l, acc = _softmax_update(s, first_row, stacked(vnw_ref), *st)[:3]
        ow_scr[...] = finish(l, acc)
        m_scr[...], l_scr[...], acc_scr[...] = init()

    blocks_per_page = PAGE // L_SLC
    kt = jnp.concatenate([k_pages[j][...].reshape(rows_all, PAGE) for j in range(n_pages)], axis=1).astype(BF16)
    vt = jnp.concatenate([v_pages[j][...].reshape(rows_all, PAGE) for j in range(n_pages)], axis=1).astype(BF16)
    s = _dot_tn(kt, qt)
    rows = []
    for r in range(n_pages * blocks_per_page):
        e = sel_scr[pl.ds(p * n_pages * blocks_per_page + r, 1), :]
        rows.append(jnp.broadcast_to(e, (L_SLC, LANES)))
    mask = jnp.concatenate(rows, axis=0) > 0.5
    m_scr[...], l_scr[...], acc_scr[...] = _softmax_update(s, mask, vt, m_scr[...], l_scr[...], acc_scr[...])[:3]

    @pl.when(p == pl.num_programs(1) - 1)
    def _():
        s = _dot_tn(stacked(kns_ref), qt)
        _, l, acc = _softmax_update(s, first_row, stacked(vns_ref), m_scr[...], l_scr[...], acc_scr[...])[:3]
        o_ref[...] = (gt_ref[0:1, :] * oc_scr[...] + gt_ref[1:2, :] * finish(l, acc) + gt_ref[2:3, :] * ow_scr[...])


def _attn_sample(qs, kct, vct, pool_k, pool_v, page_table, layer, kwt, vwt, kns, vns, knw, vnw, gs,
                 pages_per_step=16):
    b, n_pages_total = page_table.shape
    pages_per_step = min(pages_per_step, n_pages_total)
    past = n_pages_total * PAGE
    n_cmp = past // CMP_STRIDE - 1
    ncp = kct.shape[-1]
    n_blocks = past // L_SLC + 1
    nsp = -(-n_blocks // LANES) * LANES
    mt = _slc_matrix(nsp, ncp, n_cmp)
    steps = n_pages_total // pages_per_step
    page_spec = lambda j: pl.BlockSpec((None, None, KVH, HD, PAGE),
                                       lambda i, p, pt: (layer, pt[i, p * pages_per_step + j], 0, 0, 0))
    per_b = lambda a: pl.BlockSpec((None,) + a.shape[1:], lambda i, p, pt: (i,) + (0,) * (a.ndim - 1))
    small = [qs, kct, vct, kwt, vwt, kns, vns, knw, vnw, gs]
    grid_spec = pltpu.PrefetchScalarGridSpec(
        num_scalar_prefetch=1,
        grid=(b, steps),
        in_specs=[page_spec(j) for j in range(pages_per_step)] * 2 + [per_b(a) for a in small]
        + [pl.BlockSpec(mt.shape, lambda i, p, pt: (0, 0))],
        out_specs=pl.BlockSpec((None, KVH * HD, LANES), lambda i, p, pt: (i, 0, 0)),
        scratch_shapes=[pltpu.VMEM((nsp, LANES), F32), pltpu.VMEM((KVH * HD, LANES), F32),
                        pltpu.VMEM((KVH * HD, LANES), F32), pltpu.VMEM((1, LANES), F32),
                        pltpu.VMEM((1, LANES), F32), pltpu.VMEM((KVH * HD, LANES), F32)],
    )
    return pl.pallas_call(
        functools.partial(_attn_sample_kernel, n_pages=pages_per_step, n_cmp=n_cmp, n_blocks=n_blocks,
                          n_sel=min(N_SEL, n_blocks)),
        grid_spec=grid_spec,
        out_shape=jax.ShapeDtypeStruct((b, KVH * HD, LANES), F32),
        compiler_params=_params("parallel", "arbitrary"),
        name="nsa_attn_sample",
    )(page_table, *([pool_k] * pages_per_step), *([pool_v] * pages_per_step), *small, mt)


def _rope_tables(pos):
    inv_freq = ROPE_THETA ** (-jnp.arange(ROT_HALF, dtype=F32) * 2.0 / (2 * ROT_HALF))
    ang = pos.astype(F32)[:, None] * inv_freq[None, :]
    return jnp.cos(ang).T, jnp.sin(ang).T


def _nsa_layer(xp, xs, pools, win_bufs, page_table, layer, norm_g, w_in, b_gate, g_q, g_k, g_kcmp,
               cmp_pe, cmp_w1, cmp_b1, cmp_w2, w_out):
    b, t, d = xp.shape
    bs = xs.shape[0]
    past = page_table.shape[1] * PAGE
    nq, nkv = GRP * KVH * HD, KVH * HD
    n_gate = 3 * GRP * KVH

    wt_g = jnp.pad(w_in[:, nq + 6 * nkv:].T.reshape(KVH, n_gate // KVH, d), ((0, 0), (0, 16 - n_gate // KVH), (0, 0)))
    wt = jnp.concatenate([w_in[:, :nq + 6 * nkv].T, wt_g.reshape(KVH * 16, d)], axis=0).astype(BF16)
    bg = jnp.pad(b_gate.reshape(KVH, n_gate // KVH), ((0, 0), (0, 16 - n_gate // KVH))).reshape(KVH * 16, 1)
    gn = norm_g.reshape(1, d)
    wo = w_out.astype(BF16)

    def project(x, pos, tm):
        cos, sin = _rope_tables(pos)
        col = lambda v: jnp.broadcast_to(v[..., None], v.shape + (tm,))
        return _nsa_proj(x, gn, wt, col(g_q), col(g_k), col(bg[:, 0]), cos, sin, tm)

    tm = min(512, t)
    qt, kct_rows, vct_rows, kst, vst, kwt, vwt, gt, ksa, vsa, kwb, vwa = project(xp, jnp.arange(t), tm)
    pages = t // PAGE
    nh = pages * (PAGE // CMP_STRIDE)
    cmp_w = [_cmp_weights(cmp_pe[i], cmp_w1[i], cmp_b1[i], cmp_w2[i], g_kcmp, nh) for i in range(2)]
    pps = min(16, pages)
    kct = _cmp_out(_cmp_fs(kct_rows, cmp_w[0][0], pages, pps), *cmp_w[0][1:], is_key=True)
    vct = _cmp_out(_cmp_fs(vct_rows, cmp_w[1][0], pages, pps), *cmp_w[1][1:], is_key=False)
    ot = _attn_prompt(qt, kct, vct, ksa, vsa, kwb, vwa, gt)
    yp = _nsa_out(ot, xp, wo, tm)
    n_win = min(WINDOW, t)
    rows_p = [kct_rows, vct_rows, kst, vst, kwt[..., t - n_win:], vwt[..., t - n_win:]]

    xs_pad = jnp.pad(xs.reshape(1, bs, d), ((0, 0), (0, LANES - bs), (0, 0)))
    outs = project(xs_pad, jnp.full((LANES,), past), LANES)
    qt_s, rows_s, gt_s = outs[0], outs[1:7], outs[7]
    pages_s = page_table.shape[1]
    nh_s = pages_s * (PAGE // CMP_STRIDE)
    cmp_ws = [_cmp_weights(cmp_pe[i], cmp_w1[i], cmp_b1[i], cmp_w2[i], g_kcmp, nh_s) for i in range(2)]
    pool5 = [jnp.transpose(pl_, (0, 1, 3, 4, 2)) for pl_ in pools]
    pps = min(32, pages_s)
    kct_s = _cmp_out(_cmp_fs(pool5[0], cmp_ws[0][0], pages_s, pps, page_table, layer), *cmp_ws[0][1:], is_key=True)
    vct_s = _cmp_out(_cmp_fs(pool5[1], cmp_ws[1][0], pages_s, pps, page_table, layer), *cmp_ws[1][1:], is_key=False)
    qs = jnp.transpose(qt_s[0, :, :, :bs].reshape(KVH, GRP, HD, bs), (3, 0, 2, 1))
    qs = jnp.einsum("bgdh,gk->bgdkh", qs, jnp.eye(KVH, dtype=qs.dtype))
    qs = jnp.pad(qs.reshape(bs, KVH * HD, KVH * GRP), ((0, 0), (0, 0), (0, LANES - KVH * GRP)))
    new = lambda a: jnp.pad(jnp.transpose(a[0, :, :, :bs], (2, 0, 1))[..., None],
                            ((0, 0), (0, 0), (0, 0), (0, LANES - 1)))
    gs = jnp.transpose(gt_s[0, :, :n_gate // KVH, :bs].reshape(KVH, GRP, 3, bs), (3, 2, 0, 1))
    gs = jnp.pad(gs.reshape(bs, 3, KVH * GRP), ((0, 0), (0, 8 - 3), (0, LANES - KVH * GRP)))
    win5 = [jnp.transpose(wb[layer], (0, 2, 3, 1)) for wb in win_bufs]
    ot_s = _attn_sample(qs, kct_s, vct_s, pool5[2], pool5[3], page_table, layer, win5[0], win5[1],
                        new(rows_s[2]), new(rows_s[3]), new(rows_s[4]), new(rows_s[5]), gs)
    ot_s = ot_s.reshape(bs, KVH, HD, LANES)
    ot_s = jnp.stack([ot_s[:, g, :, g * GRP:(g + 1) * GRP] for g in range(KVH)])
    ot_s = jnp.transpose(ot_s, (0, 3, 2, 1)).reshape(1, GRP * KVH, HD, bs)
    ot_s = jnp.pad(ot_s, ((0, 0), (0, 0), (0, 0), (0, LANES - bs))).astype(BF16)
    ys = _nsa_out(ot_s, xs_pad, wo, LANES)[0, :bs].reshape(bs, 1, d)

    to_rows = lambda a: jnp.transpose(a, (0, 3, 1, 2))
    rows_p = [to_rows(a) for a in rows_p]
    rows_s = [to_rows(a[:, :, :, :bs]).reshape(bs, 1, KVH, HD) for a in rows_s]
    return yp, ys, rows_p, rows_s


def kernel(x_prompt, x_sample, state_gla, cache_k_cmp, cache_v_cmp, cache_k_slc, cache_v_slc, cache_k_win, cache_v_win, page_table, norm_mix, norm_mlp, mlp_up, mlp_down, gla_w_in, gla_w_gate2, gla_b_gate, gla_g_out, gla_w_out, nsa_w_in, nsa_b_gate, nsa_g_q, nsa_g_k, nsa_g_kcmp, nsa_cmp_pe, nsa_cmp_w1, nsa_cmp_b1, nsa_cmp_w2, nsa_w_out):
    depth = norm_mix.shape[0]
    b, t, d = x_prompt.shape
    bs = x_sample.shape[0]
    xp, xs = x_prompt, x_sample
    gla_p, gla_s, nsa_p, nsa_s = [], [], [], []
    pools = (cache_k_cmp, cache_v_cmp, cache_k_slc, cache_v_slc)
    for i in range(depth):
        j = i // 2
        if i % 2 == 0:
            xp, xs, sp, ss = _gla_layer(xp, xs, state_gla[j], norm_mix[i], gla_w_in[j], gla_w_gate2[j],
                                        gla_b_gate[j], gla_g_out[j], gla_w_out[j])
            gla_p.append(sp)
            gla_s.append(ss)
        else:
            xp, xs, rp, rs = _nsa_layer(xp, xs, pools, (cache_k_win, cache_v_win), page_table, j, norm_mix[i],
                                        nsa_w_in[j], nsa_b_gate[j], nsa_g_q[j], nsa_g_k[j], nsa_g_kcmp[j],
                                        nsa_cmp_pe[j], nsa_cmp_w1[j], nsa_cmp_b1[j], nsa_cmp_w2[j], nsa_w_out[j])
            nsa_p.append(rp)
            nsa_s.append(rs)
        g = norm_mlp[i].reshape(1, d)
        wu, wd = mlp_up[i].astype(BF16), mlp_down[i].astype(BF16)
        xp = _mlp(xp.reshape(b * t, d), g, wu, wd, tm=min(1024, b * t)).reshape(b, t, d)
        xs = _mlp(xs.reshape(bs, d), g, wu, wd, tm=bs).reshape(bs, 1, d)
    stack = lambda lst, r: jnp.stack([e[r] for e in lst])
    return (xp, xs, jnp.stack(gla_p), jnp.stack(gla_s),
            stack(nsa_p, 0), stack(nsa_s, 0), stack(nsa_p, 1), stack(nsa_s, 1),
            stack(nsa_p, 2), stack(nsa_s, 2), stack(nsa_p, 3), stack(nsa_s, 3),
            stack(nsa_p, 4), stack(nsa_s, 4), stack(nsa_p, 5), stack(nsa_s, 5))
```

```python
import functools

import jax
import jax.numpy as jnp
from jax import lax
from jax.experimental import pallas as pl
from jax.experimental.pallas import tpu as pltpu

F32 = jnp.float32
BF16 = jnp.bfloat16
EPS = 1e-6
NEG = -1e30
VMEM_LIMIT_BYTES = 48 * 1024 * 1024
LANES = 128
PAGE = 128
HD = 64
KVH = 4
GRP = 4
L_CMP, CMP_STRIDE, L_SLC, N_SEL, WINDOW = 32, 16, 64, 16, 512
GLA_H, GLA_DK, GLA_DV, GLA_CHUNK = 4, 128, 256, 64
ROPE_THETA, ROT_HALF = 500000.0, 8
LOG2E = 1.4426950408889634
V_PAD = 16
KT = 512


def _params(*sem):
    return pltpu.CompilerParams(dimension_semantics=sem, vmem_limit_bytes=VMEM_LIMIT_BYTES)


def _dot(a, b):
    return jnp.dot(a, b, preferred_element_type=F32)


def _dot_nt(a, b):
    return lax.dot_general(a, b, (((1,), (1,)), ((), ())), preferred_element_type=F32)


def _dot_tn(a, b):
    return lax.dot_general(a, b, (((0,), (0,)), ((), ())), preferred_element_type=F32)


def _rms_rows(x, g):
    ms = jnp.mean(x * x, axis=-1, keepdims=True)
    return x * lax.rsqrt(ms + EPS) * g


def _mlp_kernel(x_ref, g_ref, wu_ref, wd_ref, o_ref, h_scr, acc_scr):
    f = pl.program_id(1)

    @pl.when(f == 0)
    def _():
        h_scr[...] = _rms_rows(x_ref[...], g_ref[...]).astype(BF16)
        acc_scr[...] = jnp.zeros_like(acc_scr)

    u = jnp.maximum(_dot(h_scr[...], wu_ref[...]), 0.0)
    acc_scr[...] += _dot((u * u).astype(BF16), wd_ref[...])

    @pl.when(f == pl.num_programs(1) - 1)
    def _():
        o_ref[...] = x_ref[...] + acc_scr[...]


def _mlp(x, g, wu, wd, tm, tf=1024):
    m, d = x.shape
    ff = wu.shape[1]
    return pl.pallas_call(
        _mlp_kernel,
        grid=(m // tm, ff // tf),
        in_specs=[pl.BlockSpec((tm, d), lambda i, f: (i, 0)),
                  pl.BlockSpec((1, d), lambda i, f: (0, 0)),
                  pl.BlockSpec((d, tf), lambda i, f: (0, f)),
                  pl.BlockSpec((tf, d), lambda i, f: (f, 0))],
        out_specs=pl.BlockSpec((tm, d), lambda i, f: (i, 0)),
        out_shape=jax.ShapeDtypeStruct((m, d), F32),
        scratch_shapes=[pltpu.VMEM((tm, d), BF16), pltpu.VMEM((tm, d), F32)],
        compiler_params=_params("parallel", "arbitrary"),
        name="mlp_block",
    )(x, g, wu, wd)


def _gla_proj_kernel(x_ref, g_ref, w_ref, wg2_ref, bg_ref, q_ref, k_ref, v_ref, r_ref, gl_ref):
    h = _rms_rows(x_ref[...], g_ref[...]).astype(BF16)
    dk, dv = GLA_H * GLA_DK, GLA_H * GLA_DV
    q_ref[...] = _dot(h, w_ref[:, 0:dk]) * (GLA_DK ** -0.5)
    k_ref[...] = _dot(h, w_ref[:, dk:2 * dk])
    v_ref[...] = _dot(h, w_ref[:, 2 * dk:2 * dk + dv])
    r_ref[...] = _dot(h, w_ref[:, 2 * dk + dv:2 * dk + 2 * dv])
    gr = _dot(h, w_ref[:, 2 * dk + 2 * dv:])
    xg = _dot(gr.astype(BF16), wg2_ref[...]) + bg_ref[...]
    gl_ref[...] = jax.nn.log_sigmoid(xg) * (1.0 / 16.0)


def _gla_proj(x, g, w, wg2, bg, tm):
    m, d = x.shape
    dk, dv = GLA_H * GLA_DK, GLA_H * GLA_DV
    row = lambda n: pl.BlockSpec((tm, n), lambda i: (i, 0))
    full = lambda a: pl.BlockSpec(a.shape, lambda i: (0, 0))
    return pl.pallas_call(
        _gla_proj_kernel,
        grid=(m // tm,),
        in_specs=[row(d), full(g), full(w), full(wg2), full(bg)],
        out_specs=[row(dk), row(dk), row(dv), row(dv), row(dk)],
        out_shape=[jax.ShapeDtypeStruct((m, n), F32) for n in (dk, dk, dv, dv, dk)],
        compiler_params=_params("parallel"),
        name="gla_proj",
    )(x, g, w, wg2, bg)


def _gla_scan_kernel(q_ref, k_ref, g_ref, v_ref, o_ref, s_ref, st_scr, *, n_chunks, n_heads):
    c_len = GLA_CHUNK
    step = pl.program_id(2)

    @pl.when(step == 0)
    def _():
        st_scr[...] = jnp.zeros_like(st_scr)

    rowi = lax.broadcasted_iota(jnp.int32, (c_len, GLA_DK), 0)
    causal = (lax.broadcasted_iota(jnp.int32, (c_len, c_len), 0)
              >= lax.broadcasted_iota(jnp.int32, (c_len, c_len), 1))

    def body(c, carry):
        sl = pl.ds(pl.multiple_of(c * c_len, c_len), c_len)
        for h in range(n_heads):
            ks = slice(h * GLA_DK, (h + 1) * GLA_DK)
            vs = slice(h * GLA_DV, (h + 1) * GLA_DV)
            b = g_ref[sl, ks]
            sh = 1
            while sh < c_len:
                b = b + jnp.where(rowi >= sh, pltpu.roll(b, sh, 0), 0.0)
                sh *= 2
            b_last = b[c_len - 1:c_len, :]
            b_mid = b[c_len // 2 - 1:c_len // 2, :]
            q = q_ref[sl, ks]
            k = k_ref[sl, ks]
            v = v_ref[sl, vs].astype(BF16)
            qe = (q * jnp.exp(b)).astype(BF16)
            qa = (q * jnp.exp(b - b_mid)).astype(BF16)
            ka = (k * jnp.exp(b_mid - b)).astype(BF16)
            kd = (k * jnp.exp(b_last - b)).astype(BF16)
            a = jnp.where(causal, _dot_nt(qa, ka), 0.0)
            st = st_scr[h]
            o_ref[sl, vs] = _dot_nt(qe, st.astype(BF16)) + _dot(a.astype(BF16), v)
            st_scr[h] = st * jnp.exp(b_last) + _dot_tn(v, kd)
        return carry

    lax.fori_loop(0, n_chunks, body, 0)

    @pl.when(step == pl.num_programs(2) - 1)
    def _():
        for h in range(n_heads):
            s_ref[h] = st_scr[h].T


def _gla_scan(q, k, gl, v, n_heads=4, t_blk=1024):
    b, t, _ = q.shape
    t_blk = min(t_blk, t)
    kq = pl.BlockSpec((None, t_blk, n_heads * GLA_DK), lambda i, h, j: (i, j, h))
    vv = pl.BlockSpec((None, t_blk, n_heads * GLA_DV), lambda i, h, j: (i, j, h))
    return pl.pallas_call(
        functools.partial(_gla_scan_kernel, n_chunks=t_blk // GLA_CHUNK, n_heads=n_heads),
        grid=(b, GLA_H // n_heads, t // t_blk),
        in_specs=[kq, kq, kq, vv],
        out_specs=[vv, pl.BlockSpec((None, n_heads, GLA_DK, GLA_DV), lambda i, h, j: (i, h, 0, 0))],
        out_shape=[jax.ShapeDtypeStruct((b, t, GLA_H * GLA_DV), F32),
                   jax.ShapeDtypeStruct((b, GLA_H, GLA_DK, GLA_DV), F32)],
        scratch_shapes=[pltpu.VMEM((n_heads, GLA_DV, GLA_DK), F32)],
        compiler_params=_params("parallel", "parallel", "arbitrary"),
        name="gla_scan",
    )(q, k, gl, v)


def _gla_step_kernel(q_ref, k_ref, g_ref, v_ref, s0_ref, o_ref, s_ref):
    def col(x):
        return jnp.broadcast_to(x, (LANES, LANES)).T

    for h in range(GLA_H):
        ks = slice(h * GLA_DK, (h + 1) * GLA_DK)
        qc, kc, ec = col(q_ref[:, ks]), col(k_ref[:, ks]), col(jnp.exp(g_ref[:, ks]))
        for half in range(GLA_DV // LANES):
            vs = slice(h * GLA_DV + half * LANES, h * GLA_DV + (half + 1) * LANES)
            ss = slice(half * LANES, (half + 1) * LANES)
            sn = ec * s0_ref[h, :, ss] + kc * v_ref[:, vs]
            s_ref[h, :, ss] = sn
            o_ref[:, vs] = jnp.sum(qc * sn, axis=0, keepdims=True)


def _gla_step(q, k, gl, v, s0):
    b = q.shape[0]
    kq = pl.BlockSpec((None, 1, GLA_H * GLA_DK), lambda i: (i, 0, 0))
    vv = pl.BlockSpec((None, 1, GLA_H * GLA_DV), lambda i: (i, 0, 0))
    st = pl.BlockSpec((None, GLA_H, GLA_DK, GLA_DV), lambda i: (i, 0, 0, 0))
    return pl.pallas_call(
        _gla_step_kernel,
        grid=(b,),
        in_specs=[kq, kq, kq, vv, st],
        out_specs=[vv, st],
        out_shape=[jax.ShapeDtypeStruct((b, 1, GLA_H * GLA_DV), F32),
                   jax.ShapeDtypeStruct((b, GLA_H, GLA_DK, GLA_DV), F32)],
        compiler_params=_params("parallel"),
        name="gla_step",
    )(q, k, gl, v, s0)


def _gla_out_kernel(o_ref, r_ref, x_ref, go_ref, w_ref, y_ref):
    parts = []
    for h in range(GLA_H):
        sl = slice(h * GLA_DV, (h + 1) * GLA_DV)
        r = r_ref[:, sl]
        parts.append((_rms_rows(o_ref[:, sl], go_ref[...]) * (r * jax.nn.sigmoid(r))).astype(BF16))
    y_ref[...] = x_ref[...] + _dot(jnp.concatenate(parts, axis=1), w_ref[...])


def _gla_out(o, r, x, go, w, tm):
    m, d = x.shape
    row = pl.BlockSpec((tm, d), lambda i: (i, 0))
    full = lambda a: pl.BlockSpec(a.shape, lambda i: (0, 0))
    return pl.pallas_call(
        _gla_out_kernel,
        grid=(m // tm,),
        in_specs=[row, row, row, full(go), full(w)],
        out_specs=row,
        out_shape=jax.ShapeDtypeStruct((m, d), F32),
        compiler_params=_params("parallel"),
        name="gla_out",
    )(o, r, x, go, w)


def _gla_layer(xp, xs, s0, norm_g, w_in, w_gate2, b_gate, g_out, w_out):
    b, t, d = xp.shape
    bs = xs.shape[0]
    dk, dv = GLA_H * GLA_DK, GLA_H * GLA_DV
    rank = w_gate2.shape[0]
    w = jnp.pad(w_in, ((0, 0), (0, LANES - rank))).astype(BF16)
    wg2 = jnp.pad(w_gate2, ((0, LANES - rank), (0, 0))).astype(BF16)
    g = norm_g.reshape(1, d)
    bg = b_gate.reshape(1, dk)
    go = g_out.reshape(1, GLA_DV)
    wo = w_out.astype(BF16)
    x2 = xp.reshape(b * t, d)
    q, k, v, r, gl = _gla_proj(x2, g, w, wg2, bg, tm=512)
    o, sp = _gla_scan(q.reshape(b, t, dk), k.reshape(b, t, dk), gl.reshape(b, t, dk), v.reshape(b, t, dv))
    yp = _gla_out(o.reshape(b * t, dv), r, x2, go, wo, tm=512).reshape(b, t, d)
    xs2 = xs.reshape(bs, d)
    q, k, v, r, gl = _gla_proj(xs2, g, w, wg2, bg, tm=bs)
    o, ss = _gla_step(q.reshape(bs, 1, dk), k.reshape(bs, 1, dk), gl.reshape(bs, 1, dk), v.reshape(bs, 1, dv), s0)
    ys = _gla_out(o.reshape(bs, dv), r, xs2, go, wo, tm=bs).reshape(bs, 1, d)
    return yp, ys, sp, ss


def _nsa_proj_kernel(x_ref, gn_ref, wt_ref, gq_ref, gk_ref, bg_ref, cos_ref, sin_ref,
                     q_ref, kc_ref, vc_ref, ks_ref, vs_ref, kw_ref, vw_ref, gt_ref,
                     ksa_ref, vsa_ref, kwb_ref, vwa_ref):
    tm = x_ref.shape[0]
    nq = GRP * KVH * HD
    nkv = KVH * HD
    h = _rms_rows(x_ref[...], gn_ref[...]).astype(BF16)
    cos = cos_ref[...][None]
    sin = sin_ref[...][None]

    def norm_rope(z, g, nh):
        z3 = z.reshape(nh, HD, tm)
        y = z3 * lax.rsqrt(jnp.mean(z3 * z3, axis=1, keepdims=True) + EPS) * g[None]
        x1 = y[:, 0:ROT_HALF, :]
        x2 = y[:, ROT_HALF:2 * ROT_HALF, :]
        return jnp.concatenate([x1 * cos - x2 * sin, x1 * sin + x2 * cos, y[:, 2 * ROT_HALF:, :]], axis=1)

    zq = _dot_nt(wt_ref[0:nq, :], h)
    q_ref[...] = (norm_rope(zq, gq_ref[...], GRP * KVH) * (HD ** -0.5 * LOG2E)).astype(BF16)
    zkv = _dot_nt(wt_ref[nq:nq + 6 * nkv, :], h)
    outs = (kc_ref, vc_ref, ks_ref, vs_ref, kw_ref, vw_ref)
    rows = []
    for i in range(6):
        z = zkv[i * nkv:(i + 1) * nkv, :]
        rows.append(norm_rope(z, gk_ref[i // 2], KVH) if i % 2 == 0 else z.reshape(KVH, HD, tm))
        outs[i][...] = rows[i]
    zg = _dot_nt(wt_ref[nq + 6 * nkv:, :], h) + bg_ref[...]
    gt_ref[...] = jax.nn.sigmoid(zg).reshape(KVH, 16, tm)

    nsp = ksa_ref.shape[1] - HD
    blk = lax.broadcasted_iota(jnp.int32, (KVH, nsp, tm), 1)
    tok = pl.program_id(1) * tm + lax.broadcasted_iota(jnp.int32, (KVH, nsp, tm), 2)
    onehot = jnp.where(tok // L_SLC == blk, 1.0, 0.0)
    ksa_ref[...] = jnp.concatenate([rows[2], onehot], axis=1).astype(BF16)
    kwb_ref[...] = rows[4].astype(BF16)
    ones = jnp.where(lax.broadcasted_iota(jnp.int32, (KVH, V_PAD, tm), 1) == 0, 1.0, 0.0)
    vsa_ref[...] = jnp.concatenate([rows[3], ones], axis=1).astype(BF16)
    vwa_ref[...] = jnp.concatenate([rows[5], ones], axis=1).astype(BF16)


def _sel_rows(t):
    return -(-(-(-t // L_SLC)) // 16) * 16


def _nsa_proj(x, gn, wt, gq, gk, bg, cos, sin, tm):
    b, t, d = x.shape
    full = lambda a: pl.BlockSpec(a.shape, lambda i, j: (0,) * a.ndim)
    rows_spec = lambda n: pl.BlockSpec((None, KVH, n, tm), lambda i, j: (i, 0, 0, j))
    rows_shape = lambda n, dt: jax.ShapeDtypeStruct((b, KVH, n, t), dt)
    aug = [HD + _sel_rows(t), HD + V_PAD, HD, HD + V_PAD]
    return pl.pallas_call(
        _nsa_proj_kernel,
        grid=(b, t // tm),
        in_specs=[pl.BlockSpec((None, tm, d), lambda i, j: (i, j, 0)), full(gn), full(wt), full(gq), full(gk),
                  full(bg), pl.BlockSpec((ROT_HALF, tm), lambda i, j: (0, j)),
                  pl.BlockSpec((ROT_HALF, tm), lambda i, j: (0, j))],
        out_specs=[pl.BlockSpec((None, GRP * KVH, HD, tm), lambda i, j: (i, 0, 0, j))] + [rows_spec(HD)] * 6
        + [rows_spec(16)] + [rows_spec(n) for n in aug],
        out_shape=[jax.ShapeDtypeStruct((b, GRP * KVH, HD, t), BF16)] + [rows_shape(HD, F32)] * 6
        + [rows_shape(16, F32)] + [rows_shape(n, BF16) for n in aug],
        compiler_params=_params("parallel", "parallel"),
        name="nsa_proj",
    )(x, gn, wt, gq, gk, bg, cos, sin)


def _nsa_out_kernel(ot_ref, x_ref, w_ref, y_ref):
    ot = ot_ref[...].reshape(GRP * KVH * HD, ot_ref.shape[-1])
    y_ref[...] = x_ref[...] + _dot_tn(ot, w_ref[...])


def _nsa_out(ot, x, w, tm):
    b, t, d = x.shape
    return pl.pallas_call(
        _nsa_out_kernel,
        grid=(b, t // tm),
        in_specs=[pl.BlockSpec((None, GRP * KVH, HD, tm), lambda i, j: (i, 0, 0, j)),
                  pl.BlockSpec((None, tm, d), lambda i, j: (i, j, 0)),
                  pl.BlockSpec(w.shape, lambda i, j: (0, 0))],
        out_specs=pl.BlockSpec((None, tm, d), lambda i, j: (i, j, 0)),
        out_shape=jax.ShapeDtypeStruct((b, t, d), F32),
        compiler_params=_params("parallel", "parallel"),
        name="nsa_out",
    )(ot, x, w)


def _cmp_fs_kernel(*refs, n_pages, paged):
    refs = refs[1:] if paged else refs
    pages = refs[:n_pages]
    perm_ref, w_ref, o_ref, lhs_scr = refs[n_pages:]
    p = pl.program_id(1)
    half = CMP_STRIDE
    n_half = PAGE // half
    low = lax.broadcasted_iota(jnp.int32, (n_half, LANES), 1) < HD
    odd_slot = (lax.broadcasted_iota(jnp.int32, (2 * HD, PAGE), 1) // n_half) % 2 == 1
    for pair in range(KVH // 2):
        for j in range(n_pages):
            y = _dot(pages[j][2 * pair:2 * pair + 2].reshape(2 * HD, PAGE).astype(BF16), perm_ref[...])
            x = jnp.where(odd_slot, jnp.concatenate([y[HD:], y[:HD]], axis=0), y).T
            rows = pl.ds(pl.multiple_of((p * n_pages + j) * n_half, n_half), n_half)
            for t in range(half // 2):
                ev = x[2 * t * n_half:(2 * t + 1) * n_half]
                od = x[(2 * t + 1) * n_half:(2 * t + 2) * n_half]
                lanes = slice(t * LANES, (t + 1) * LANES)
                lhs_scr[2 * pair, rows, lanes] = jnp.where(low, ev, od)
                lhs_scr[2 * pair + 1, rows, lanes] = jnp.where(low, od, ev)

    @pl.when(p == pl.num_programs(1) - 1)
    def _():
        n = w_ref.shape[-1]
        for kvh in range(KVH):
            o_ref[:, kvh * n:(kvh + 1) * n] = _dot(lhs_scr[kvh].astype(BF16), w_ref[kvh % 2])


def _cmp_fs(src, w, n_pages_total, pages_per_step, page_table=None, layer=0):
    paged = page_table is not None
    b = page_table.shape[0] if paged else src.shape[0]
    n_half = PAGE // CMP_STRIDE
    nh = n_pages_total * n_half
    steps = n_pages_total // pages_per_step
    tok = jnp.arange(PAGE)
    perm = (((tok % CMP_STRIDE) * n_half + tok // CMP_STRIDE)[:, None] == tok[None, :]).astype(BF16)
    if paged:
        page_spec = lambda j: pl.BlockSpec(
            (None, None, KVH, HD, PAGE), lambda i, p, pt: (layer, pt[i, p * pages_per_step + j], 0, 0, 0))
        full = lambda a: pl.BlockSpec(a.shape, lambda i, p, pt: (0,) * a.ndim)
        o_spec = pl.BlockSpec((None, nh, KVH * 512), lambda i, p, pt: (i, 0, 0))
    else:
        page_spec = lambda j: pl.BlockSpec((None, KVH, HD, PAGE), lambda i, p: (i, 0, 0, p * pages_per_step + j))
        full = lambda a: pl.BlockSpec(a.shape, lambda i, p: (0,) * a.ndim)
        o_spec = pl.BlockSpec((None, nh, KVH * 512), lambda i, p: (i, 0, 0))
    grid_spec = pltpu.PrefetchScalarGridSpec(
        num_scalar_prefetch=1 if paged else 0,
        grid=(b, steps),
        in_specs=[page_spec(j) for j in range(pages_per_step)] + [full(perm), full(w)],
        out_specs=o_spec,
        scratch_shapes=[pltpu.VMEM((KVH, nh, CMP_STRIDE * HD), F32)],
    )
    args = ([page_table] if paged else []) + [src] * pages_per_step + [perm, w]
    return pl.pallas_call(
        functools.partial(_cmp_fs_kernel, n_pages=pages_per_step, paged=paged),
        grid_spec=grid_spec,
        out_shape=jax.ShapeDtypeStruct((b, nh, KVH * 512), F32),
        compiler_params=_params("parallel", "arbitrary"),
        name="cmp_fs",
    )(*args)


def _cmp_out_kernel(fs_ref, b1_ref, pe_ref, w1_ref, w2t_ref, gk_ref, o_ref, *, is_key):
    nh = fs_ref.shape[0]
    hidden = b1_ref.shape[1]
    c = _dot(pe_ref[0].astype(BF16), w1_ref[0]) + _dot(pe_ref[1].astype(BF16), w1_ref[1])
    bias = c[0:1, :] + b1_ref[...]
    for kvh in range(KVH):
        first = fs_ref[:, kvh * 2 * hidden:kvh * 2 * hidden + hidden]
        second = fs_ref[:, kvh * 2 * hidden + hidden:(kvh + 1) * 2 * hidden]
        hid = jax.nn.gelu(first + pltpu.roll(second, nh - 1, 0) + bias)
        yt = _dot_nt(w2t_ref[...], hid.astype(BF16))
        if is_key:
            yt = yt * lax.rsqrt(jnp.mean(yt * yt, axis=0, keepdims=True) + EPS) * gk_ref[...]
        o_ref[kvh] = yt.astype(BF16)


def _cmp_out(fs, b1, pe, w1, w2t, gk, is_key):
    b, nh, _ = fs.shape
    full = lambda a: pl.BlockSpec(a.shape, lambda i: (0,) * a.ndim)
    return pl.pallas_call(
        functools.partial(_cmp_out_kernel, is_key=is_key),
        grid=(b,),
        in_specs=[pl.BlockSpec((None, nh, fs.shape[2]), lambda i: (i, 0, 0)), full(b1), full(pe), full(w1),
                  full(w2t), full(gk)],
        out_specs=pl.BlockSpec((None, KVH, HD, nh), lambda i: (i, 0, 0, 0)),
        out_shape=jax.ShapeDtypeStruct((b, KVH, HD, nh), BF16),
        compiler_params=_params("parallel"),
        name="cmp_out",
    )(fs, b1, pe, w1, w2t, gk)


def _cmp_weights(pe, w1, b1, w2, g_kcmp, nh):
    hidden = w1.shape[-1]
    wfs = jnp.concatenate([w1[:CMP_STRIDE], w1[CMP_STRIDE:]], axis=-1)
    swapped = wfs.reshape(CMP_STRIDE // 2, 2, HD, 2 * hidden)[:, ::-1]
    w_pair = jnp.stack([wfs, swapped.reshape(wfs.shape)]).reshape(2, CMP_STRIDE * HD, 2 * hidden).astype(BF16)
    pe2 = jnp.zeros((2, 8, CMP_STRIDE * HD), F32).at[:, 0, :].set(pe.reshape(2, CMP_STRIDE * HD))
    w1f = w1.reshape(2, CMP_STRIDE * HD, hidden).astype(BF16)
    gk = jnp.broadcast_to(g_kcmp.reshape(HD, 1), (HD, nh))
    return w_pair, b1.reshape(1, hidden), pe2, w1f, w2.T.astype(BF16), gk


def _softmax_update(s, mask, vt, m, l, acc):
    m_new = jnp.maximum(m, jnp.max(jnp.where(mask, s, NEG), axis=0, keepdims=True))
    alpha = jnp.exp2(m - m_new)
    p = jnp.where(mask, jnp.exp2(s - m_new), 0.0)
    l_new = l * alpha + jnp.sum(p, axis=0, keepdims=True)
    acc_new = acc * alpha + _dot(vt, p.astype(BF16))
    return m_new, l_new, acc_new, p


def _softmax_init(lanes):
    return jnp.full((1, lanes), NEG, F32), jnp.zeros((1, lanes), F32), jnp.zeros((HD, lanes), F32)


def _finish(l, acc):
    return acc / jnp.maximum(l, 1e-30)


def _split_dot(mt, x):
    hi = x.astype(BF16)
    lo = (x - hi.astype(F32)).astype(BF16)
    return _dot(mt, hi) + _dot(mt, lo)


def _attn_prompt_kernel(q_ref, kc_ref, vc_ref, ksa_ref, vsa_ref, kwb_ref, vwa_ref, gt_ref, mt_ref, o_ref,
                        s_scr, s2_scr, *, tq, n_cmp, n_sel):
    i = pl.program_id(2)
    lanes = GRP * tq
    ncp = kc_ref.shape[1]
    nsp = mt_ref.shape[0]
    qt = jnp.concatenate([q_ref[h] for h in range(GRP)], axis=1)

    def col_max(mx, s):
        return jnp.maximum(mx, jnp.max(s.reshape(s.shape[0] // 8, 8, lanes), axis=0))

    def finish(acc):
        return acc[0:HD] / jnp.maximum(acc[HD:HD + 1], 1e-30)

    tpos1 = i * tq + lax.broadcasted_iota(jnp.int32, (1, tq), 1)
    tpos = jnp.concatenate([tpos1] * GRP, axis=1)

    n_idx = lax.broadcasted_iota(jnp.int32, (ncp, tq), 0)
    mask = (n_idx * CMP_STRIDE + (L_CMP - 1) <= tpos1) & (n_idx < n_cmp)
    imp = jnp.zeros((ncp, tq), F32)
    o_c = []
    for h in range(GRP):
        s = _dot_tn(kc_ref[...], q_ref[h])
        m, l, acc, p = _softmax_update(s, mask, vc_ref[...], *_softmax_init(tq))
        o_c.append(_finish(l, acc))
        imp = imp + p / jnp.maximum(l, 1e-30)
    o_c = jnp.concatenate(o_c, axis=1)
    p_slc = _split_dot(mt_ref[...], imp)

    j_idx = lax.broadcasted_iota(jnp.int32, (nsp, tq), 0)
    cur = tpos1 // L_SLC
    valid = j_idx <= cur
    forced = (j_idx == 0) | (j_idx == cur) | (j_idx == cur - 1)
    score = jnp.where(valid & forced, jnp.inf, jnp.where(valid, p_slc, -jnp.inf))
    bits = lax.bitcast_convert_type(score, jnp.int32)
    key = jnp.where(bits >= 0, bits, bits ^ 0x7FFFFFFF)
    n_grp = nsp // 8
    keys = [key[8 * r:8 * r + 8] for r in range(n_grp)]
    keys_m1 = [k - 1 for k in keys]
    sub = lax.broadcasted_iota(jnp.int32, (8, tq), 0)

    def count_group(grp, ranks):
        ranks = list(ranks)
        for u in range(8):
            row = jnp.broadcast_to(keys[grp][u:u + 1, :], (8, tq))
            for r in range(n_grp):
                thr = keys[r] if r < grp else keys_m1[r] if r > grp else jnp.where(sub > u, keys_m1[r], keys[r])
                ranks[r] = ranks[r] + jnp.where(row > thr, 1.0, 0.0)
        return tuple(ranks)

    ranks = tuple(jnp.zeros((8, tq), F32) for _ in range(n_grp))
    last_valid = (i * tq + tq - 1) // L_SLC
    for grp in range(n_grp):
        ranks = lax.cond(8 * grp <= last_valid, functools.partial(count_group, grp), lambda r: r, ranks)
    bias = jnp.where(jnp.concatenate(ranks, axis=0) < n_sel, 0.0, NEG)
    qa = jnp.concatenate([qt, jnp.concatenate([bias] * GRP, axis=1).astype(BF16)], axis=0)

    row_k = lax.broadcasted_iota(jnp.int32, (KT, tq), 0)

    def slc_scores(st, buf):
        start = pl.multiple_of(st * KT, KT)
        buf[0:KT, :] = _dot_tn(ksa_ref[:, pl.ds(start, KT)], qa)

    def slc_softmax(st, buf, carry, diagonal=False):
        m, acc = carry
        vt = vsa_ref[:, pl.ds(pl.multiple_of(st * KT, KT), KT)]
        ms, accs = [], []
        for h in range(GRP):
            ls = slice(h * tq, (h + 1) * tq)
            if diagonal:
                buf[0:KT, ls] = jnp.where(st * KT + row_k <= tpos1, buf[0:KT, ls], NEG)
            mx = jnp.max(buf[0:KT, ls].reshape(KT // 8, 8, tq), axis=0)
            m_new = jnp.maximum(m[:, ls], jnp.max(mx, axis=0, keepdims=True))
            p = jnp.exp2(buf[0:KT, ls] - m_new).astype(BF16)
            accs.append(acc[:, ls] * jnp.exp2(m[:, ls] - m_new) + _dot(vt, p))
            ms.append(m_new)
        return jnp.concatenate(ms, axis=1), jnp.concatenate(accs, axis=1)

    def slc_pair(u, carry):
        slc_scores(2 * u + 1, s2_scr)
        carry = slc_softmax(2 * u, s_scr, carry)
        slc_scores(2 * u + 2, s_scr)
        return slc_softmax(2 * u + 1, s2_scr, carry)

    def slc_tail_odd(carry):
        slc_scores(n_full, s2_scr)
        carry = slc_softmax(n_full - 1, s_scr, carry)
        return slc_softmax(n_full, s2_scr, carry, diagonal=True)

    def slc_tail_even(carry):
        return slc_softmax(n_full, s_scr, carry, diagonal=True)

    n_full = (i * tq) // KT
    slc_scores(0, s_scr)
    init = (jnp.full((1, lanes), NEG, F32), jnp.zeros((HD + V_PAD, lanes), F32))
    carry = lax.fori_loop(0, n_full // 2, slc_pair, init)
    o_s = finish(lax.cond(n_full % 2 == 1, slc_tail_odd, slc_tail_even, carry)[1])

    n_wt = WINDOW // tq + 1
    w0 = jnp.maximum(i - (n_wt - 1), 0) * tq
    row_w = lax.broadcasted_iota(jnp.int32, (tq, lanes), 0)
    col_w = jnp.concatenate([lax.broadcasted_iota(jnp.int32, (tq, tq), 1)] * GRP, axis=1)

    def window(steady):
        mx = jnp.full((8, lanes), NEG, F32)
        for r in range(n_wt):
            start = pl.multiple_of(w0 + r * tq, tq)
            s = _dot_tn(kwb_ref[:, pl.ds(start, tq)], qt)
            if not steady:
                rel = tpos - (start + row_w)
                s = jnp.where((rel >= 0) & (rel <= WINDOW), s, NEG)
            elif r == 0:
                s = jnp.where(row_w >= col_w, s, NEG)
            elif r == n_wt - 1:
                s = jnp.where(row_w <= col_w, s, NEG)
            s_scr[r * tq:(r + 1) * tq, :] = s
            mx = col_max(mx, s)
        m = jnp.max(mx, axis=0, keepdims=True)
        acc = jnp.zeros((HD + V_PAD, lanes), F32)
        for r in range(n_wt):
            start = pl.multiple_of(w0 + r * tq, tq)
            p = jnp.exp2(s_scr[r * tq:(r + 1) * tq, :] - m).astype(BF16)
            acc = acc + _dot(vwa_ref[:, pl.ds(start, tq)], p)
        return finish(acc)

    o_w = lax.cond(i >= n_wt - 1, functools.partial(window, True), functools.partial(window, False))

    for h in range(GRP):
        ls = slice(h * tq, (h + 1) * tq)
        o = (gt_ref[3 * h:3 * h + 1, :] * o_c[:, ls] + gt_ref[3 * h + 1:3 * h + 2, :] * o_s[:, ls]
             + gt_ref[3 * h + 2:3 * h + 3, :] * o_w[:, ls])
        o_ref[h] = o.astype(BF16)


def _slc_matrix(nsp, ncp, n_cmp):
    ratio = L_SLC // CMP_STRIDE
    j = jnp.arange(nsp)[:, None]
    n = jnp.arange(ncp)[None, :]
    m = ((n >= ratio * j) & (n <= ratio * j + ratio - 1)).astype(F32)
    m = m + ((n >= ratio * j - 1) & (n <= ratio * j + ratio - 2)).astype(F32)
    return jnp.where(n < n_cmp, m, 0.0).astype(BF16)


def _attn_prompt(qt, kct, vct, ksa, vsa, kwb, vwa, gt, tq=4 * LANES):
    b, _, _, t = qt.shape
    assert t % KT == 0 and WINDOW % tq == 0 and t >= WINDOW + tq
    ncp = kct.shape[-1]
    n_cmp = t // CMP_STRIDE - 1
    ns = -(-t // L_SLC)
    mt = _slc_matrix(_sel_rows(t), ncp, n_cmp)
    res = lambda a: pl.BlockSpec((None, None) + a.shape[2:], lambda i, g, j: (i, g, 0, 0))
    return pl.pallas_call(
        functools.partial(_attn_prompt_kernel, tq=tq, n_cmp=n_cmp, n_sel=min(N_SEL, ns)),
        grid=(b, KVH, t // tq),
        in_specs=[pl.BlockSpec((None, GRP, HD, tq), lambda i, g, j: (i, g, 0, j)),
                  res(kct), res(vct), res(ksa), res(vsa), res(kwb), res(vwa),
                  pl.BlockSpec((None, None, 16, tq), lambda i, g, j: (i, g, 0, j)),
                  pl.BlockSpec(mt.shape, lambda i, g, j: (0, 0))],
        out_specs=pl.BlockSpec((None, GRP, HD, tq), lambda i, g, j: (i, g, 0, j)),
        out_shape=jax.ShapeDtypeStruct(qt.shape, BF16),
        scratch_shapes=[pltpu.VMEM((max(KT, WINDOW + tq), GRP * tq), F32), pltpu.VMEM((KT, GRP * tq), F32)],
        compiler_params=_params("parallel", "parallel", "arbitrary"),
        name="nsa_attn_prompt",
    )(qt, kct, vct, ksa, vsa, kwb, vwa, gt, mt)


def _attn_sample_kernel(*refs, n_pages, n_cmp, n_blocks, n_sel):
    refs = refs[1:]
    k_pages = refs[:n_pages]
    v_pages = refs[n_pages:2 * n_pages]
    (q_ref, kc_ref, vc_ref, kw_ref, vw_ref, kns_ref, vns_ref, knw_ref, vnw_ref, gt_ref, mt_ref,
     o_ref, sel_scr, oc_scr, ow_scr, m_scr, l_scr, acc_scr) = refs[2 * n_pages:]
    p = pl.program_id(1)
    ncp = kc_ref.shape[-1]
    nsp = mt_ref.shape[0]
    rows_all = KVH * HD
    first_row = lax.broadcasted_iota(jnp.int32, (LANES, LANES), 0) == 0
    qt = q_ref[...]

    def stacked(ref):
        return ref[...].reshape(rows_all, ref.shape[-1]).astype(BF16)

    def finish(l, acc):
        return acc / jnp.maximum(l, 1e-30)

    def init():
        return jnp.full((1, LANES), NEG, F32), jnp.zeros((1, LANES), F32), jnp.zeros((rows_all, LANES), F32)

    @pl.when(p == 0)
    def _():
        lane = lax.broadcasted_iota(jnp.int32, (ncp, LANES), 1)
        n_idx = lax.broadcasted_iota(jnp.int32, (ncp, LANES), 0)
        j_col = lax.broadcasted_iota(jnp.int32, (nsp, LANES), 0)
        lane_s = lax.broadcasted_iota(jnp.int32, (nsp, LANES), 1)
        jp_idx = lax.broadcasted_iota(jnp.int32, (nsp, nsp), 0)
        j_idx = lax.broadcasted_iota(jnp.int32, (nsp, nsp), 1)
        cur = n_blocks - 1
        s = _dot_tn(stacked(kc_ref), qt)
        m, l, acc, pr = _softmax_update(s, n_idx < n_cmp, stacked(vc_ref), *init())
        oc_scr[...] = finish(l, acc)
        pr = pr / jnp.maximum(l, 1e-30)
        sel_all = jnp.zeros((nsp, LANES), F32)
        for g in range(KVH):
            mine = (lane >= g * GRP) & (lane < (g + 1) * GRP)
            imp = jnp.sum(jnp.where(mine, pr, 0.0), axis=1, keepdims=True)
            p_slc = _split_dot(mt_ref[...], jnp.broadcast_to(imp, (ncp, LANES)))
            valid = j_col <= cur
            forced = (j_col == 0) | (j_col == cur) | (j_col == cur - 1)
            score = jnp.where(valid & forced, jnp.inf, jnp.where(valid, p_slc, -jnp.inf))
            col = jnp.concatenate([score] * (nsp // LANES), axis=1)
            row = col.T
            ahead = (col > row) | ((col == row) & (jp_idx < j_idx))
            rank = jnp.sum(jnp.where(ahead, 1.0, 0.0), axis=0, keepdims=True)
            sel = jnp.broadcast_to(jnp.where(rank < n_sel, 1.0, 0.0), (LANES, nsp)).T
            sel_all = jnp.where((lane_s >= g * GRP) & (lane_s < (g + 1) * GRP), sel, sel_all)
        sel_scr[...] = sel_all
        s = _dot_tn(stacked(kw_ref), qt)
        st = _softmax_update(s, jnp.full(s.shape, True), stacked(vw_ref), *init())[:3]
        s = _dot_tn(stacked(knw_ref), qt)
        _, l, acc = _softmax_update(s, first_row, stacked(vnw_ref), *st)[:3]
        ow_scr[...] = finish(l, acc)
        m_scr[...], l_scr[...], acc_scr[...] = init()

    blocks_per_page = PAGE // L_SLC
    kt = jnp.concatenate([k_pages[j][...].reshape(rows_all, PAGE) for j in range(n_pages)], axis=1).astype(BF16)
    vt = jnp.concatenate([v_pages[j][...].reshape(rows_all, PAGE) for j in range(n_pages)], axis=1).astype(BF16)
    s = _dot_tn(kt, qt)
    rows = []
    for r in range(n_pages * blocks_per_page):
        e = sel_scr[pl.ds(p * n_pages * blocks_per_page + r, 1), :]
        rows.append(jnp.broadcast_to(e, (L_SLC, LANES)))
    mask = jnp.concatenate(rows, axis=0) > 0.5
    m_scr[...], l_scr[...], acc_scr[...] = _softmax_update(s, mask, vt, m_scr[...], l_scr[...], acc_scr[...])[:3]

    @pl.when(p == pl.num_programs(1) - 1)
    def _():
        s = _dot_tn(stacked(kns_ref), qt)
        _, l, acc = _softmax_update(s, first_row, stacked(vns_ref), m_scr[...], l_scr[...], acc_scr[...])[:3]
        o_ref[...] = (gt_ref[0:1, :] * oc_scr[...] + gt_ref[1:2, :] * finish(l, acc) + gt_ref[2:3, :] * ow_scr[...])


def _attn_sample(qs, kct, vct, pool_k, pool_v, page_table, layer, kwt, vwt, kns, vns, knw, vnw, gs,
                 pages_per_step=16):
    b, n_pages_total = page_table.shape
    pages_per_step = min(pages_per_step, n_pages_total)
    past = n_pages_total * PAGE
    n_cmp = past // CMP_STRIDE - 1
    ncp = kct.shape[-1]
    n_blocks = past // L_SLC + 1
    nsp = -(-n_blocks // LANES) * LANES
    mt = _slc_matrix(nsp, ncp, n_cmp)
    steps = n_pages_total // pages_per_step
    page_spec = lambda j: pl.BlockSpec((None, None, KVH, HD, PAGE),
                                       lambda i, p, pt: (layer, pt[i, p * pages_per_step + j], 0, 0, 0))
    per_b = lambda a: pl.BlockSpec((None,) + a.shape[1:], lambda i, p, pt: (i,) + (0,) * (a.ndim - 1))
    small = [qs, kct, vct, kwt, vwt, kns, vns, knw, vnw, gs]
    grid_spec = pltpu.PrefetchScalarGridSpec(
        num_scalar_prefetch=1,
        grid=(b, steps),
        in_specs=[page_spec(j) for j in range(pages_per_step)] * 2 + [per_b(a) for a in small]
        + [pl.BlockSpec(mt.shape, lambda i, p, pt: (0, 0))],
        out_specs=pl.BlockSpec((None, KVH * HD, LANES), lambda i, p, pt: (i, 0, 0)),
        scratch_shapes=[pltpu.VMEM((nsp, LANES), F32), pltpu.VMEM((KVH * HD, LANES), F32),
                        pltpu.VMEM((KVH * HD, LANES), F32), pltpu.VMEM((1, LANES), F32),
                        pltpu.VMEM((1, LANES), F32), pltpu.VMEM((KVH * HD, LANES), F32)],
    )
    return pl.pallas_call(
        functools.partial(_attn_sample_kernel, n_pages=pages_per_step, n_cmp=n_cmp, n_blocks=n_blocks,
                          n_sel=min(N_SEL, n_blocks)),
        grid_spec=grid_spec,
        out_shape=jax.ShapeDtypeStruct((b, KVH * HD, LANES), F32),
        compiler_params=_params("parallel", "arbitrary"),
        name="nsa_attn_sample",
    )(page_table, *([pool_k] * pages_per_step), *([pool_v] * pages_per_step), *small, mt)


def _rope_tables(pos):
    inv_freq = ROPE_THETA ** (-jnp.arange(ROT_HALF, dtype=F32) * 2.0 / (2 * ROT_HALF))
    ang = pos.astype(F32)[:, None] * inv_freq[None, :]
    return jnp.cos(ang).T, jnp.sin(ang).T


def _nsa_layer(xp, xs, pools, win_bufs, page_table, layer, norm_g, w_in, b_gate, g_q, g_k, g_kcmp,
               cmp_pe, cmp_w1, cmp_b1, cmp_w2, w_out):
    b, t, d = xp.shape
    bs = xs.shape[0]
    past = page_table.shape[1] * PAGE
    nq, nkv = GRP * KVH * HD, KVH * HD
    n_gate = 3 * GRP * KVH

    wt_g = jnp.pad(w_in[:, nq + 6 * nkv:].T.reshape(KVH, n_gate // KVH, d), ((0, 0), (0, 16 - n_gate // KVH), (0, 0)))
    wt = jnp.concatenate([w_in[:, :nq + 6 * nkv].T, wt_g.reshape(KVH * 16, d)], axis=0).astype(BF16)
    bg = jnp.pad(b_gate.reshape(KVH, n_gate // KVH), ((0, 0), (0, 16 - n_gate // KVH))).reshape(KVH * 16, 1)
    gn = norm_g.reshape(1, d)
    wo = w_out.astype(BF16)

    def project(x, pos, tm):
        cos, sin = _rope_tables(pos)
        col = lambda v: jnp.broadcast_to(v[..., None], v.shape + (tm,))
        return _nsa_proj(x, gn, wt, col(g_q), col(g_k), col(bg[:, 0]), cos, sin, tm)

    tm = min(512, t)
    qt, kct_rows, vct_rows, kst, vst, kwt, vwt, gt, ksa, vsa, kwb, vwa = project(xp, jnp.arange(t), tm)
    pages = t // PAGE
    nh = pages * (PAGE // CMP_STRIDE)
    cmp_w = [_cmp_weights(cmp_pe[i], cmp_w1[i], cmp_b1[i], cmp_w2[i], g_kcmp, nh) for i in range(2)]
    pps = min(16, pages)
    kct = _cmp_out(_cmp_fs(kct_rows, cmp_w[0][0], pages, pps), *cmp_w[0][1:], is_key=True)
    vct = _cmp_out(_cmp_fs(vct_rows, cmp_w[1][0], pages, pps), *cmp_w[1][1:], is_key=False)
    ot = _attn_prompt(qt, kct, vct, ksa, vsa, kwb, vwa, gt)
    yp = _nsa_out(ot, xp, wo, tm)
    n_win = min(WINDOW, t)
    rows_p = [kct_rows, vct_rows, kst, vst, kwt[..., t - n_win:], vwt[..., t - n_win:]]

    xs_pad = jnp.pad(xs.reshape(1, bs, d), ((0, 0), (0, LANES - bs), (0, 0)))
    outs = project(xs_pad, jnp.full((LANES,), past), LANES)
    qt_s, rows_s, gt_s = outs[0], outs[1:7], outs[7]
    pages_s = page_table.shape[1]
    nh_s = pages_s * (PAGE // CMP_STRIDE)
    cmp_ws = [_cmp_weights(cmp_pe[i], cmp_w1[i], cmp_b1[i], cmp_w2[i], g_kcmp, nh_s) for i in range(2)]
    pool5 = [jnp.transpose(pl_, (0, 1, 3, 4, 2)) for pl_ in pools]
    pps = min(32, pages_s)
    kct_s = _cmp_out(_cmp_fs(pool5[0], cmp_ws[0][0], pages_s, pps, page_table, layer), *cmp_ws[0][1:], is_key=True)
    vct_s = _cmp_out(_cmp_fs(pool5[1], cmp_ws[1][0], pages_s, pps, page_table, layer), *cmp_ws[1][1:], is_key=False)
    qs = jnp.transpose(qt_s[0, :, :, :bs].reshape(KVH, GRP, HD, bs), (3, 0, 2, 1))
    qs = jnp.einsum("bgdh,gk->bgdkh", qs, jnp.eye(KVH, dtype=qs.dtype))
    qs = jnp.pad(qs.reshape(bs, KVH * HD, KVH * GRP), ((0, 0), (0, 0), (0, LANES - KVH * GRP)))
    new = lambda a: jnp.pad(jnp.transpose(a[0, :, :, :bs], (2, 0, 1))[..., None],
                            ((0, 0), (0, 0), (0, 0), (0, LANES - 1)))
    gs = jnp.transpose(gt_s[0, :, :n_gate // KVH, :bs].reshape(KVH, GRP, 3, bs), (3, 2, 0, 1))
    gs = jnp.pad(gs.reshape(bs, 3, KVH * GRP), ((0, 0), (0, 8 - 3), (0, LANES - KVH * GRP)))
    win5 = [jnp.transpose(wb[layer], (0, 2, 3, 1)) for wb in win_bufs]
    ot_s = _attn_sample(qs, kct_s, vct_s, pool5[2], pool5[3], page_table, layer, win5[0], win5[1],
                        new(rows_s[2]), new(rows_s[3]), new(rows_s[4]), new(rows_s[5]), gs)
    ot_s = ot_s.reshape(bs, KVH, HD, LANES)
    ot_s = jnp.stack([ot_s[:, g, :, g * GRP:(g + 1) * GRP] for g in range(KVH)])
    ot_s = jnp.transpose(ot_s, (0, 3, 2, 1)).reshape(1, GRP * KVH, HD, bs)
    ot_s = jnp.pad(ot_s, ((0, 0), (0, 0), (0, 0), (0, LANES - bs))).astype(BF16)
    ys = _nsa_out(ot_s, xs_pad, wo, LANES)[0, :bs].reshape(bs, 1, d)

    to_rows = lambda a: jnp.transpose(a, (0, 3, 1, 2))
    rows_p = [to_rows(a) for a in rows_p]
    rows_s = [to_rows(a[:, :, :, :bs]).reshape(bs, 1, KVH, HD) for a in rows_s]
    return yp, ys, rows_p, rows_s


def kernel(x_prompt, x_sample, state_gla, cache_k_cmp, cache_v_cmp, cache_k_slc, cache_v_slc, cache_k_win, cache_v_win, page_table, norm_mix, norm_mlp, mlp_up, mlp_down, gla_w_in, gla_w_gate2, gla_b_gate, gla_g_out, gla_w_out, nsa_w_in, nsa_b_gate, nsa_g_q, nsa_g_k, nsa_g_kcmp, nsa_cmp_pe, nsa_cmp_w1, nsa_cmp_b1, nsa_cmp_w2, nsa_w_out):
    depth = norm_mix.shape[0]
    b, t, d = x_prompt.shape
    bs = x_sample.shape[0]
    xp, xs = x_prompt, x_sample
    gla_p, gla_s, nsa_p, nsa_s = [], [], [], []
    pools = (cache_k_cmp, cache_v_cmp, cache_k_slc, cache_v_slc)
    for i in range(depth):
        j = i // 2
        if i % 2 == 0:
            xp, xs, sp, ss = _gla_layer(xp, xs, state_gla[j], norm_mix[i], gla_w_in[j], gla_w_gate2[j],
                                        gla_b_gate[j], gla_g_out[j], gla_w_out[j])
            gla_p.append(sp)
            gla_s.append(ss)
        else:
            xp, xs, rp, rs = _nsa_layer(xp, xs, pools, (cache_k_win, cache_v_win), page_table, j, norm_mix[i],
                                        nsa_w_in[j], nsa_b_gate[j], nsa_g_q[j], nsa_g_k[j], nsa_g_kcmp[j],
                                        nsa_cmp_pe[j], nsa_cmp_w1[j], nsa_cmp_b1[j], nsa_cmp_w2[j], nsa_w_out[j])
            nsa_p.append(rp)
            nsa_s.append(rs)
        g = norm_mlp[i].reshape(1, d)
        wu, wd = mlp_up[i].astype(BF16), mlp_down[i].astype(BF16)
        xp = _mlp(xp.reshape(b * t, d), g, wu, wd, tm=min(1024, b * t)).reshape(b, t, d)
        xs = _mlp(xs.reshape(bs, d), g, wu, wd, tm=bs).reshape(bs, 1, d)
    stack = lambda lst, r: jnp.stack([e[r] for e in lst])
    return (xp, xs, jnp.stack(gla_p), jnp.stack(gla_s),
            stack(nsa_p, 0), stack(nsa_s, 0), stack(nsa_p, 1), stack(nsa_s, 1),
            stack(nsa_p, 2), stack(nsa_s, 2), stack(nsa_p, 3), stack(nsa_s, 3),
            stack(nsa_p, 4), stack(nsa_s, 4), stack(nsa_p, 5), stack(nsa_s, 5))
```

```python
import functools

import jax
import jax.numpy as jnp
from jax import lax
from jax.experimental import pallas as pl
from jax.experimental.pallas import tpu as pltpu

F32 = jnp.float32
BF16 = jnp.bfloat16
EPS = 1e-6
NEG = -1e30
VMEM_LIMIT_BYTES = 48 * 1024 * 1024
LANES = 128
PAGE = 128
HD = 64
KVH = 4
GRP = 4
L_CMP, CMP_STRIDE, L_SLC, N_SEL, WINDOW = 32, 16, 64, 16, 512
GLA_H, GLA_DK, GLA_DV, GLA_CHUNK = 4, 128, 256, 64
ROPE_THETA, ROT_HALF = 500000.0, 8
LOG2E = 1.4426950408889634
V_PAD = 16
KT = 512


def _params(*sem):
    return pltpu.CompilerParams(dimension_semantics=sem, vmem_limit_bytes=VMEM_LIMIT_BYTES)


def _dot(a, b):
    return jnp.dot(a, b, preferred_element_type=F32)


def _dot_nt(a, b):
    return lax.dot_general(a, b, (((1,), (1,)), ((), ())), preferred_element_type=F32)


def _dot_tn(a, b):
    return lax.dot_general(a, b, (((0,), (0,)), ((), ())), preferred_element_type=F32)


def _rms_rows(x, g):
    ms = jnp.mean(x * x, axis=-1, keepdims=True)
    return x * lax.rsqrt(ms + EPS) * g


def _mlp_kernel(x_ref, g_ref, wu_ref, wd_ref, o_ref, h_scr, acc_scr):
    f = pl.program_id(1)

    @pl.when(f == 0)
    def _():
        h_scr[...] = _rms_rows(x_ref[...], g_ref[...]).astype(BF16)
        acc_scr[...] = jnp.zeros_like(acc_scr)

    u = jnp.maximum(_dot(h_scr[...], wu_ref[...]), 0.0)
    acc_scr[...] += _dot((u * u).astype(BF16), wd_ref[...])

    @pl.when(f == pl.num_programs(1) - 1)
    def _():
        o_ref[...] = x_ref[...] + acc_scr[...]


def _mlp(x, g, wu, wd, tm, tf=1024):
    m, d = x.shape
    ff = wu.shape[1]
    return pl.pallas_call(
        _mlp_kernel,
        grid=(m // tm, ff // tf),
        in_specs=[pl.BlockSpec((tm, d), lambda i, f: (i, 0)),
                  pl.BlockSpec((1, d), lambda i, f: (0, 0)),
                  pl.BlockSpec((d, tf), lambda i, f: (0, f)),
                  pl.BlockSpec((tf, d), lambda i, f: (f, 0))],
        out_specs=pl.BlockSpec((tm, d), lambda i, f: (i, 0)),
        out_shape=jax.ShapeDtypeStruct((m, d), F32),
        scratch_shapes=[pltpu.VMEM((tm, d), BF16), pltpu.VMEM((tm, d), F32)],
        compiler_params=_params("parallel", "arbitrary"),
        name="mlp_block",
    )(x, g, wu, wd)


def _gla_proj_kernel(x_ref, g_ref, w_ref, wg2_ref, bg_ref, q_ref, k_ref, v_ref, r_ref, gl_ref):
    h = _rms_rows(x_ref[...], g_ref[...]).astype(BF16)
    dk, dv = GLA_H * GLA_DK, GLA_H * GLA_DV
    q_ref[...] = _dot(h, w_ref[:, 0:dk]) * (GLA_DK ** -0.5)
    k_ref[...] = _dot(h, w_ref[:, dk:2 * dk])
    v_ref[...] = _dot(h, w_ref[:, 2 * dk:2 * dk + dv])
    r_ref[...] = _dot(h, w_ref[:, 2 * dk + dv:2 * dk + 2 * dv])
    gr = _dot(h, w_ref[:, 2 * dk + 2 * dv:])
    xg = _dot(gr.astype(BF16), wg2_ref[...]) + bg_ref[...]
    gl_ref[...] = jax.nn.log_sigmoid(xg) * (1.0 / 16.0)


def _gla_proj(x, g, w, wg2, bg, tm):
    m, d = x.shape
    dk, dv = GLA_H * GLA_DK, GLA_H * GLA_DV
    row = lambda n: pl.BlockSpec((tm, n), lambda i: (i, 0))
    full = lambda a: pl.BlockSpec(a.shape, lambda i: (0, 0))
    return pl.pallas_call(
        _gla_proj_kernel,
        grid=(m // tm,),
        in_specs=[row(d), full(g), full(w), full(wg2), full(bg)],
        out_specs=[row(dk), row(dk), row(dv), row(dv), row(dk)],
        out_shape=[jax.ShapeDtypeStruct((m, n), F32) for n in (dk, dk, dv, dv, dk)],
        compiler_params=_params("parallel"),
        name="gla_proj",
    )(x, g, w, wg2, bg)


def _gla_scan_kernel(q_ref, k_ref, g_ref, v_ref, o_ref, s_ref, st_scr, *, n_chunks, n_heads):
    c_len = GLA_CHUNK
    step = pl.program_id(2)

    @pl.when(step == 0)
    def _():
        st_scr[...] = jnp.zeros_like(st_scr)

    rowi = lax.broadcasted_iota(jnp.int32, (c_len, GLA_DK), 0)
    causal = (lax.broadcasted_iota(jnp.int32, (c_len, c_len), 0)
              >= lax.broadcasted_iota(jnp.int32, (c_len, c_len), 1))

    def body(c, carry):
        sl = pl.ds(pl.multiple_of(c * c_len, c_len), c_len)
        for h in range(n_heads):
            ks = slice(h * GLA_DK, (h + 1) * GLA_DK)
            vs = slice(h * GLA_DV, (h + 1) * GLA_DV)
            b = g_ref[sl, ks]
            sh = 1
            while sh < c_len:
                b = b + jnp.where(rowi >= sh, pltpu.roll(b, sh, 0), 0.0)
                sh *= 2
            b_last = b[c_len - 1:c_len, :]
            b_mid = b[c_len // 2 - 1:c_len // 2, :]
            q = q_ref[sl, ks]
            k = k_ref[sl, ks]
            v = v_ref[sl, vs].astype(BF16)
            qe = (q * jnp.exp(b)).astype(BF16)
            qa = (q * jnp.exp(b - b_mid)).astype(BF16)
            ka = (k * jnp.exp(b_mid - b)).astype(BF16)
            kd = (k * jnp.exp(b_last - b)).astype(BF16)
            a = jnp.where(causal, _dot_nt(qa, ka), 0.0)
            st = st_scr[h]
            o_ref[sl, vs] = _dot_nt(qe, st.astype(BF16)) + _dot(a.astype(BF16), v)
            st_scr[h] = st * jnp.exp(b_last) + _dot_tn(v, kd)
        return carry

    lax.fori_loop(0, n_chunks, body, 0)

    @pl.when(step == pl.num_programs(2) - 1)
    def _():
        for h in range(n_heads):
            s_ref[h] = st_scr[h].T


def _gla_scan(q, k, gl, v, n_heads=4, t_blk=1024):
    b, t, _ = q.shape
    t_blk = min(t_blk, t)
    kq = pl.BlockSpec((None, t_blk, n_heads * GLA_DK), lambda i, h, j: (i, j, h))
    vv = pl.BlockSpec((None, t_blk, n_heads * GLA_DV), lambda i, h, j: (i, j, h))
    return pl.pallas_call(
        functools.partial(_gla_scan_kernel, n_chunks=t_blk // GLA_CHUNK, n_heads=n_heads),
        grid=(b, GLA_H // n_heads, t // t_blk),
        in_specs=[kq, kq, kq, vv],
        out_specs=[vv, pl.BlockSpec((None, n_heads, GLA_DK, GLA_DV), lambda i, h, j: (i, h, 0, 0))],
        out_shape=[jax.ShapeDtypeStruct((b, t, GLA_H * GLA_DV), F32),
                   jax.ShapeDtypeStruct((b, GLA_H, GLA_DK, GLA_DV), F32)],
        scratch_shapes=[pltpu.VMEM((n_heads, GLA_DV, GLA_DK), F32)],
        compiler_params=_params("parallel", "parallel", "arbitrary"),
        name="gla_scan",
    )(q, k, gl, v)


def _gla_step_kernel(q_ref, k_ref, g_ref, v_ref, s0_ref, o_ref, s_ref):
    def col(x):
        return jnp.broadcast_to(x, (LANES, LANES)).T

    for h in range(GLA_H):
        ks = slice(h * GLA_DK, (h + 1) * GLA_DK)
        qc, kc, ec = col(q_ref[:, ks]), col(k_ref[:, ks]), col(jnp.exp(g_ref[:, ks]))
        for half in range(GLA_DV // LANES):
            vs = slice(h * GLA_DV + half * LANES, h * GLA_DV + (half + 1) * LANES)
            ss = slice(half * LANES, (half + 1) * LANES)
            sn = ec * s0_ref[h, :, ss] + kc * v_ref[:, vs]
            s_ref[h, :, ss] = sn
            o_ref[:, vs] = jnp.sum(qc * sn, axis=0, keepdims=True)


def _gla_step(q, k, gl, v, s0):
    b = q.shape[0]
    kq = pl.BlockSpec((None, 1, GLA_H * GLA_DK), lambda i: (i, 0, 0))
    vv = pl.BlockSpec((None, 1, GLA_H * GLA_DV), lambda i: (i, 0, 0))
    st = pl.BlockSpec((None, GLA_H, GLA_DK, GLA_DV), lambda i: (i, 0, 0, 0))
    return pl.pallas_call(
        _gla_step_kernel,
        grid=(b,),
        in_specs=[kq, kq, kq, vv, st],
        out_specs=[vv, st],
        out_shape=[jax.ShapeDtypeStruct((b, 1, GLA_H * GLA_DV), F32),
                   jax.ShapeDtypeStruct((b, GLA_H, GLA_DK, GLA_DV), F32)],
        compiler_params=_params("parallel"),
        name="gla_step",
    )(q, k, gl, v, s0)


def _gla_out_kernel(o_ref, r_ref, x_ref, go_ref, w_ref, y_ref):
    parts = []
    for h in range(GLA_H):
        sl = slice(h * GLA_DV, (h + 1) * GLA_DV)
        r = r_ref[:, sl]
        parts.append((_rms_rows(o_ref[:, sl], go_ref[...]) * (r * jax.nn.sigmoid(r))).astype(BF16))
    y_ref[...] = x_ref[...] + _dot(jnp.concatenate(parts, axis=1), w_ref[...])


def _gla_out(o, r, x, go, w, tm):
    m, d = x.shape
    row = pl.BlockSpec((tm, d), lambda i: (i, 0))
    full = lambda a: pl.BlockSpec(a.shape, lambda i: (0, 0))
    return pl.pallas_call(
        _gla_out_kernel,
        grid=(m // tm,),
        in_specs=[row, row, row, full(go), full(w)],
        out_specs=row,
        out_shape=jax.ShapeDtypeStruct((m, d), F32),
        compiler_params=_params("parallel"),
        name="gla_out",
    )(o, r, x, go, w)


def _gla_layer(xp, xs, s0, norm_g, w_in, w_gate2, b_gate, g_out, w_out):
    b, t, d = xp.shape
    bs = xs.shape[0]
    dk, dv = GLA_H * GLA_DK, GLA_H * GLA_DV
    rank = w_gate2.shape[0]
    w = jnp.pad(w_in, ((0, 0), (0, LANES - rank))).astype(BF16)
    wg2 = jnp.pad(w_gate2, ((0, LANES - rank), (0, 0))).astype(BF16)
    g = norm_g.reshape(1, d)
    bg = b_gate.reshape(1, dk)
    go = g_out.reshape(1, GLA_DV)
    wo = w_out.astype(BF16)
    x2 = xp.reshape(b * t, d)
    q, k, v, r, gl = _gla_proj(x2, g, w, wg2, bg, tm=512)
    o, sp = _gla_scan(q.reshape(b, t, dk), k.reshape(b, t, dk), gl.reshape(b, t, dk), v.reshape(b, t, dv))
    yp = _gla_out(o.reshape(b * t, dv), r, x2, go, wo, tm=512).reshape(b, t, d)
    xs2 = xs.reshape(bs, d)
    q, k, v, r, gl = _gla_proj(xs2, g, w, wg2, bg, tm=bs)
    o, ss = _gla_step(q.reshape(bs, 1, dk), k.reshape(bs, 1, dk), gl.reshape(bs, 1, dk), v.reshape(bs, 1, dv), s0)
    ys = _gla_out(o.reshape(bs, dv), r, xs2, go, wo, tm=bs).reshape(bs, 1, d)
    return yp, ys, sp, ss


def _nsa_proj_kernel(x_ref, gn_ref, wt_ref, gq_ref, gk_ref, bg_ref, cos_ref, sin_ref, *rest):
    (q_ref, kc_ref, vc_ref, ks_ref, vs_ref, kw_ref, vw_ref, gt_ref,
     ksa_ref, vsa_ref, kwb_ref, vwa_ref) = rest[-12:]
    tm = x_ref.shape[0]
    nq = GRP * KVH * HD
    nkv = KVH * HD
    h = _rms_rows(x_ref[...], gn_ref[...]).astype(BF16)
    cos = cos_ref[...][None]
    sin = sin_ref[...][None]

    def norm_rope(z, g, nh):
        z3 = z.reshape(nh, HD, tm)
        y = z3 * lax.rsqrt(jnp.mean(z3 * z3, axis=1, keepdims=True) + EPS) * g[None]
        x1 = y[:, 0:ROT_HALF, :]
        x2 = y[:, ROT_HALF:2 * ROT_HALF, :]
        return jnp.concatenate([x1 * cos - x2 * sin, x1 * sin + x2 * cos, y[:, 2 * ROT_HALF:, :]], axis=1)

    zq = _dot_nt(wt_ref[0:nq, :], h)
    q_ref[...] = (norm_rope(zq, gq_ref[...], GRP * KVH) * (HD ** -0.5 * LOG2E)).astype(BF16)
    zkv = _dot_nt(wt_ref[nq:nq + 6 * nkv, :], h)
    outs = (kc_ref, vc_ref, ks_ref, vs_ref, kw_ref, vw_ref)
    rows = []
    for i in range(6):
        z = zkv[i * nkv:(i + 1) * nkv, :]
        rows.append(norm_rope(z, gk_ref[i // 2], KVH) if i % 2 == 0 else z.reshape(KVH, HD, tm))
        outs[i][...] = rows[i]
    zg = _dot_nt(wt_ref[nq + 6 * nkv:, :], h) + bg_ref[...]
    gt_ref[...] = jax.nn.sigmoid(zg).reshape(KVH, 16, tm)

    nsp = ksa_ref.shape[1] - HD
    blk = lax.broadcasted_iota(jnp.int32, (KVH, nsp, tm), 1)
    tok = pl.program_id(1) * tm + lax.broadcasted_iota(jnp.int32, (KVH, nsp, tm), 2)
    onehot = jnp.where(tok // L_SLC == blk, 1.0, 0.0)
    ksa_ref[...] = jnp.concatenate([rows[2], onehot], axis=1).astype(BF16)
    kwb_ref[...] = rows[4].astype(BF16)
    ones = jnp.where(lax.broadcasted_iota(jnp.int32, (KVH, V_PAD, tm), 1) == 0, 1.0, 0.0)
    vsa_ref[...] = jnp.concatenate([rows[3], ones], axis=1).astype(BF16)
    vwa_ref[...] = jnp.concatenate([rows[5], ones], axis=1).astype(BF16)


def _sel_rows(t):
    return -(-(-(-t // L_SLC)) // 16) * 16


def _nsa_proj(x, gn, wt, gq, gk, bg, cos, sin, tm, layer=0, n_layers=1, prev=None):
    b, t, d = x.shape
    full = lambda a: pl.BlockSpec(a.shape, lambda i, j: (0,) * a.ndim)
    rows_spec = lambda n: pl.BlockSpec((None, KVH, n, tm), lambda i, j: (i, 0, 0, j))
    rows_shape = lambda n, dt: jax.ShapeDtypeStruct((b, KVH, n, t), dt)
    kv_spec = pl.BlockSpec((None, None, KVH, HD, tm), lambda i, j: (layer, i, 0, 0, j))
    kv_shape = jax.ShapeDtypeStruct((n_layers, b, KVH, HD, t), F32)
    aug = [HD + _sel_rows(t), HD + V_PAD, HD, HD + V_PAD]
    prev = list(prev) if prev is not None else []
    n_in = 8
    return pl.pallas_call(
        _nsa_proj_kernel,
        grid=(b, t // tm),
        in_specs=[pl.BlockSpec((None, tm, d), lambda i, j: (i, j, 0)), full(gn), full(wt), full(gq), full(gk),
                  full(bg), pl.BlockSpec((ROT_HALF, tm), lambda i, j: (0, j)),
                  pl.BlockSpec((ROT_HALF, tm), lambda i, j: (0, j))]
        + [pl.BlockSpec(memory_space=pl.ANY)] * len(prev),
        out_specs=[pl.BlockSpec((None, GRP * KVH, HD, tm), lambda i, j: (i, 0, 0, j))] + [kv_spec] * 6
        + [rows_spec(16)] + [rows_spec(n) for n in aug],
        out_shape=[jax.ShapeDtypeStruct((b, GRP * KVH, HD, t), BF16)] + [kv_shape] * 6
        + [rows_shape(16, F32)] + [rows_shape(n, BF16) for n in aug],
        input_output_aliases={n_in + k: 1 + k for k in range(len(prev))},
        compiler_params=_params("parallel", "parallel"),
        name="nsa_proj",
    )(x, gn, wt, gq, gk, bg, cos, sin, *prev)


def _nsa_out_kernel(ot_ref, x_ref, w_ref, y_ref):
    ot = ot_ref[...].reshape(GRP * KVH * HD, ot_ref.shape[-1])
    y_ref[...] = x_ref[...] + _dot_tn(ot, w_ref[...])


def _nsa_out(ot, x, w, tm):
    b, t, d = x.shape
    return pl.pallas_call(
        _nsa_out_kernel,
        grid=(b, t // tm),
        in_specs=[pl.BlockSpec((None, GRP * KVH, HD, tm), lambda i, j: (i, 0, 0, j)),
                  pl.BlockSpec((None, tm, d), lambda i, j: (i, j, 0)),
                  pl.BlockSpec(w.shape, lambda i, j: (0, 0))],
        out_specs=pl.BlockSpec((None, tm, d), lambda i, j: (i, j, 0)),
        out_shape=jax.ShapeDtypeStruct((b, t, d), F32),
        compiler_params=_params("parallel", "parallel"),
        name="nsa_out",
    )(ot, x, w)


def _compress_kernel(*refs, n_pages, paged, is_key):
    refs = refs[1:] if paged else refs
    pages = refs[:n_pages]
    perm_ref, w_ref, b1_ref, pe_ref, w1_ref, w2t_ref, gk_ref, o_ref, lhs_scr = refs[n_pages:]
    p = pl.program_id(1)
    half = CMP_STRIDE
    n_half = PAGE // half
    low = lax.broadcasted_iota(jnp.int32, (n_half, LANES), 1) < HD
    odd_slot = (lax.broadcasted_iota(jnp.int32, (2 * HD, PAGE), 1) // n_half) % 2 == 1
    for pair in range(KVH // 2):
        for j in range(n_pages):
            y = _dot(pages[j][2 * pair:2 * pair + 2].reshape(2 * HD, PAGE).astype(BF16), perm_ref[...])
            x = jnp.where(odd_slot, jnp.concatenate([y[HD:], y[:HD]], axis=0), y).T
            rows = pl.ds(pl.multiple_of((p * n_pages + j) * n_half, n_half), n_half)
            for t in range(half // 2):
                ev = x[2 * t * n_half:(2 * t + 1) * n_half]
                od = x[(2 * t + 1) * n_half:(2 * t + 2) * n_half]
                lanes = slice(t * LANES, (t + 1) * LANES)
                lhs_scr[2 * pair, rows, lanes] = jnp.where(low, ev, od)
                lhs_scr[2 * pair + 1, rows, lanes] = jnp.where(low, od, ev)

    @pl.when(p == pl.num_programs(1) - 1)
    def _():
        nh = lhs_scr.shape[1]
        hidden = b1_ref.shape[1]
        c = _dot(pe_ref[0].astype(BF16), w1_ref[0]) + _dot(pe_ref[1].astype(BF16), w1_ref[1])
        bias = c[0:1, :] + b1_ref[...]
        for kvh in range(KVH):
            fs = _dot(lhs_scr[kvh].astype(BF16), w_ref[kvh % 2])
            hid = jax.nn.gelu(fs[:, :hidden] + pltpu.roll(fs[:, hidden:], nh - 1, 0) + bias)
            yt = _dot_nt(w2t_ref[...], hid.astype(BF16))
            if is_key:
                yt = yt * lax.rsqrt(jnp.mean(yt * yt, axis=0, keepdims=True) + EPS) * gk_ref[...]
            o_ref[kvh] = yt.astype(BF16)


def _compress(src, weights, n_pages_total, pages_per_step, is_key, page_table=None, layer=0):
    paged = page_table is not None
    b = page_table.shape[0] if paged else src.shape[1]
    n_half = PAGE // CMP_STRIDE
    nh = n_pages_total * n_half
    steps = n_pages_total // pages_per_step
    tok = jnp.arange(PAGE)
    perm = (((tok % CMP_STRIDE) * n_half + tok // CMP_STRIDE)[:, None] == tok[None, :]).astype(BF16)
    if paged:
        page_spec = lambda j: pl.BlockSpec(
            (None, None, KVH, HD, PAGE), lambda i, p, pt: (layer, pt[i, p * pages_per_step + j], 0, 0, 0))
        full = lambda a: pl.BlockSpec(a.shape, lambda i, p, pt: (0,) * a.ndim)
        o_spec = pl.BlockSpec((None, KVH, HD, nh), lambda i, p, pt: (i, 0, 0, 0))
    else:
        page_spec = lambda j: pl.BlockSpec((None, None, KVH, HD, PAGE),
                                           lambda i, p: (layer, i, 0, 0, p * pages_per_step + j))
        full = lambda a: pl.BlockSpec(a.shape, lambda i, p: (0,) * a.ndim)
        o_spec = pl.BlockSpec((None, KVH, HD, nh), lambda i, p: (i, 0, 0, 0))
    consts = [perm] + list(weights)
    grid_spec = pltpu.PrefetchScalarGridSpec(
        num_scalar_prefetch=1 if paged else 0,
        grid=(b, steps),
        in_specs=[page_spec(j) for j in range(pages_per_step)] + [full(a) for a in consts],
        out_specs=o_spec,
        scratch_shapes=[pltpu.VMEM((KVH, nh, CMP_STRIDE * HD), F32)],
    )
    args = ([page_table] if paged else []) + [src] * pages_per_step + consts
    return pl.pallas_call(
        functools.partial(_compress_kernel, n_pages=pages_per_step, paged=paged, is_key=is_key),
        grid_spec=grid_spec,
        out_shape=jax.ShapeDtypeStruct((b, KVH, HD, nh), BF16),
        compiler_params=_params("parallel", "arbitrary"),
        name="nsa_compress",
    )(*args)


def _cmp_weights(pe, w1, b1, w2, g_kcmp, nh):
    hidden = w1.shape[-1]
    wfs = jnp.concatenate([w1[:CMP_STRIDE], w1[CMP_STRIDE:]], axis=-1)
    swapped = wfs.reshape(CMP_STRIDE // 2, 2, HD, 2 * hidden)[:, ::-1]
    w_pair = jnp.stack([wfs, swapped.reshape(wfs.shape)]).reshape(2, CMP_STRIDE * HD, 2 * hidden).astype(BF16)
    pe2 = jnp.zeros((2, 8, CMP_STRIDE * HD), F32).at[:, 0, :].set(pe.reshape(2, CMP_STRIDE * HD))
    w1f = w1.reshape(2, CMP_STRIDE * HD, hidden).astype(BF16)
    gk = jnp.broadcast_to(g_kcmp.reshape(HD, 1), (HD, nh))
    return w_pair, b1.reshape(1, hidden), pe2, w1f, w2.T.astype(BF16), gk


def _softmax_update(s, mask, vt, m, l, acc):
    m_new = jnp.maximum(m, jnp.max(jnp.where(mask, s, NEG), axis=0, keepdims=True))
    alpha = jnp.exp2(m - m_new)
    p = jnp.where(mask, jnp.exp2(s - m_new), 0.0)
    l_new = l * alpha + jnp.sum(p, axis=0, keepdims=True)
    acc_new = acc * alpha + _dot(vt, p.astype(BF16))
    return m_new, l_new, acc_new, p


def _softmax_init(lanes):
    return jnp.full((1, lanes), NEG, F32), jnp.zeros((1, lanes), F32), jnp.zeros((HD, lanes), F32)


def _finish(l, acc):
    return acc / jnp.maximum(l, 1e-30)


def _split_dot(mt, x):
    hi = x.astype(BF16)
    lo = (x - hi.astype(F32)).astype(BF16)
    return _dot(mt, hi) + _dot(mt, lo)


def _attn_prompt_kernel(q_ref, kc_ref, vc_ref, ksa_ref, vsa_ref, kwb_ref, vwa_ref, gt_ref, mt_ref, o_ref,
                        s_scr, s2_scr, *, tq, n_cmp, n_sel):
    i = pl.program_id(2)
    lanes = GRP * tq
    ncp = kc_ref.shape[1]
    nsp = mt_ref.shape[0]
    qt = jnp.concatenate([q_ref[h] for h in range(GRP)], axis=1)

    def col_max(mx, s):
        return jnp.maximum(mx, jnp.max(s.reshape(s.shape[0] // 8, 8, lanes), axis=0))

    def finish(acc):
        return acc[0:HD] / jnp.maximum(acc[HD:HD + 1], 1e-30)

    tpos1 = i * tq + lax.broadcasted_iota(jnp.int32, (1, tq), 1)
    tpos = jnp.concatenate([tpos1] * GRP, axis=1)

    s = _dot_tn(kc_ref[...], qt)
    n_idx = lax.broadcasted_iota(jnp.int32, (ncp, lanes), 0)
    mask = (n_idx * CMP_STRIDE + (L_CMP - 1) <= tpos) & (n_idx < n_cmp)
    m, l, acc, p = _softmax_update(s, mask, vc_ref[...], *_softmax_init(lanes))
    o_c = _finish(l, acc)
    p = p / jnp.maximum(l, 1e-30)
    imp = p[:, 0:tq]
    for h in range(1, GRP):
        imp = imp + p[:, h * tq:(h + 1) * tq]
    p_slc = _split_dot(mt_ref[...], imp)

    j_idx = lax.broadcasted_iota(jnp.int32, (nsp, tq), 0)
    cur = tpos1 // L_SLC
    valid = j_idx <= cur
    forced = (j_idx == 0) | (j_idx == cur) | (j_idx == cur - 1)
    score = jnp.where(valid & forced, jnp.inf, jnp.where(valid, p_slc, -jnp.inf))
    bits = lax.bitcast_convert_type(score, jnp.int32)
    key = jnp.where(bits >= 0, bits, bits ^ 0x7FFFFFFF)
    n_grp = nsp // 8
    keys = [key[8 * r:8 * r + 8] for r in range(n_grp)]
    keys_m1 = [k - 1 for k in keys]
    sub = lax.broadcasted_iota(jnp.int32, (8, tq), 0)

    def count_group(grp, ranks):
        ranks = list(ranks)
        for u in range(8):
            row = jnp.broadcast_to(keys[grp][u:u + 1, :], (8, tq))
            for r in range(n_grp):
                thr = keys[r] if r < grp else keys_m1[r] if r > grp else jnp.where(sub > u, keys_m1[r], keys[r])
                ranks[r] = ranks[r] + jnp.where(row > thr, 1.0, 0.0)
        return tuple(ranks)

    ranks = tuple(jnp.zeros((8, tq), F32) for _ in range(n_grp))
    last_valid = (i * tq + tq - 1) // L_SLC
    for grp in range(n_grp):
        ranks = lax.cond(8 * grp <= last_valid, functools.partial(count_group, grp), lambda r: r, ranks)
    bias = jnp.where(jnp.concatenate(ranks, axis=0) < n_sel, 0.0, NEG)
    qa = jnp.concatenate([qt, jnp.concatenate([bias] * GRP, axis=1).astype(BF16)], axis=0)

    row_k = lax.broadcasted_iota(jnp.int32, (KT, lanes), 0)

    def slc_scores(st, buf):
        start = pl.multiple_of(st * KT, KT)
        buf[0:KT, :] = _dot_tn(ksa_ref[:, pl.ds(start, KT)], qa)

    def slc_softmax(st, buf, carry, diagonal=False):
        m, acc = carry
        s = buf[0:KT, :]
        if diagonal:
            s = jnp.where(st * KT + row_k <= tpos, s, NEG)
        m_new = jnp.maximum(m, jnp.max(col_max(jnp.full((8, lanes), NEG, F32), s), axis=0, keepdims=True))
        p = jnp.exp2(s - m_new).astype(BF16)
        pv = _dot(vsa_ref[:, pl.ds(pl.multiple_of(st * KT, KT), KT)], p)
        return m_new, acc * jnp.exp2(m - m_new) + pv

    def slc_pair(u, carry):
        slc_scores(2 * u + 1, s2_scr)
        carry = slc_softmax(2 * u, s_scr, carry)
        slc_scores(2 * u + 2, s_scr)
        return slc_softmax(2 * u + 1, s2_scr, carry)

    def slc_tail_odd(carry):
        slc_scores(n_full, s2_scr)
        carry = slc_softmax(n_full - 1, s_scr, carry)
        return slc_softmax(n_full, s2_scr, carry, diagonal=True)

    def slc_tail_even(carry):
        return slc_softmax(n_full, s_scr, carry, diagonal=True)

    n_full = (i * tq) // KT
    slc_scores(0, s_scr)
    init = (jnp.full((1, lanes), NEG, F32), jnp.zeros((HD + V_PAD, lanes), F32))
    carry = lax.fori_loop(0, n_full // 2, slc_pair, init)
    o_s = finish(lax.cond(n_full % 2 == 1, slc_tail_odd, slc_tail_even, carry)[1])

    n_wt = WINDOW // tq + 1
    w0 = jnp.maximum(i - (n_wt - 1), 0) * tq
    row_w = lax.broadcasted_iota(jnp.int32, (tq, lanes), 0)
    col_w = jnp.concatenate([lax.broadcasted_iota(jnp.int32, (tq, tq), 1)] * GRP, axis=1)

    def window(steady):
        mx = jnp.full((8, lanes), NEG, F32)
        for r in range(n_wt):
            start = pl.multiple_of(w0 + r * tq, tq)
            s = _dot_tn(kwb_ref[:, pl.ds(start, tq)], qt)
            if not steady:
                rel = tpos - (start + row_w)
                s = jnp.where((rel >= 0) & (rel <= WINDOW), s, NEG)
            elif r == 0:
                s = jnp.where(row_w >= col_w, s, NEG)
            elif r == n_wt - 1:
                s = jnp.where(row_w <= col_w, s, NEG)
            s_scr[r * tq:(r + 1) * tq, :] = s
            mx = col_max(mx, s)
        m = jnp.max(mx, axis=0, keepdims=True)
        acc = jnp.zeros((HD + V_PAD, lanes), F32)
        for r in range(n_wt):
            start = pl.multiple_of(w0 + r * tq, tq)
            p = jnp.exp2(s_scr[r * tq:(r + 1) * tq, :] - m).astype(BF16)
            acc = acc + _dot(vwa_ref[:, pl.ds(start, tq)], p)
        return finish(acc)

    o_w = lax.cond(i >= n_wt - 1, functools.partial(window, True), functools.partial(window, False))

    for h in range(GRP):
        ls = slice(h * tq, (h + 1) * tq)
        o = (gt_ref[3 * h:3 * h + 1, :] * o_c[:, ls] + gt_ref[3 * h + 1:3 * h + 2, :] * o_s[:, ls]
             + gt_ref[3 * h + 2:3 * h + 3, :] * o_w[:, ls])
        o_ref[h] = o.astype(BF16)


def _slc_matrix(nsp, ncp, n_cmp):
    ratio = L_SLC // CMP_STRIDE
    j = jnp.arange(nsp)[:, None]
    n = jnp.arange(ncp)[None, :]
    m = ((n >= ratio * j) & (n <= ratio * j + ratio - 1)).astype(F32)
    m = m + ((n >= ratio * j - 1) & (n <= ratio * j + ratio - 2)).astype(F32)
    return jnp.where(n < n_cmp, m, 0.0).astype(BF16)


def _attn_prompt(qt, kct, vct, ksa, vsa, kwb, vwa, gt, tq=4 * LANES):
    b, _, _, t = qt.shape
    assert t % KT == 0 and WINDOW % tq == 0 and t >= WINDOW + tq
    ncp = kct.shape[-1]
    n_cmp = t // CMP_STRIDE - 1
    ns = -(-t // L_SLC)
    mt = _slc_matrix(_sel_rows(t), ncp, n_cmp)
    res = lambda a: pl.BlockSpec((None, None) + a.shape[2:], lambda i, g, j: (i, g, 0, 0))
    return pl.pallas_call(
        functools.partial(_attn_prompt_kernel, tq=tq, n_cmp=n_cmp, n_sel=min(N_SEL, ns)),
        grid=(b, KVH, t // tq),
        in_specs=[pl.BlockSpec((None, GRP, HD, tq), lambda i, g, j: (i, g, 0, j)),
                  res(kct), res(vct), res(ksa), res(vsa), res(kwb), res(vwa),
                  pl.BlockSpec((None, None, 16, tq), lambda i, g, j: (i, g, 0, j)),
                  pl.BlockSpec(mt.shape, lambda i, g, j: (0, 0))],
        out_specs=pl.BlockSpec((None, GRP, HD, tq), lambda i, g, j: (i, g, 0, j)),
        out_shape=jax.ShapeDtypeStruct(qt.shape, BF16),
        scratch_shapes=[pltpu.VMEM((max(KT, WINDOW + tq), GRP * tq), F32), pltpu.VMEM((KT, GRP * tq), F32)],
        compiler_params=_params("parallel", "parallel", "arbitrary"),
        name="nsa_attn_prompt",
    )(qt, kct, vct, ksa, vsa, kwb, vwa, gt, mt)


def _attn_sample_kernel(*refs, n_pages, n_cmp, n_blocks, n_sel):
    refs = refs[1:]
    k_pages = refs[:n_pages]
    v_pages = refs[n_pages:2 * n_pages]
    (q_ref, kc_ref, vc_ref, kw_ref, vw_ref, kns_ref, vns_ref, knw_ref, vnw_ref, gt_ref, mt_ref,
     o_ref, sel_scr, oc_scr, ow_scr, m_scr, l_scr, acc_scr) = refs[2 * n_pages:]
    p = pl.program_id(1)
    ncp = kc_ref.shape[-1]
    nsp = mt_ref.shape[0]
    rows_all = KVH * HD
    first_row = lax.broadcasted_iota(jnp.int32, (LANES, LANES), 0) == 0
    qt = q_ref[...]

    def stacked(ref):
        return ref[...].reshape(rows_all, ref.shape[-1]).astype(BF16)

    def finish(l, acc):
        return acc / jnp.maximum(l, 1e-30)

    def init():
        return jnp.full((1, LANES), NEG, F32), jnp.zeros((1, LANES), F32), jnp.zeros((rows_all, LANES), F32)

    @pl.when(p == 0)
    def _():
        lane = lax.broadcasted_iota(jnp.int32, (ncp, LANES), 1)
        n_idx = lax.broadcasted_iota(jnp.int32, (ncp, LANES), 0)
        j_col = lax.broadcasted_iota(jnp.int32, (nsp, LANES), 0)
        lane_s = lax.broadcasted_iota(jnp.int32, (nsp, LANES), 1)
        jp_idx = lax.broadcasted_iota(jnp.int32, (nsp, nsp), 0)
        j_idx = lax.broadcasted_iota(jnp.int32, (nsp, nsp), 1)
        cur = n_blocks - 1
        s = _dot_tn(stacked(kc_ref), qt)
        m, l, acc, pr = _softmax_update(s, n_idx < n_cmp, stacked(vc_ref), *init())
        oc_scr[...] = finish(l, acc)
        pr = pr / jnp.maximum(l, 1e-30)
        sel_all = jnp.zeros((nsp, LANES), F32)
        for g in range(KVH):
            mine = (lane >= g * GRP) & (lane < (g + 1) * GRP)
            imp = jnp.sum(jnp.where(mine, pr, 0.0), axis=1, keepdims=True)
            p_slc = _split_dot(mt_ref[...], jnp.broadcast_to(imp, (ncp, LANES)))
            valid = j_col <= cur
            forced = (j_col == 0) | (j_col == cur) | (j_col == cur - 1)
            score = jnp.where(valid & forced, jnp.inf, jnp.where(valid, p_slc, -jnp.inf))
            col = jnp.concatenate([score] * (nsp // LANES), axis=1)
            row = col.T
            ahead = (col > row) | ((col == row) & (jp_idx < j_idx))
            rank = jnp.sum(jnp.where(ahead, 1.0, 0.0), axis=0, keepdims=True)
            sel = jnp.broadcast_to(jnp.where(rank < n_sel, 1.0, 0.0), (LANES, nsp)).T
            sel_all = jnp.where((lane_s >= g * GRP) & (lane_s < (g + 1) * GRP), sel, sel_all)
        sel_scr[...] = sel_all
        s = _dot_tn(stacked(kw_ref), qt)
        st = _softmax_update(s, jnp.full(s.shape, True), stacked(vw_ref), *init())[:3]
        s = _dot_tn(stacked(knw_ref), qt)
        _, l, acc = _softmax_update(s, first_row, stacked(vnw_ref), *st)[:3]
        ow_scr[...] = finish(l, acc)
        m_scr[...], l_scr[...], acc_scr[...] = init()

    blocks_per_page = PAGE // L_SLC
    kt = jnp.concatenate([k_pages[j][...].reshape(rows_all, PAGE) for j in range(n_pages)], axis=1).astype(BF16)
    vt = jnp.concatenate([v_pages[j][...].reshape(rows_all, PAGE) for j in range(n_pages)], axis=1).astype(BF16)
    s = _dot_tn(kt, qt)
    rows = []
    for r in range(n_pages * blocks_per_page):
        e = sel_scr[pl.ds(p * n_pages * blocks_per_page + r, 1), :]
        rows.append(jnp.broadcast_to(e, (L_SLC, LANES)))
    mask = jnp.concatenate(rows, axis=0) > 0.5
    m_scr[...], l_scr[...], acc_scr[...] = _softmax_update(s, mask, vt, m_scr[...], l_scr[...], acc_scr[...])[:3]

    @pl.when(p == pl.num_programs(1) - 1)
    def _():
        s = _dot_tn(stacked(kns_ref), qt)
        _, l, acc = _softmax_update(s, first_row, stacked(vns_ref), m_scr[...], l_scr[...], acc_scr[...])[:3]
        o_ref[...] = (gt_ref[0:1, :] * oc_scr[...] + gt_ref[1:2, :] * finish(l, acc) + gt_ref[2:3, :] * ow_scr[...])


def _attn_sample(qs, kct, vct, pool_k, pool_v, page_table, layer, kwt, vwt, kns, vns, knw, vnw, gs,
                 pages_per_step=16):
    b, n_pages_total = page_table.shape
    pages_per_step = min(pages_per_step, n_pages_total)
    past = n_pages_total * PAGE
    n_cmp = past // CMP_STRIDE - 1
    ncp = kct.shape[-1]
    n_blocks = past // L_SLC + 1
    nsp = -(-n_blocks // LANES) * LANES
    mt = _slc_matrix(nsp, ncp, n_cmp)
    steps = n_pages_total // pages_per_step
    page_spec = lambda j: pl.BlockSpec((None, None, KVH, HD, PAGE),
                                       lambda i, p, pt: (layer, pt[i, p * pages_per_step + j], 0, 0, 0))
    def per_b(a):
        if a.ndim == 5:
            return pl.BlockSpec((None, None) + a.shape[2:], lambda i, p, pt: (layer, i, 0, 0, 0))
        return pl.BlockSpec((None,) + a.shape[1:], lambda i, p, pt: (i,) + (0,) * (a.ndim - 1))

    small = [qs, kct, vct, kwt, vwt, kns, vns, knw, vnw, gs]
    grid_spec = pltpu.PrefetchScalarGridSpec(
        num_scalar_prefetch=1,
        grid=(b, steps),
        in_specs=[page_spec(j) for j in range(pages_per_step)] * 2 + [per_b(a) for a in small]
        + [pl.BlockSpec(mt.shape, lambda i, p, pt: (0, 0))],
        out_specs=pl.BlockSpec((None, KVH * HD, LANES), lambda i, p, pt: (i, 0, 0)),
        scratch_shapes=[pltpu.VMEM((nsp, LANES), F32), pltpu.VMEM((KVH * HD, LANES), F32),
                        pltpu.VMEM((KVH * HD, LANES), F32), pltpu.VMEM((1, LANES), F32),
                        pltpu.VMEM((1, LANES), F32), pltpu.VMEM((KVH * HD, LANES), F32)],
    )
    return pl.pallas_call(
        functools.partial(_attn_sample_kernel, n_pages=pages_per_step, n_cmp=n_cmp, n_blocks=n_blocks,
                          n_sel=min(N_SEL, n_blocks)),
        grid_spec=grid_spec,
        out_shape=jax.ShapeDtypeStruct((b, KVH * HD, LANES), F32),
        compiler_params=_params("parallel", "arbitrary"),
        name="nsa_attn_sample",
    )(page_table, *([pool_k] * pages_per_step), *([pool_v] * pages_per_step), *small, mt)


def _rope_tables(pos):
    inv_freq = ROPE_THETA ** (-jnp.arange(ROT_HALF, dtype=F32) * 2.0 / (2 * ROT_HALF))
    ang = pos.astype(F32)[:, None] * inv_freq[None, :]
    return jnp.cos(ang).T, jnp.sin(ang).T


def _nsa_layer(xp, xs, pools, win_bufs, page_table, layer, n_layers, prev_rows, norm_g, w_in, b_gate, g_q, g_k,
               g_kcmp, cmp_pe, cmp_w1, cmp_b1, cmp_w2, w_out):
    b, t, d = xp.shape
    bs = xs.shape[0]
    past = page_table.shape[1] * PAGE
    nq, nkv = GRP * KVH * HD, KVH * HD
    n_gate = 3 * GRP * KVH

    wt_g = jnp.pad(w_in[:, nq + 6 * nkv:].T.reshape(KVH, n_gate // KVH, d), ((0, 0), (0, 16 - n_gate // KVH), (0, 0)))
    wt = jnp.concatenate([w_in[:, :nq + 6 * nkv].T, wt_g.reshape(KVH * 16, d)], axis=0).astype(BF16)
    bg = jnp.pad(b_gate.reshape(KVH, n_gate // KVH), ((0, 0), (0, 16 - n_gate // KVH))).reshape(KVH * 16, 1)
    gn = norm_g.reshape(1, d)
    wo = w_out.astype(BF16)

    def project(x, pos, tm, **stacking):
        cos, sin = _rope_tables(pos)
        col = lambda v: jnp.broadcast_to(v[..., None], v.shape + (tm,))
        return _nsa_proj(x, gn, wt, col(g_q), col(g_k), col(bg[:, 0]), cos, sin, tm, **stacking)

    tm = min(512, t)
    outs = project(xp, jnp.arange(t), tm, layer=layer, n_layers=n_layers, prev=prev_rows)
    qt, rows_p, (gt, ksa, vsa, kwb, vwa) = outs[0], list(outs[1:7]), outs[7:]
    pages = t // PAGE
    nh = pages * (PAGE // CMP_STRIDE)
    cmp_w = [_cmp_weights(cmp_pe[i], cmp_w1[i], cmp_b1[i], cmp_w2[i], g_kcmp, nh) for i in range(2)]
    pps = min(16, pages)
    kct = _compress(rows_p[0], cmp_w[0], pages, pps, True, layer=layer)
    vct = _compress(rows_p[1], cmp_w[1], pages, pps, False, layer=layer)
    ot = _attn_prompt(qt, kct, vct, ksa, vsa, kwb, vwa, gt)
    yp = _nsa_out(ot, xp, wo, tm)

    xs_pad = jnp.pad(xs.reshape(1, bs, d), ((0, 0), (0, LANES - bs), (0, 0)))
    outs = project(xs_pad, jnp.full((LANES,), past), LANES)
    qt_s, rows_s, gt_s = outs[0], outs[1:7], outs[7]
    pages_s = page_table.shape[1]
    nh_s = pages_s * (PAGE // CMP_STRIDE)
    cmp_ws = [_cmp_weights(cmp_pe[i], cmp_w1[i], cmp_b1[i], cmp_w2[i], g_kcmp, nh_s) for i in range(2)]
    pool5 = [jnp.transpose(pl_, (0, 1, 3, 4, 2)) for pl_ in pools]
    pps = min(32, pages_s)
    kct_s = _compress(pool5[0], cmp_ws[0], pages_s, pps, True, page_table, layer)
    vct_s = _compress(pool5[1], cmp_ws[1], pages_s, pps, False, page_table, layer)
    qs = jnp.transpose(qt_s[0, :, :, :bs].reshape(KVH, GRP, HD, bs), (3, 0, 2, 1))
    qs = jnp.einsum("bgdh,gk->bgdkh", qs, jnp.eye(KVH, dtype=qs.dtype))
    qs = jnp.pad(qs.reshape(bs, KVH * HD, KVH * GRP), ((0, 0), (0, 0), (0, LANES - KVH * GRP)))
    new = lambda a: jnp.pad(jnp.transpose(a[0, 0, :, :, :bs], (2, 0, 1))[..., None],
                            ((0, 0), (0, 0), (0, 0), (0, LANES - 1)))
    gs = jnp.transpose(gt_s[0, :, :n_gate // KVH, :bs].reshape(KVH, GRP, 3, bs), (3, 2, 0, 1))
    gs = jnp.pad(gs.reshape(bs, 3, KVH * GRP), ((0, 0), (0, 8 - 3), (0, LANES - KVH * GRP)))
    win5 = [jnp.transpose(wb, (0, 1, 3, 4, 2)) for wb in win_bufs]
    ot_s = _attn_sample(qs, kct_s, vct_s, pool5[2], pool5[3], page_table, layer, win5[0], win5[1],
                        new(rows_s[2]), new(rows_s[3]), new(rows_s[4]), new(rows_s[5]), gs)
    ot_s = ot_s.reshape(bs, KVH, HD, LANES)
    ot_s = jnp.stack([ot_s[:, g, :, g * GRP:(g + 1) * GRP] for g in range(KVH)])
    ot_s = jnp.transpose(ot_s, (0, 3, 2, 1)).reshape(1, GRP * KVH, HD, bs)
    ot_s = jnp.pad(ot_s, ((0, 0), (0, 0), (0, 0), (0, LANES - bs))).astype(BF16)
    ys = _nsa_out(ot_s, xs_pad, wo, LANES)[0, :bs].reshape(bs, 1, d)

    rows_s = [jnp.transpose(a[0, 0, :, :, :bs], (2, 0, 1)).reshape(bs, 1, KVH, HD) for a in rows_s]
    return yp, ys, rows_p, rows_s


def kernel(x_prompt, x_sample, state_gla, cache_k_cmp, cache_v_cmp, cache_k_slc, cache_v_slc, cache_k_win, cache_v_win, page_table, norm_mix, norm_mlp, mlp_up, mlp_down, gla_w_in, gla_w_gate2, gla_b_gate, gla_g_out, gla_w_out, nsa_w_in, nsa_b_gate, nsa_g_q, nsa_g_k, nsa_g_kcmp, nsa_cmp_pe, nsa_cmp_w1, nsa_cmp_b1, nsa_cmp_w2, nsa_w_out):
    depth = norm_mix.shape[0]
    b, t, d = x_prompt.shape
    bs = x_sample.shape[0]
    xp, xs = x_prompt, x_sample
    gla_p, gla_s, nsa_s = [], [], []
    nsa_p = None
    pools = (cache_k_cmp, cache_v_cmp, cache_k_slc, cache_v_slc)
    for i in range(depth):
        j = i // 2
        if i % 2 == 0:
            xp, xs, sp, ss = _gla_layer(xp, xs, state_gla[j], norm_mix[i], gla_w_in[j], gla_w_gate2[j],
                                        gla_b_gate[j], gla_g_out[j], gla_w_out[j])
            gla_p.append(sp)
            gla_s.append(ss)
        else:
            xp, xs, nsa_p, rs = _nsa_layer(xp, xs, pools, (cache_k_win, cache_v_win), page_table, j, depth // 2,
                                           nsa_p, norm_mix[i], nsa_w_in[j], nsa_b_gate[j], nsa_g_q[j], nsa_g_k[j],
                                           nsa_g_kcmp[j], nsa_cmp_pe[j], nsa_cmp_w1[j], nsa_cmp_b1[j],
                                           nsa_cmp_w2[j], nsa_w_out[j])
            nsa_s.append(rs)
        g = norm_mlp[i].reshape(1, d)
        wu, wd = mlp_up[i].astype(BF16), mlp_down[i].astype(BF16)
        xp = _mlp(xp.reshape(b * t, d), g, wu, wd, tm=min(1024, b * t)).reshape(b, t, d)
        xs = _mlp(xs.reshape(bs, d), g, wu, wd, tm=bs).reshape(bs, 1, d)
    stack = lambda lst, r: jnp.stack([e[r] for e in lst])
    n_win = min(WINDOW, t)
    rows = lambda r, n: jnp.transpose(nsa_p[r][..., t - n:], (0, 1, 4, 2, 3))
    return (xp, xs, jnp.stack(gla_p), jnp.stack(gla_s),
            rows(0, t), stack(nsa_s, 0), rows(1, t), stack(nsa_s, 1),
            rows(2, t), stack(nsa_s, 2), rows(3, t), stack(nsa_s, 3),
            rows(4, n_win), stack(nsa_s, 4), rows(5, n_win), stack(nsa_s, 5))
```

```python
import functools

import jax
import jax.numpy as jnp
from jax import lax
from jax.experimental import pallas as pl
from jax.experimental.pallas import tpu as pltpu

F32 = jnp.float32
BF16 = jnp.bfloat16
EPS = 1e-6
NEG = -1e30
VMEM_LIMIT_BYTES = 48 * 1024 * 1024
LANES = 128
PAGE = 128
HD = 64
KVH = 4
GRP = 4
L_CMP, CMP_STRIDE, L_SLC, N_SEL, WINDOW = 32, 16, 64, 16, 512
GLA_H, GLA_DK, GLA_DV, GLA_CHUNK = 4, 128, 256, 64
ROPE_THETA, ROT_HALF = 500000.0, 8
LOG2E = 1.4426950408889634
V_PAD = 16
KT = 512


def _params(*sem):
    return pltpu.CompilerParams(dimension_semantics=sem, vmem_limit_bytes=VMEM_LIMIT_BYTES)


def _dot(a, b):
    return jnp.dot(a, b, preferred_element_type=F32)


def _dot_nt(a, b):
    return lax.dot_general(a, b, (((1,), (1,)), ((), ())), preferred_element_type=F32)


def _dot_tn(a, b):
    return lax.dot_general(a, b, (((0,), (0,)), ((), ())), preferred_element_type=F32)


def _rms_rows(x, g):
    ms = jnp.mean(x * x, axis=-1, keepdims=True)
    return x * lax.rsqrt(ms + EPS) * g


def _mlp_kernel(x_ref, g_ref, wu_ref, wd_ref, o_ref, h_scr, acc_scr):
    f = pl.program_id(1)

    @pl.when(f == 0)
    def _():
        h_scr[...] = _rms_rows(x_ref[...], g_ref[...]).astype(BF16)
        acc_scr[...] = jnp.zeros_like(acc_scr)

    u = jnp.maximum(_dot(h_scr[...], wu_ref[...]), 0.0)
    acc_scr[...] += _dot((u * u).astype(BF16), wd_ref[...])

    @pl.when(f == pl.num_programs(1) - 1)
    def _():
        o_ref[...] = x_ref[...] + acc_scr[...]


def _mlp(x, g, wu, wd, layer, tm, tf=2048):
    m, d = x.shape
    ff = wu.shape[2]
    return pl.pallas_call(
        _mlp_kernel,
        grid=(m // tm, ff // tf),
        in_specs=[pl.BlockSpec((tm, d), lambda i, f: (i, 0)),
                  pl.BlockSpec((1, d), lambda i, f: (0, 0)),
                  pl.BlockSpec((None, d, tf), lambda i, f: (layer, 0, f)),
                  pl.BlockSpec((None, tf, d), lambda i, f: (layer, f, 0))],
        out_specs=pl.BlockSpec((tm, d), lambda i, f: (i, 0)),
        out_shape=jax.ShapeDtypeStruct((m, d), F32),
        scratch_shapes=[pltpu.VMEM((tm, d), BF16), pltpu.VMEM((tm, d), F32)],
        compiler_params=_params("parallel", "arbitrary"),
        name="mlp_block",
    )(x, g, wu, wd)


def _gla_proj_kernel(x_ref, g_ref, w_ref, wg2_ref, bg_ref, q_ref, k_ref, v_ref, r_ref, gl_ref):
    h = _rms_rows(x_ref[...], g_ref[...]).astype(BF16)
    dk, dv = GLA_H * GLA_DK, GLA_H * GLA_DV
    q_ref[...] = _dot(h, w_ref[:, 0:dk]) * (GLA_DK ** -0.5)
    k_ref[...] = _dot(h, w_ref[:, dk:2 * dk])
    v_ref[...] = _dot(h, w_ref[:, 2 * dk:2 * dk + dv])
    r_ref[...] = _dot(h, w_ref[:, 2 * dk + dv:2 * dk + 2 * dv])
    gr = _dot(h, w_ref[:, 2 * dk + 2 * dv:])
    xg = _dot(gr.astype(BF16), wg2_ref[...]) + bg_ref[...]
    gl_ref[...] = jax.nn.log_sigmoid(xg) * (1.0 / 16.0)


def _gla_proj(x, g, w, wg2, bg, tm):
    m, d = x.shape
    dk, dv = GLA_H * GLA_DK, GLA_H * GLA_DV
    row = lambda n: pl.BlockSpec((tm, n), lambda i: (i, 0))
    full = lambda a: pl.BlockSpec(a.shape, lambda i: (0, 0))
    return pl.pallas_call(
        _gla_proj_kernel,
        grid=(m // tm,),
        in_specs=[row(d), full(g), full(w), full(wg2), full(bg)],
        out_specs=[row(dk), row(dk), row(dv), row(dv), row(dk)],
        out_shape=[jax.ShapeDtypeStruct((m, n), F32) for n in (dk, dk, dv, dv, dk)],
        compiler_params=_params("parallel"),
        name="gla_proj",
    )(x, g, w, wg2, bg)


def _gla_scan_kernel(q_ref, k_ref, g_ref, v_ref, o_ref, s_ref, st_scr, *, n_chunks, n_heads):
    c_len = GLA_CHUNK
    step = pl.program_id(2)

    @pl.when(step == 0)
    def _():
        st_scr[...] = jnp.zeros_like(st_scr)

    rowi = lax.broadcasted_iota(jnp.int32, (c_len, GLA_DK), 0)
    causal = (lax.broadcasted_iota(jnp.int32, (c_len, c_len), 0)
              >= lax.broadcasted_iota(jnp.int32, (c_len, c_len), 1))

    def body(c, carry):
        sl = pl.ds(pl.multiple_of(c * c_len, c_len), c_len)
        for h in range(n_heads):
            ks = slice(h * GLA_DK, (h + 1) * GLA_DK)
            vs = slice(h * GLA_DV, (h + 1) * GLA_DV)
            b = g_ref[sl, ks]
            sh = 1
            while sh < c_len:
                b = b + jnp.where(rowi >= sh, pltpu.roll(b, sh, 0), 0.0)
                sh *= 2
            b_last = b[c_len - 1:c_len, :]
            b_mid = b[c_len // 2 - 1:c_len // 2, :]
            q = q_ref[sl, ks]
            k = k_ref[sl, ks]
            v = v_ref[sl, vs].astype(BF16)
            qe = (q * jnp.exp(b)).astype(BF16)
            qa = (q * jnp.exp(b - b_mid)).astype(BF16)
            ka = (k * jnp.exp(b_mid - b)).astype(BF16)
            kd = (k * jnp.exp(b_last - b)).astype(BF16)
            a = jnp.where(causal, _dot_nt(qa, ka), 0.0)
            st = st_scr[h]
            o_ref[sl, vs] = _dot_nt(qe, st.astype(BF16)) + _dot(a.astype(BF16), v)
            st_scr[h] = st * jnp.exp(b_last) + _dot_tn(v, kd)
        return carry

    lax.fori_loop(0, n_chunks, body, 0)

    @pl.when(step == pl.num_programs(2) - 1)
    def _():
        for h in range(n_heads):
            s_ref[h] = st_scr[h].T


def _gla_scan(q, k, gl, v, n_heads=4, t_blk=1024):
    b, t, _ = q.shape
    t_blk = min(t_blk, t)
    kq = pl.BlockSpec((None, t_blk, n_heads * GLA_DK), lambda i, h, j: (i, j, h))
    vv = pl.BlockSpec((None, t_blk, n_heads * GLA_DV), lambda i, h, j: (i, j, h))
    return pl.pallas_call(
        functools.partial(_gla_scan_kernel, n_chunks=t_blk // GLA_CHUNK, n_heads=n_heads),
        grid=(b, GLA_H // n_heads, t // t_blk),
        in_specs=[kq, kq, kq, vv],
        out_specs=[vv, pl.BlockSpec((None, n_heads, GLA_DK, GLA_DV), lambda i, h, j: (i, h, 0, 0))],
        out_shape=[jax.ShapeDtypeStruct((b, t, GLA_H * GLA_DV), F32),
                   jax.ShapeDtypeStruct((b, GLA_H, GLA_DK, GLA_DV), F32)],
        scratch_shapes=[pltpu.VMEM((n_heads, GLA_DV, GLA_DK), F32)],
        compiler_params=_params("parallel", "parallel", "arbitrary"),
        name="gla_scan",
    )(q, k, gl, v)


def _gla_step_kernel(q_ref, k_ref, g_ref, v_ref, s0_ref, o_ref, s_ref):
    def col(x):
        return jnp.broadcast_to(x, (LANES, LANES)).T

    for h in range(GLA_H):
        ks = slice(h * GLA_DK, (h + 1) * GLA_DK)
        qc, kc, ec = col(q_ref[:, ks]), col(k_ref[:, ks]), col(jnp.exp(g_ref[:, ks]))
        for half in range(GLA_DV // LANES):
            vs = slice(h * GLA_DV + half * LANES, h * GLA_DV + (half + 1) * LANES)
            ss = slice(half * LANES, (half + 1) * LANES)
            sn = ec * s0_ref[h, :, ss] + kc * v_ref[:, vs]
            s_ref[h, :, ss] = sn
            o_ref[:, vs] = jnp.sum(qc * sn, axis=0, keepdims=True)


def _gla_step_aliased_kernel(q_ref, k_ref, g_ref, v_ref, s0_ref, prev_ref, o_ref, s_ref):
    del prev_ref
    _gla_step_kernel(q_ref, k_ref, g_ref, v_ref, s0_ref, o_ref, s_ref)


def _gla_step(q, k, gl, v, s0, layer, prev=None):
    b = q.shape[0]
    kq = pl.BlockSpec((None, 1, GLA_H * GLA_DK), lambda i: (i, 0, 0))
    vv = pl.BlockSpec((None, 1, GLA_H * GLA_DV), lambda i: (i, 0, 0))
    st = pl.BlockSpec((None, None, GLA_H, GLA_DK, GLA_DV), lambda i: (layer, i, 0, 0, 0))
    extra = [] if prev is None else [prev]
    return pl.pallas_call(
        _gla_step_kernel if prev is None else _gla_step_aliased_kernel,
        grid=(b,),
        in_specs=[kq, kq, kq, vv, st] + [pl.BlockSpec(memory_space=pl.ANY)] * len(extra),
        out_specs=[vv, st],
        out_shape=[jax.ShapeDtypeStruct((b, 1, GLA_H * GLA_DV), F32), jax.ShapeDtypeStruct(s0.shape, F32)],
        input_output_aliases={5: 1} if extra else {},
        compiler_params=_params("parallel"),
        name="gla_step",
    )(q, k, gl, v, s0, *extra)


def _gla_out_kernel(o_ref, r_ref, x_ref, go_ref, w_ref, y_ref):
    parts = []
    for h in range(GLA_H):
        sl = slice(h * GLA_DV, (h + 1) * GLA_DV)
        r = r_ref[:, sl]
        parts.append((_rms_rows(o_ref[:, sl], go_ref[...]) * (r * jax.nn.sigmoid(r))).astype(BF16))
    y_ref[...] = x_ref[...] + _dot(jnp.concatenate(parts, axis=1), w_ref[...])


def _gla_out(o, r, x, go, w, tm):
    m, d = x.shape
    row = pl.BlockSpec((tm, d), lambda i: (i, 0))
    full = lambda a: pl.BlockSpec(a.shape, lambda i: (0, 0))
    return pl.pallas_call(
        _gla_out_kernel,
        grid=(m // tm,),
        in_specs=[row, row, row, full(go), full(w)],
        out_specs=row,
        out_shape=jax.ShapeDtypeStruct((m, d), F32),
        compiler_params=_params("parallel"),
        name="gla_out",
    )(o, r, x, go, w)


def _gla_layer(xp, xs, s0, layer, prev_state, norm_g, w_in, w_gate2, b_gate, g_out, w_out):
    b, t, d = xp.shape
    bs = xs.shape[0]
    dk, dv = GLA_H * GLA_DK, GLA_H * GLA_DV
    rank = w_gate2.shape[0]
    w = jnp.pad(w_in, ((0, 0), (0, LANES - rank))).astype(BF16)
    wg2 = jnp.pad(w_gate2, ((0, LANES - rank), (0, 0))).astype(BF16)
    g = norm_g.reshape(1, d)
    bg = b_gate.reshape(1, dk)
    go = g_out.reshape(1, GLA_DV)
    wo = w_out.astype(BF16)
    x2 = xp.reshape(b * t, d)
    q, k, v, r, gl = _gla_proj(x2, g, w, wg2, bg, tm=512)
    o, sp = _gla_scan(q.reshape(b, t, dk), k.reshape(b, t, dk), gl.reshape(b, t, dk), v.reshape(b, t, dv))
    yp = _gla_out(o.reshape(b * t, dv), r, x2, go, wo, tm=512).reshape(b, t, d)
    xs2 = xs.reshape(bs, d)
    q, k, v, r, gl = _gla_proj(xs2, g, w, wg2, bg, tm=bs)
    o, ss = _gla_step(q.reshape(bs, 1, dk), k.reshape(bs, 1, dk), gl.reshape(bs, 1, dk), v.reshape(bs, 1, dv), s0,
                      layer, prev_state)
    ys = _gla_out(o.reshape(bs, dv), r, xs2, go, wo, tm=bs).reshape(bs, 1, d)
    return yp, ys, sp, ss


def _nsa_proj_kernel(x_ref, gn_ref, wt_ref, gq_ref, gk_ref, bg_ref, cos_ref, sin_ref, *rest):
    (q_ref, kc_ref, vc_ref, ks_ref, vs_ref, kw_ref, vw_ref, gt_ref,
     ksa_ref, vsa_ref, kwb_ref, vwa_ref) = rest[-12:]
    tm = x_ref.shape[0]
    nq = GRP * KVH * HD
    nkv = KVH * HD
    h = _rms_rows(x_ref[...], gn_ref[...]).astype(BF16)
    cos = cos_ref[...][None]
    sin = sin_ref[...][None]

    def norm_rope(z, g, nh):
        z3 = z.reshape(nh, HD, tm)
        y = z3 * lax.rsqrt(jnp.mean(z3 * z3, axis=1, keepdims=True) + EPS) * g[None]
        x1 = y[:, 0:ROT_HALF, :]
        x2 = y[:, ROT_HALF:2 * ROT_HALF, :]
        return jnp.concatenate([x1 * cos - x2 * sin, x1 * sin + x2 * cos, y[:, 2 * ROT_HALF:, :]], axis=1)

    zq = _dot_nt(wt_ref[0:nq, :], h)
    q_ref[...] = (norm_rope(zq, gq_ref[...], GRP * KVH) * (HD ** -0.5 * LOG2E)).astype(BF16)
    zkv = _dot_nt(wt_ref[nq:nq + 6 * nkv, :], h)
    outs = (kc_ref, vc_ref, ks_ref, vs_ref, kw_ref, vw_ref)
    rows = []
    for i in range(6):
        z = zkv[i * nkv:(i + 1) * nkv, :]
        rows.append(norm_rope(z, gk_ref[i // 2], KVH) if i % 2 == 0 else z.reshape(KVH, HD, tm))
        outs[i][...] = rows[i]
    zg = _dot_nt(wt_ref[nq + 6 * nkv:, :], h) + bg_ref[...]
    gt_ref[...] = jax.nn.sigmoid(zg).reshape(KVH, 16, tm)

    nsp = ksa_ref.shape[1] - HD
    blk = lax.broadcasted_iota(jnp.int32, (KVH, nsp, tm), 1)
    tok = pl.program_id(1) * tm + lax.broadcasted_iota(jnp.int32, (KVH, nsp, tm), 2)
    onehot = jnp.where(tok // L_SLC == blk, 1.0, 0.0)
    ksa_ref[...] = jnp.concatenate([rows[2], onehot], axis=1).astype(BF16)
    kwb_ref[...] = rows[4].astype(BF16)
    ones = jnp.where(lax.broadcasted_iota(jnp.int32, (KVH, V_PAD, tm), 1) == 0, 1.0, 0.0)
    vsa_ref[...] = jnp.concatenate([rows[3], ones], axis=1).astype(BF16)
    vwa_ref[...] = jnp.concatenate([rows[5], ones], axis=1).astype(BF16)


def _sel_rows(t):
    return -(-(-(-t // L_SLC)) // 16) * 16


def _nsa_proj(x, gn, wt, gq, gk, bg, cos, sin, tm, layer=0, n_layers=1, prev=None):
    b, t, d = x.shape
    full = lambda a: pl.BlockSpec(a.shape, lambda i, j: (0,) * a.ndim)
    rows_spec = lambda n: pl.BlockSpec((None, KVH, n, tm), lambda i, j: (i, 0, 0, j))
    rows_shape = lambda n, dt: jax.ShapeDtypeStruct((b, KVH, n, t), dt)
    kv_spec = pl.BlockSpec((None, None, KVH, HD, tm), lambda i, j: (layer, i, 0, 0, j))
    kv_shape = jax.ShapeDtypeStruct((n_layers, b, KVH, HD, t), F32)
    aug = [HD + _sel_rows(t), HD + V_PAD, HD, HD + V_PAD]
    prev = list(prev) if prev is not None else []
    n_in = 8
    return pl.pallas_call(
        _nsa_proj_kernel,
        grid=(b, t // tm),
        in_specs=[pl.BlockSpec((None, tm, d), lambda i, j: (i, j, 0)), full(gn), full(wt), full(gq), full(gk),
                  full(bg), pl.BlockSpec((ROT_HALF, tm), lambda i, j: (0, j)),
                  pl.BlockSpec((ROT_HALF, tm), lambda i, j: (0, j))]
        + [pl.BlockSpec(memory_space=pl.ANY)] * len(prev),
        out_specs=[pl.BlockSpec((None, GRP * KVH, HD, tm), lambda i, j: (i, 0, 0, j))] + [kv_spec] * 6
        + [rows_spec(16)] + [rows_spec(n) for n in aug],
        out_shape=[jax.ShapeDtypeStruct((b, GRP * KVH, HD, t), BF16)] + [kv_shape] * 6
        + [rows_shape(16, F32)] + [rows_shape(n, BF16) for n in aug],
        input_output_aliases={n_in + k: 1 + k for k in range(len(prev))},
        compiler_params=_params("parallel", "parallel"),
        name="nsa_proj",
    )(x, gn, wt, gq, gk, bg, cos, sin, *prev)


def _nsa_out_kernel(ot_ref, x_ref, w_ref, y_ref):
    ot = ot_ref[...].reshape(GRP * KVH * HD, ot_ref.shape[-1])
    y_ref[...] = x_ref[...] + _dot_tn(ot, w_ref[...])


def _nsa_out(ot, x, w, tm):
    b, t, d = x.shape
    return pl.pallas_call(
        _nsa_out_kernel,
        grid=(b, t // tm),
        in_specs=[pl.BlockSpec((None, GRP * KVH, HD, tm), lambda i, j: (i, 0, 0, j)),
                  pl.BlockSpec((None, tm, d), lambda i, j: (i, j, 0)),
                  pl.BlockSpec(w.shape, lambda i, j: (0, 0))],
        out_specs=pl.BlockSpec((None, tm, d), lambda i, j: (i, j, 0)),
        out_shape=jax.ShapeDtypeStruct((b, t, d), F32),
        compiler_params=_params("parallel", "parallel"),
        name="nsa_out",
    )(ot, x, w)


def _compress_kernel(*refs, n_pages, paged, is_key):
    refs = refs[1:] if paged else refs
    pages = refs[:n_pages]
    perm_ref, w_ref, b1_ref, pe_ref, w1_ref, w2t_ref, gk_ref, o_ref, lhs_scr = refs[n_pages:]
    p = pl.program_id(1)
    half = CMP_STRIDE
    n_half = PAGE // half
    low = lax.broadcasted_iota(jnp.int32, (n_half, LANES), 1) < HD
    odd_slot = (lax.broadcasted_iota(jnp.int32, (2 * HD, PAGE), 1) // n_half) % 2 == 1
    for pair in range(KVH // 2):
        for j in range(n_pages):
            y = _dot(pages[j][2 * pair:2 * pair + 2].reshape(2 * HD, PAGE).astype(BF16), perm_ref[...])
            x = jnp.where(odd_slot, jnp.concatenate([y[HD:], y[:HD]], axis=0), y).T
            rows = pl.ds(pl.multiple_of((p * n_pages + j) * n_half, n_half), n_half)
            for t in range(half // 2):
                ev = x[2 * t * n_half:(2 * t + 1) * n_half]
                od = x[(2 * t + 1) * n_half:(2 * t + 2) * n_half]
                lanes = slice(t * LANES, (t + 1) * LANES)
                lhs_scr[2 * pair, rows, lanes] = jnp.where(low, ev, od)
                lhs_scr[2 * pair + 1, rows, lanes] = jnp.where(low, od, ev)

    @pl.when(p == pl.num_programs(1) - 1)
    def _():
        nh = lhs_scr.shape[1]
        hidden = b1_ref.shape[1]
        c = _dot(pe_ref[0].astype(BF16), w1_ref[0]) + _dot(pe_ref[1].astype(BF16), w1_ref[1])
        bias = c[0:1, :] + b1_ref[...]
        for kvh in range(KVH):
            fs = _dot(lhs_scr[kvh].astype(BF16), w_ref[kvh % 2])
            hid = jax.nn.gelu(fs[:, :hidden] + pltpu.roll(fs[:, hidden:], nh - 1, 0) + bias)
            yt = _dot_nt(w2t_ref[...], hid.astype(BF16))
            if is_key:
                yt = yt * lax.rsqrt(jnp.mean(yt * yt, axis=0, keepdims=True) + EPS) * gk_ref[...]
            o_ref[kvh] = yt.astype(BF16)


def _compress(src, weights, n_pages_total, pages_per_step, is_key, page_table=None, layer=0):
    paged = page_table is not None
    b = page_table.shape[0] if paged else src.shape[1]
    n_half = PAGE // CMP_STRIDE
    nh = n_pages_total * n_half
    steps = n_pages_total // pages_per_step
    tok = jnp.arange(PAGE)
    perm = (((tok % CMP_STRIDE) * n_half + tok // CMP_STRIDE)[:, None] == tok[None, :]).astype(BF16)
    if paged:
        page_spec = lambda j: pl.BlockSpec(
            (None, None, KVH, HD, PAGE), lambda i, p, pt: (layer, pt[i, p * pages_per_step + j], 0, 0, 0))
        full = lambda a: pl.BlockSpec(a.shape, lambda i, p, pt: (0,) * a.ndim)
        o_spec = pl.BlockSpec((None, KVH, HD, nh), lambda i, p, pt: (i, 0, 0, 0))
    else:
        page_spec = lambda j: pl.BlockSpec((None, None, KVH, HD, PAGE),
                                           lambda i, p: (layer, i, 0, 0, p * pages_per_step + j))
        full = lambda a: pl.BlockSpec(a.shape, lambda i, p: (0,) * a.ndim)
        o_spec = pl.BlockSpec((None, KVH, HD, nh), lambda i, p: (i, 0, 0, 0))
    consts = [perm] + list(weights)
    grid_spec = pltpu.PrefetchScalarGridSpec(
        num_scalar_prefetch=1 if paged else 0,
        grid=(b, steps),
        in_specs=[page_spec(j) for j in range(pages_per_step)] + [full(a) for a in consts],
        out_specs=o_spec,
        scratch_shapes=[pltpu.VMEM((KVH, nh, CMP_STRIDE * HD), F32)],
    )
    args = ([page_table] if paged else []) + [src] * pages_per_step + consts
    return pl.pallas_call(
        functools.partial(_compress_kernel, n_pages=pages_per_step, paged=paged, is_key=is_key),
        grid_spec=grid_spec,
        out_shape=jax.ShapeDtypeStruct((b, KVH, HD, nh), BF16),
        compiler_params=_params("parallel", "arbitrary"),
        name="nsa_compress",
    )(*args)


def _cmp_weights(pe, w1, b1, w2, g_kcmp, nh):
    hidden = w1.shape[-1]
    wfs = jnp.concatenate([w1[:CMP_STRIDE], w1[CMP_STRIDE:]], axis=-1)
    swapped = wfs.reshape(CMP_STRIDE // 2, 2, HD, 2 * hidden)[:, ::-1]
    w_pair = jnp.stack([wfs, swapped.reshape(wfs.shape)]).reshape(2, CMP_STRIDE * HD, 2 * hidden).astype(BF16)
    pe2 = jnp.zeros((2, 8, CMP_STRIDE * HD), F32).at[:, 0, :].set(pe.reshape(2, CMP_STRIDE * HD))
    w1f = w1.reshape(2, CMP_STRIDE * HD, hidden).astype(BF16)
    gk = jnp.broadcast_to(g_kcmp.reshape(HD, 1), (HD, nh))
    return w_pair, b1.reshape(1, hidden), pe2, w1f, w2.T.astype(BF16), gk


def _softmax_update(s, mask, vt, m, l, acc):
    m_new = jnp.maximum(m, jnp.max(jnp.where(mask, s, NEG), axis=0, keepdims=True))
    alpha = jnp.exp2(m - m_new)
    p = jnp.where(mask, jnp.exp2(s - m_new), 0.0)
    l_new = l * alpha + jnp.sum(p, axis=0, keepdims=True)
    acc_new = acc * alpha + _dot(vt, p.astype(BF16))
    return m_new, l_new, acc_new, p


def _softmax_init(lanes):
    return jnp.full((1, lanes), NEG, F32), jnp.zeros((1, lanes), F32), jnp.zeros((HD, lanes), F32)


def _finish(l, acc):
    return acc / jnp.maximum(l, 1e-30)


def _split_dot(mt, x):
    hi = x.astype(BF16)
    lo = (x - hi.astype(F32)).astype(BF16)
    return _dot(mt, hi) + _dot(mt, lo)


def _attn_prompt_kernel(q_ref, kc_ref, vc_ref, ksa_ref, vsa_ref, kwb_ref, vwa_ref, gt_ref, mt_ref, o_ref,
                        s_scr, s2_scr, *, tq, n_cmp, n_sel):
    i = pl.program_id(2)
    lanes = GRP * tq
    ncp = kc_ref.shape[1]
    nsp = mt_ref.shape[0]
    qt = jnp.concatenate([q_ref[h] for h in range(GRP)], axis=1)

    def col_max(mx, s):
        return jnp.maximum(mx, jnp.max(s.reshape(s.shape[0] // 8, 8, lanes), axis=0))

    def finish(acc):
        return acc[0:HD] / jnp.maximum(acc[HD:HD + 1], 1e-30)

    tpos1 = i * tq + lax.broadcasted_iota(jnp.int32, (1, tq), 1)
    tpos = jnp.concatenate([tpos1] * GRP, axis=1)

    s = _dot_tn(kc_ref[...], qt)
    n_idx = lax.broadcasted_iota(jnp.int32, (ncp, lanes), 0)
    mask = (n_idx * CMP_STRIDE + (L_CMP - 1) <= tpos) & (n_idx < n_cmp)
    m, l, acc, p = _softmax_update(s, mask, vc_ref[...], *_softmax_init(lanes))
    o_c = _finish(l, acc)
    p = p / jnp.maximum(l, 1e-30)
    imp = p[:, 0:tq]
    for h in range(1, GRP):
        imp = imp + p[:, h * tq:(h + 1) * tq]
    p_slc = _split_dot(mt_ref[...], imp)

    j_idx = lax.broadcasted_iota(jnp.int32, (nsp, tq), 0)
    cur = tpos1 // L_SLC
    valid = j_idx <= cur
    forced = (j_idx == 0) | (j_idx == cur) | (j_idx == cur - 1)
    score = jnp.where(valid & forced, jnp.inf, jnp.where(valid, p_slc, -jnp.inf))
    bits = lax.bitcast_convert_type(score, jnp.int32)
    key = jnp.where(bits >= 0, bits, bits ^ 0x7FFFFFFF)
    n_grp = nsp // 8
    keys = [key[8 * r:8 * r + 8] for r in range(n_grp)]
    keys_m1 = [k - 1 for k in keys]
    sub = lax.broadcasted_iota(jnp.int32, (8, tq), 0)

    def count_group(grp, ranks):
        ranks = list(ranks)
        for u in range(8):
            row = jnp.broadcast_to(keys[grp][u:u + 1, :], (8, tq))
            for r in range(n_grp):
                thr = keys[r] if r < grp else keys_m1[r] if r > grp else jnp.where(sub > u, keys_m1[r], keys[r])
                ranks[r] = ranks[r] + jnp.where(row > thr, 1.0, 0.0)
        return tuple(ranks)

    ranks = tuple(jnp.zeros((8, tq), F32) for _ in range(n_grp))
    last_valid = (i * tq + tq - 1) // L_SLC
    for grp in range(n_grp):
        ranks = lax.cond(8 * grp <= last_valid, functools.partial(count_group, grp), lambda r: r, ranks)
    bias = jnp.where(jnp.concatenate(ranks, axis=0) < n_sel, 0.0, NEG)
    qa = jnp.concatenate([qt, jnp.concatenate([bias] * GRP, axis=1).astype(BF16)], axis=0)

    row_k = lax.broadcasted_iota(jnp.int32, (KT, lanes), 0)

    def slc_scores(st, buf):
        start = pl.multiple_of(st * KT, KT)
        buf[0:KT, :] = _dot_tn(ksa_ref[:, pl.ds(start, KT)], qa)

    def slc_softmax(st, buf, carry, diagonal=False):
        m, acc = carry
        s = buf[0:KT, :]
        if diagonal:
            s = jnp.where(st * KT + row_k <= tpos, s, NEG)
        m_new = jnp.maximum(m, jnp.max(col_max(jnp.full((8, lanes), NEG, F32), s), axis=0, keepdims=True))
        p = jnp.exp2(s - m_new).astype(BF16)
        pv = _dot(vsa_ref[:, pl.ds(pl.multiple_of(st * KT, KT), KT)], p)
        return m_new, acc * jnp.exp2(m - m_new) + pv

    def slc_pair(u, carry):
        slc_scores(2 * u + 1, s2_scr)
        carry = slc_softmax(2 * u, s_scr, carry)
        slc_scores(2 * u + 2, s_scr)
        return slc_softmax(2 * u + 1, s2_scr, carry)

    def slc_tail_odd(carry):
        slc_scores(n_full, s2_scr)
        carry = slc_softmax(n_full - 1, s_scr, carry)
        return slc_softmax(n_full, s2_scr, carry, diagonal=True)

    def slc_tail_even(carry):
        return slc_softmax(n_full, s_scr, carry, diagonal=True)

    n_full = (i * tq) // KT
    slc_scores(0, s_scr)
    init = (jnp.full((1, lanes), NEG, F32), jnp.zeros((HD + V_PAD, lanes), F32))
    carry = lax.fori_loop(0, n_full // 2, slc_pair, init)
    o_s = finish(lax.cond(n_full % 2 == 1, slc_tail_odd, slc_tail_even, carry)[1])

    n_wt = WINDOW // tq + 1
    w0 = jnp.maximum(i - (n_wt - 1), 0) * tq
    row_w = lax.broadcasted_iota(jnp.int32, (tq, lanes), 0)
    col_w = jnp.concatenate([lax.broadcasted_iota(jnp.int32, (tq, tq), 1)] * GRP, axis=1)

    def window(steady):
        mx = jnp.full((8, lanes), NEG, F32)
        for r in range(n_wt):
            start = pl.multiple_of(w0 + r * tq, tq)
            s = _dot_tn(kwb_ref[:, pl.ds(start, tq)], qt)
            if not steady:
                rel = tpos - (start + row_w)
                s = jnp.where((rel >= 0) & (rel <= WINDOW), s, NEG)
            elif r == 0:
                s = jnp.where(row_w >= col_w, s, NEG)
            elif r == n_wt - 1:
                s = jnp.where(row_w <= col_w, s, NEG)
            s_scr[r * tq:(r + 1) * tq, :] = s
            mx = col_max(mx, s)
        m = jnp.max(mx, axis=0, keepdims=True)
        acc = jnp.zeros((HD + V_PAD, lanes), F32)
        for r in range(n_wt):
            start = pl.multiple_of(w0 + r * tq, tq)
            p = jnp.exp2(s_scr[r * tq:(r + 1) * tq, :] - m).astype(BF16)
            acc = acc + _dot(vwa_ref[:, pl.ds(start, tq)], p)
        return finish(acc)

    o_w = lax.cond(i >= n_wt - 1, functools.partial(window, True), functools.partial(window, False))

    for h in range(GRP):
        ls = slice(h * tq, (h + 1) * tq)
        o = (gt_ref[3 * h:3 * h + 1, :] * o_c[:, ls] + gt_ref[3 * h + 1:3 * h + 2, :] * o_s[:, ls]
             + gt_ref[3 * h + 2:3 * h + 3, :] * o_w[:, ls])
        o_ref[h] = o.astype(BF16)


def _slc_matrix(nsp, ncp, n_cmp):
    ratio = L_SLC // CMP_STRIDE
    j = jnp.arange(nsp)[:, None]
    n = jnp.arange(ncp)[None, :]
    m = ((n >= ratio * j) & (n <= ratio * j + ratio - 1)).astype(F32)
    m = m + ((n >= ratio * j - 1) & (n <= ratio * j + ratio - 2)).astype(F32)
    return jnp.where(n < n_cmp, m, 0.0).astype(BF16)


def _attn_prompt(qt, kct, vct, ksa, vsa, kwb, vwa, gt, tq=4 * LANES):
    b, _, _, t = qt.shape
    assert t % KT == 0 and WINDOW % tq == 0 and t >= WINDOW + tq
    ncp = kct.shape[-1]
    n_cmp = t // CMP_STRIDE - 1
    ns = -(-t // L_SLC)
    mt = _slc_matrix(_sel_rows(t), ncp, n_cmp)
    res = lambda a: pl.BlockSpec((None, None) + a.shape[2:], lambda i, g, j: (i, g, 0, 0))
    return pl.pallas_call(
        functools.partial(_attn_prompt_kernel, tq=tq, n_cmp=n_cmp, n_sel=min(N_SEL, ns)),
        grid=(b, KVH, t // tq),
        in_specs=[pl.BlockSpec((None, GRP, HD, tq), lambda i, g, j: (i, g, 0, j)),
                  res(kct), res(vct), res(ksa), res(vsa), res(kwb), res(vwa),
                  pl.BlockSpec((None, None, 16, tq), lambda i, g, j: (i, g, 0, j)),
                  pl.BlockSpec(mt.shape, lambda i, g, j: (0, 0))],
        out_specs=pl.BlockSpec((None, GRP, HD, tq), lambda i, g, j: (i, g, 0, j)),
        out_shape=jax.ShapeDtypeStruct(qt.shape, BF16),
        scratch_shapes=[pltpu.VMEM((max(KT, WINDOW + tq), GRP * tq), F32), pltpu.VMEM((KT, GRP * tq), F32)],
        compiler_params=_params("parallel", "parallel", "arbitrary"),
        name="nsa_attn_prompt",
    )(qt, kct, vct, ksa, vsa, kwb, vwa, gt, mt)


def _attn_sample_kernel(*refs, n_pages, n_cmp, n_blocks, n_sel):
    refs = refs[1:]
    k_pages = refs[:n_pages]
    v_pages = refs[n_pages:2 * n_pages]
    (q_ref, kc_ref, vc_ref, kw_ref, vw_ref, kns_ref, vns_ref, knw_ref, vnw_ref, gt_ref, mt_ref,
     o_ref, sel_scr, oc_scr, ow_scr, m_scr, l_scr, acc_scr) = refs[2 * n_pages:]
    p = pl.program_id(1)
    ncp = kc_ref.shape[-1]
    nsp = mt_ref.shape[0]
    rows_all = KVH * HD
    first_row = lax.broadcasted_iota(jnp.int32, (LANES, LANES), 0) == 0
    qt = q_ref[...]

    def stacked(ref):
        return ref[...].reshape(rows_all, ref.shape[-1]).astype(BF16)

    def finish(l, acc):
        return acc / jnp.maximum(l, 1e-30)

    def init():
        return jnp.full((1, LANES), NEG, F32), jnp.zeros((1, LANES), F32), jnp.zeros((rows_all, LANES), F32)

    @pl.when(p == 0)
    def _():
        lane = lax.broadcasted_iota(jnp.int32, (ncp, LANES), 1)
        n_idx = lax.broadcasted_iota(jnp.int32, (ncp, LANES), 0)
        j_col = lax.broadcasted_iota(jnp.int32, (nsp, LANES), 0)
        lane_s = lax.broadcasted_iota(jnp.int32, (nsp, LANES), 1)
        jp_idx = lax.broadcasted_iota(jnp.int32, (nsp, nsp), 0)
        j_idx = lax.broadcasted_iota(jnp.int32, (nsp, nsp), 1)
        cur = n_blocks - 1
        s = _dot_tn(stacked(kc_ref), qt)
        m, l, acc, pr = _softmax_update(s, n_idx < n_cmp, stacked(vc_ref), *init())
        oc_scr[...] = finish(l, acc)
        pr = pr / jnp.maximum(l, 1e-30)
        sel_all = jnp.zeros((nsp, LANES), F32)
        for g in range(KVH):
            mine = (lane >= g * GRP) & (lane < (g + 1) * GRP)
            imp = jnp.sum(jnp.where(mine, pr, 0.0), axis=1, keepdims=True)
            p_slc = _split_dot(mt_ref[...], jnp.broadcast_to(imp, (ncp, LANES)))
            valid = j_col <= cur
            forced = (j_col == 0) | (j_col == cur) | (j_col == cur - 1)
            score = jnp.where(valid & forced, jnp.inf, jnp.where(valid, p_slc, -jnp.inf))
            col = jnp.concatenate([score] * (nsp // LANES), axis=1)
            row = col.T
            ahead = (col > row) | ((col == row) & (jp_idx < j_idx))
            rank = jnp.sum(jnp.where(ahead, 1.0, 0.0), axis=0, keepdims=True)
            sel = jnp.broadcast_to(jnp.where(rank < n_sel, 1.0, 0.0), (LANES, nsp)).T
            sel_all = jnp.where((lane_s >= g * GRP) & (lane_s < (g + 1) * GRP), sel, sel_all)
        sel_scr[...] = sel_all
        s = _dot_tn(stacked(kw_ref), qt)
        st = _softmax_update(s, jnp.full(s.shape, True), stacked(vw_ref), *init())[:3]
        s = _dot_tn(stacked(knw_ref), qt)
        _, l, acc = _softmax_update(s, first_row, stacked(vnw_ref), *st)[:3]
        ow_scr[...] = finish(l, acc)
        m_scr[...], l_scr[...], acc_scr[...] = init()

    blocks_per_page = PAGE // L_SLC
    kt = jnp.concatenate([k_pages[j][...].reshape(rows_all, PAGE) for j in range(n_pages)], axis=1).astype(BF16)
    vt = jnp.concatenate([v_pages[j][...].reshape(rows_all, PAGE) for j in range(n_pages)], axis=1).astype(BF16)
    s = _dot_tn(kt, qt)
    rows = []
    for r in range(n_pages * blocks_per_page):
        e = sel_scr[pl.ds(p * n_pages * blocks_per_page + r, 1), :]
        rows.append(jnp.broadcast_to(e, (L_SLC, LANES)))
    mask = jnp.concatenate(rows, axis=0) > 0.5
    m_scr[...], l_scr[...], acc_scr[...] = _softmax_update(s, mask, vt, m_scr[...], l_scr[...], acc_scr[...])[:3]

    @pl.when(p == pl.num_programs(1) - 1)
    def _():
        s = _dot_tn(stacked(kns_ref), qt)
        _, l, acc = _softmax_update(s, first_row, stacked(vns_ref), m_scr[...], l_scr[...], acc_scr[...])[:3]
        o_ref[...] = (gt_ref[0:1, :] * oc_scr[...] + gt_ref[1:2, :] * finish(l, acc) + gt_ref[2:3, :] * ow_scr[...])


def _attn_sample(qs, kct, vct, pool_k, pool_v, page_table, layer, kwt, vwt, kns, vns, knw, vnw, gs,
                 pages_per_step=16):
    b, n_pages_total = page_table.shape
    pages_per_step = min(pages_per_step, n_pages_total)
    past = n_pages_total * PAGE
    n_cmp = past // CMP_STRIDE - 1
    ncp = kct.shape[-1]
    n_blocks = past // L_SLC + 1
    nsp = -(-n_blocks // LANES) * LANES
    mt = _slc_matrix(nsp, ncp, n_cmp)
    steps = n_pages_total // pages_per_step
    page_spec = lambda j: pl.BlockSpec((None, None, KVH, HD, PAGE),
                                       lambda i, p, pt: (layer, pt[i, p * pages_per_step + j], 0, 0, 0))
    def per_b(a):
        if a.ndim == 5:
            return pl.BlockSpec((None, None) + a.shape[2:], lambda i, p, pt: (layer, i, 0, 0, 0))
        return pl.BlockSpec((None,) + a.shape[1:], lambda i, p, pt: (i,) + (0,) * (a.ndim - 1))

    small = [qs, kct, vct, kwt, vwt, kns, vns, knw, vnw, gs]
    grid_spec = pltpu.PrefetchScalarGridSpec(
        num_scalar_prefetch=1,
        grid=(b, steps),
        in_specs=[page_spec(j) for j in range(pages_per_step)] * 2 + [per_b(a) for a in small]
        + [pl.BlockSpec(mt.shape, lambda i, p, pt: (0, 0))],
        out_specs=pl.BlockSpec((None, KVH * HD, LANES), lambda i, p, pt: (i, 0, 0)),
        scratch_shapes=[pltpu.VMEM((nsp, LANES), F32), pltpu.VMEM((KVH * HD, LANES), F32),
                        pltpu.VMEM((KVH * HD, LANES), F32), pltpu.VMEM((1, LANES), F32),
                        pltpu.VMEM((1, LANES), F32), pltpu.VMEM((KVH * HD, LANES), F32)],
    )
    return pl.pallas_call(
        functools.partial(_attn_sample_kernel, n_pages=pages_per_step, n_cmp=n_cmp, n_blocks=n_blocks,
                          n_sel=min(N_SEL, n_blocks)),
        grid_spec=grid_spec,
        out_shape=jax.ShapeDtypeStruct((b, KVH * HD, LANES), F32),
        compiler_params=_params("parallel", "arbitrary"),
        name="nsa_attn_sample",
    )(page_table, *([pool_k] * pages_per_step), *([pool_v] * pages_per_step), *small, mt)


def _rope_tables(pos):
    inv_freq = ROPE_THETA ** (-jnp.arange(ROT_HALF, dtype=F32) * 2.0 / (2 * ROT_HALF))
    ang = pos.astype(F32)[:, None] * inv_freq[None, :]
    return jnp.cos(ang).T, jnp.sin(ang).T


def _nsa_layer(xp, xs, pools, win_bufs, page_table, layer, n_layers, prev_rows, norm_g, w_in, b_gate, g_q, g_k,
               g_kcmp, cmp_pe, cmp_w1, cmp_b1, cmp_w2, w_out):
    b, t, d = xp.shape
    bs = xs.shape[0]
    past = page_table.shape[1] * PAGE
    nq, nkv = GRP * KVH * HD, KVH * HD
    n_gate = 3 * GRP * KVH

    wt_g = jnp.pad(w_in[:, nq + 6 * nkv:].T.reshape(KVH, n_gate // KVH, d), ((0, 0), (0, 16 - n_gate // KVH), (0, 0)))
    wt = jnp.concatenate([w_in[:, :nq + 6 * nkv].T, wt_g.reshape(KVH * 16, d)], axis=0).astype(BF16)
    bg = jnp.pad(b_gate.reshape(KVH, n_gate // KVH), ((0, 0), (0, 16 - n_gate // KVH))).reshape(KVH * 16, 1)
    gn = norm_g.reshape(1, d)
    wo = w_out.astype(BF16)

    def project(x, pos, tm, **stacking):
        cos, sin = _rope_tables(pos)
        col = lambda v: jnp.broadcast_to(v[..., None], v.shape + (tm,))
        return _nsa_proj(x, gn, wt, col(g_q), col(g_k), col(bg[:, 0]), cos, sin, tm, **stacking)

    tm = min(512, t)
    outs = project(xp, jnp.arange(t), tm, layer=layer, n_layers=n_layers, prev=prev_rows)
    qt, rows_p, (gt, ksa, vsa, kwb, vwa) = outs[0], list(outs[1:7]), outs[7:]
    pages = t // PAGE
    nh = pages * (PAGE // CMP_STRIDE)
    cmp_w = [_cmp_weights(cmp_pe[i], cmp_w1[i], cmp_b1[i], cmp_w2[i], g_kcmp, nh) for i in range(2)]
    pps = min(16, pages)
    kct = _compress(rows_p[0], cmp_w[0], pages, pps, True, layer=layer)
    vct = _compress(rows_p[1], cmp_w[1], pages, pps, False, layer=layer)
    ot = _attn_prompt(qt, kct, vct, ksa, vsa, kwb, vwa, gt)
    yp = _nsa_out(ot, xp, wo, tm)

    xs_pad = jnp.pad(xs.reshape(1, bs, d), ((0, 0), (0, LANES - bs), (0, 0)))
    outs = project(xs_pad, jnp.full((LANES,), past), LANES)
    qt_s, rows_s, gt_s = outs[0], outs[1:7], outs[7]
    pages_s = page_table.shape[1]
    nh_s = pages_s * (PAGE // CMP_STRIDE)
    cmp_ws = [_cmp_weights(cmp_pe[i], cmp_w1[i], cmp_b1[i], cmp_w2[i], g_kcmp, nh_s) for i in range(2)]
    pool5 = [jnp.transpose(pl_, (0, 1, 3, 4, 2)) for pl_ in pools]
    pps = min(32, pages_s)
    kct_s = _compress(pool5[0], cmp_ws[0], pages_s, pps, True, page_table, layer)
    vct_s = _compress(pool5[1], cmp_ws[1], pages_s, pps, False, page_table, layer)
    qs = jnp.transpose(qt_s[0, :, :, :bs].reshape(KVH, GRP, HD, bs), (3, 0, 2, 1))
    qs = jnp.einsum("bgdh,gk->bgdkh", qs, jnp.eye(KVH, dtype=qs.dtype))
    qs = jnp.pad(qs.reshape(bs, KVH * HD, KVH * GRP), ((0, 0), (0, 0), (0, LANES - KVH * GRP)))
    new = lambda a: jnp.pad(jnp.transpose(a[0, 0, :, :, :bs], (2, 0, 1))[..., None],
                            ((0, 0), (0, 0), (0, 0), (0, LANES - 1)))
    gs = jnp.transpose(gt_s[0, :, :n_gate // KVH, :bs].reshape(KVH, GRP, 3, bs), (3, 2, 0, 1))
    gs = jnp.pad(gs.reshape(bs, 3, KVH * GRP), ((0, 0), (0, 8 - 3), (0, LANES - KVH * GRP)))
    win5 = [jnp.transpose(wb, (0, 1, 3, 4, 2)) for wb in win_bufs]
    ot_s = _attn_sample(qs, kct_s, vct_s, pool5[2], pool5[3], page_table, layer, win5[0], win5[1],
                        new(rows_s[2]), new(rows_s[3]), new(rows_s[4]), new(rows_s[5]), gs)
    ot_s = ot_s.reshape(bs, KVH, HD, LANES)
    ot_s = jnp.stack([ot_s[:, g, :, g * GRP:(g + 1) * GRP] for g in range(KVH)])
    ot_s = jnp.transpose(ot_s, (0, 3, 2, 1)).reshape(1, GRP * KVH, HD, bs)
    ot_s = jnp.pad(ot_s, ((0, 0), (0, 0), (0, 0), (0, LANES - bs))).astype(BF16)
    ys = _nsa_out(ot_s, xs_pad, wo, LANES)[0, :bs].reshape(bs, 1, d)

    rows_s = [jnp.transpose(a[0, 0, :, :, :bs], (2, 0, 1)).reshape(bs, 1, KVH, HD) for a in rows_s]
    return yp, ys, rows_p, rows_s


def kernel(x_prompt, x_sample, state_gla, cache_k_cmp, cache_v_cmp, cache_k_slc, cache_v_slc, cache_k_win, cache_v_win, page_table, norm_mix, norm_mlp, mlp_up, mlp_down, gla_w_in, gla_w_gate2, gla_b_gate, gla_g_out, gla_w_out, nsa_w_in, nsa_b_gate, nsa_g_q, nsa_g_k, nsa_g_kcmp, nsa_cmp_pe, nsa_cmp_w1, nsa_cmp_b1, nsa_cmp_w2, nsa_w_out):
    depth = norm_mix.shape[0]
    b, t, d = x_prompt.shape
    bs = x_sample.shape[0]
    xp, xs = x_prompt, x_sample
    gla_p, nsa_s = [], []
    gla_s = None
    nsa_p = None
    pools = (cache_k_cmp, cache_v_cmp, cache_k_slc, cache_v_slc)
    wu, wd = mlp_up.astype(BF16), mlp_down.astype(BF16)
    for i in range(depth):
        j = i // 2
        if i % 2 == 0:
            xp, xs, sp, gla_s = _gla_layer(xp, xs, state_gla, j, gla_s, norm_mix[i], gla_w_in[j], gla_w_gate2[j],
                                           gla_b_gate[j], gla_g_out[j], gla_w_out[j])
            gla_p.append(sp)
        else:
            xp, xs, nsa_p, rs = _nsa_layer(xp, xs, pools, (cache_k_win, cache_v_win), page_table, j, depth // 2,
                                           nsa_p, norm_mix[i], nsa_w_in[j], nsa_b_gate[j], nsa_g_q[j], nsa_g_k[j],
                                           nsa_g_kcmp[j], nsa_cmp_pe[j], nsa_cmp_w1[j], nsa_cmp_b1[j],
                                           nsa_cmp_w2[j], nsa_w_out[j])
            nsa_s.append(rs)
        g = norm_mlp[i].reshape(1, d)
        xp = _mlp(xp.reshape(b * t, d), g, wu, wd, i, tm=min(1024, b * t)).reshape(b, t, d)
        xs = _mlp(xs.reshape(bs, d), g, wu, wd, i, tm=bs).reshape(bs, 1, d)
    stack = lambda lst, r: jnp.stack([e[r] for e in lst])
    n_win = min(WINDOW, t)
    rows = lambda r, n: jnp.transpose(nsa_p[r][..., t - n:], (0, 1, 4, 2, 3))
    return (xp, xs, jnp.stack(gla_p), gla_s,
            rows(0, t), stack(nsa_s, 0), rows(1, t), stack(nsa_s, 1),
            rows(2, t), stack(nsa_s, 2), rows(3, t), stack(nsa_s, 3),
            rows(4, n_win), stack(nsa_s, 4), rows(5, n_win), stack(nsa_s, 5))
```

```python
import functools

import jax
import jax.numpy as jnp
from jax import lax
from jax.experimental import pallas as pl
from jax.experimental.pallas import tpu as pltpu

F32 = jnp.float32
BF16 = jnp.bfloat16
EPS = 1e-6
NEG = -1e30
VMEM_LIMIT_BYTES = 48 * 1024 * 1024
LANES = 128
PAGE = 128
HD = 64
KVH = 4
GRP = 4
L_CMP, CMP_STRIDE, L_SLC, N_SEL, WINDOW = 32, 16, 64, 16, 512
GLA_H, GLA_DK, GLA_DV, GLA_CHUNK = 4, 128, 256, 64
ROPE_THETA, ROT_HALF = 500000.0, 8
LOG2E = 1.4426950408889634
V_PAD = 16
KT = 512


def _params(*sem):
    return pltpu.CompilerParams(dimension_semantics=sem, vmem_limit_bytes=VMEM_LIMIT_BYTES)


def _dot(a, b):
    return jnp.dot(a, b, preferred_element_type=F32)


def _dot_nt(a, b):
    return lax.dot_general(a, b, (((1,), (1,)), ((), ())), preferred_element_type=F32)


def _dot_tn(a, b):
    return lax.dot_general(a, b, (((0,), (0,)), ((), ())), preferred_element_type=F32)


def _rms_rows(x, g):
    ms = jnp.mean(x * x, axis=-1, keepdims=True)
    return x * lax.rsqrt(ms + EPS) * g


def _mlp_kernel(x_ref, g_ref, wu_ref, wd_ref, o_ref, h_scr, acc_scr):
    f = pl.program_id(1)

    @pl.when(f == 0)
    def _():
        h_scr[...] = _rms_rows(x_ref[...], g_ref[...]).astype(BF16)
        acc_scr[...] = jnp.zeros_like(acc_scr)

    u = jnp.maximum(_dot(h_scr[...], wu_ref[...]), 0.0)
    acc_scr[...] += _dot((u * u).astype(BF16), wd_ref[...])

    @pl.when(f == pl.num_programs(1) - 1)
    def _():
        o_ref[...] = x_ref[...] + acc_scr[...]


def _mlp(x, g, wu, wd, layer, tm, tf=2048):
    m, d = x.shape
    ff = wu.shape[2]
    return pl.pallas_call(
        _mlp_kernel,
        grid=(m // tm, ff // tf),
        in_specs=[pl.BlockSpec((tm, d), lambda i, f: (i, 0)),
                  pl.BlockSpec((1, d), lambda i, f: (0, 0)),
                  pl.BlockSpec((None, d, tf), lambda i, f: (layer, 0, f)),
                  pl.BlockSpec((None, tf, d), lambda i, f: (layer, f, 0))],
        out_specs=pl.BlockSpec((tm, d), lambda i, f: (i, 0)),
        out_shape=jax.ShapeDtypeStruct((m, d), F32),
        scratch_shapes=[pltpu.VMEM((tm, d), BF16), pltpu.VMEM((tm, d), F32)],
        compiler_params=_params("parallel", "arbitrary"),
        name="mlp_block",
    )(x, g, wu, wd)


def _gla_proj_kernel(x_ref, g_ref, w_ref, wg2_ref, bg_ref, q_ref, k_ref, v_ref, r_ref, gl_ref):
    h = _rms_rows(x_ref[...], g_ref[...]).astype(BF16)
    dk, dv = GLA_H * GLA_DK, GLA_H * GLA_DV
    q_ref[...] = _dot(h, w_ref[:, 0:dk]) * (GLA_DK ** -0.5)
    k_ref[...] = _dot(h, w_ref[:, dk:2 * dk])
    v_ref[...] = _dot(h, w_ref[:, 2 * dk:2 * dk + dv])
    r_ref[...] = _dot(h, w_ref[:, 2 * dk + dv:2 * dk + 2 * dv])
    gr = _dot(h, w_ref[:, 2 * dk + 2 * dv:])
    xg = _dot(gr.astype(BF16), wg2_ref[...]) + bg_ref[...]
    gl_ref[...] = jax.nn.log_sigmoid(xg) * (1.0 / 16.0)


def _gla_proj(x, g, w, wg2, bg, tm):
    m, d = x.shape
    dk, dv = GLA_H * GLA_DK, GLA_H * GLA_DV
    row = lambda n: pl.BlockSpec((tm, n), lambda i: (i, 0))
    full = lambda a: pl.BlockSpec(a.shape, lambda i: (0, 0))
    return pl.pallas_call(
        _gla_proj_kernel,
        grid=(m // tm,),
        in_specs=[row(d), full(g), full(w), full(wg2), full(bg)],
        out_specs=[row(dk), row(dk), row(dv), row(dv), row(dk)],
        out_shape=[jax.ShapeDtypeStruct((m, n), F32) for n in (dk, dk, dv, dv, dk)],
        compiler_params=_params("parallel"),
        name="gla_proj",
    )(x, g, w, wg2, bg)


def _gla_scan_kernel(q_ref, k_ref, g_ref, v_ref, o_ref, s_ref, st_scr, *, n_chunks, n_heads):
    c_len = GLA_CHUNK
    step = pl.program_id(2)

    @pl.when(step == 0)
    def _():
        st_scr[...] = jnp.zeros_like(st_scr)

    rowi = lax.broadcasted_iota(jnp.int32, (c_len, GLA_DK), 0)
    causal = (lax.broadcasted_iota(jnp.int32, (c_len, c_len), 0)
              >= lax.broadcasted_iota(jnp.int32, (c_len, c_len), 1))

    def body(c, carry):
        sl = pl.ds(pl.multiple_of(c * c_len, c_len), c_len)
        for h in range(n_heads):
            ks = slice(h * GLA_DK, (h + 1) * GLA_DK)
            vs = slice(h * GLA_DV, (h + 1) * GLA_DV)
            b = g_ref[sl, ks]
            sh = 1
            while sh < c_len:
                b = b + jnp.where(rowi >= sh, pltpu.roll(b, sh, 0), 0.0)
                sh *= 2
            b_last = b[c_len - 1:c_len, :]
            b_mid = b[c_len // 2 - 1:c_len // 2, :]
            q = q_ref[sl, ks]
            k = k_ref[sl, ks]
            v = v_ref[sl, vs].astype(BF16)
            qe = (q * jnp.exp(b)).astype(BF16)
            qa = (q * jnp.exp(b - b_mid)).astype(BF16)
            ka = (k * jnp.exp(b_mid - b)).astype(BF16)
            kd = (k * jnp.exp(b_last - b)).astype(BF16)
            a = jnp.where(causal, _dot_nt(qa, ka), 0.0)
            st = st_scr[h]
            o_ref[sl, vs] = _dot_nt(qe, st.astype(BF16)) + _dot(a.astype(BF16), v)
            st_scr[h] = st * jnp.exp(b_last) + _dot_tn(v, kd)
        return carry

    lax.fori_loop(0, n_chunks, body, 0)

    @pl.when(step == pl.num_programs(2) - 1)
    def _():
        for h in range(n_heads):
            s_ref[h] = st_scr[h].T


def _gla_scan(q, k, gl, v, n_heads=4, t_blk=1024):
    b, t, _ = q.shape
    t_blk = min(t_blk, t)
    kq = pl.BlockSpec((None, t_blk, n_heads * GLA_DK), lambda i, h, j: (i, j, h))
    vv = pl.BlockSpec((None, t_blk, n_heads * GLA_DV), lambda i, h, j: (i, j, h))
    return pl.pallas_call(
        functools.partial(_gla_scan_kernel, n_chunks=t_blk // GLA_CHUNK, n_heads=n_heads),
        grid=(b, GLA_H // n_heads, t // t_blk),
        in_specs=[kq, kq, kq, vv],
        out_specs=[vv, pl.BlockSpec((None, n_heads, GLA_DK, GLA_DV), lambda i, h, j: (i, h, 0, 0))],
        out_shape=[jax.ShapeDtypeStruct((b, t, GLA_H * GLA_DV), F32),
                   jax.ShapeDtypeStruct((b, GLA_H, GLA_DK, GLA_DV), F32)],
        scratch_shapes=[pltpu.VMEM((n_heads, GLA_DV, GLA_DK), F32)],
        compiler_params=_params("parallel", "parallel", "arbitrary"),
        name="gla_scan",
    )(q, k, gl, v)


def _gla_step_kernel(q_ref, k_ref, g_ref, v_ref, s0_ref, o_ref, s_ref):
    def col(x):
        return jnp.broadcast_to(x, (LANES, LANES)).T

    for h in range(GLA_H):
        ks = slice(h * GLA_DK, (h + 1) * GLA_DK)
        qc, kc, ec = col(q_ref[:, ks]), col(k_ref[:, ks]), col(jnp.exp(g_ref[:, ks]))
        for half in range(GLA_DV // LANES):
            vs = slice(h * GLA_DV + half * LANES, h * GLA_DV + (half + 1) * LANES)
            ss = slice(half * LANES, (half + 1) * LANES)
            sn = ec * s0_ref[h, :, ss] + kc * v_ref[:, vs]
            s_ref[h, :, ss] = sn
            o_ref[:, vs] = jnp.sum(qc * sn, axis=0, keepdims=True)


def _gla_step_aliased_kernel(q_ref, k_ref, g_ref, v_ref, s0_ref, prev_ref, o_ref, s_ref):
    del prev_ref
    _gla_step_kernel(q_ref, k_ref, g_ref, v_ref, s0_ref, o_ref, s_ref)


def _gla_step(q, k, gl, v, s0, layer, prev=None):
    b = q.shape[0]
    kq = pl.BlockSpec((None, 1, GLA_H * GLA_DK), lambda i: (i, 0, 0))
    vv = pl.BlockSpec((None, 1, GLA_H * GLA_DV), lambda i: (i, 0, 0))
    st = pl.BlockSpec((None, None, GLA_H, GLA_DK, GLA_DV), lambda i: (layer, i, 0, 0, 0))
    extra = [] if prev is None else [prev]
    return pl.pallas_call(
        _gla_step_kernel if prev is None else _gla_step_aliased_kernel,
        grid=(b,),
        in_specs=[kq, kq, kq, vv, st] + [pl.BlockSpec(memory_space=pl.ANY)] * len(extra),
        out_specs=[vv, st],
        out_shape=[jax.ShapeDtypeStruct((b, 1, GLA_H * GLA_DV), F32), jax.ShapeDtypeStruct(s0.shape, F32)],
        input_output_aliases={5: 1} if extra else {},
        compiler_params=_params("parallel"),
        name="gla_step",
    )(q, k, gl, v, s0, *extra)


def _gla_out_kernel(o_ref, r_ref, x_ref, go_ref, w_ref, y_ref):
    parts = []
    for h in range(GLA_H):
        sl = slice(h * GLA_DV, (h + 1) * GLA_DV)
        r = r_ref[:, sl]
        parts.append((_rms_rows(o_ref[:, sl], go_ref[...]) * (r * jax.nn.sigmoid(r))).astype(BF16))
    y_ref[...] = x_ref[...] + _dot(jnp.concatenate(parts, axis=1), w_ref[...])


def _gla_out(o, r, x, go, w, tm):
    m, d = x.shape
    row = pl.BlockSpec((tm, d), lambda i: (i, 0))
    full = lambda a: pl.BlockSpec(a.shape, lambda i: (0, 0))
    return pl.pallas_call(
        _gla_out_kernel,
        grid=(m // tm,),
        in_specs=[row, row, row, full(go), full(w)],
        out_specs=row,
        out_shape=jax.ShapeDtypeStruct((m, d), F32),
        compiler_params=_params("parallel"),
        name="gla_out",
    )(o, r, x, go, w)


def _gla_layer(xp, xs, s0, layer, prev_state, norm_g, w_in, w_gate2, b_gate, g_out, w_out):
    b, t, d = xp.shape
    bs = xs.shape[0]
    dk, dv = GLA_H * GLA_DK, GLA_H * GLA_DV
    rank = w_gate2.shape[0]
    w = jnp.pad(w_in, ((0, 0), (0, LANES - rank))).astype(BF16)
    wg2 = jnp.pad(w_gate2, ((0, LANES - rank), (0, 0))).astype(BF16)
    g = norm_g.reshape(1, d)
    bg = b_gate.reshape(1, dk)
    go = g_out.reshape(1, GLA_DV)
    wo = w_out.astype(BF16)
    x2 = xp.reshape(b * t, d)
    q, k, v, r, gl = _gla_proj(x2, g, w, wg2, bg, tm=512)
    o, sp = _gla_scan(q.reshape(b, t, dk), k.reshape(b, t, dk), gl.reshape(b, t, dk), v.reshape(b, t, dv))
    yp = _gla_out(o.reshape(b * t, dv), r, x2, go, wo, tm=512).reshape(b, t, d)
    xs2 = xs.reshape(bs, d)
    q, k, v, r, gl = _gla_proj(xs2, g, w, wg2, bg, tm=bs)
    o, ss = _gla_step(q.reshape(bs, 1, dk), k.reshape(bs, 1, dk), gl.reshape(bs, 1, dk), v.reshape(bs, 1, dv), s0,
                      layer, prev_state)
    ys = _gla_out(o.reshape(bs, dv), r, xs2, go, wo, tm=bs).reshape(bs, 1, d)
    return yp, ys, sp, ss


def _nsa_proj_kernel(x_ref, gn_ref, wt_ref, gq_ref, gk_ref, bg_ref, cos_ref, sin_ref, *rest):
    (q_ref, kc_ref, vc_ref, ks_ref, vs_ref, kw_ref, vw_ref, gt_ref,
     ksa_ref, vsa_ref, kwb_ref, vwa_ref) = rest[-12:]
    tm = x_ref.shape[0]
    nq = GRP * KVH * HD
    nkv = KVH * HD
    h = _rms_rows(x_ref[...], gn_ref[...]).astype(BF16)
    cos = cos_ref[...][None]
    sin = sin_ref[...][None]

    def norm_rope(z, g, nh):
        z3 = z.reshape(nh, HD, tm)
        y = z3 * lax.rsqrt(jnp.mean(z3 * z3, axis=1, keepdims=True) + EPS) * g[None]
        x1 = y[:, 0:ROT_HALF, :]
        x2 = y[:, ROT_HALF:2 * ROT_HALF, :]
        return jnp.concatenate([x1 * cos - x2 * sin, x1 * sin + x2 * cos, y[:, 2 * ROT_HALF:, :]], axis=1)

    zq = _dot_nt(wt_ref[0:nq, :], h)
    q_ref[...] = (norm_rope(zq, gq_ref[...], GRP * KVH) * (HD ** -0.5 * LOG2E)).astype(BF16)
    zkv = _dot_nt(wt_ref[nq:nq + 6 * nkv, :], h)
    outs = (kc_ref, vc_ref, ks_ref, vs_ref, kw_ref, vw_ref)
    rows = []
    for i in range(6):
        z = zkv[i * nkv:(i + 1) * nkv, :]
        rows.append(norm_rope(z, gk_ref[i // 2], KVH) if i % 2 == 0 else z.reshape(KVH, HD, tm))
        outs[i][...] = rows[i]
    zg = _dot_nt(wt_ref[nq + 6 * nkv:, :], h) + bg_ref[...]
    gt_ref[...] = jax.nn.sigmoid(zg).reshape(KVH, 16, tm)

    nsp = ksa_ref.shape[1] - HD
    blk = lax.broadcasted_iota(jnp.int32, (KVH, nsp, tm), 1)
    tok = pl.program_id(1) * tm + lax.broadcasted_iota(jnp.int32, (KVH, nsp, tm), 2)
    onehot = jnp.where(tok // L_SLC == blk, 1.0, 0.0)
    ksa_ref[...] = jnp.concatenate([rows[2], onehot], axis=1).astype(BF16)
    kwb_ref[...] = rows[4].astype(BF16)
    ones = jnp.where(lax.broadcasted_iota(jnp.int32, (KVH, V_PAD, tm), 1) == 0, 1.0, 0.0)
    vsa_ref[...] = jnp.concatenate([rows[3], ones], axis=1).astype(BF16)
    vwa_ref[...] = jnp.concatenate([rows[5], ones], axis=1).astype(BF16)


def _sel_rows(t):
    return -(-(-(-t // L_SLC)) // 16) * 16


def _nsa_proj(x, gn, wt, gq, gk, bg, cos, sin, tm, layer=0, n_layers=1, prev=None):
    b, t, d = x.shape
    full = lambda a: pl.BlockSpec(a.shape, lambda i, j: (0,) * a.ndim)
    rows_spec = lambda n: pl.BlockSpec((None, KVH, n, tm), lambda i, j: (i, 0, 0, j))
    rows_shape = lambda n, dt: jax.ShapeDtypeStruct((b, KVH, n, t), dt)
    kv_spec = pl.BlockSpec((None, None, KVH, HD, tm), lambda i, j: (layer, i, 0, 0, j))
    kv_shape = jax.ShapeDtypeStruct((n_layers, b, KVH, HD, t), F32)
    aug = [HD + _sel_rows(t), HD + V_PAD, HD, HD + V_PAD]
    prev = list(prev) if prev is not None else []
    n_in = 8
    return pl.pallas_call(
        _nsa_proj_kernel,
        grid=(b, t // tm),
        in_specs=[pl.BlockSpec((None, tm, d), lambda i, j: (i, j, 0)), full(gn), full(wt), full(gq), full(gk),
                  full(bg), pl.BlockSpec((ROT_HALF, tm), lambda i, j: (0, j)),
                  pl.BlockSpec((ROT_HALF, tm), lambda i, j: (0, j))]
        + [pl.BlockSpec(memory_space=pl.ANY)] * len(prev),
        out_specs=[pl.BlockSpec((None, GRP * KVH, HD, tm), lambda i, j: (i, 0, 0, j))] + [kv_spec] * 6
        + [rows_spec(16)] + [rows_spec(n) for n in aug],
        out_shape=[jax.ShapeDtypeStruct((b, GRP * KVH, HD, t), BF16)] + [kv_shape] * 6
        + [rows_shape(16, F32)] + [rows_shape(n, BF16) for n in aug],
        input_output_aliases={n_in + k: 1 + k for k in range(len(prev))},
        compiler_params=_params("parallel", "parallel"),
        name="nsa_proj",
    )(x, gn, wt, gq, gk, bg, cos, sin, *prev)


def _nsa_out_kernel(ot_ref, x_ref, w_ref, y_ref):
    ot = ot_ref[...].reshape(GRP * KVH * HD, ot_ref.shape[-1])
    y_ref[...] = x_ref[...] + _dot_tn(ot, w_ref[...])


def _nsa_out(ot, x, w, tm):
    b, t, d = x.shape
    return pl.pallas_call(
        _nsa_out_kernel,
        grid=(b, t // tm),
        in_specs=[pl.BlockSpec((None, GRP * KVH, HD, tm), lambda i, j: (i, 0, 0, j)),
                  pl.BlockSpec((None, tm, d), lambda i, j: (i, j, 0)),
                  pl.BlockSpec(w.shape, lambda i, j: (0, 0))],
        out_specs=pl.BlockSpec((None, tm, d), lambda i, j: (i, j, 0)),
        out_shape=jax.ShapeDtypeStruct((b, t, d), F32),
        compiler_params=_params("parallel", "parallel"),
        name="nsa_out",
    )(ot, x, w)


def _compress_kernel(*refs, n_pages, paged, is_key):
    refs = refs[1:] if paged else refs
    pages = refs[:n_pages]
    perm_ref, w_ref, b1_ref, pe_ref, w1_ref, w2t_ref, gk_ref, o_ref, lhs_scr = refs[n_pages:]
    p = pl.program_id(1)
    half = CMP_STRIDE
    n_half = PAGE // half
    low = lax.broadcasted_iota(jnp.int32, (n_half, LANES), 1) < HD
    odd_slot = (lax.broadcasted_iota(jnp.int32, (2 * HD, PAGE), 1) // n_half) % 2 == 1
    for pair in range(KVH // 2):
        for j in range(n_pages):
            y = _dot(pages[j][2 * pair:2 * pair + 2].reshape(2 * HD, PAGE).astype(BF16), perm_ref[...])
            x = jnp.where(odd_slot, jnp.concatenate([y[HD:], y[:HD]], axis=0), y).T
            rows = pl.ds(pl.multiple_of((p * n_pages + j) * n_half, n_half), n_half)
            for t in range(half // 2):
                ev = x[2 * t * n_half:(2 * t + 1) * n_half]
                od = x[(2 * t + 1) * n_half:(2 * t + 2) * n_half]
                lanes = slice(t * LANES, (t + 1) * LANES)
                lhs_scr[2 * pair, rows, lanes] = jnp.where(low, ev, od)
                lhs_scr[2 * pair + 1, rows, lanes] = jnp.where(low, od, ev)

    @pl.when(p == pl.num_programs(1) - 1)
    def _():
        nh = lhs_scr.shape[1]
        hidden = b1_ref.shape[1]
        c = _dot(pe_ref[0].astype(BF16), w1_ref[0]) + _dot(pe_ref[1].astype(BF16), w1_ref[1])
        bias = c[0:1, :] + b1_ref[...]
        for kvh in range(KVH):
            fs = _dot(lhs_scr[kvh].astype(BF16), w_ref[kvh % 2])
            hid = jax.nn.gelu(fs[:, :hidden] + pltpu.roll(fs[:, hidden:], nh - 1, 0) + bias)
            yt = _dot_nt(w2t_ref[...], hid.astype(BF16))
            if is_key:
                yt = yt * lax.rsqrt(jnp.mean(yt * yt, axis=0, keepdims=True) + EPS) * gk_ref[...]
            o_ref[kvh] = yt.astype(BF16)


def _compress(src, weights, n_pages_total, pages_per_step, is_key, page_table=None, layer=0):
    paged = page_table is not None
    b = page_table.shape[0] if paged else src.shape[1]
    n_half = PAGE // CMP_STRIDE
    nh = n_pages_total * n_half
    steps = n_pages_total // pages_per_step
    tok = jnp.arange(PAGE)
    perm = (((tok % CMP_STRIDE) * n_half + tok // CMP_STRIDE)[:, None] == tok[None, :]).astype(BF16)
    if paged:
        page_spec = lambda j: pl.BlockSpec(
            (None, None, KVH, HD, PAGE), lambda i, p, pt: (layer, pt[i, p * pages_per_step + j], 0, 0, 0))
        full = lambda a: pl.BlockSpec(a.shape, lambda i, p, pt: (0,) * a.ndim)
        o_spec = pl.BlockSpec((None, KVH, HD, nh), lambda i, p, pt: (i, 0, 0, 0))
    else:
        page_spec = lambda j: pl.BlockSpec((None, None, KVH, HD, PAGE),
                                           lambda i, p: (layer, i, 0, 0, p * pages_per_step + j))
        full = lambda a: pl.BlockSpec(a.shape, lambda i, p: (0,) * a.ndim)
        o_spec = pl.BlockSpec((None, KVH, HD, nh), lambda i, p: (i, 0, 0, 0))
    consts = [perm] + list(weights)
    grid_spec = pltpu.PrefetchScalarGridSpec(
        num_scalar_prefetch=1 if paged else 0,
        grid=(b, steps),
        in_specs=[page_spec(j) for j in range(pages_per_step)] + [full(a) for a in consts],
        out_specs=o_spec,
        scratch_shapes=[pltpu.VMEM((KVH, nh, CMP_STRIDE * HD), F32)],
    )
    args = ([page_table] if paged else []) + [src] * pages_per_step + consts
    return pl.pallas_call(
        functools.partial(_compress_kernel, n_pages=pages_per_step, paged=paged, is_key=is_key),
        grid_spec=grid_spec,
        out_shape=jax.ShapeDtypeStruct((b, KVH, HD, nh), BF16),
        compiler_params=_params("parallel", "arbitrary"),
        name="nsa_compress",
    )(*args)


def _cmp_weights(pe, w1, b1, w2, g_kcmp, nh):
    hidden = w1.shape[-1]
    wfs = jnp.concatenate([w1[:CMP_STRIDE], w1[CMP_STRIDE:]], axis=-1)
    swapped = wfs.reshape(CMP_STRIDE // 2, 2, HD, 2 * hidden)[:, ::-1]
    w_pair = jnp.stack([wfs, swapped.reshape(wfs.shape)]).reshape(2, CMP_STRIDE * HD, 2 * hidden).astype(BF16)
    pe2 = jnp.zeros((2, 8, CMP_STRIDE * HD), F32).at[:, 0, :].set(pe.reshape(2, CMP_STRIDE * HD))
    w1f = w1.reshape(2, CMP_STRIDE * HD, hidden).astype(BF16)
    gk = jnp.broadcast_to(g_kcmp.reshape(HD, 1), (HD, nh))
    return w_pair, b1.reshape(1, hidden), pe2, w1f, w2.T.astype(BF16), gk


def _softmax_update(s, mask, vt, m, l, acc):
    m_new = jnp.maximum(m, jnp.max(jnp.where(mask, s, NEG), axis=0, keepdims=True))
    alpha = jnp.exp2(m - m_new)
    p = jnp.where(mask, jnp.exp2(s - m_new), 0.0)
    l_new = l * alpha + jnp.sum(p, axis=0, keepdims=True)
    acc_new = acc * alpha + _dot(vt, p.astype(BF16))
    return m_new, l_new, acc_new, p


def _softmax_init(lanes):
    return jnp.full((1, lanes), NEG, F32), jnp.zeros((1, lanes), F32), jnp.zeros((HD, lanes), F32)


def _finish(l, acc):
    return acc / jnp.maximum(l, 1e-30)


def _split_dot(mt, x):
    hi = x.astype(BF16)
    lo = (x - hi.astype(F32)).astype(BF16)
    return _dot(mt, hi) + _dot(mt, lo)


def _attn_prompt_kernel(q_ref, kc_ref, vc_ref, ksa_ref, vsa_ref, kwb_ref, vwa_ref, gt_ref, mt_ref, o_ref,
                        s_scr, s2_scr, *, tq, n_cmp, n_sel):
    i = pl.program_id(2)
    lanes = GRP * tq
    ncp = kc_ref.shape[1]
    nsp = mt_ref.shape[0]
    qt = jnp.concatenate([q_ref[h] for h in range(GRP)], axis=1)

    def col_max(mx, s):
        return jnp.maximum(mx, jnp.max(s.reshape(s.shape[0] // 8, 8, lanes), axis=0))

    def finish(acc):
        return acc[0:HD] / jnp.maximum(acc[HD:HD + 1], 1e-30)

    tpos1 = i * tq + lax.broadcasted_iota(jnp.int32, (1, tq), 1)
    tpos = jnp.concatenate([tpos1] * GRP, axis=1)

    s = _dot_tn(kc_ref[...], qt)
    n_idx = lax.broadcasted_iota(jnp.int32, (ncp, lanes), 0)
    mask = (n_idx * CMP_STRIDE + (L_CMP - 1) <= tpos) & (n_idx < n_cmp)
    m, l, acc, p = _softmax_update(s, mask, vc_ref[...], *_softmax_init(lanes))
    o_c = _finish(l, acc)
    p = p / jnp.maximum(l, 1e-30)
    imp = p[:, 0:tq]
    for h in range(1, GRP):
        imp = imp + p[:, h * tq:(h + 1) * tq]
    p_slc = _split_dot(mt_ref[...], imp)

    j_idx = lax.broadcasted_iota(jnp.int32, (nsp, tq), 0)
    cur = tpos1 // L_SLC
    valid = j_idx <= cur
    forced = (j_idx == 0) | (j_idx == cur) | (j_idx == cur - 1)
    score = jnp.where(valid & forced, jnp.inf, jnp.where(valid, p_slc, -jnp.inf))
    bits = lax.bitcast_convert_type(score, jnp.int32)
    key = jnp.where(bits >= 0, bits, bits ^ 0x7FFFFFFF)
    n_grp = nsp // 8
    keys = [key[8 * r:8 * r + 8] for r in range(n_grp)]
    keys_m1 = [k - 1 for k in keys]
    sub = lax.broadcasted_iota(jnp.int32, (8, tq), 0)

    def count_group(grp, ranks):
        ranks = list(ranks)
        for u in range(8):
            row = jnp.broadcast_to(keys[grp][u:u + 1, :], (8, tq))
            for r in range(n_grp):
                thr = keys[r] if r < grp else keys_m1[r] if r > grp else jnp.where(sub > u, keys_m1[r], keys[r])
                ranks[r] = ranks[r] + jnp.where(row > thr, 1.0, 0.0)
        return tuple(ranks)

    ranks = tuple(jnp.zeros((8, tq), F32) for _ in range(n_grp))
    last_valid = (i * tq + tq - 1) // L_SLC
    for grp in range(n_grp):
        ranks = lax.cond(8 * grp <= last_valid, functools.partial(count_group, grp), lambda r: r, ranks)
    bias = jnp.where(jnp.concatenate(ranks, axis=0) < n_sel, 0.0, NEG)
    qa = jnp.concatenate([qt, jnp.concatenate([bias] * GRP, axis=1).astype(BF16)], axis=0)

    row_k = lax.broadcasted_iota(jnp.int32, (KT, lanes), 0)

    def slc_scores(st, buf):
        start = pl.multiple_of(st * KT, KT)
        buf[0:KT, :] = _dot_tn(ksa_ref[:, pl.ds(start, KT)], qa)

    def slc_softmax(st, buf, carry, diagonal=False):
        m, acc = carry
        s = buf[0:KT, :]
        if diagonal:
            s = jnp.where(st * KT + row_k <= tpos, s, NEG)
        m_new = jnp.maximum(m, jnp.max(col_max(jnp.full((8, lanes), NEG, F32), s), axis=0, keepdims=True))
        p = jnp.exp2(s - m_new).astype(BF16)
        pv = _dot(vsa_ref[:, pl.ds(pl.multiple_of(st * KT, KT), KT)], p)
        return m_new, acc * jnp.exp2(m - m_new) + pv

    def slc_pair(u, carry):
        slc_scores(2 * u + 1, s2_scr)
        carry = slc_softmax(2 * u, s_scr, carry)
        slc_scores(2 * u + 2, s_scr)
        return slc_softmax(2 * u + 1, s2_scr, carry)

    def slc_tail_odd(carry):
        slc_scores(n_full, s2_scr)
        carry = slc_softmax(n_full - 1, s_scr, carry)
        return slc_softmax(n_full, s2_scr, carry, diagonal=True)

    def slc_tail_even(carry):
        return slc_softmax(n_full, s_scr, carry, diagonal=True)

    n_full = (i * tq) // KT
    slc_scores(0, s_scr)
    init = (jnp.full((1, lanes), NEG, F32), jnp.zeros((HD + V_PAD, lanes), F32))
    carry = lax.fori_loop(0, n_full // 2, slc_pair, init)
    o_s = finish(lax.cond(n_full % 2 == 1, slc_tail_odd, slc_tail_even, carry)[1])

    n_wt = WINDOW // tq + 1
    w0 = jnp.maximum(i - (n_wt - 1), 0) * tq
    row_w = lax.broadcasted_iota(jnp.int32, (tq, lanes), 0)
    col_w = jnp.concatenate([lax.broadcasted_iota(jnp.int32, (tq, tq), 1)] * GRP, axis=1)

    def window(steady):
        mx = jnp.full((8, lanes), NEG, F32)
        for r in range(n_wt):
            start = pl.multiple_of(w0 + r * tq, tq)
            s = _dot_tn(kwb_ref[:, pl.ds(start, tq)], qt)
            if not steady:
                rel = tpos - (start + row_w)
                s = jnp.where((rel >= 0) & (rel <= WINDOW), s, NEG)
            elif r == 0:
                s = jnp.where(row_w >= col_w, s, NEG)
            elif r == n_wt - 1:
                s = jnp.where(row_w <= col_w, s, NEG)
            s_scr[r * tq:(r + 1) * tq, :] = s
            mx = col_max(mx, s)
        m = jnp.max(mx, axis=0, keepdims=True)
        acc = jnp.zeros((HD + V_PAD, lanes), F32)
        for r in range(n_wt):
            start = pl.multiple_of(w0 + r * tq, tq)
            p = jnp.exp2(s_scr[r * tq:(r + 1) * tq, :] - m).astype(BF16)
            acc = acc + _dot(vwa_ref[:, pl.ds(start, tq)], p)
        return finish(acc)

    o_w = lax.cond(i >= n_wt - 1, functools.partial(window, True), functools.partial(window, False))

    for h in range(GRP):
        ls = slice(h * tq, (h + 1) * tq)
        o = (gt_ref[3 * h:3 * h + 1, :] * o_c[:, ls] + gt_ref[3 * h + 1:3 * h + 2, :] * o_s[:, ls]
             + gt_ref[3 * h + 2:3 * h + 3, :] * o_w[:, ls])
        o_ref[h] = o.astype(BF16)


def _slc_matrix(nsp, ncp, n_cmp):
    ratio = L_SLC // CMP_STRIDE
    j = jnp.arange(nsp)[:, None]
    n = jnp.arange(ncp)[None, :]
    m = ((n >= ratio * j) & (n <= ratio * j + ratio - 1)).astype(F32)
    m = m + ((n >= ratio * j - 1) & (n <= ratio * j + ratio - 2)).astype(F32)
    return jnp.where(n < n_cmp, m, 0.0).astype(BF16)


def _attn_prompt(qt, kct, vct, ksa, vsa, kwb, vwa, gt, tq=4 * LANES):
    b, _, _, t = qt.shape
    assert t % KT == 0 and WINDOW % tq == 0 and t >= WINDOW + tq
    ncp = kct.shape[-1]
    n_cmp = t // CMP_STRIDE - 1
    ns = -(-t // L_SLC)
    mt = _slc_matrix(_sel_rows(t), ncp, n_cmp)
    res = lambda a: pl.BlockSpec((None, None) + a.shape[2:], lambda i, g, j: (i, g, 0, 0))
    return pl.pallas_call(
        functools.partial(_attn_prompt_kernel, tq=tq, n_cmp=n_cmp, n_sel=min(N_SEL, ns)),
        grid=(b, KVH, t // tq),
        in_specs=[pl.BlockSpec((None, GRP, HD, tq), lambda i, g, j: (i, g, 0, j)),
                  res(kct), res(vct), res(ksa), res(vsa), res(kwb), res(vwa),
                  pl.BlockSpec((None, None, 16, tq), lambda i, g, j: (i, g, 0, j)),
                  pl.BlockSpec(mt.shape, lambda i, g, j: (0, 0))],
        out_specs=pl.BlockSpec((None, GRP, HD, tq), lambda i, g, j: (i, g, 0, j)),
        out_shape=jax.ShapeDtypeStruct(qt.shape, BF16),
        scratch_shapes=[pltpu.VMEM((max(KT, WINDOW + tq), GRP * tq), F32), pltpu.VMEM((KT, GRP * tq), F32)],
        compiler_params=_params("parallel", "parallel", "arbitrary"),
        name="nsa_attn_prompt",
    )(qt, kct, vct, ksa, vsa, kwb, vwa, gt, mt)


def _split_dot_rows(x, mt):
    hi = x.astype(BF16)
    lo = (x - hi.astype(F32)).astype(BF16)
    return _dot(hi, mt) + _dot(lo, mt)


def _attn_sample_rows_kernel(*refs, n_pages, n_cmp, n_blocks, n_sel):
    refs = refs[1:]
    k_pages = refs[:n_pages]
    v_pages = refs[n_pages:2 * n_pages]
    (q_ref, kc_ref, vc_ref, kw_ref, vw_ref, kns_ref, vns_ref, knw_ref, vnw_ref, gt_ref, mtt_ref, exp_ref,
     o_ref, bias_scr, oc_scr, ow_scr, m_scr, l_scr, acc_scr) = refs[2 * n_pages:]
    p = pl.program_id(1)
    ncp = kc_ref.shape[-1]
    nsp = mtt_ref.shape[1]
    n_q = KVH * GRP
    rows_all = KVH * HD
    q = q_ref[...]

    def stacked(ref):
        return ref[...].reshape(rows_all, ref.shape[-1]).astype(BF16)

    def update(s, valid, vt, m, l, acc):
        if valid is not None:
            s = jnp.where(valid, s, NEG)
        m_new = jnp.maximum(m, jnp.max(s, axis=1, keepdims=True))
        alpha = jnp.exp2(m - m_new)
        pr = jnp.exp2(s - m_new)
        if valid is not None:
            pr = jnp.where(valid, pr, 0.0)
        l_new = l * alpha + jnp.sum(pr, axis=1, keepdims=True)
        return m_new, l_new, acc * alpha + _dot_nt(pr.astype(BF16), vt), pr

    def init():
        return jnp.full((n_q, 1), NEG, F32), jnp.zeros((n_q, 1), F32), jnp.zeros((n_q, rows_all), F32)

    def save(m, l, acc):
        m_scr[...] = jnp.broadcast_to(m, m_scr.shape)
        l_scr[...] = jnp.broadcast_to(l, l_scr.shape)
        acc_scr[...] = acc

    @pl.when(p == 0)
    def _():
        cur = n_blocks - 1
        row_q = lax.broadcasted_iota(jnp.int32, (n_q, ncp), 0)
        row_b = lax.broadcasted_iota(jnp.int32, (n_q, nsp), 0)
        j_row = lax.broadcasted_iota(jnp.int32, (1, nsp), 1)
        jp_idx = lax.broadcasted_iota(jnp.int32, (nsp, nsp), 0)
        j_idx = lax.broadcasted_iota(jnp.int32, (nsp, nsp), 1)
        s = _dot(q, stacked(kc_ref))
        valid = lax.broadcasted_iota(jnp.int32, (n_q, ncp), 1) < n_cmp
        m, l, acc, pr = update(s, valid, stacked(vc_ref), *init())
        oc_scr[...] = acc / jnp.maximum(l, 1e-30)
        pr = pr / jnp.maximum(l, 1e-30)
        bias_blocks = jnp.zeros((n_q, nsp), F32)
        for g in range(KVH):
            imp = jnp.sum(jnp.where(row_q // GRP == g, pr, 0.0), axis=0, keepdims=True)
            p_slc = _split_dot_rows(jnp.broadcast_to(imp, (8, ncp)), mtt_ref[...])[0:1]
            ok = j_row <= cur
            forced = (j_row == 0) | (j_row == cur) | (j_row == cur - 1)
            score = jnp.where(ok & forced, jnp.inf, jnp.where(ok, p_slc, -jnp.inf))
            row = jnp.broadcast_to(score, (nsp, nsp))
            col = row.T
            ahead = (col > row) | ((col == row) & (jp_idx < j_idx))
            rank = jnp.sum(jnp.where(ahead, 1.0, 0.0), axis=0, keepdims=True)
            bias = jnp.broadcast_to(jnp.where(rank < n_sel, 0.0, NEG), (n_q, nsp))
            bias_blocks = jnp.where(row_b // GRP == g, bias, bias_blocks)
        bias_scr[...] = _dot(bias_blocks.astype(BF16), exp_ref[...])
        n_win = kw_ref.shape[-1]
        kt = jnp.concatenate([stacked(kw_ref), stacked(knw_ref)], axis=1)
        vt = jnp.concatenate([stacked(vw_ref), stacked(vnw_ref)], axis=1)
        valid = lax.broadcasted_iota(jnp.int32, (n_q, n_win + LANES), 1) <= n_win
        _, l, acc, _ = update(_dot(q, kt), valid, vt, *init())
        ow_scr[...] = acc / jnp.maximum(l, 1e-30)
        save(*init())

    n_keys = n_pages * PAGE
    kt = jnp.concatenate([k_pages[j][...].reshape(rows_all, PAGE) for j in range(n_pages)], axis=1).astype(BF16)
    vt = jnp.concatenate([v_pages[j][...].reshape(rows_all, PAGE) for j in range(n_pages)], axis=1).astype(BF16)
    s = _dot(q, kt) + bias_scr[:, pl.ds(pl.multiple_of(p * n_keys, n_keys), n_keys)]
    save(*update(s, None, vt, m_scr[:, 0:1], l_scr[:, 0:1], acc_scr[...])[:3])

    @pl.when(p == pl.num_programs(1) - 1)
    def _():
        valid = lax.broadcasted_iota(jnp.int32, (n_q, LANES), 1) == 0
        _, l, acc, _ = update(_dot(q, stacked(kns_ref)), valid, stacked(vns_ref),
                              m_scr[:, 0:1], l_scr[:, 0:1], acc_scr[...])
        o_ref[...] = (gt_ref[:, 0:1] * oc_scr[...] + gt_ref[:, 1:2] * (acc / jnp.maximum(l, 1e-30))
                      + gt_ref[:, 2:3] * ow_scr[...])


def _attn_sample(qs, kct, vct, pool_k, pool_v, page_table, layer, kwt, vwt, kns, vns, knw, vnw, gs,
                 pages_per_step=16):
    b, n_pages_total = page_table.shape
    pages_per_step = min(pages_per_step, n_pages_total)
    past = n_pages_total * PAGE
    n_cmp = past // CMP_STRIDE - 1
    ncp = kct.shape[-1]
    n_blocks = past // L_SLC + 1
    nsp = -(-n_blocks // LANES) * LANES
    mtt = _slc_matrix(nsp, ncp, n_cmp).T
    expand = (jnp.arange(nsp)[:, None] == jnp.arange(past)[None, :] // L_SLC).astype(BF16)
    n_q, rows_all = KVH * GRP, KVH * HD
    steps = n_pages_total // pages_per_step
    page_spec = lambda j: pl.BlockSpec((None, None, KVH, HD, PAGE),
                                       lambda i, p, pt: (layer, pt[i, p * pages_per_step + j], 0, 0, 0))
    def per_b(a):
        if a.ndim == 5:
            return pl.BlockSpec((None, None) + a.shape[2:], lambda i, p, pt: (layer, i, 0, 0, 0))
        return pl.BlockSpec((None,) + a.shape[1:], lambda i, p, pt: (i,) + (0,) * (a.ndim - 1))

    small = [qs, kct, vct, kwt, vwt, kns, vns, knw, vnw, gs]
    grid_spec = pltpu.PrefetchScalarGridSpec(
        num_scalar_prefetch=1,
        grid=(b, steps),
        in_specs=[page_spec(j) for j in range(pages_per_step)] * 2 + [per_b(a) for a in small]
        + [pl.BlockSpec(a.shape, lambda i, p, pt: (0, 0)) for a in (mtt, expand)],
        out_specs=pl.BlockSpec((None, n_q, rows_all), lambda i, p, pt: (i, 0, 0)),
        scratch_shapes=[pltpu.VMEM((n_q, past), F32), pltpu.VMEM((n_q, rows_all), F32),
                        pltpu.VMEM((n_q, rows_all), F32), pltpu.VMEM((n_q, LANES), F32),
                        pltpu.VMEM((n_q, LANES), F32), pltpu.VMEM((n_q, rows_all), F32)],
    )
    return pl.pallas_call(
        functools.partial(_attn_sample_rows_kernel, n_pages=pages_per_step, n_cmp=n_cmp, n_blocks=n_blocks,
                          n_sel=min(N_SEL, n_blocks)),
        grid_spec=grid_spec,
        out_shape=jax.ShapeDtypeStruct((b, n_q, rows_all), F32),
        compiler_params=_params("parallel", "arbitrary"),
        name="nsa_attn_sample",
    )(page_table, *([pool_k] * pages_per_step), *([pool_v] * pages_per_step), *small, mtt, expand)


def _rope_tables(pos):
    inv_freq = ROPE_THETA ** (-jnp.arange(ROT_HALF, dtype=F32) * 2.0 / (2 * ROT_HALF))
    ang = pos.astype(F32)[:, None] * inv_freq[None, :]
    return jnp.cos(ang).T, jnp.sin(ang).T


def _nsa_layer(xp, xs, pools, win_bufs, page_table, layer, n_layers, prev_rows, norm_g, w_in, b_gate, g_q, g_k,
               g_kcmp, cmp_pe, cmp_w1, cmp_b1, cmp_w2, w_out):
    b, t, d = xp.shape
    bs = xs.shape[0]
    past = page_table.shape[1] * PAGE
    nq, nkv = GRP * KVH * HD, KVH * HD
    n_gate = 3 * GRP * KVH

    wt_g = jnp.pad(w_in[:, nq + 6 * nkv:].T.reshape(KVH, n_gate // KVH, d), ((0, 0), (0, 16 - n_gate // KVH), (0, 0)))
    wt = jnp.concatenate([w_in[:, :nq + 6 * nkv].T, wt_g.reshape(KVH * 16, d)], axis=0).astype(BF16)
    bg = jnp.pad(b_gate.reshape(KVH, n_gate // KVH), ((0, 0), (0, 16 - n_gate // KVH))).reshape(KVH * 16, 1)
    gn = norm_g.reshape(1, d)
    wo = w_out.astype(BF16)

    def project(x, pos, tm, **stacking):
        cos, sin = _rope_tables(pos)
        col = lambda v: jnp.broadcast_to(v[..., None], v.shape + (tm,))
        return _nsa_proj(x, gn, wt, col(g_q), col(g_k), col(bg[:, 0]), cos, sin, tm, **stacking)

    tm = min(512, t)
    outs = project(xp, jnp.arange(t), tm, layer=layer, n_layers=n_layers, prev=prev_rows)
    qt, rows_p, (gt, ksa, vsa, kwb, vwa) = outs[0], list(outs[1:7]), outs[7:]
    pages = t // PAGE
    nh = pages * (PAGE // CMP_STRIDE)
    cmp_w = [_cmp_weights(cmp_pe[i], cmp_w1[i], cmp_b1[i], cmp_w2[i], g_kcmp, nh) for i in range(2)]
    pps = min(16, pages)
    kct = _compress(rows_p[0], cmp_w[0], pages, pps, True, layer=layer)
    vct = _compress(rows_p[1], cmp_w[1], pages, pps, False, layer=layer)
    ot = _attn_prompt(qt, kct, vct, ksa, vsa, kwb, vwa, gt)
    yp = _nsa_out(ot, xp, wo, tm)

    xs_pad = jnp.pad(xs.reshape(1, bs, d), ((0, 0), (0, LANES - bs), (0, 0)))
    outs = project(xs_pad, jnp.full((LANES,), past), LANES)
    qt_s, rows_s, gt_s = outs[0], outs[1:7], outs[7]
    pages_s = page_table.shape[1]
    nh_s = pages_s * (PAGE // CMP_STRIDE)
    cmp_ws = [_cmp_weights(cmp_pe[i], cmp_w1[i], cmp_b1[i], cmp_w2[i], g_kcmp, nh_s) for i in range(2)]
    pool5 = [jnp.transpose(pl_, (0, 1, 3, 4, 2)) for pl_ in pools]
    pps = min(32, pages_s)
    kct_s = _compress(pool5[0], cmp_ws[0], pages_s, pps, True, page_table, layer)
    vct_s = _compress(pool5[1], cmp_ws[1], pages_s, pps, False, page_table, layer)
    qs = jnp.transpose(qt_s[0, :, :, :bs].reshape(KVH, GRP, HD, bs), (3, 0, 2, 1))
    qs = jnp.einsum("bgdh,gk->bghkd", qs, jnp.eye(KVH, dtype=qs.dtype))
    qs = qs.reshape(bs, KVH * GRP, KVH * HD)
    new = lambda a: jnp.pad(jnp.transpose(a[0, 0, :, :, :bs], (2, 0, 1))[..., None],
                            ((0, 0), (0, 0), (0, 0), (0, LANES - 1)))
    gs = jnp.transpose(gt_s[0, :, :n_gate // KVH, :bs].reshape(KVH, GRP, 3, bs), (3, 0, 1, 2))
    gs = jnp.pad(gs.reshape(bs, KVH * GRP, 3), ((0, 0), (0, 0), (0, LANES - 3)))
    win5 = [jnp.transpose(wb, (0, 1, 3, 4, 2)) for wb in win_bufs]
    ot_s = _attn_sample(qs, kct_s, vct_s, pool5[2], pool5[3], page_table, layer, win5[0], win5[1],
                        new(rows_s[2]), new(rows_s[3]), new(rows_s[4]), new(rows_s[5]), gs)
    ot_s = jnp.stack([ot_s[:, g * GRP:(g + 1) * GRP, g * HD:(g + 1) * HD] for g in range(KVH)])
    ot_s = jnp.transpose(ot_s, (0, 2, 3, 1)).reshape(1, GRP * KVH, HD, bs)
    ot_s = jnp.pad(ot_s, ((0, 0), (0, 0), (0, 0), (0, LANES - bs))).astype(BF16)
    ys = _nsa_out(ot_s, xs_pad, wo, LANES)[0, :bs].reshape(bs, 1, d)

    rows_s = [jnp.transpose(a[0, 0, :, :, :bs], (2, 0, 1)).reshape(bs, 1, KVH, HD) for a in rows_s]
    return yp, ys, rows_p, rows_s


def kernel(x_prompt, x_sample, state_gla, cache_k_cmp, cache_v_cmp, cache_k_slc, cache_v_slc, cache_k_win, cache_v_win, page_table, norm_mix, norm_mlp, mlp_up, mlp_down, gla_w_in, gla_w_gate2, gla_b_gate, gla_g_out, gla_w_out, nsa_w_in, nsa_b_gate, nsa_g_q, nsa_g_k, nsa_g_kcmp, nsa_cmp_pe, nsa_cmp_w1, nsa_cmp_b1, nsa_cmp_w2, nsa_w_out):
    depth = norm_mix.shape[0]
    b, t, d = x_prompt.shape
    bs = x_sample.shape[0]
    xp, xs = x_prompt, x_sample
    gla_p, nsa_s = [], []
    gla_s = None
    nsa_p = None
    pools = (cache_k_cmp, cache_v_cmp, cache_k_slc, cache_v_slc)
    wu, wd = mlp_up.astype(BF16), mlp_down.astype(BF16)
    for i in range(depth):
        j = i // 2
        if i % 2 == 0:
            xp, xs, sp, gla_s = _gla_layer(xp, xs, state_gla, j, gla_s, norm_mix[i], gla_w_in[j], gla_w_gate2[j],
                                           gla_b_gate[j], gla_g_out[j], gla_w_out[j])
            gla_p.append(sp)
        else:
            xp, xs, nsa_p, rs = _nsa_layer(xp, xs, pools, (cache_k_win, cache_v_win), page_table, j, depth // 2,
                                           nsa_p, norm_mix[i], nsa_w_in[j], nsa_b_gate[j], nsa_g_q[j], nsa_g_k[j],
                                           nsa_g_kcmp[j], nsa_cmp_pe[j], nsa_cmp_w1[j], nsa_cmp_b1[j],
                                           nsa_cmp_w2[j], nsa_w_out[j])
            nsa_s.append(rs)
        g = norm_mlp[i].reshape(1, d)
        xp = _mlp(xp.reshape(b * t, d), g, wu, wd, i, tm=min(1024, b * t)).reshape(b, t, d)
        xs = _mlp(xs.reshape(bs, d), g, wu, wd, i, tm=bs).reshape(bs, 1, d)
    stack = lambda lst, r: jnp.stack([e[r] for e in lst])
    n_win = min(WINDOW, t)
    rows = lambda r, n: jnp.transpose(nsa_p[r][..., t - n:], (0, 1, 4, 2, 3))
    return (xp, xs, jnp.stack(gla_p), gla_s,
            rows(0, t), stack(nsa_s, 0), rows(1, t), stack(nsa_s, 1),
            rows(2, t), stack(nsa_s, 2), rows(3, t), stack(nsa_s, 3),
            rows(4, n_win), stack(nsa_s, 4), rows(5, n_win), stack(nsa_s, 5))
```

```python
import functools

import jax
import jax.numpy as jnp
from jax import lax
from jax.experimental import pallas as pl
from jax.experimental.pallas import tpu as pltpu

F32 = jnp.float32
BF16 = jnp.bfloat16
EPS = 1e-6
NEG = -1e30
VMEM_LIMIT_BYTES = 48 * 1024 * 1024
LANES = 128
PAGE = 128
HD = 64
KVH = 4
GRP = 4
L_CMP, CMP_STRIDE, L_SLC, N_SEL, WINDOW = 32, 16, 64, 16, 512
GLA_H, GLA_DK, GLA_DV, GLA_CHUNK = 4, 128, 256, 64
ROPE_THETA, ROT_HALF = 500000.0, 8
LOG2E = 1.4426950408889634
V_PAD = 16
KT = 512


def _params(*sem):
    return pltpu.CompilerParams(dimension_semantics=sem, vmem_limit_bytes=VMEM_LIMIT_BYTES)


def _dot(a, b):
    return jnp.dot(a, b, preferred_element_type=F32)


def _dot_nt(a, b):
    return lax.dot_general(a, b, (((1,), (1,)), ((), ())), preferred_element_type=F32)


def _dot_tn(a, b):
    return lax.dot_general(a, b, (((0,), (0,)), ((), ())), preferred_element_type=F32)


def _rms_rows(x, g):
    ms = jnp.mean(x * x, axis=-1, keepdims=True)
    return x * lax.rsqrt(ms + EPS) * g


def _mlp_kernel(x_ref, g_ref, wu_ref, wd_ref, o_ref, h_scr, acc_scr):
    f = pl.program_id(1)

    @pl.when(f == 0)
    def _():
        h_scr[...] = _rms_rows(x_ref[...], g_ref[...]).astype(BF16)
        acc_scr[...] = jnp.zeros_like(acc_scr)

    u = jnp.maximum(_dot(h_scr[...], wu_ref[...]), 0.0)
    acc_scr[...] += _dot((u * u).astype(BF16), wd_ref[...])

    @pl.when(f == pl.num_programs(1) - 1)
    def _():
        o_ref[...] = x_ref[...] + acc_scr[...]


def _mlp(x, g, wu, wd, layer, tm, tf=2048):
    m, d = x.shape
    ff = wu.shape[2]
    return pl.pallas_call(
        _mlp_kernel,
        grid=(m // tm, ff // tf),
        in_specs=[pl.BlockSpec((tm, d), lambda i, f: (i, 0)),
                  pl.BlockSpec((1, d), lambda i, f: (0, 0)),
                  pl.BlockSpec((None, d, tf), lambda i, f: (layer, 0, f)),
                  pl.BlockSpec((None, tf, d), lambda i, f: (layer, f, 0))],
        out_specs=pl.BlockSpec((tm, d), lambda i, f: (i, 0)),
        out_shape=jax.ShapeDtypeStruct((m, d), F32),
        scratch_shapes=[pltpu.VMEM((tm, d), BF16), pltpu.VMEM((tm, d), F32)],
        compiler_params=_params("parallel", "arbitrary"),
        name="mlp_block",
    )(x, g, wu, wd)


def _gla_proj_kernel(x_ref, g_ref, w_ref, wg2_ref, bg_ref, q_ref, k_ref, v_ref, r_ref, gl_ref):
    h = _rms_rows(x_ref[...], g_ref[...]).astype(BF16)
    dk, dv = GLA_H * GLA_DK, GLA_H * GLA_DV
    q_ref[...] = _dot(h, w_ref[:, 0:dk]) * (GLA_DK ** -0.5)
    k_ref[...] = _dot(h, w_ref[:, dk:2 * dk])
    v_ref[...] = _dot(h, w_ref[:, 2 * dk:2 * dk + dv])
    r_ref[...] = _dot(h, w_ref[:, 2 * dk + dv:2 * dk + 2 * dv])
    gr = _dot(h, w_ref[:, 2 * dk + 2 * dv:])
    xg = _dot(gr.astype(BF16), wg2_ref[...]) + bg_ref[...]
    gl_ref[...] = jax.nn.log_sigmoid(xg) * (1.0 / 16.0)


def _gla_proj(x, g, w, wg2, bg, tm):
    m, d = x.shape
    dk, dv = GLA_H * GLA_DK, GLA_H * GLA_DV
    row = lambda n: pl.BlockSpec((tm, n), lambda i: (i, 0))
    full = lambda a: pl.BlockSpec(a.shape, lambda i: (0, 0))
    return pl.pallas_call(
        _gla_proj_kernel,
        grid=(m // tm,),
        in_specs=[row(d), full(g), full(w), full(wg2), full(bg)],
        out_specs=[row(dk), row(dk), row(dv), row(dv), row(dk)],
        out_shape=[jax.ShapeDtypeStruct((m, n), F32) for n in (dk, dk, dv, dv, dk)],
        compiler_params=_params("parallel"),
        name="gla_proj",
    )(x, g, w, wg2, bg)


def _gla_scan_kernel(q_ref, k_ref, g_ref, v_ref, o_ref, s_ref, st_scr, *, n_chunks, n_heads):
    c_len = GLA_CHUNK
    step = pl.program_id(2)

    @pl.when(step == 0)
    def _():
        st_scr[...] = jnp.zeros_like(st_scr)

    rowi = lax.broadcasted_iota(jnp.int32, (c_len, GLA_DK), 0)
    causal = (lax.broadcasted_iota(jnp.int32, (c_len, c_len), 0)
              >= lax.broadcasted_iota(jnp.int32, (c_len, c_len), 1))

    def body(c, carry):
        sl = pl.ds(pl.multiple_of(c * c_len, c_len), c_len)
        for h in range(n_heads):
            ks = slice(h * GLA_DK, (h + 1) * GLA_DK)
            vs = slice(h * GLA_DV, (h + 1) * GLA_DV)
            b = g_ref[sl, ks]
            sh = 1
            while sh < c_len:
                b = b + jnp.where(rowi >= sh, pltpu.roll(b, sh, 0), 0.0)
                sh *= 2
            b_last = b[c_len - 1:c_len, :]
            b_mid = b[c_len // 2 - 1:c_len // 2, :]
            q = q_ref[sl, ks]
            k = k_ref[sl, ks]
            v = v_ref[sl, vs].astype(BF16)
            qe = (q * jnp.exp(b)).astype(BF16)
            qa = (q * jnp.exp(b - b_mid)).astype(BF16)
            ka = (k * jnp.exp(b_mid - b)).astype(BF16)
            kd = (k * jnp.exp(b_last - b)).astype(BF16)
            a = jnp.where(causal, _dot_nt(qa, ka), 0.0)
            st = st_scr[h]
            o_ref[sl, vs] = _dot_nt(qe, st.astype(BF16)) + _dot(a.astype(BF16), v)
            st_scr[h] = st * jnp.exp(b_last) + _dot_tn(v, kd)
        return carry

    lax.fori_loop(0, n_chunks, body, 0)

    @pl.when(step == pl.num_programs(2) - 1)
    def _():
        for h in range(n_heads):
            s_ref[h] = st_scr[h].T


def _gla_scan(q, k, gl, v, n_heads=4, t_blk=1024):
    b, t, _ = q.shape
    t_blk = min(t_blk, t)
    kq = pl.BlockSpec((None, t_blk, n_heads * GLA_DK), lambda i, h, j: (i, j, h))
    vv = pl.BlockSpec((None, t_blk, n_heads * GLA_DV), lambda i, h, j: (i, j, h))
    return pl.pallas_call(
        functools.partial(_gla_scan_kernel, n_chunks=t_blk // GLA_CHUNK, n_heads=n_heads),
        grid=(b, GLA_H // n_heads, t // t_blk),
        in_specs=[kq, kq, kq, vv],
        out_specs=[vv, pl.BlockSpec((None, n_heads, GLA_DK, GLA_DV), lambda i, h, j: (i, h, 0, 0))],
        out_shape=[jax.ShapeDtypeStruct((b, t, GLA_H * GLA_DV), F32),
                   jax.ShapeDtypeStruct((b, GLA_H, GLA_DK, GLA_DV), F32)],
        scratch_shapes=[pltpu.VMEM((n_heads, GLA_DV, GLA_DK), F32)],
        compiler_params=_params("parallel", "parallel", "arbitrary"),
        name="gla_scan",
    )(q, k, gl, v)


def _gla_step_kernel(q_ref, k_ref, g_ref, v_ref, s0_ref, o_ref, s_ref):
    def col(x):
        return jnp.broadcast_to(x, (LANES, LANES)).T

    for h in range(GLA_H):
        ks = slice(h * GLA_DK, (h + 1) * GLA_DK)
        qc, kc, ec = col(q_ref[:, ks]), col(k_ref[:, ks]), col(jnp.exp(g_ref[:, ks]))
        for half in range(GLA_DV // LANES):
            vs = slice(h * GLA_DV + half * LANES, h * GLA_DV + (half + 1) * LANES)
            ss = slice(half * LANES, (half + 1) * LANES)
            sn = ec * s0_ref[h, :, ss] + kc * v_ref[:, vs]
            s_ref[h, :, ss] = sn
            o_ref[:, vs] = jnp.sum(qc * sn, axis=0, keepdims=True)


def _gla_step_aliased_kernel(q_ref, k_ref, g_ref, v_ref, s0_ref, prev_ref, o_ref, s_ref):
    del prev_ref
    _gla_step_kernel(q_ref, k_ref, g_ref, v_ref, s0_ref, o_ref, s_ref)


def _gla_step(q, k, gl, v, s0, layer, prev=None):
    b = q.shape[0]
    kq = pl.BlockSpec((None, 1, GLA_H * GLA_DK), lambda i: (i, 0, 0))
    vv = pl.BlockSpec((None, 1, GLA_H * GLA_DV), lambda i: (i, 0, 0))
    st = pl.BlockSpec((None, None, GLA_H, GLA_DK, GLA_DV), lambda i: (layer, i, 0, 0, 0))
    extra = [] if prev is None else [prev]
    return pl.pallas_call(
        _gla_step_kernel if prev is None else _gla_step_aliased_kernel,
        grid=(b,),
        in_specs=[kq, kq, kq, vv, st] + [pl.BlockSpec(memory_space=pl.ANY)] * len(extra),
        out_specs=[vv, st],
        out_shape=[jax.ShapeDtypeStruct((b, 1, GLA_H * GLA_DV), F32), jax.ShapeDtypeStruct(s0.shape, F32)],
        input_output_aliases={5: 1} if extra else {},
        compiler_params=_params("parallel"),
        name="gla_step",
    )(q, k, gl, v, s0, *extra)


def _gla_out_kernel(o_ref, r_ref, x_ref, go_ref, w_ref, y_ref):
    parts = []
    for h in range(GLA_H):
        sl = slice(h * GLA_DV, (h + 1) * GLA_DV)
        r = r_ref[:, sl]
        parts.append((_rms_rows(o_ref[:, sl], go_ref[...]) * (r * jax.nn.sigmoid(r))).astype(BF16))
    y_ref[...] = x_ref[...] + _dot(jnp.concatenate(parts, axis=1), w_ref[...])


def _gla_out(o, r, x, go, w, tm):
    m, d = x.shape
    row = pl.BlockSpec((tm, d), lambda i: (i, 0))
    full = lambda a: pl.BlockSpec(a.shape, lambda i: (0, 0))
    return pl.pallas_call(
        _gla_out_kernel,
        grid=(m // tm,),
        in_specs=[row, row, row, full(go), full(w)],
        out_specs=row,
        out_shape=jax.ShapeDtypeStruct((m, d), F32),
        compiler_params=_params("parallel"),
        name="gla_out",
    )(o, r, x, go, w)


def _gla_layer(xp, xs, s0, layer, prev_state, norm_g, w_in, w_gate2, b_gate, g_out, w_out):
    b, t, d = xp.shape
    bs = xs.shape[0]
    dk, dv = GLA_H * GLA_DK, GLA_H * GLA_DV
    rank = w_gate2.shape[0]
    w = jnp.pad(w_in, ((0, 0), (0, LANES - rank))).astype(BF16)
    wg2 = jnp.pad(w_gate2, ((0, LANES - rank), (0, 0))).astype(BF16)
    g = norm_g.reshape(1, d)
    bg = b_gate.reshape(1, dk)
    go = g_out.reshape(1, GLA_DV)
    wo = w_out.astype(BF16)
    x2 = xp.reshape(b * t, d)
    q, k, v, r, gl = _gla_proj(x2, g, w, wg2, bg, tm=512)
    o, sp = _gla_scan(q.reshape(b, t, dk), k.reshape(b, t, dk), gl.reshape(b, t, dk), v.reshape(b, t, dv))
    yp = _gla_out(o.reshape(b * t, dv), r, x2, go, wo, tm=512).reshape(b, t, d)
    xs2 = xs.reshape(bs, d)
    q, k, v, r, gl = _gla_proj(xs2, g, w, wg2, bg, tm=bs)
    o, ss = _gla_step(q.reshape(bs, 1, dk), k.reshape(bs, 1, dk), gl.reshape(bs, 1, dk), v.reshape(bs, 1, dv), s0,
                      layer, prev_state)
    ys = _gla_out(o.reshape(bs, dv), r, xs2, go, wo, tm=bs).reshape(bs, 1, d)
    return yp, ys, sp, ss


def _nsa_proj_kernel(x_ref, gn_ref, wt_ref, gq_ref, gk_ref, bg_ref, cos_ref, sin_ref, *rest):
    (q_ref, kc_ref, vc_ref, ks_ref, vs_ref, kw_ref, vw_ref, gt_ref,
     ksa_ref, vsa_ref, kwb_ref, vwa_ref) = rest[-12:]
    tm = x_ref.shape[0]
    nq = GRP * KVH * HD
    nkv = KVH * HD
    h = _rms_rows(x_ref[...], gn_ref[...]).astype(BF16)
    cos = cos_ref[...][None]
    sin = sin_ref[...][None]

    def norm_rope(z, g, nh):
        z3 = z.reshape(nh, HD, tm)
        y = z3 * lax.rsqrt(jnp.mean(z3 * z3, axis=1, keepdims=True) + EPS) * g[None]
        x1 = y[:, 0:ROT_HALF, :]
        x2 = y[:, ROT_HALF:2 * ROT_HALF, :]
        return jnp.concatenate([x1 * cos - x2 * sin, x1 * sin + x2 * cos, y[:, 2 * ROT_HALF:, :]], axis=1)

    zq = _dot_nt(wt_ref[0:nq, :], h)
    q_ref[...] = (norm_rope(zq, gq_ref[...], GRP * KVH) * (HD ** -0.5 * LOG2E)).astype(BF16)
    zkv = _dot_nt(wt_ref[nq:nq + 6 * nkv, :], h)
    outs = (kc_ref, vc_ref, ks_ref, vs_ref, kw_ref, vw_ref)
    rows = []
    for i in range(6):
        z = zkv[i * nkv:(i + 1) * nkv, :]
        rows.append(norm_rope(z, gk_ref[i // 2], KVH) if i % 2 == 0 else z.reshape(KVH, HD, tm))
        outs[i][...] = rows[i]
    zg = _dot_nt(wt_ref[nq + 6 * nkv:, :], h) + bg_ref[...]
    gt_ref[...] = jax.nn.sigmoid(zg).reshape(KVH, 16, tm)

    nsp = ksa_ref.shape[1] - HD
    blk = lax.broadcasted_iota(jnp.int32, (KVH, nsp, tm), 1)
    tok = pl.program_id(1) * tm + lax.broadcasted_iota(jnp.int32, (KVH, nsp, tm), 2)
    onehot = jnp.where(tok // L_SLC == blk, 1.0, 0.0)
    ksa_ref[...] = jnp.concatenate([rows[2], onehot], axis=1).astype(BF16)
    kwb_ref[...] = rows[4].astype(BF16)
    ones = jnp.where(lax.broadcasted_iota(jnp.int32, (KVH, V_PAD, tm), 1) == 0, 1.0, 0.0)
    vsa_ref[...] = jnp.concatenate([rows[3], ones], axis=1).astype(BF16)
    vwa_ref[...] = jnp.concatenate([rows[5], ones], axis=1).astype(BF16)


def _sel_rows(t):
    return -(-(-(-t // L_SLC)) // 16) * 16


def _nsa_proj(x, gn, wt, gq, gk, bg, cos, sin, tm, layer=0, n_layers=1, prev=None):
    b, t, d = x.shape
    full = lambda a: pl.BlockSpec(a.shape, lambda i, j: (0,) * a.ndim)
    rows_spec = lambda n: pl.BlockSpec((None, KVH, n, tm), lambda i, j: (i, 0, 0, j))
    rows_shape = lambda n, dt: jax.ShapeDtypeStruct((b, KVH, n, t), dt)
    kv_spec = pl.BlockSpec((None, None, KVH, HD, tm), lambda i, j: (layer, i, 0, 0, j))
    kv_shape = jax.ShapeDtypeStruct((n_layers, b, KVH, HD, t), F32)
    aug = [HD + _sel_rows(t), HD + V_PAD, HD, HD + V_PAD]
    prev = list(prev) if prev is not None else []
    n_in = 8
    return pl.pallas_call(
        _nsa_proj_kernel,
        grid=(b, t // tm),
        in_specs=[pl.BlockSpec((None, tm, d), lambda i, j: (i, j, 0)), full(gn), full(wt), full(gq), full(gk),
                  full(bg), pl.BlockSpec((ROT_HALF, tm), lambda i, j: (0, j)),
                  pl.BlockSpec((ROT_HALF, tm), lambda i, j: (0, j))]
        + [pl.BlockSpec(memory_space=pl.ANY)] * len(prev),
        out_specs=[pl.BlockSpec((None, GRP * KVH, HD, tm), lambda i, j: (i, 0, 0, j))] + [kv_spec] * 6
        + [rows_spec(16)] + [rows_spec(n) for n in aug],
        out_shape=[jax.ShapeDtypeStruct((b, GRP * KVH, HD, t), BF16)] + [kv_shape] * 6
        + [rows_shape(16, F32)] + [rows_shape(n, BF16) for n in aug],
        input_output_aliases={n_in + k: 1 + k for k in range(len(prev))},
        compiler_params=_params("parallel", "parallel"),
        name="nsa_proj",
    )(x, gn, wt, gq, gk, bg, cos, sin, *prev)


def _nsa_out_kernel(ot_ref, x_ref, w_ref, y_ref):
    ot = ot_ref[...].reshape(GRP * KVH * HD, ot_ref.shape[-1])
    y_ref[...] = x_ref[...] + _dot_tn(ot, w_ref[...])


def _nsa_out(ot, x, w, tm):
    b, t, d = x.shape
    return pl.pallas_call(
        _nsa_out_kernel,
        grid=(b, t // tm),
        in_specs=[pl.BlockSpec((None, GRP * KVH, HD, tm), lambda i, j: (i, 0, 0, j)),
                  pl.BlockSpec((None, tm, d), lambda i, j: (i, j, 0)),
                  pl.BlockSpec(w.shape, lambda i, j: (0, 0))],
        out_specs=pl.BlockSpec((None, tm, d), lambda i, j: (i, j, 0)),
        out_shape=jax.ShapeDtypeStruct((b, t, d), F32),
        compiler_params=_params("parallel", "parallel"),
        name="nsa_out",
    )(ot, x, w)


def _compress_kernel(*refs, n_pages, paged, is_key):
    refs = refs[1:] if paged else refs
    pages = refs[:n_pages]
    perm_ref, w_ref, b1_ref, pe_ref, w1_ref, w2t_ref, gk_ref, o_ref, lhs_scr = refs[n_pages:]
    p = pl.program_id(1)
    half = CMP_STRIDE
    n_half = PAGE // half
    low = lax.broadcasted_iota(jnp.int32, (n_half, LANES), 1) < HD
    odd_slot = (lax.broadcasted_iota(jnp.int32, (2 * HD, PAGE), 1) // n_half) % 2 == 1
    for pair in range(KVH // 2):
        for j in range(n_pages):
            y = _dot(pages[j][2 * pair:2 * pair + 2].reshape(2 * HD, PAGE).astype(BF16), perm_ref[...])
            x = jnp.where(odd_slot, jnp.concatenate([y[HD:], y[:HD]], axis=0), y).T
            rows = pl.ds(pl.multiple_of((p * n_pages + j) * n_half, n_half), n_half)
            for t in range(half // 2):
                ev = x[2 * t * n_half:(2 * t + 1) * n_half]
                od = x[(2 * t + 1) * n_half:(2 * t + 2) * n_half]
                lanes = slice(t * LANES, (t + 1) * LANES)
                lhs_scr[2 * pair, rows, lanes] = jnp.where(low, ev, od)
                lhs_scr[2 * pair + 1, rows, lanes] = jnp.where(low, od, ev)

    @pl.when(p == pl.num_programs(1) - 1)
    def _():
        nh = lhs_scr.shape[1]
        hidden = b1_ref.shape[1]
        c = _dot(pe_ref[0].astype(BF16), w1_ref[0]) + _dot(pe_ref[1].astype(BF16), w1_ref[1])
        bias = c[0:1, :] + b1_ref[...]
        for kvh in range(KVH):
            fs = _dot(lhs_scr[kvh].astype(BF16), w_ref[kvh % 2])
            hid = jax.nn.gelu(fs[:, :hidden] + pltpu.roll(fs[:, hidden:], nh - 1, 0) + bias)
            yt = _dot_nt(w2t_ref[...], hid.astype(BF16))
            if is_key:
                yt = yt * lax.rsqrt(jnp.mean(yt * yt, axis=0, keepdims=True) + EPS) * gk_ref[...]
            o_ref[kvh] = yt.astype(BF16)


def _compress(src, weights, n_pages_total, pages_per_step, is_key, page_table=None, layer=0):
    paged = page_table is not None
    b = page_table.shape[0] if paged else src.shape[1]
    n_half = PAGE // CMP_STRIDE
    nh = n_pages_total * n_half
    steps = n_pages_total // pages_per_step
    tok = jnp.arange(PAGE)
    perm = (((tok % CMP_STRIDE) * n_half + tok // CMP_STRIDE)[:, None] == tok[None, :]).astype(BF16)
    if paged:
        page_spec = lambda j: pl.BlockSpec(
            (None, None, KVH, HD, PAGE), lambda i, p, pt: (layer, pt[i, p * pages_per_step + j], 0, 0, 0))
        full = lambda a: pl.BlockSpec(a.shape, lambda i, p, pt: (0,) * a.ndim)
        o_spec = pl.BlockSpec((None, KVH, HD, nh), lambda i, p, pt: (i, 0, 0, 0))
    else:
        page_spec = lambda j: pl.BlockSpec((None, None, KVH, HD, PAGE),
                                           lambda i, p: (layer, i, 0, 0, p * pages_per_step + j))
        full = lambda a: pl.BlockSpec(a.shape, lambda i, p: (0,) * a.ndim)
        o_spec = pl.BlockSpec((None, KVH, HD, nh), lambda i, p: (i, 0, 0, 0))
    consts = [perm] + list(weights)
    grid_spec = pltpu.PrefetchScalarGridSpec(
        num_scalar_prefetch=1 if paged else 0,
        grid=(b, steps),
        in_specs=[page_spec(j) for j in range(pages_per_step)] + [full(a) for a in consts],
        out_specs=o_spec,
        scratch_shapes=[pltpu.VMEM((KVH, nh, CMP_STRIDE * HD), F32)],
    )
    args = ([page_table] if paged else []) + [src] * pages_per_step + consts
    return pl.pallas_call(
        functools.partial(_compress_kernel, n_pages=pages_per_step, paged=paged, is_key=is_key),
        grid_spec=grid_spec,
        out_shape=jax.ShapeDtypeStruct((b, KVH, HD, nh), BF16),
        compiler_params=_params("parallel", "arbitrary"),
        name="nsa_compress",
    )(*args)


def _cmp_weights(pe, w1, b1, w2, g_kcmp, nh):
    hidden = w1.shape[-1]
    wfs = jnp.concatenate([w1[:CMP_STRIDE], w1[CMP_STRIDE:]], axis=-1)
    swapped = wfs.reshape(CMP_STRIDE // 2, 2, HD, 2 * hidden)[:, ::-1]
    w_pair = jnp.stack([wfs, swapped.reshape(wfs.shape)]).reshape(2, CMP_STRIDE * HD, 2 * hidden).astype(BF16)
    pe2 = jnp.zeros((2, 8, CMP_STRIDE * HD), F32).at[:, 0, :].set(pe.reshape(2, CMP_STRIDE * HD))
    w1f = w1.reshape(2, CMP_STRIDE * HD, hidden).astype(BF16)
    gk = jnp.broadcast_to(g_kcmp.reshape(HD, 1), (HD, nh))
    return w_pair, b1.reshape(1, hidden), pe2, w1f, w2.T.astype(BF16), gk


def _softmax_update(s, mask, vt, m, l, acc):
    m_new = jnp.maximum(m, jnp.max(jnp.where(mask, s, NEG), axis=0, keepdims=True))
    alpha = jnp.exp2(m - m_new)
    p = jnp.where(mask, jnp.exp2(s - m_new), 0.0)
    l_new = l * alpha + jnp.sum(p, axis=0, keepdims=True)
    acc_new = acc * alpha + _dot(vt, p.astype(BF16))
    return m_new, l_new, acc_new, p


def _softmax_init(lanes):
    return jnp.full((1, lanes), NEG, F32), jnp.zeros((1, lanes), F32), jnp.zeros((HD, lanes), F32)


def _finish(l, acc):
    return acc / jnp.maximum(l, 1e-30)


def _split_dot(mt, x):
    hi = x.astype(BF16)
    lo = (x - hi.astype(F32)).astype(BF16)
    return _dot(mt, hi) + _dot(mt, lo)


def _attn_prompt_kernel(q_ref, kc_ref, vc_ref, ksa_ref, vsa_ref, kwb_ref, vwa_ref, gt_ref, mt_ref, o_ref,
                        s_scr, s2_scr, *, tq, n_cmp, n_sel):
    i = pl.program_id(2)
    lanes = GRP * tq
    ncp = kc_ref.shape[1]
    nsp = mt_ref.shape[0]
    qt = jnp.concatenate([q_ref[h] for h in range(GRP)], axis=1)

    def col_max(mx, s):
        return jnp.maximum(mx, jnp.max(s.reshape(s.shape[0] // 8, 8, lanes), axis=0))

    def finish(acc):
        return acc[0:HD] / jnp.maximum(acc[HD:HD + 1], 1e-30)

    tpos1 = i * tq + lax.broadcasted_iota(jnp.int32, (1, tq), 1)
    tpos = jnp.concatenate([tpos1] * GRP, axis=1)

    s = _dot_tn(kc_ref[...], qt)
    n_idx = lax.broadcasted_iota(jnp.int32, (ncp, lanes), 0)
    mask = (n_idx * CMP_STRIDE + (L_CMP - 1) <= tpos) & (n_idx < n_cmp)
    m, l, acc, p = _softmax_update(s, mask, vc_ref[...], *_softmax_init(lanes))
    o_c = _finish(l, acc)
    p = p / jnp.maximum(l, 1e-30)
    imp = p[:, 0:tq]
    for h in range(1, GRP):
        imp = imp + p[:, h * tq:(h + 1) * tq]
    p_slc = _split_dot(mt_ref[...], imp)

    j_idx = lax.broadcasted_iota(jnp.int32, (nsp, tq), 0)
    cur = tpos1 // L_SLC
    valid = j_idx <= cur
    forced = (j_idx == 0) | (j_idx == cur) | (j_idx == cur - 1)
    score = jnp.where(valid & forced, jnp.inf, jnp.where(valid, p_slc, -jnp.inf))
    bits = lax.bitcast_convert_type(score, jnp.int32)
    key = jnp.where(bits >= 0, bits, bits ^ 0x7FFFFFFF)
    n_grp = nsp // 8
    keys = [key[8 * r:8 * r + 8] for r in range(n_grp)]
    keys_m1 = [k - 1 for k in keys]
    sub = lax.broadcasted_iota(jnp.int32, (8, tq), 0)

    def count_group(grp, ranks):
        ranks = list(ranks)
        for u in range(8):
            row = jnp.broadcast_to(keys[grp][u:u + 1, :], (8, tq))
            for r in range(n_grp):
                thr = keys[r] if r < grp else keys_m1[r] if r > grp else jnp.where(sub > u, keys_m1[r], keys[r])
                ranks[r] = ranks[r] + jnp.where(row > thr, 1.0, 0.0)
        return tuple(ranks)

    ranks = tuple(jnp.zeros((8, tq), F32) for _ in range(n_grp))
    last_valid = (i * tq + tq - 1) // L_SLC
    for grp in range(n_grp):
        ranks = lax.cond(8 * grp <= last_valid, functools.partial(count_group, grp), lambda r: r, ranks)
    bias = jnp.where(jnp.concatenate(ranks, axis=0) < n_sel, 0.0, NEG)
    qa = jnp.concatenate([qt, jnp.concatenate([bias] * GRP, axis=1).astype(BF16)], axis=0)

    row_k = lax.broadcasted_iota(jnp.int32, (KT, lanes), 0)

    def slc_scores(st, buf):
        start = pl.multiple_of(st * KT, KT)
        buf[0:KT, :] = _dot_tn(ksa_ref[:, pl.ds(start, KT)], qa)

    def slc_softmax(st, buf, carry, diagonal=False):
        m, acc = carry
        s = buf[0:KT, :]
        if diagonal:
            s = jnp.where(st * KT + row_k <= tpos, s, NEG)
        m_new = jnp.maximum(m, jnp.max(col_max(jnp.full((8, lanes), NEG, F32), s), axis=0, keepdims=True))
        p = jnp.exp2(s - m_new).astype(BF16)
        pv = _dot(vsa_ref[:, pl.ds(pl.multiple_of(st * KT, KT), KT)], p)
        return m_new, acc * jnp.exp2(m - m_new) + pv

    def slc_pair(u, carry):
        slc_scores(2 * u + 1, s2_scr)
        carry = slc_softmax(2 * u, s_scr, carry)
        slc_scores(2 * u + 2, s_scr)
        return slc_softmax(2 * u + 1, s2_scr, carry)

    def slc_tail_odd(carry):
        slc_scores(n_full, s2_scr)
        carry = slc_softmax(n_full - 1, s_scr, carry)
        return slc_softmax(n_full, s2_scr, carry, diagonal=True)

    def slc_tail_even(carry):
        return slc_softmax(n_full, s_scr, carry, diagonal=True)

    n_full = (i * tq) // KT
    slc_scores(0, s_scr)
    init = (jnp.full((1, lanes), NEG, F32), jnp.zeros((HD + V_PAD, lanes), F32))
    carry = lax.fori_loop(0, n_full // 2, slc_pair, init)
    o_s = finish(lax.cond(n_full % 2 == 1, slc_tail_odd, slc_tail_even, carry)[1])

    n_wt = WINDOW // tq + 1
    w0 = jnp.maximum(i - (n_wt - 1), 0) * tq
    row_w = lax.broadcasted_iota(jnp.int32, (tq, lanes), 0)
    col_w = jnp.concatenate([lax.broadcasted_iota(jnp.int32, (tq, tq), 1)] * GRP, axis=1)

    def window(steady):
        mx = jnp.full((8, lanes), NEG, F32)
        for r in range(n_wt):
            start = pl.multiple_of(w0 + r * tq, tq)
            s = _dot_tn(kwb_ref[:, pl.ds(start, tq)], qt)
            if not steady:
                rel = tpos - (start + row_w)
                s = jnp.where((rel >= 0) & (rel <= WINDOW), s, NEG)
            elif r == 0:
                s = jnp.where(row_w >= col_w, s, NEG)
            elif r == n_wt - 1:
                s = jnp.where(row_w <= col_w, s, NEG)
            s_scr[r * tq:(r + 1) * tq, :] = s
            mx = col_max(mx, s)
        m = jnp.max(mx, axis=0, keepdims=True)
        acc = jnp.zeros((HD + V_PAD, lanes), F32)
        for r in range(n_wt):
            start = pl.multiple_of(w0 + r * tq, tq)
            p = jnp.exp2(s_scr[r * tq:(r + 1) * tq, :] - m).astype(BF16)
            acc = acc + _dot(vwa_ref[:, pl.ds(start, tq)], p)
        return finish(acc)

    o_w = lax.cond(i >= n_wt - 1, functools.partial(window, True), functools.partial(window, False))

    for h in range(GRP):
        ls = slice(h * tq, (h + 1) * tq)
        o = (gt_ref[3 * h:3 * h + 1, :] * o_c[:, ls] + gt_ref[3 * h + 1:3 * h + 2, :] * o_s[:, ls]
             + gt_ref[3 * h + 2:3 * h + 3, :] * o_w[:, ls])
        o_ref[h] = o.astype(BF16)


def _slc_matrix(nsp, ncp, n_cmp):
    ratio = L_SLC // CMP_STRIDE
    j = jnp.arange(nsp)[:, None]
    n = jnp.arange(ncp)[None, :]
    m = ((n >= ratio * j) & (n <= ratio * j + ratio - 1)).astype(F32)
    m = m + ((n >= ratio * j - 1) & (n <= ratio * j + ratio - 2)).astype(F32)
    return jnp.where(n < n_cmp, m, 0.0).astype(BF16)


def _attn_prompt(qt, kct, vct, ksa, vsa, kwb, vwa, gt, tq=4 * LANES):
    b, _, _, t = qt.shape
    assert t % KT == 0 and WINDOW % tq == 0 and t >= WINDOW + tq
    ncp = kct.shape[-1]
    n_cmp = t // CMP_STRIDE - 1
    ns = -(-t // L_SLC)
    mt = _slc_matrix(_sel_rows(t), ncp, n_cmp)
    res = lambda a: pl.BlockSpec((None, None) + a.shape[2:], lambda i, g, j: (i, g, 0, 0))
    return pl.pallas_call(
        functools.partial(_attn_prompt_kernel, tq=tq, n_cmp=n_cmp, n_sel=min(N_SEL, ns)),
        grid=(b, KVH, t // tq),
        in_specs=[pl.BlockSpec((None, GRP, HD, tq), lambda i, g, j: (i, g, 0, j)),
                  res(kct), res(vct), res(ksa), res(vsa), res(kwb), res(vwa),
                  pl.BlockSpec((None, None, 16, tq), lambda i, g, j: (i, g, 0, j)),
                  pl.BlockSpec(mt.shape, lambda i, g, j: (0, 0))],
        out_specs=pl.BlockSpec((None, GRP, HD, tq), lambda i, g, j: (i, g, 0, j)),
        out_shape=jax.ShapeDtypeStruct(qt.shape, BF16),
        scratch_shapes=[pltpu.VMEM((max(KT, WINDOW + tq), GRP * tq), F32), pltpu.VMEM((KT, GRP * tq), F32)],
        compiler_params=_params("parallel", "parallel", "arbitrary"),
        name="nsa_attn_prompt",
    )(qt, kct, vct, ksa, vsa, kwb, vwa, gt, mt)


def _split_dot_rows(x, mt):
    hi = x.astype(BF16)
    lo = (x - hi.astype(F32)).astype(BF16)
    return _dot(hi, mt) + _dot(lo, mt)


def _attn_sample_rows_kernel(*refs, n_pages, n_cmp, n_blocks, n_sel):
    refs = refs[1:]
    k_pages = refs[:n_pages]
    v_pages = refs[n_pages:2 * n_pages]
    (q_ref, kc_ref, vc_ref, kw_ref, vw_ref, kns_ref, vns_ref, knw_ref, vnw_ref, gt_ref, mtt_ref, exp_ref,
     o_ref, bias_scr, oc_scr, ow_scr, m_scr, l_scr, acc_scr) = refs[2 * n_pages:]
    p = pl.program_id(1)
    ncp = kc_ref.shape[-1]
    nsp = mtt_ref.shape[1]
    n_q = KVH * GRP
    rows_all = KVH * HD
    q = q_ref[...]

    def stacked(ref):
        return ref[...].reshape(rows_all, ref.shape[-1]).astype(BF16)

    def update(s, valid, vt, m, l, acc):
        if valid is not None:
            s = jnp.where(valid, s, NEG)
        m_new = jnp.maximum(m, jnp.max(s, axis=1, keepdims=True))
        alpha = jnp.exp2(m - m_new)
        pr = jnp.exp2(s - m_new)
        if valid is not None:
            pr = jnp.where(valid, pr, 0.0)
        l_new = l * alpha + jnp.sum(pr, axis=1, keepdims=True)
        return m_new, l_new, acc * alpha + _dot_nt(pr.astype(BF16), vt), pr

    def init():
        return jnp.full((n_q, 1), NEG, F32), jnp.zeros((n_q, 1), F32), jnp.zeros((n_q, rows_all), F32)

    def save(m, l, acc):
        m_scr[...] = jnp.broadcast_to(m, m_scr.shape)
        l_scr[...] = jnp.broadcast_to(l, l_scr.shape)
        acc_scr[...] = acc

    @pl.when(p == 0)
    def _():
        cur = n_blocks - 1
        row_q = lax.broadcasted_iota(jnp.int32, (n_q, ncp), 0)
        row_b = lax.broadcasted_iota(jnp.int32, (n_q, nsp), 0)
        j_row = lax.broadcasted_iota(jnp.int32, (1, nsp), 1)
        jp_idx = lax.broadcasted_iota(jnp.int32, (nsp, nsp), 0)
        j_idx = lax.broadcasted_iota(jnp.int32, (nsp, nsp), 1)
        s = _dot(q, stacked(kc_ref))
        valid = lax.broadcasted_iota(jnp.int32, (n_q, ncp), 1) < n_cmp
        m, l, acc, pr = update(s, valid, stacked(vc_ref), *init())
        oc_scr[...] = acc / jnp.maximum(l, 1e-30)
        pr = pr / jnp.maximum(l, 1e-30)
        bias_blocks = jnp.zeros((n_q, nsp), F32)
        for g in range(KVH):
            imp = jnp.sum(jnp.where(row_q // GRP == g, pr, 0.0), axis=0, keepdims=True)
            p_slc = _split_dot_rows(jnp.broadcast_to(imp, (8, ncp)), mtt_ref[...])[0:1]
            ok = j_row <= cur
            forced = (j_row == 0) | (j_row == cur) | (j_row == cur - 1)
            score = jnp.where(ok & forced, jnp.inf, jnp.where(ok, p_slc, -jnp.inf))
            row = jnp.broadcast_to(score, (nsp, nsp))
            col = row.T
            ahead = (col > row) | ((col == row) & (jp_idx < j_idx))
            rank = jnp.sum(jnp.where(ahead, 1.0, 0.0), axis=0, keepdims=True)
            bias = jnp.broadcast_to(jnp.where(rank < n_sel, 0.0, NEG), (n_q, nsp))
            bias_blocks = jnp.where(row_b // GRP == g, bias, bias_blocks)
        bias_scr[...] = _dot(bias_blocks.astype(BF16), exp_ref[...])
        n_win = kw_ref.shape[-1]
        kt = jnp.concatenate([stacked(kw_ref), stacked(knw_ref)], axis=1)
        vt = jnp.concatenate([stacked(vw_ref), stacked(vnw_ref)], axis=1)
        valid = lax.broadcasted_iota(jnp.int32, (n_q, n_win + LANES), 1) <= n_win
        _, l, acc, _ = update(_dot(q, kt), valid, vt, *init())
        ow_scr[...] = acc / jnp.maximum(l, 1e-30)
        save(*init())

    n_keys = n_pages * PAGE
    kt = jnp.concatenate([k_pages[j][...].reshape(rows_all, PAGE) for j in range(n_pages)], axis=1).astype(BF16)
    vt = jnp.concatenate([v_pages[j][...].reshape(rows_all, PAGE) for j in range(n_pages)], axis=1).astype(BF16)
    s = _dot(q, kt) + bias_scr[:, pl.ds(pl.multiple_of(p * n_keys, n_keys), n_keys)]
    save(*update(s, None, vt, m_scr[:, 0:1], l_scr[:, 0:1], acc_scr[...])[:3])

    @pl.when(p == pl.num_programs(1) - 1)
    def _():
        valid = lax.broadcasted_iota(jnp.int32, (n_q, LANES), 1) == 0
        _, l, acc, _ = update(_dot(q, stacked(kns_ref)), valid, stacked(vns_ref),
                              m_scr[:, 0:1], l_scr[:, 0:1], acc_scr[...])
        o_ref[...] = (gt_ref[:, 0:1] * oc_scr[...] + gt_ref[:, 1:2] * (acc / jnp.maximum(l, 1e-30))
                      + gt_ref[:, 2:3] * ow_scr[...])


def _attn_sample(qs, kct, vct, pool_k, pool_v, page_table, layer, kwt, vwt, kns, vns, knw, vnw, gs,
                 pages_per_step=32):
    b, n_pages_total = page_table.shape
    pages_per_step = min(pages_per_step, n_pages_total)
    past = n_pages_total * PAGE
    n_cmp = past // CMP_STRIDE - 1
    ncp = kct.shape[-1]
    n_blocks = past // L_SLC + 1
    nsp = -(-n_blocks // LANES) * LANES
    mtt = _slc_matrix(nsp, ncp, n_cmp).T
    expand = (jnp.arange(nsp)[:, None] == jnp.arange(past)[None, :] // L_SLC).astype(BF16)
    n_q, rows_all = KVH * GRP, KVH * HD
    steps = n_pages_total // pages_per_step
    page_spec = lambda j: pl.BlockSpec((None, None, KVH, HD, PAGE),
                                       lambda i, p, pt: (layer, pt[i, p * pages_per_step + j], 0, 0, 0))
    def per_b(a):
        if a.ndim == 5:
            return pl.BlockSpec((None, None) + a.shape[2:], lambda i, p, pt: (layer, i, 0, 0, 0))
        return pl.BlockSpec((None,) + a.shape[1:], lambda i, p, pt: (i,) + (0,) * (a.ndim - 1))

    small = [qs, kct, vct, kwt, vwt, kns, vns, knw, vnw, gs]
    grid_spec = pltpu.PrefetchScalarGridSpec(
        num_scalar_prefetch=1,
        grid=(b, steps),
        in_specs=[page_spec(j) for j in range(pages_per_step)] * 2 + [per_b(a) for a in small]
        + [pl.BlockSpec(a.shape, lambda i, p, pt: (0, 0)) for a in (mtt, expand)],
        out_specs=pl.BlockSpec((None, n_q, rows_all), lambda i, p, pt: (i, 0, 0)),
        scratch_shapes=[pltpu.VMEM((n_q, past), F32), pltpu.VMEM((n_q, rows_all), F32),
                        pltpu.VMEM((n_q, rows_all), F32), pltpu.VMEM((n_q, LANES), F32),
                        pltpu.VMEM((n_q, LANES), F32), pltpu.VMEM((n_q, rows_all), F32)],
    )
    return pl.pallas_call(
        functools.partial(_attn_sample_rows_kernel, n_pages=pages_per_step, n_cmp=n_cmp, n_blocks=n_blocks,
                          n_sel=min(N_SEL, n_blocks)),
        grid_spec=grid_spec,
        out_shape=jax.ShapeDtypeStruct((b, n_q, rows_all), F32),
        compiler_params=_params("parallel", "arbitrary"),
        name="nsa_attn_sample",
    )(page_table, *([pool_k] * pages_per_step), *([pool_v] * pages_per_step), *small, mtt, expand)


def _rope_tables(pos):
    inv_freq = ROPE_THETA ** (-jnp.arange(ROT_HALF, dtype=F32) * 2.0 / (2 * ROT_HALF))
    ang = pos.astype(F32)[:, None] * inv_freq[None, :]
    return jnp.cos(ang).T, jnp.sin(ang).T


def _nsa_layer(xp, xs, pools, win_bufs, page_table, layer, n_layers, prev_rows, norm_g, w_in, b_gate, g_q, g_k,
               g_kcmp, cmp_pe, cmp_w1, cmp_b1, cmp_w2, w_out):
    b, t, d = xp.shape
    bs = xs.shape[0]
    past = page_table.shape[1] * PAGE
    nq, nkv = GRP * KVH * HD, KVH * HD
    n_gate = 3 * GRP * KVH

    wt_g = jnp.pad(w_in[:, nq + 6 * nkv:].T.reshape(KVH, n_gate // KVH, d), ((0, 0), (0, 16 - n_gate // KVH), (0, 0)))
    wt = jnp.concatenate([w_in[:, :nq + 6 * nkv].T, wt_g.reshape(KVH * 16, d)], axis=0).astype(BF16)
    bg = jnp.pad(b_gate.reshape(KVH, n_gate // KVH), ((0, 0), (0, 16 - n_gate // KVH))).reshape(KVH * 16, 1)
    gn = norm_g.reshape(1, d)
    wo = w_out.astype(BF16)

    def project(x, pos, tm, **stacking):
        cos, sin = _rope_tables(pos)
        col = lambda v: jnp.broadcast_to(v[..., None], v.shape + (tm,))
        return _nsa_proj(x, gn, wt, col(g_q), col(g_k), col(bg[:, 0]), cos, sin, tm, **stacking)

    tm = min(512, t)
    outs = project(xp, jnp.arange(t), tm, layer=layer, n_layers=n_layers, prev=prev_rows)
    qt, rows_p, (gt, ksa, vsa, kwb, vwa) = outs[0], list(outs[1:7]), outs[7:]
    pages = t // PAGE
    nh = pages * (PAGE // CMP_STRIDE)
    cmp_w = [_cmp_weights(cmp_pe[i], cmp_w1[i], cmp_b1[i], cmp_w2[i], g_kcmp, nh) for i in range(2)]
    pps = min(32, pages)
    kct = _compress(rows_p[0], cmp_w[0], pages, pps, True, layer=layer)
    vct = _compress(rows_p[1], cmp_w[1], pages, pps, False, layer=layer)
    ot = _attn_prompt(qt, kct, vct, ksa, vsa, kwb, vwa, gt)
    yp = _nsa_out(ot, xp, wo, tm)

    xs_pad = jnp.pad(xs.reshape(1, bs, d), ((0, 0), (0, LANES - bs), (0, 0)))
    outs = project(xs_pad, jnp.full((LANES,), past), LANES)
    qt_s, rows_s, gt_s = outs[0], outs[1:7], outs[7]
    pages_s = page_table.shape[1]
    nh_s = pages_s * (PAGE // CMP_STRIDE)
    cmp_ws = [_cmp_weights(cmp_pe[i], cmp_w1[i], cmp_b1[i], cmp_w2[i], g_kcmp, nh_s) for i in range(2)]
    pool5 = [jnp.transpose(pl_, (0, 1, 3, 4, 2)) for pl_ in pools]
    pps = min(64, pages_s)
    kct_s = _compress(pool5[0], cmp_ws[0], pages_s, pps, True, page_table, layer)
    vct_s = _compress(pool5[1], cmp_ws[1], pages_s, pps, False, page_table, layer)
    qs = jnp.transpose(qt_s[0, :, :, :bs].reshape(KVH, GRP, HD, bs), (3, 0, 2, 1))
    qs = jnp.einsum("bgdh,gk->bghkd", qs, jnp.eye(KVH, dtype=qs.dtype))
    qs = qs.reshape(bs, KVH * GRP, KVH * HD)
    new = lambda a: jnp.pad(jnp.transpose(a[0, 0, :, :, :bs], (2, 0, 1))[..., None],
                            ((0, 0), (0, 0), (0, 0), (0, LANES - 1)))
    gs = jnp.transpose(gt_s[0, :, :n_gate // KVH, :bs].reshape(KVH, GRP, 3, bs), (3, 0, 1, 2))
    gs = jnp.pad(gs.reshape(bs, KVH * GRP, 3), ((0, 0), (0, 0), (0, LANES - 3)))
    win5 = [jnp.transpose(wb, (0, 1, 3, 4, 2)) for wb in win_bufs]
    ot_s = _attn_sample(qs, kct_s, vct_s, pool5[2], pool5[3], page_table, layer, win5[0], win5[1],
                        new(rows_s[2]), new(rows_s[3]), new(rows_s[4]), new(rows_s[5]), gs)
    ot_s = jnp.stack([ot_s[:, g * GRP:(g + 1) * GRP, g * HD:(g + 1) * HD] for g in range(KVH)])
    ot_s = jnp.transpose(ot_s, (0, 2, 3, 1)).reshape(1, GRP * KVH, HD, bs)
    ot_s = jnp.pad(ot_s, ((0, 0), (0, 0), (0, 0), (0, LANES - bs))).astype(BF16)
    ys = _nsa_out(ot_s, xs_pad, wo, LANES)[0, :bs].reshape(bs, 1, d)

    rows_s = [jnp.transpose(a[0, 0, :, :, :bs], (2, 0, 1)).reshape(bs, 1, KVH, HD) for a in rows_s]
    return yp, ys, rows_p, rows_s


def kernel(x_prompt, x_sample, state_gla, cache_k_cmp, cache_v_cmp, cache_k_slc, cache_v_slc, cache_k_win, cache_v_win, page_table, norm_mix, norm_mlp, mlp_up, mlp_down, gla_w_in, gla_w_gate2, gla_b_gate, gla_g_out, gla_w_out, nsa_w_in, nsa_b_gate, nsa_g_q, nsa_g_k, nsa_g_kcmp, nsa_cmp_pe, nsa_cmp_w1, nsa_cmp_b1, nsa_cmp_w2, nsa_w_out):
    depth = norm_mix.shape[0]
    b, t, d = x_prompt.shape
    bs = x_sample.shape[0]
    xp, xs = x_prompt, x_sample
    gla_p, nsa_s = [], []
    gla_s = None
    nsa_p = None
    pools = (cache_k_cmp, cache_v_cmp, cache_k_slc, cache_v_slc)
    wu, wd = mlp_up.astype(BF16), mlp_down.astype(BF16)
    for i in range(depth):
        j = i // 2
        if i % 2 == 0:
            xp, xs, sp, gla_s = _gla_layer(xp, xs, state_gla, j, gla_s, norm_mix[i], gla_w_in[j], gla_w_gate2[j],
                                           gla_b_gate[j], gla_g_out[j], gla_w_out[j])
            gla_p.append(sp)
        else:
            xp, xs, nsa_p, rs = _nsa_layer(xp, xs, pools, (cache_k_win, cache_v_win), page_table, j, depth // 2,
                                           nsa_p, norm_mix[i], nsa_w_in[j], nsa_b_gate[j], nsa_g_q[j], nsa_g_k[j],
                                           nsa_g_kcmp[j], nsa_cmp_pe[j], nsa_cmp_w1[j], nsa_cmp_b1[j],
                                           nsa_cmp_w2[j], nsa_w_out[j])
            nsa_s.append(rs)
        g = norm_mlp[i].reshape(1, d)
        xp = _mlp(xp.reshape(b * t, d), g, wu, wd, i, tm=min(1024, b * t)).reshape(b, t, d)
        xs = _mlp(xs.reshape(bs, d), g, wu, wd, i, tm=bs).reshape(bs, 1, d)
    stack = lambda lst, r: jnp.stack([e[r] for e in lst])
    n_win = min(WINDOW, t)
    rows = lambda r, n: jnp.transpose(nsa_p[r][..., t - n:], (0, 1, 4, 2, 3))
    return (xp, xs, jnp.stack(gla_p), gla_s,
            rows(0, t), stack(nsa_s, 0), rows(1, t), stack(nsa_s, 1),
            rows(2, t), stack(nsa_s, 2), rows(3, t), stack(nsa_s, 3),
            rows(4, n_win), stack(nsa_s, 4), rows(5, n_win), stack(nsa_s, 5))
```

```python
import functools

import jax
import jax.numpy as jnp
from jax import lax
from jax.experimental import pallas as pl
from jax.experimental.pallas import tpu as pltpu

F32 = jnp.float32
BF16 = jnp.bfloat16
EPS = 1e-6
NEG = -1e30
VMEM_LIMIT_BYTES = 48 * 1024 * 1024
LANES = 128
PAGE = 128
HD = 64
KVH = 4
GRP = 4
L_CMP, CMP_STRIDE, L_SLC, N_SEL, WINDOW = 32, 16, 64, 16, 512
GLA_H, GLA_DK, GLA_DV, GLA_CHUNK = 4, 128, 256, 64
ROPE_THETA, ROT_HALF = 500000.0, 8
LOG2E = 1.4426950408889634
V_PAD = 16
KT = 512


def _params(*sem):
    return pltpu.CompilerParams(dimension_semantics=sem, vmem_limit_bytes=VMEM_LIMIT_BYTES)


def _dot(a, b):
    return jnp.dot(a, b, preferred_element_type=F32)


def _dot_nt(a, b):
    return lax.dot_general(a, b, (((1,), (1,)), ((), ())), preferred_element_type=F32)


def _dot_tn(a, b):
    return lax.dot_general(a, b, (((0,), (0,)), ((), ())), preferred_element_type=F32)


def _rms_rows(x, g):
    ms = jnp.mean(x * x, axis=-1, keepdims=True)
    return x * lax.rsqrt(ms + EPS) * g


def _mlp_kernel(x_ref, g_ref, wu_ref, wd_ref, o_ref, h_scr, acc_scr):
    f = pl.program_id(1)

    @pl.when(f == 0)
    def _():
        h_scr[...] = _rms_rows(x_ref[...], g_ref[...]).astype(BF16)
        acc_scr[...] = jnp.zeros_like(acc_scr)

    u = jnp.maximum(_dot(h_scr[...], wu_ref[...]), 0.0)
    acc_scr[...] += _dot((u * u).astype(BF16), wd_ref[...])

    @pl.when(f == pl.num_programs(1) - 1)
    def _():
        o_ref[...] = x_ref[...] + acc_scr[...]


def _mlp(x, g, wu, wd, layer, tm, tf=2048):
    m, d = x.shape
    ff = wu.shape[2]
    return pl.pallas_call(
        _mlp_kernel,
        grid=(m // tm, ff // tf),
        in_specs=[pl.BlockSpec((tm, d), lambda i, f: (i, 0)),
                  pl.BlockSpec((1, d), lambda i, f: (0, 0)),
                  pl.BlockSpec((None, d, tf), lambda i, f: (layer, 0, f)),
                  pl.BlockSpec((None, tf, d), lambda i, f: (layer, f, 0))],
        out_specs=pl.BlockSpec((tm, d), lambda i, f: (i, 0)),
        out_shape=jax.ShapeDtypeStruct((m, d), F32),
        scratch_shapes=[pltpu.VMEM((tm, d), BF16), pltpu.VMEM((tm, d), F32)],
        compiler_params=_params("parallel", "arbitrary"),
        name="mlp_block",
    )(x, g, wu, wd)


def _gla_proj_kernel(x_ref, g_ref, w_ref, wg2_ref, bg_ref, q_ref, k_ref, v_ref, r_ref, gl_ref):
    h = _rms_rows(x_ref[...], g_ref[...]).astype(BF16)
    dk, dv = GLA_H * GLA_DK, GLA_H * GLA_DV
    q_ref[...] = _dot(h, w_ref[:, 0:dk]) * (GLA_DK ** -0.5)
    k_ref[...] = _dot(h, w_ref[:, dk:2 * dk])
    v_ref[...] = _dot(h, w_ref[:, 2 * dk:2 * dk + dv])
    r_ref[...] = _dot(h, w_ref[:, 2 * dk + dv:2 * dk + 2 * dv])
    gr = _dot(h, w_ref[:, 2 * dk + 2 * dv:])
    xg = _dot(gr.astype(BF16), wg2_ref[...]) + bg_ref[...]
    gl_ref[...] = jax.nn.log_sigmoid(xg) * (1.0 / 16.0)


def _gla_proj(x, g, w, wg2, bg, tm):
    m, d = x.shape
    dk, dv = GLA_H * GLA_DK, GLA_H * GLA_DV
    row = lambda n: pl.BlockSpec((tm, n), lambda i: (i, 0))
    full = lambda a: pl.BlockSpec(a.shape, lambda i: (0, 0))
    return pl.pallas_call(
        _gla_proj_kernel,
        grid=(m // tm,),
        in_specs=[row(d), full(g), full(w), full(wg2), full(bg)],
        out_specs=[row(dk), row(dk), row(dv), row(dv), row(dk)],
        out_shape=[jax.ShapeDtypeStruct((m, n), F32) for n in (dk, dk, dv, dv, dk)],
        compiler_params=_params("parallel"),
        name="gla_proj",
    )(x, g, w, wg2, bg)


def _gla_scan_kernel(q_ref, k_ref, g_ref, v_ref, o_ref, s_ref, st_scr, *, n_chunks, n_heads):
    c_len = GLA_CHUNK
    step = pl.program_id(2)

    @pl.when(step == 0)
    def _():
        st_scr[...] = jnp.zeros_like(st_scr)

    rowi = lax.broadcasted_iota(jnp.int32, (c_len, GLA_DK), 0)
    causal = (lax.broadcasted_iota(jnp.int32, (c_len, c_len), 0)
              >= lax.broadcasted_iota(jnp.int32, (c_len, c_len), 1))

    def body(c, carry):
        sl = pl.ds(pl.multiple_of(c * c_len, c_len), c_len)
        for h in range(n_heads):
            ks = slice(h * GLA_DK, (h + 1) * GLA_DK)
            vs = slice(h * GLA_DV, (h + 1) * GLA_DV)
            b = g_ref[sl, ks]
            sh = 1
            while sh < c_len:
                b = b + jnp.where(rowi >= sh, pltpu.roll(b, sh, 0), 0.0)
                sh *= 2
            b_last = b[c_len - 1:c_len, :]
            b_mid = b[c_len // 2 - 1:c_len // 2, :]
            q = q_ref[sl, ks]
            k = k_ref[sl, ks]
            v = v_ref[sl, vs].astype(BF16)
            qe = (q * jnp.exp(b)).astype(BF16)
            qa = (q * jnp.exp(b - b_mid)).astype(BF16)
            ka = (k * jnp.exp(b_mid - b)).astype(BF16)
            kd = (k * jnp.exp(b_last - b)).astype(BF16)
            a = jnp.where(causal, _dot_nt(qa, ka), 0.0)
            st = st_scr[h]
            o_ref[sl, vs] = _dot_nt(qe, st.astype(BF16)) + _dot(a.astype(BF16), v)
            st_scr[h] = st * jnp.exp(b_last) + _dot_tn(v, kd)
        return carry

    lax.fori_loop(0, n_chunks, body, 0)

    @pl.when(step == pl.num_programs(2) - 1)
    def _():
        for h in range(n_heads):
            s_ref[h] = st_scr[h].T


def _gla_scan(q, k, gl, v, n_heads=4, t_blk=1024):
    b, t, _ = q.shape
    t_blk = min(t_blk, t)
    kq = pl.BlockSpec((None, t_blk, n_heads * GLA_DK), lambda i, h, j: (i, j, h))
    vv = pl.BlockSpec((None, t_blk, n_heads * GLA_DV), lambda i, h, j: (i, j, h))
    return pl.pallas_call(
        functools.partial(_gla_scan_kernel, n_chunks=t_blk // GLA_CHUNK, n_heads=n_heads),
        grid=(b, GLA_H // n_heads, t // t_blk),
        in_specs=[kq, kq, kq, vv],
        out_specs=[vv, pl.BlockSpec((None, n_heads, GLA_DK, GLA_DV), lambda i, h, j: (i, h, 0, 0))],
        out_shape=[jax.ShapeDtypeStruct((b, t, GLA_H * GLA_DV), F32),
                   jax.ShapeDtypeStruct((b, GLA_H, GLA_DK, GLA_DV), F32)],
        scratch_shapes=[pltpu.VMEM((n_heads, GLA_DV, GLA_DK), F32)],
        compiler_params=_params("parallel", "parallel", "arbitrary"),
        name="gla_scan",
    )(q, k, gl, v)


def _gla_step_kernel(q_ref, k_ref, g_ref, v_ref, s0_ref, o_ref, s_ref):
    def col(x):
        return jnp.broadcast_to(x, (LANES, LANES)).T

    for h in range(GLA_H):
        ks = slice(h * GLA_DK, (h + 1) * GLA_DK)
        qc, kc, ec = col(q_ref[:, ks]), col(k_ref[:, ks]), col(jnp.exp(g_ref[:, ks]))
        for half in range(GLA_DV // LANES):
            vs = slice(h * GLA_DV + half * LANES, h * GLA_DV + (half + 1) * LANES)
            ss = slice(half * LANES, (half + 1) * LANES)
            sn = ec * s0_ref[h, :, ss] + kc * v_ref[:, vs]
            s_ref[h, :, ss] = sn
            o_ref[:, vs] = jnp.sum(qc * sn, axis=0, keepdims=True)


def _gla_step_aliased_kernel(q_ref, k_ref, g_ref, v_ref, s0_ref, prev_ref, o_ref, s_ref):
    del prev_ref
    _gla_step_kernel(q_ref, k_ref, g_ref, v_ref, s0_ref, o_ref, s_ref)


def _gla_step(q, k, gl, v, s0, layer, prev=None):
    b = q.shape[0]
    kq = pl.BlockSpec((None, 1, GLA_H * GLA_DK), lambda i: (i, 0, 0))
    vv = pl.BlockSpec((None, 1, GLA_H * GLA_DV), lambda i: (i, 0, 0))
    st = pl.BlockSpec((None, None, GLA_H, GLA_DK, GLA_DV), lambda i: (layer, i, 0, 0, 0))
    extra = [] if prev is None else [prev]
    return pl.pallas_call(
        _gla_step_kernel if prev is None else _gla_step_aliased_kernel,
        grid=(b,),
        in_specs=[kq, kq, kq, vv, st] + [pl.BlockSpec(memory_space=pl.ANY)] * len(extra),
        out_specs=[vv, st],
        out_shape=[jax.ShapeDtypeStruct((b, 1, GLA_H * GLA_DV), F32), jax.ShapeDtypeStruct(s0.shape, F32)],
        input_output_aliases={5: 1} if extra else {},
        compiler_params=_params("parallel"),
        name="gla_step",
    )(q, k, gl, v, s0, *extra)


def _gla_out_kernel(o_ref, r_ref, x_ref, go_ref, w_ref, y_ref):
    parts = []
    for h in range(GLA_H):
        sl = slice(h * GLA_DV, (h + 1) * GLA_DV)
        r = r_ref[:, sl]
        parts.append((_rms_rows(o_ref[:, sl], go_ref[...]) * (r * jax.nn.sigmoid(r))).astype(BF16))
    y_ref[...] = x_ref[...] + _dot(jnp.concatenate(parts, axis=1), w_ref[...])


def _gla_out(o, r, x, go, w, tm):
    m, d = x.shape
    row = pl.BlockSpec((tm, d), lambda i: (i, 0))
    full = lambda a: pl.BlockSpec(a.shape, lambda i: (0, 0))
    return pl.pallas_call(
        _gla_out_kernel,
        grid=(m // tm,),
        in_specs=[row, row, row, full(go), full(w)],
        out_specs=row,
        out_shape=jax.ShapeDtypeStruct((m, d), F32),
        compiler_params=_params("parallel"),
        name="gla_out",
    )(o, r, x, go, w)


def _gla_layer(xp, xs, s0, layer, prev_state, norm_g, w_in, w_gate2, b_gate, g_out, w_out):
    b, t, d = xp.shape
    bs = xs.shape[0]
    dk, dv = GLA_H * GLA_DK, GLA_H * GLA_DV
    rank = w_gate2.shape[0]
    w = jnp.pad(w_in, ((0, 0), (0, LANES - rank))).astype(BF16)
    wg2 = jnp.pad(w_gate2, ((0, LANES - rank), (0, 0))).astype(BF16)
    g = norm_g.reshape(1, d)
    bg = b_gate.reshape(1, dk)
    go = g_out.reshape(1, GLA_DV)
    wo = w_out.astype(BF16)
    x2 = xp.reshape(b * t, d)
    q, k, v, r, gl = _gla_proj(x2, g, w, wg2, bg, tm=512)
    o, sp = _gla_scan(q.reshape(b, t, dk), k.reshape(b, t, dk), gl.reshape(b, t, dk), v.reshape(b, t, dv))
    yp = _gla_out(o.reshape(b * t, dv), r, x2, go, wo, tm=1024).reshape(b, t, d)
    xs2 = xs.reshape(bs, d)
    q, k, v, r, gl = _gla_proj(xs2, g, w, wg2, bg, tm=bs)
    o, ss = _gla_step(q.reshape(bs, 1, dk), k.reshape(bs, 1, dk), gl.reshape(bs, 1, dk), v.reshape(bs, 1, dv), s0,
                      layer, prev_state)
    ys = _gla_out(o.reshape(bs, dv), r, xs2, go, wo, tm=bs).reshape(bs, 1, d)
    return yp, ys, sp, ss


def _nsa_proj_kernel(x_ref, gn_ref, wt_ref, gq_ref, gk_ref, bg_ref, cos_ref, sin_ref, *rest):
    (q_ref, kc_ref, vc_ref, ks_ref, vs_ref, kw_ref, vw_ref, gt_ref,
     ksa_ref, vsa_ref, kwb_ref, vwa_ref) = rest[-12:]
    tm = x_ref.shape[0]
    nq = GRP * KVH * HD
    nkv = KVH * HD
    h = _rms_rows(x_ref[...], gn_ref[...]).astype(BF16)
    cos = cos_ref[...][None]
    sin = sin_ref[...][None]

    def norm_rope(z, g, nh):
        z3 = z.reshape(nh, HD, tm)
        y = z3 * lax.rsqrt(jnp.mean(z3 * z3, axis=1, keepdims=True) + EPS) * g[None]
        x1 = y[:, 0:ROT_HALF, :]
        x2 = y[:, ROT_HALF:2 * ROT_HALF, :]
        return jnp.concatenate([x1 * cos - x2 * sin, x1 * sin + x2 * cos, y[:, 2 * ROT_HALF:, :]], axis=1)

    zq = _dot_nt(wt_ref[0:nq, :], h)
    q_ref[...] = (norm_rope(zq, gq_ref[...], GRP * KVH) * (HD ** -0.5 * LOG2E)).astype(BF16)
    zkv = _dot_nt(wt_ref[nq:nq + 6 * nkv, :], h)
    outs = (kc_ref, vc_ref, ks_ref, vs_ref, kw_ref, vw_ref)
    rows = []
    for i in range(6):
        z = zkv[i * nkv:(i + 1) * nkv, :]
        rows.append(norm_rope(z, gk_ref[i // 2], KVH) if i % 2 == 0 else z.reshape(KVH, HD, tm))
        outs[i][...] = rows[i]
    zg = _dot_nt(wt_ref[nq + 6 * nkv:, :], h) + bg_ref[...]
    gt_ref[...] = jax.nn.sigmoid(zg).reshape(KVH, 16, tm)

    nsp = ksa_ref.shape[1] - HD
    blk = lax.broadcasted_iota(jnp.int32, (KVH, nsp, tm), 1)
    tok = pl.program_id(1) * tm + lax.broadcasted_iota(jnp.int32, (KVH, nsp, tm), 2)
    onehot = jnp.where(tok // L_SLC == blk, 1.0, 0.0)
    ksa_ref[...] = jnp.concatenate([rows[2], onehot], axis=1).astype(BF16)
    kwb_ref[...] = rows[4].astype(BF16)
    ones = jnp.where(lax.broadcasted_iota(jnp.int32, (KVH, V_PAD, tm), 1) == 0, 1.0, 0.0)
    vsa_ref[...] = jnp.concatenate([rows[3], ones], axis=1).astype(BF16)
    vwa_ref[...] = jnp.concatenate([rows[5], ones], axis=1).astype(BF16)


def _sel_rows(t):
    return -(-(-(-t // L_SLC)) // 16) * 16


def _nsa_proj(x, gn, wt, gq, gk, bg, cos, sin, tm, layer=0, n_layers=1, prev=None):
    b, t, d = x.shape
    full = lambda a: pl.BlockSpec(a.shape, lambda i, j: (0,) * a.ndim)
    rows_spec = lambda n: pl.BlockSpec((None, KVH, n, tm), lambda i, j: (i, 0, 0, j))
    rows_shape = lambda n, dt: jax.ShapeDtypeStruct((b, KVH, n, t), dt)
    kv_spec = pl.BlockSpec((None, None, KVH, HD, tm), lambda i, j: (layer, i, 0, 0, j))
    kv_shape = jax.ShapeDtypeStruct((n_layers, b, KVH, HD, t), F32)
    aug = [HD + _sel_rows(t), HD + V_PAD, HD, HD + V_PAD]
    prev = list(prev) if prev is not None else []
    n_in = 8
    return pl.pallas_call(
        _nsa_proj_kernel,
        grid=(b, t // tm),
        in_specs=[pl.BlockSpec((None, tm, d), lambda i, j: (i, j, 0)), full(gn), full(wt), full(gq), full(gk),
                  full(bg), pl.BlockSpec((ROT_HALF, tm), lambda i, j: (0, j)),
                  pl.BlockSpec((ROT_HALF, tm), lambda i, j: (0, j))]
        + [pl.BlockSpec(memory_space=pl.ANY)] * len(prev),
        out_specs=[pl.BlockSpec((None, GRP * KVH, HD, tm), lambda i, j: (i, 0, 0, j))] + [kv_spec] * 6
        + [rows_spec(16)] + [rows_spec(n) for n in aug],
        out_shape=[jax.ShapeDtypeStruct((b, GRP * KVH, HD, t), BF16)] + [kv_shape] * 6
        + [rows_shape(16, F32)] + [rows_shape(n, BF16) for n in aug],
        input_output_aliases={n_in + k: 1 + k for k in range(len(prev))},
        compiler_params=_params("parallel", "parallel"),
        name="nsa_proj",
    )(x, gn, wt, gq, gk, bg, cos, sin, *prev)


def _nsa_out_kernel(ot_ref, x_ref, w_ref, y_ref):
    ot = ot_ref[...].reshape(GRP * KVH * HD, ot_ref.shape[-1])
    y_ref[...] = x_ref[...] + _dot_tn(ot, w_ref[...])


def _nsa_out(ot, x, w, tm):
    b, t, d = x.shape
    return pl.pallas_call(
        _nsa_out_kernel,
        grid=(b, t // tm),
        in_specs=[pl.BlockSpec((None, GRP * KVH, HD, tm), lambda i, j: (i, 0, 0, j)),
                  pl.BlockSpec((None, tm, d), lambda i, j: (i, j, 0)),
                  pl.BlockSpec(w.shape, lambda i, j: (0, 0))],
        out_specs=pl.BlockSpec((None, tm, d), lambda i, j: (i, j, 0)),
        out_shape=jax.ShapeDtypeStruct((b, t, d), F32),
        compiler_params=_params("parallel", "parallel"),
        name="nsa_out",
    )(ot, x, w)


def _compress_kernel(*refs, n_pages, paged, is_key):
    refs = refs[1:] if paged else refs
    pages = refs[:n_pages]
    perm_ref, w_ref, b1_ref, pe_ref, w1_ref, w2t_ref, gk_ref, o_ref, lhs_scr = refs[n_pages:]
    p = pl.program_id(1)
    half = CMP_STRIDE
    n_half = PAGE // half
    low = lax.broadcasted_iota(jnp.int32, (n_half, LANES), 1) < HD
    odd_slot = (lax.broadcasted_iota(jnp.int32, (2 * HD, PAGE), 1) // n_half) % 2 == 1
    for pair in range(KVH // 2):
        for j in range(n_pages):
            y = _dot(pages[j][2 * pair:2 * pair + 2].reshape(2 * HD, PAGE).astype(BF16), perm_ref[...])
            x = jnp.where(odd_slot, jnp.concatenate([y[HD:], y[:HD]], axis=0), y).T
            rows = pl.ds(pl.multiple_of((p * n_pages + j) * n_half, n_half), n_half)
            for t in range(half // 2):
                ev = x[2 * t * n_half:(2 * t + 1) * n_half]
                od = x[(2 * t + 1) * n_half:(2 * t + 2) * n_half]
                lanes = slice(t * LANES, (t + 1) * LANES)
                lhs_scr[2 * pair, rows, lanes] = jnp.where(low, ev, od)
                lhs_scr[2 * pair + 1, rows, lanes] = jnp.where(low, od, ev)

    @pl.when(p == pl.num_programs(1) - 1)
    def _():
        nh = lhs_scr.shape[1]
        hidden = b1_ref.shape[1]
        c = _dot(pe_ref[0].astype(BF16), w1_ref[0]) + _dot(pe_ref[1].astype(BF16), w1_ref[1])
        bias = c[0:1, :] + b1_ref[...]
        for kvh in range(KVH):
            fs = _dot(lhs_scr[kvh].astype(BF16), w_ref[kvh % 2])
            hid = jax.nn.gelu(fs[:, :hidden] + pltpu.roll(fs[:, hidden:], nh - 1, 0) + bias)
            yt = _dot_nt(w2t_ref[...], hid.astype(BF16))
            if is_key:
                yt = yt * lax.rsqrt(jnp.mean(yt * yt, axis=0, keepdims=True) + EPS) * gk_ref[...]
            o_ref[kvh] = yt.astype(BF16)


def _compress(src, weights, n_pages_total, pages_per_step, is_key, page_table=None, layer=0):
    paged = page_table is not None
    b = page_table.shape[0] if paged else src.shape[1]
    n_half = PAGE // CMP_STRIDE
    nh = n_pages_total * n_half
    steps = n_pages_total // pages_per_step
    tok = jnp.arange(PAGE)
    perm = (((tok % CMP_STRIDE) * n_half + tok // CMP_STRIDE)[:, None] == tok[None, :]).astype(BF16)
    if paged:
        page_spec = lambda j: pl.BlockSpec(
            (None, None, KVH, HD, PAGE), lambda i, p, pt: (layer, pt[i, p * pages_per_step + j], 0, 0, 0))
        full = lambda a: pl.BlockSpec(a.shape, lambda i, p, pt: (0,) * a.ndim)
        o_spec = pl.BlockSpec((None, KVH, HD, nh), lambda i, p, pt: (i, 0, 0, 0))
    else:
        page_spec = lambda j: pl.BlockSpec((None, None, KVH, HD, PAGE),
                                           lambda i, p: (layer, i, 0, 0, p * pages_per_step + j))
        full = lambda a: pl.BlockSpec(a.shape, lambda i, p: (0,) * a.ndim)
        o_spec = pl.BlockSpec((None, KVH, HD, nh), lambda i, p: (i, 0, 0, 0))
    consts = [perm] + list(weights)
    grid_spec = pltpu.PrefetchScalarGridSpec(
        num_scalar_prefetch=1 if paged else 0,
        grid=(b, steps),
        in_specs=[page_spec(j) for j in range(pages_per_step)] + [full(a) for a in consts],
        out_specs=o_spec,
        scratch_shapes=[pltpu.VMEM((KVH, nh, CMP_STRIDE * HD), F32)],
    )
    args = ([page_table] if paged else []) + [src] * pages_per_step + consts
    return pl.pallas_call(
        functools.partial(_compress_kernel, n_pages=pages_per_step, paged=paged, is_key=is_key),
        grid_spec=grid_spec,
        out_shape=jax.ShapeDtypeStruct((b, KVH, HD, nh), BF16),
        compiler_params=_params("parallel", "arbitrary"),
        name="nsa_compress",
    )(*args)


def _cmp_weights(pe, w1, b1, w2, g_kcmp, nh):
    hidden = w1.shape[-1]
    wfs = jnp.concatenate([w1[:CMP_STRIDE], w1[CMP_STRIDE:]], axis=-1)
    swapped = wfs.reshape(CMP_STRIDE // 2, 2, HD, 2 * hidden)[:, ::-1]
    w_pair = jnp.stack([wfs, swapped.reshape(wfs.shape)]).reshape(2, CMP_STRIDE * HD, 2 * hidden).astype(BF16)
    pe2 = jnp.zeros((2, 8, CMP_STRIDE * HD), F32).at[:, 0, :].set(pe.reshape(2, CMP_STRIDE * HD))
    w1f = w1.reshape(2, CMP_STRIDE * HD, hidden).astype(BF16)
    gk = jnp.broadcast_to(g_kcmp.reshape(HD, 1), (HD, nh))
    return w_pair, b1.reshape(1, hidden), pe2, w1f, w2.T.astype(BF16), gk


def _softmax_update(s, mask, vt, m, l, acc):
    m_new = jnp.maximum(m, jnp.max(jnp.where(mask, s, NEG), axis=0, keepdims=True))
    alpha = jnp.exp2(m - m_new)
    p = jnp.where(mask, jnp.exp2(s - m_new), 0.0)
    l_new = l * alpha + jnp.sum(p, axis=0, keepdims=True)
    acc_new = acc * alpha + _dot(vt, p.astype(BF16))
    return m_new, l_new, acc_new, p


def _softmax_init(lanes):
    return jnp.full((1, lanes), NEG, F32), jnp.zeros((1, lanes), F32), jnp.zeros((HD, lanes), F32)


def _finish(l, acc):
    return acc / jnp.maximum(l, 1e-30)


def _split_dot(mt, x):
    hi = x.astype(BF16)
    lo = (x - hi.astype(F32)).astype(BF16)
    return _dot(mt, hi) + _dot(mt, lo)


def _attn_prompt_kernel(q_ref, kc_ref, vc_ref, ksa_ref, vsa_ref, kwb_ref, vwa_ref, gt_ref, mt_ref, o_ref,
                        s_scr, s2_scr, *, tq, n_cmp, n_sel):
    i = pl.program_id(2)
    lanes = GRP * tq
    ncp = kc_ref.shape[1]
    nsp = mt_ref.shape[0]
    qt = jnp.concatenate([q_ref[h] for h in range(GRP)], axis=1)

    def col_max(mx, s):
        return jnp.maximum(mx, jnp.max(s.reshape(s.shape[0] // 8, 8, lanes), axis=0))

    def finish(acc):
        return acc[0:HD] / jnp.maximum(acc[HD:HD + 1], 1e-30)

    tpos1 = i * tq + lax.broadcasted_iota(jnp.int32, (1, tq), 1)
    tpos = jnp.concatenate([tpos1] * GRP, axis=1)

    s = _dot_tn(kc_ref[...], qt)
    n_idx = lax.broadcasted_iota(jnp.int32, (ncp, lanes), 0)
    mask = (n_idx * CMP_STRIDE + (L_CMP - 1) <= tpos) & (n_idx < n_cmp)
    m, l, acc, p = _softmax_update(s, mask, vc_ref[...], *_softmax_init(lanes))
    o_c = _finish(l, acc)
    p = p / jnp.maximum(l, 1e-30)
    imp = p[:, 0:tq]
    for h in range(1, GRP):
        imp = imp + p[:, h * tq:(h + 1) * tq]
    p_slc = _split_dot(mt_ref[...], imp)

    j_idx = lax.broadcasted_iota(jnp.int32, (nsp, tq), 0)
    cur = tpos1 // L_SLC
    valid = j_idx <= cur
    forced = (j_idx == 0) | (j_idx == cur) | (j_idx == cur - 1)
    score = jnp.where(valid & forced, jnp.inf, jnp.where(valid, p_slc, -jnp.inf))
    bits = lax.bitcast_convert_type(score, jnp.int32)
    key = jnp.where(bits >= 0, bits, bits ^ 0x7FFFFFFF)
    n_grp = nsp // 8
    keys = [key[8 * r:8 * r + 8] for r in range(n_grp)]
    keys_m1 = [k - 1 for k in keys]
    sub = lax.broadcasted_iota(jnp.int32, (8, tq), 0)

    def count_group(grp, ranks):
        ranks = list(ranks)
        for u in range(8):
            row = jnp.broadcast_to(keys[grp][u:u + 1, :], (8, tq))
            for r in range(n_grp):
                thr = keys[r] if r < grp else keys_m1[r] if r > grp else jnp.where(sub > u, keys_m1[r], keys[r])
                ranks[r] = ranks[r] + jnp.where(row > thr, 1.0, 0.0)
        return tuple(ranks)

    ranks = tuple(jnp.zeros((8, tq), F32) for _ in range(n_grp))
    last_valid = (i * tq + tq - 1) // L_SLC
    for grp in range(n_grp):
        ranks = lax.cond(8 * grp <= last_valid, functools.partial(count_group, grp), lambda r: r, ranks)
    bias = jnp.where(jnp.concatenate(ranks, axis=0) < n_sel, 0.0, NEG)
    qa = jnp.concatenate([qt, jnp.concatenate([bias] * GRP, axis=1).astype(BF16)], axis=0)

    row_k = lax.broadcasted_iota(jnp.int32, (KT, lanes), 0)

    def slc_scores(st, buf):
        start = pl.multiple_of(st * KT, KT)
        buf[0:KT, :] = _dot_tn(ksa_ref[:, pl.ds(start, KT)], qa)

    def slc_softmax(st, buf, carry, diagonal=False):
        m, acc = carry
        s = buf[0:KT, :]
        if diagonal:
            s = jnp.where(st * KT + row_k <= tpos, s, NEG)
        m_new = jnp.maximum(m, jnp.max(col_max(jnp.full((8, lanes), NEG, F32), s), axis=0, keepdims=True))
        p = jnp.exp2(s - m_new).astype(BF16)
        pv = _dot(vsa_ref[:, pl.ds(pl.multiple_of(st * KT, KT), KT)], p)
        return m_new, acc * jnp.exp2(m - m_new) + pv

    def slc_pair(u, carry):
        slc_scores(2 * u + 1, s2_scr)
        carry = slc_softmax(2 * u, s_scr, carry)
        slc_scores(2 * u + 2, s_scr)
        return slc_softmax(2 * u + 1, s2_scr, carry)

    def slc_tail_odd(carry):
        slc_scores(n_full, s2_scr)
        carry = slc_softmax(n_full - 1, s_scr, carry)
        return slc_softmax(n_full, s2_scr, carry, diagonal=True)

    def slc_tail_even(carry):
        return slc_softmax(n_full, s_scr, carry, diagonal=True)

    n_full = (i * tq) // KT
    slc_scores(0, s_scr)
    init = (jnp.full((1, lanes), NEG, F32), jnp.zeros((HD + V_PAD, lanes), F32))
    carry = lax.fori_loop(0, n_full // 2, slc_pair, init)
    o_s = finish(lax.cond(n_full % 2 == 1, slc_tail_odd, slc_tail_even, carry)[1])

    n_wt = WINDOW // tq + 1
    w0 = jnp.maximum(i - (n_wt - 1), 0) * tq
    row_w = lax.broadcasted_iota(jnp.int32, (tq, lanes), 0)
    col_w = jnp.concatenate([lax.broadcasted_iota(jnp.int32, (tq, tq), 1)] * GRP, axis=1)

    def window(steady):
        mx = jnp.full((8, lanes), NEG, F32)
        for r in range(n_wt):
            start = pl.multiple_of(w0 + r * tq, tq)
            s = _dot_tn(kwb_ref[:, pl.ds(start, tq)], qt)
            if not steady:
                rel = tpos - (start + row_w)
                s = jnp.where((rel >= 0) & (rel <= WINDOW), s, NEG)
            elif r == 0:
                s = jnp.where(row_w >= col_w, s, NEG)
            elif r == n_wt - 1:
                s = jnp.where(row_w <= col_w, s, NEG)
            s_scr[r * tq:(r + 1) * tq, :] = s
            mx = col_max(mx, s)
        m = jnp.max(mx, axis=0, keepdims=True)
        acc = jnp.zeros((HD + V_PAD, lanes), F32)
        for r in range(n_wt):
            start = pl.multiple_of(w0 + r * tq, tq)
            p = jnp.exp2(s_scr[r * tq:(r + 1) * tq, :] - m).astype(BF16)
            acc = acc + _dot(vwa_ref[:, pl.ds(start, tq)], p)
        return finish(acc)

    o_w = lax.cond(i >= n_wt - 1, functools.partial(window, True), functools.partial(window, False))

    for h in range(GRP):
        ls = slice(h * tq, (h + 1) * tq)
        o = (gt_ref[3 * h:3 * h + 1, :] * o_c[:, ls] + gt_ref[3 * h + 1:3 * h + 2, :] * o_s[:, ls]
             + gt_ref[3 * h + 2:3 * h + 3, :] * o_w[:, ls])
        o_ref[h] = o.astype(BF16)


def _slc_matrix(nsp, ncp, n_cmp):
    ratio = L_SLC // CMP_STRIDE
    j = jnp.arange(nsp)[:, None]
    n = jnp.arange(ncp)[None, :]
    m = ((n >= ratio * j) & (n <= ratio * j + ratio - 1)).astype(F32)
    m = m + ((n >= ratio * j - 1) & (n <= ratio * j + ratio - 2)).astype(F32)
    return jnp.where(n < n_cmp, m, 0.0).astype(BF16)


def _attn_prompt(qt, kct, vct, ksa, vsa, kwb, vwa, gt, tq=4 * LANES):
    b, _, _, t = qt.shape
    assert t % KT == 0 and WINDOW % tq == 0 and t >= WINDOW + tq
    ncp = kct.shape[-1]
    n_cmp = t // CMP_STRIDE - 1
    ns = -(-t // L_SLC)
    mt = _slc_matrix(_sel_rows(t), ncp, n_cmp)
    res = lambda a: pl.BlockSpec((None, None) + a.shape[2:], lambda i, g, j: (i, g, 0, 0))
    return pl.pallas_call(
        functools.partial(_attn_prompt_kernel, tq=tq, n_cmp=n_cmp, n_sel=min(N_SEL, ns)),
        grid=(b, KVH, t // tq),
        in_specs=[pl.BlockSpec((None, GRP, HD, tq), lambda i, g, j: (i, g, 0, j)),
                  res(kct), res(vct), res(ksa), res(vsa), res(kwb), res(vwa),
                  pl.BlockSpec((None, None, 16, tq), lambda i, g, j: (i, g, 0, j)),
                  pl.BlockSpec(mt.shape, lambda i, g, j: (0, 0))],
        out_specs=pl.BlockSpec((None, GRP, HD, tq), lambda i, g, j: (i, g, 0, j)),
        out_shape=jax.ShapeDtypeStruct(qt.shape, BF16),
        scratch_shapes=[pltpu.VMEM((max(KT, WINDOW + tq), GRP * tq), F32), pltpu.VMEM((KT, GRP * tq), F32)],
        compiler_params=_params("parallel", "parallel", "arbitrary"),
        name="nsa_attn_prompt",
    )(qt, kct, vct, ksa, vsa, kwb, vwa, gt, mt)


def _split_dot_rows(x, mt):
    hi = x.astype(BF16)
    lo = (x - hi.astype(F32)).astype(BF16)
    return _dot(hi, mt) + _dot(lo, mt)


def _attn_sample_rows_kernel(*refs, n_pages, n_cmp, n_blocks, n_sel):
    refs = refs[1:]
    k_pages = refs[:n_pages]
    v_pages = refs[n_pages:2 * n_pages]
    (q_ref, kc_ref, vc_ref, kw_ref, vw_ref, kns_ref, vns_ref, knw_ref, vnw_ref, gt_ref, mtt_ref, exp_ref,
     o_ref, bias_scr, oc_scr, ow_scr, m_scr, l_scr, acc_scr) = refs[2 * n_pages:]
    p = pl.program_id(1)
    ncp = kc_ref.shape[-1]
    nsp = mtt_ref.shape[1]
    n_q = KVH * GRP
    rows_all = KVH * HD
    q = q_ref[...]

    def stacked(ref):
        return ref[...].reshape(rows_all, ref.shape[-1]).astype(BF16)

    def update(s, valid, vt, m, l, acc):
        if valid is not None:
            s = jnp.where(valid, s, NEG)
        m_new = jnp.maximum(m, jnp.max(s, axis=1, keepdims=True))
        alpha = jnp.exp2(m - m_new)
        pr = jnp.exp2(s - m_new)
        if valid is not None:
            pr = jnp.where(valid, pr, 0.0)
        l_new = l * alpha + jnp.sum(pr, axis=1, keepdims=True)
        return m_new, l_new, acc * alpha + _dot_nt(pr.astype(BF16), vt), pr

    def init():
        return jnp.full((n_q, 1), NEG, F32), jnp.zeros((n_q, 1), F32), jnp.zeros((n_q, rows_all), F32)

    def save(m, l, acc):
        m_scr[...] = jnp.broadcast_to(m, m_scr.shape)
        l_scr[...] = jnp.broadcast_to(l, l_scr.shape)
        acc_scr[...] = acc

    @pl.when(p == 0)
    def _():
        cur = n_blocks - 1
        row_q = lax.broadcasted_iota(jnp.int32, (n_q, ncp), 0)
        row_b = lax.broadcasted_iota(jnp.int32, (n_q, nsp), 0)
        j_row = lax.broadcasted_iota(jnp.int32, (1, nsp), 1)
        jp_idx = lax.broadcasted_iota(jnp.int32, (nsp, nsp), 0)
        j_idx = lax.broadcasted_iota(jnp.int32, (nsp, nsp), 1)
        s = _dot(q, stacked(kc_ref))
        valid = lax.broadcasted_iota(jnp.int32, (n_q, ncp), 1) < n_cmp
        m, l, acc, pr = update(s, valid, stacked(vc_ref), *init())
        oc_scr[...] = acc / jnp.maximum(l, 1e-30)
        pr = pr / jnp.maximum(l, 1e-30)
        bias_blocks = jnp.zeros((n_q, nsp), F32)
        for g in range(KVH):
            imp = jnp.sum(jnp.where(row_q // GRP == g, pr, 0.0), axis=0, keepdims=True)
            p_slc = _split_dot_rows(jnp.broadcast_to(imp, (8, ncp)), mtt_ref[...])[0:1]
            ok = j_row <= cur
            forced = (j_row == 0) | (j_row == cur) | (j_row == cur - 1)
            score = jnp.where(ok & forced, jnp.inf, jnp.where(ok, p_slc, -jnp.inf))
            row = jnp.broadcast_to(score, (nsp, nsp))
            col = row.T
            ahead = (col > row) | ((col == row) & (jp_idx < j_idx))
            rank = jnp.sum(jnp.where(ahead, 1.0, 0.0), axis=0, keepdims=True)
            bias = jnp.broadcast_to(jnp.where(rank < n_sel, 0.0, NEG), (n_q, nsp))
            bias_blocks = jnp.where(row_b // GRP == g, bias, bias_blocks)
        bias_scr[...] = _dot(bias_blocks.astype(BF16), exp_ref[...])
        n_win = kw_ref.shape[-1]
        kt = jnp.concatenate([stacked(kw_ref), stacked(knw_ref)], axis=1)
        vt = jnp.concatenate([stacked(vw_ref), stacked(vnw_ref)], axis=1)
        valid = lax.broadcasted_iota(jnp.int32, (n_q, n_win + LANES), 1) <= n_win
        _, l, acc, _ = update(_dot(q, kt), valid, vt, *init())
        ow_scr[...] = acc / jnp.maximum(l, 1e-30)
        save(*init())

    n_keys = n_pages * PAGE
    kt = jnp.concatenate([k_pages[j][...].reshape(rows_all, PAGE) for j in range(n_pages)], axis=1).astype(BF16)
    vt = jnp.concatenate([v_pages[j][...].reshape(rows_all, PAGE) for j in range(n_pages)], axis=1).astype(BF16)
    s = _dot(q, kt) + bias_scr[:, pl.ds(pl.multiple_of(p * n_keys, n_keys), n_keys)]
    save(*update(s, None, vt, m_scr[:, 0:1], l_scr[:, 0:1], acc_scr[...])[:3])

    @pl.when(p == pl.num_programs(1) - 1)
    def _():
        valid = lax.broadcasted_iota(jnp.int32, (n_q, LANES), 1) == 0
        _, l, acc, _ = update(_dot(q, stacked(kns_ref)), valid, stacked(vns_ref),
                              m_scr[:, 0:1], l_scr[:, 0:1], acc_scr[...])
        o_ref[...] = (gt_ref[:, 0:1] * oc_scr[...] + gt_ref[:, 1:2] * (acc / jnp.maximum(l, 1e-30))
                      + gt_ref[:, 2:3] * ow_scr[...])


def _attn_sample(qs, kct, vct, pool_k, pool_v, page_table, layer, kwt, vwt, kns, vns, knw, vnw, gs,
                 pages_per_step=64):
    b, n_pages_total = page_table.shape
    pages_per_step = min(pages_per_step, n_pages_total)
    past = n_pages_total * PAGE
    n_cmp = past // CMP_STRIDE - 1
    ncp = kct.shape[-1]
    n_blocks = past // L_SLC + 1
    nsp = -(-n_blocks // LANES) * LANES
    mtt = _slc_matrix(nsp, ncp, n_cmp).T
    expand = (jnp.arange(nsp)[:, None] == jnp.arange(past)[None, :] // L_SLC).astype(BF16)
    n_q, rows_all = KVH * GRP, KVH * HD
    steps = n_pages_total // pages_per_step
    page_spec = lambda j: pl.BlockSpec((None, None, KVH, HD, PAGE),
                                       lambda i, p, pt: (layer, pt[i, p * pages_per_step + j], 0, 0, 0))
    def per_b(a):
        if a.ndim == 5:
            return pl.BlockSpec((None, None) + a.shape[2:], lambda i, p, pt: (layer, i, 0, 0, 0))
        return pl.BlockSpec((None,) + a.shape[1:], lambda i, p, pt: (i,) + (0,) * (a.ndim - 1))

    small = [qs, kct, vct, kwt, vwt, kns, vns, knw, vnw, gs]
    grid_spec = pltpu.PrefetchScalarGridSpec(
        num_scalar_prefetch=1,
        grid=(b, steps),
        in_specs=[page_spec(j) for j in range(pages_per_step)] * 2 + [per_b(a) for a in small]
        + [pl.BlockSpec(a.shape, lambda i, p, pt: (0, 0)) for a in (mtt, expand)],
        out_specs=pl.BlockSpec((None, n_q, rows_all), lambda i, p, pt: (i, 0, 0)),
        scratch_shapes=[pltpu.VMEM((n_q, past), F32), pltpu.VMEM((n_q, rows_all), F32),
                        pltpu.VMEM((n_q, rows_all), F32), pltpu.VMEM((n_q, LANES), F32),
                        pltpu.VMEM((n_q, LANES), F32), pltpu.VMEM((n_q, rows_all), F32)],
    )
    return pl.pallas_call(
        functools.partial(_attn_sample_rows_kernel, n_pages=pages_per_step, n_cmp=n_cmp, n_blocks=n_blocks,
                          n_sel=min(N_SEL, n_blocks)),
        grid_spec=grid_spec,
        out_shape=jax.ShapeDtypeStruct((b, n_q, rows_all), F32),
        compiler_params=_params("parallel", "arbitrary"),
        name="nsa_attn_sample",
    )(page_table, *([pool_k] * pages_per_step), *([pool_v] * pages_per_step), *small, mtt, expand)


def _rope_tables(pos):
    inv_freq = ROPE_THETA ** (-jnp.arange(ROT_HALF, dtype=F32) * 2.0 / (2 * ROT_HALF))
    ang = pos.astype(F32)[:, None] * inv_freq[None, :]
    return jnp.cos(ang).T, jnp.sin(ang).T


def _nsa_layer(xp, xs, pools, win_bufs, page_table, layer, n_layers, prev_rows, norm_g, w_in, b_gate, g_q, g_k,
               g_kcmp, cmp_pe, cmp_w1, cmp_b1, cmp_w2, w_out):
    b, t, d = xp.shape
    bs = xs.shape[0]
    past = page_table.shape[1] * PAGE
    nq, nkv = GRP * KVH * HD, KVH * HD
    n_gate = 3 * GRP * KVH

    wt_g = jnp.pad(w_in[:, nq + 6 * nkv:].T.reshape(KVH, n_gate // KVH, d), ((0, 0), (0, 16 - n_gate // KVH), (0, 0)))
    wt = jnp.concatenate([w_in[:, :nq + 6 * nkv].T, wt_g.reshape(KVH * 16, d)], axis=0).astype(BF16)
    bg = jnp.pad(b_gate.reshape(KVH, n_gate // KVH), ((0, 0), (0, 16 - n_gate // KVH))).reshape(KVH * 16, 1)
    gn = norm_g.reshape(1, d)
    wo = w_out.astype(BF16)

    def project(x, pos, tm, **stacking):
        cos, sin = _rope_tables(pos)
        col = lambda v: jnp.broadcast_to(v[..., None], v.shape + (tm,))
        return _nsa_proj(x, gn, wt, col(g_q), col(g_k), col(bg[:, 0]), cos, sin, tm, **stacking)

    tm = min(512, t)
    outs = project(xp, jnp.arange(t), tm, layer=layer, n_layers=n_layers, prev=prev_rows)
    qt, rows_p, (gt, ksa, vsa, kwb, vwa) = outs[0], list(outs[1:7]), outs[7:]
    pages = t // PAGE
    nh = pages * (PAGE // CMP_STRIDE)
    cmp_w = [_cmp_weights(cmp_pe[i], cmp_w1[i], cmp_b1[i], cmp_w2[i], g_kcmp, nh) for i in range(2)]
    pps = min(32, pages)
    kct = _compress(rows_p[0], cmp_w[0], pages, pps, True, layer=layer)
    vct = _compress(rows_p[1], cmp_w[1], pages, pps, False, layer=layer)
    ot = _attn_prompt(qt, kct, vct, ksa, vsa, kwb, vwa, gt)
    yp = _nsa_out(ot, xp, wo, min(1024, t))

    xs_pad = jnp.pad(xs.reshape(1, bs, d), ((0, 0), (0, LANES - bs), (0, 0)))
    outs = project(xs_pad, jnp.full((LANES,), past), LANES)
    qt_s, rows_s, gt_s = outs[0], outs[1:7], outs[7]
    pages_s = page_table.shape[1]
    nh_s = pages_s * (PAGE // CMP_STRIDE)
    cmp_ws = [_cmp_weights(cmp_pe[i], cmp_w1[i], cmp_b1[i], cmp_w2[i], g_kcmp, nh_s) for i in range(2)]
    pool5 = [jnp.transpose(pl_, (0, 1, 3, 4, 2)) for pl_ in pools]
    pps = min(64, pages_s)
    kct_s = _compress(pool5[0], cmp_ws[0], pages_s, pps, True, page_table, layer)
    vct_s = _compress(pool5[1], cmp_ws[1], pages_s, pps, False, page_table, layer)
    qs = jnp.transpose(qt_s[0, :, :, :bs].reshape(KVH, GRP, HD, bs), (3, 0, 2, 1))
    qs = jnp.einsum("bgdh,gk->bghkd", qs, jnp.eye(KVH, dtype=qs.dtype))
    qs = qs.reshape(bs, KVH * GRP, KVH * HD)
    new = lambda a: jnp.pad(jnp.transpose(a[0, 0, :, :, :bs], (2, 0, 1))[..., None],
                            ((0, 0), (0, 0), (0, 0), (0, LANES - 1)))
    gs = jnp.transpose(gt_s[0, :, :n_gate // KVH, :bs].reshape(KVH, GRP, 3, bs), (3, 0, 1, 2))
    gs = jnp.pad(gs.reshape(bs, KVH * GRP, 3), ((0, 0), (0, 0), (0, LANES - 3)))
    win5 = [jnp.transpose(wb, (0, 1, 3, 4, 2)) for wb in win_bufs]
    ot_s = _attn_sample(qs, kct_s, vct_s, pool5[2], pool5[3], page_table, layer, win5[0], win5[1],
                        new(rows_s[2]), new(rows_s[3]), new(rows_s[4]), new(rows_s[5]), gs)
    ot_s = jnp.stack([ot_s[:, g * GRP:(g + 1) * GRP, g * HD:(g + 1) * HD] for g in range(KVH)])
    ot_s = jnp.transpose(ot_s, (0, 2, 3, 1)).reshape(1, GRP * KVH, HD, bs)
    ot_s = jnp.pad(ot_s, ((0, 0), (0, 0), (0, 0), (0, LANES - bs))).astype(BF16)
    ys = _nsa_out(ot_s, xs_pad, wo, LANES)[0, :bs].reshape(bs, 1, d)

    rows_s = [jnp.transpose(a[0, 0, :, :, :bs], (2, 0, 1)).reshape(bs, 1, KVH, HD) for a in rows_s]
    return yp, ys, rows_p, rows_s


def kernel(x_prompt, x_sample, state_gla, cache_k_cmp, cache_v_cmp, cache_k_slc, cache_v_slc, cache_k_win, cache_v_win, page_table, norm_mix, norm_mlp, mlp_up, mlp_down, gla_w_in, gla_w_gate2, gla_b_gate, gla_g_out, gla_w_out, nsa_w_in, nsa_b_gate, nsa_g_q, nsa_g_k, nsa_g_kcmp, nsa_cmp_pe, nsa_cmp_w1, nsa_cmp_b1, nsa_cmp_w2, nsa_w_out):
    depth = norm_mix.shape[0]
    b, t, d = x_prompt.shape
    bs = x_sample.shape[0]
    xp, xs = x_prompt, x_sample
    gla_p, nsa_s = [], []
    gla_s = None
    nsa_p = None
    pools = (cache_k_cmp, cache_v_cmp, cache_k_slc, cache_v_slc)
    wu, wd = mlp_up.astype(BF16), mlp_down.astype(BF16)
    for i in range(depth):
        j = i // 2
        if i % 2 == 0:
            xp, xs, sp, gla_s = _gla_layer(xp, xs, state_gla, j, gla_s, norm_mix[i], gla_w_in[j], gla_w_gate2[j],
                                           gla_b_gate[j], gla_g_out[j], gla_w_out[j])
            gla_p.append(sp)
        else:
            xp, xs, nsa_p, rs = _nsa_layer(xp, xs, pools, (cache_k_win, cache_v_win), page_table, j, depth // 2,
                                           nsa_p, norm_mix[i], nsa_w_in[j], nsa_b_gate[j], nsa_g_q[j], nsa_g_k[j],
                                           nsa_g_kcmp[j], nsa_cmp_pe[j], nsa_cmp_w1[j], nsa_cmp_b1[j],
                                           nsa_cmp_w2[j], nsa_w_out[j])
            nsa_s.append(rs)
        g = norm_mlp[i].reshape(1, d)
        xp = _mlp(xp.reshape(b * t, d), g, wu, wd, i, tm=min(1024, b * t)).reshape(b, t, d)
        xs = _mlp(xs.reshape(bs, d), g, wu, wd, i, tm=bs).reshape(bs, 1, d)
    stack = lambda lst, r: jnp.stack([e[r] for e in lst])
    n_win = min(WINDOW, t)
    rows = lambda r, n: jnp.transpose(nsa_p[r][..., t - n:], (0, 1, 4, 2, 3))
    return (xp, xs, jnp.stack(gla_p), gla_s,
            rows(0, t), stack(nsa_s, 0), rows(1, t), stack(nsa_s, 1),
            rows(2, t), stack(nsa_s, 2), rows(3, t), stack(nsa_s, 3),
            rows(4, n_win), stack(nsa_s, 4), rows(5, n_win), stack(nsa_s, 5))
```

```python
import functools

import jax
import jax.numpy as jnp
from jax import lax
from jax.experimental import pallas as pl
from jax.experimental.pallas import tpu as pltpu

F32 = jnp.float32
BF16 = jnp.bfloat16
EPS = 1e-6
NEG = -1e30
VMEM_LIMIT_BYTES = 48 * 1024 * 1024
LANES = 128
PAGE = 128
HD = 64
KVH = 4
GRP = 4
L_CMP, CMP_STRIDE, L_SLC, N_SEL, WINDOW = 32, 16, 64, 16, 512
GLA_H, GLA_DK, GLA_DV, GLA_CHUNK = 4, 128, 256, 64
ROPE_THETA, ROT_HALF = 500000.0, 8
LOG2E = 1.4426950408889634
V_PAD = 16
KT = 512


def _params(*sem):
    return pltpu.CompilerParams(dimension_semantics=sem, vmem_limit_bytes=VMEM_LIMIT_BYTES)


def _dot(a, b):
    return jnp.dot(a, b, preferred_element_type=F32)


def _dot_nt(a, b):
    return lax.dot_general(a, b, (((1,), (1,)), ((), ())), preferred_element_type=F32)


def _dot_tn(a, b):
    return lax.dot_general(a, b, (((0,), (0,)), ((), ())), preferred_element_type=F32)


def _rms_rows(x, g):
    ms = jnp.mean(x * x, axis=-1, keepdims=True)
    return x * lax.rsqrt(ms + EPS) * g


def _mlp_kernel(x_ref, g_ref, wu_ref, wd_ref, o_ref, h_scr, acc_scr):
    f = pl.program_id(1)

    @pl.when(f == 0)
    def _():
        h_scr[...] = _rms_rows(x_ref[...], g_ref[...]).astype(BF16)
        acc_scr[...] = jnp.zeros_like(acc_scr)

    u = jnp.maximum(_dot(h_scr[...], wu_ref[...]), 0.0)
    acc_scr[...] += _dot((u * u).astype(BF16), wd_ref[...])

    @pl.when(f == pl.num_programs(1) - 1)
    def _():
        o_ref[...] = x_ref[...] + acc_scr[...]


def _mlp(x, g, wu, wd, layer, tm, tf=2048):
    m, d = x.shape
    ff = wu.shape[2]
    return pl.pallas_call(
        _mlp_kernel,
        grid=(m // tm, ff // tf),
        in_specs=[pl.BlockSpec((tm, d), lambda i, f: (i, 0)),
                  pl.BlockSpec((1, d), lambda i, f: (0, 0)),
                  pl.BlockSpec((None, d, tf), lambda i, f: (layer, 0, f)),
                  pl.BlockSpec((None, tf, d), lambda i, f: (layer, f, 0))],
        out_specs=pl.BlockSpec((tm, d), lambda i, f: (i, 0)),
        out_shape=jax.ShapeDtypeStruct((m, d), F32),
        scratch_shapes=[pltpu.VMEM((tm, d), BF16), pltpu.VMEM((tm, d), F32)],
        compiler_params=_params("parallel", "arbitrary"),
        name="mlp_block",
    )(x, g, wu, wd)


def _gla_proj_kernel(x_ref, g_ref, w_ref, wg2_ref, bg_ref, q_ref, k_ref, v_ref, r_ref, gl_ref):
    h = _rms_rows(x_ref[...], g_ref[...]).astype(BF16)
    dk, dv = GLA_H * GLA_DK, GLA_H * GLA_DV
    q_ref[...] = _dot(h, w_ref[:, 0:dk]) * (GLA_DK ** -0.5)
    k_ref[...] = _dot(h, w_ref[:, dk:2 * dk])
    v_ref[...] = _dot(h, w_ref[:, 2 * dk:2 * dk + dv])
    r_ref[...] = _dot(h, w_ref[:, 2 * dk + dv:2 * dk + 2 * dv])
    gr = _dot(h, w_ref[:, 2 * dk + 2 * dv:])
    xg = _dot(gr.astype(BF16), wg2_ref[...]) + bg_ref[...]
    gl_ref[...] = jax.nn.log_sigmoid(xg) * (1.0 / 16.0)


def _gla_proj(x, g, w, wg2, bg, tm):
    m, d = x.shape
    dk, dv = GLA_H * GLA_DK, GLA_H * GLA_DV
    row = lambda n: pl.BlockSpec((tm, n), lambda i: (i, 0))
    full = lambda a: pl.BlockSpec(a.shape, lambda i: (0, 0))
    return pl.pallas_call(
        _gla_proj_kernel,
        grid=(m // tm,),
        in_specs=[row(d), full(g), full(w), full(wg2), full(bg)],
        out_specs=[row(dk), row(dk), row(dv), row(dv), row(dk)],
        out_shape=[jax.ShapeDtypeStruct((m, n), F32) for n in (dk, dk, dv, dv, dk)],
        compiler_params=_params("parallel"),
        name="gla_proj",
    )(x, g, w, wg2, bg)


def _gla_scan_kernel(q_ref, k_ref, g_ref, v_ref, o_ref, s_ref, st_scr, *, n_chunks, n_heads):
    c_len = GLA_CHUNK
    step = pl.program_id(2)

    @pl.when(step == 0)
    def _():
        st_scr[...] = jnp.zeros_like(st_scr)

    rowi = lax.broadcasted_iota(jnp.int32, (c_len, GLA_DK), 0)
    causal = (lax.broadcasted_iota(jnp.int32, (c_len, c_len), 0)
              >= lax.broadcasted_iota(jnp.int32, (c_len, c_len), 1))

    def body(c, carry):
        sl = pl.ds(pl.multiple_of(c * c_len, c_len), c_len)
        for h in range(n_heads):
            ks = slice(h * GLA_DK, (h + 1) * GLA_DK)
            vs = slice(h * GLA_DV, (h + 1) * GLA_DV)
            b = g_ref[sl, ks]
            sh = 1
            while sh < c_len:
                b = b + jnp.where(rowi >= sh, pltpu.roll(b, sh, 0), 0.0)
                sh *= 2
            b_last = b[c_len - 1:c_len, :]
            b_mid = b[c_len // 2 - 1:c_len // 2, :]
            q = q_ref[sl, ks]
            k = k_ref[sl, ks]
            v = v_ref[sl, vs].astype(BF16)
            qe = (q * jnp.exp(b)).astype(BF16)
            qa = (q * jnp.exp(b - b_mid)).astype(BF16)
            ka = (k * jnp.exp(b_mid - b)).astype(BF16)
            kd = (k * jnp.exp(b_last - b)).astype(BF16)
            a = jnp.where(causal, _dot_nt(qa, ka), 0.0)
            st = st_scr[h]
            o_ref[sl, vs] = _dot_nt(qe, st.astype(BF16)) + _dot(a.astype(BF16), v)
            st_scr[h] = st * jnp.exp(b_last) + _dot_tn(v, kd)
        return carry

    lax.fori_loop(0, n_chunks, body, 0)

    @pl.when(step == pl.num_programs(2) - 1)
    def _():
        for h in range(n_heads):
            s_ref[h] = st_scr[h].T


def _gla_scan(q, k, gl, v, n_heads=4, t_blk=1024):
    b, t, _ = q.shape
    t_blk = min(t_blk, t)
    kq = pl.BlockSpec((None, t_blk, n_heads * GLA_DK), lambda i, h, j: (i, j, h))
    vv = pl.BlockSpec((None, t_blk, n_heads * GLA_DV), lambda i, h, j: (i, j, h))
    return pl.pallas_call(
        functools.partial(_gla_scan_kernel, n_chunks=t_blk // GLA_CHUNK, n_heads=n_heads),
        grid=(b, GLA_H // n_heads, t // t_blk),
        in_specs=[kq, kq, kq, vv],
        out_specs=[vv, pl.BlockSpec((None, n_heads, GLA_DK, GLA_DV), lambda i, h, j: (i, h, 0, 0))],
        out_shape=[jax.ShapeDtypeStruct((b, t, GLA_H * GLA_DV), F32),
                   jax.ShapeDtypeStruct((b, GLA_H, GLA_DK, GLA_DV), F32)],
        scratch_shapes=[pltpu.VMEM((n_heads, GLA_DV, GLA_DK), F32)],
        compiler_params=_params("parallel", "parallel", "arbitrary"),
        name="gla_scan",
    )(q, k, gl, v)


def _gla_step_kernel(q_ref, k_ref, g_ref, v_ref, s0_ref, o_ref, s_ref):
    def col(x):
        return jnp.broadcast_to(x, (LANES, LANES)).T

    for h in range(GLA_H):
        ks = slice(h * GLA_DK, (h + 1) * GLA_DK)
        qc, kc, ec = col(q_ref[:, ks]), col(k_ref[:, ks]), col(jnp.exp(g_ref[:, ks]))
        for half in range(GLA_DV // LANES):
            vs = slice(h * GLA_DV + half * LANES, h * GLA_DV + (half + 1) * LANES)
            ss = slice(half * LANES, (half + 1) * LANES)
            sn = ec * s0_ref[h, :, ss] + kc * v_ref[:, vs]
            s_ref[h, :, ss] = sn
            o_ref[:, vs] = jnp.sum(qc * sn, axis=0, keepdims=True)


def _gla_step_aliased_kernel(q_ref, k_ref, g_ref, v_ref, s0_ref, prev_ref, o_ref, s_ref):
    del prev_ref
    _gla_step_kernel(q_ref, k_ref, g_ref, v_ref, s0_ref, o_ref, s_ref)


def _gla_step(q, k, gl, v, s0, layer, prev=None):
    b = q.shape[0]
    kq = pl.BlockSpec((None, 1, GLA_H * GLA_DK), lambda i: (i, 0, 0))
    vv = pl.BlockSpec((None, 1, GLA_H * GLA_DV), lambda i: (i, 0, 0))
    st = pl.BlockSpec((None, None, GLA_H, GLA_DK, GLA_DV), lambda i: (layer, i, 0, 0, 0))
    extra = [] if prev is None else [prev]
    return pl.pallas_call(
        _gla_step_kernel if prev is None else _gla_step_aliased_kernel,
        grid=(b,),
        in_specs=[kq, kq, kq, vv, st] + [pl.BlockSpec(memory_space=pl.ANY)] * len(extra),
        out_specs=[vv, st],
        out_shape=[jax.ShapeDtypeStruct((b, 1, GLA_H * GLA_DV), F32), jax.ShapeDtypeStruct(s0.shape, F32)],
        input_output_aliases={5: 1} if extra else {},
        compiler_params=_params("parallel"),
        name="gla_step",
    )(q, k, gl, v, s0, *extra)


def _gla_out_kernel(o_ref, r_ref, x_ref, go_ref, w_ref, y_ref):
    parts = []
    for h in range(GLA_H):
        sl = slice(h * GLA_DV, (h + 1) * GLA_DV)
        r = r_ref[:, sl]
        parts.append((_rms_rows(o_ref[:, sl], go_ref[...]) * (r * jax.nn.sigmoid(r))).astype(BF16))
    y_ref[...] = x_ref[...] + _dot(jnp.concatenate(parts, axis=1), w_ref[...])


def _gla_out(o, r, x, go, w, tm):
    m, d = x.shape
    row = pl.BlockSpec((tm, d), lambda i: (i, 0))
    full = lambda a: pl.BlockSpec(a.shape, lambda i: (0, 0))
    return pl.pallas_call(
        _gla_out_kernel,
        grid=(m // tm,),
        in_specs=[row, row, row, full(go), full(w)],
        out_specs=row,
        out_shape=jax.ShapeDtypeStruct((m, d), F32),
        compiler_params=_params("parallel"),
        name="gla_out",
    )(o, r, x, go, w)


def _gla_layer(xp, xs, s0, layer, prev_state, norm_g, w_in, w_gate2, b_gate, g_out, w_out):
    b, t, d = xp.shape
    bs = xs.shape[0]
    dk, dv = GLA_H * GLA_DK, GLA_H * GLA_DV
    rank = w_gate2.shape[0]
    w = jnp.pad(w_in, ((0, 0), (0, LANES - rank))).astype(BF16)
    wg2 = jnp.pad(w_gate2, ((0, LANES - rank), (0, 0))).astype(BF16)
    g = norm_g.reshape(1, d)
    bg = b_gate.reshape(1, dk)
    go = g_out.reshape(1, GLA_DV)
    wo = w_out.astype(BF16)
    x2 = xp.reshape(b * t, d)
    q, k, v, r, gl = _gla_proj(x2, g, w, wg2, bg, tm=512)
    o, sp = _gla_scan(q.reshape(b, t, dk), k.reshape(b, t, dk), gl.reshape(b, t, dk), v.reshape(b, t, dv))
    yp = _gla_out(o.reshape(b * t, dv), r, x2, go, wo, tm=1024).reshape(b, t, d)
    xs2 = xs.reshape(bs, d)
    q, k, v, r, gl = _gla_proj(xs2, g, w, wg2, bg, tm=bs)
    o, ss = _gla_step(q.reshape(bs, 1, dk), k.reshape(bs, 1, dk), gl.reshape(bs, 1, dk), v.reshape(bs, 1, dv), s0,
                      layer, prev_state)
    ys = _gla_out(o.reshape(bs, dv), r, xs2, go, wo, tm=bs).reshape(bs, 1, d)
    return yp, ys, sp, ss


def _nsa_proj_kernel(x_ref, gn_ref, wt_ref, gq_ref, gk_ref, bg_ref, cos_ref, sin_ref, *rest):
    (q_ref, kc_ref, vc_ref, ks_ref, vs_ref, kw_ref, vw_ref, gt_ref,
     ksa_ref, vsa_ref, kwb_ref, vwa_ref) = rest[-12:]
    tm = x_ref.shape[0]
    nq = GRP * KVH * HD
    nkv = KVH * HD
    h = _rms_rows(x_ref[...], gn_ref[...]).astype(BF16)
    cos = cos_ref[...][None]
    sin = sin_ref[...][None]

    def norm_rope(z, g, nh):
        z3 = z.reshape(nh, HD, tm)
        y = z3 * lax.rsqrt(jnp.mean(z3 * z3, axis=1, keepdims=True) + EPS) * g[None]
        x1 = y[:, 0:ROT_HALF, :]
        x2 = y[:, ROT_HALF:2 * ROT_HALF, :]
        return jnp.concatenate([x1 * cos - x2 * sin, x1 * sin + x2 * cos, y[:, 2 * ROT_HALF:, :]], axis=1)

    zq = _dot_nt(wt_ref[0:nq, :], h)
    q_ref[...] = (norm_rope(zq, gq_ref[...], GRP * KVH) * (HD ** -0.5 * LOG2E)).astype(BF16)
    zkv = _dot_nt(wt_ref[nq:nq + 6 * nkv, :], h)
    outs = (kc_ref, vc_ref, ks_ref, vs_ref, kw_ref, vw_ref)
    rows = []
    for i in range(6):
        z = zkv[i * nkv:(i + 1) * nkv, :]
        rows.append(norm_rope(z, gk_ref[i // 2], KVH) if i % 2 == 0 else z.reshape(KVH, HD, tm))
        outs[i][...] = rows[i]
    zg = _dot_nt(wt_ref[nq + 6 * nkv:, :], h) + bg_ref[...]
    gt_ref[...] = jax.nn.sigmoid(zg).reshape(KVH, 16, tm)

    nsp = ksa_ref.shape[1] - HD
    blk = lax.broadcasted_iota(jnp.int32, (KVH, nsp, tm), 1)
    tok = pl.program_id(1) * tm + lax.broadcasted_iota(jnp.int32, (KVH, nsp, tm), 2)
    onehot = jnp.where(tok // L_SLC == blk, 1.0, 0.0)
    ksa_ref[...] = jnp.concatenate([rows[2], onehot], axis=1).astype(BF16)
    kwb_ref[...] = rows[4].astype(BF16)
    ones = jnp.where(lax.broadcasted_iota(jnp.int32, (KVH, V_PAD, tm), 1) == 0, 1.0, 0.0)
    vsa_ref[...] = jnp.concatenate([rows[3], ones], axis=1).astype(BF16)
    vwa_ref[...] = jnp.concatenate([rows[5], ones], axis=1).astype(BF16)


def _sel_rows(t):
    return -(-(-(-t // L_SLC)) // 16) * 16


def _nsa_proj(x, gn, wt, gq, gk, bg, cos, sin, tm, layer=0, n_layers=1, prev=None):
    b, t, d = x.shape
    full = lambda a: pl.BlockSpec(a.shape, lambda i, j: (0,) * a.ndim)
    rows_spec = lambda n: pl.BlockSpec((None, KVH, n, tm), lambda i, j: (i, 0, 0, j))
    rows_shape = lambda n, dt: jax.ShapeDtypeStruct((b, KVH, n, t), dt)
    kv_spec = pl.BlockSpec((None, None, KVH, HD, tm), lambda i, j: (layer, i, 0, 0, j))
    kv_shape = jax.ShapeDtypeStruct((n_layers, b, KVH, HD, t), F32)
    aug = [HD + _sel_rows(t), HD + V_PAD, HD, HD + V_PAD]
    prev = list(prev) if prev is not None else []
    n_in = 8
    return pl.pallas_call(
        _nsa_proj_kernel,
        grid=(b, t // tm),
        in_specs=[pl.BlockSpec((None, tm, d), lambda i, j: (i, j, 0)), full(gn), full(wt), full(gq), full(gk),
                  full(bg), pl.BlockSpec((ROT_HALF, tm), lambda i, j: (0, j)),
                  pl.BlockSpec((ROT_HALF, tm), lambda i, j: (0, j))]
        + [pl.BlockSpec(memory_space=pl.ANY)] * len(prev),
        out_specs=[pl.BlockSpec((None, GRP * KVH, HD, tm), lambda i, j: (i, 0, 0, j))] + [kv_spec] * 6
        + [rows_spec(16)] + [rows_spec(n) for n in aug],
        out_shape=[jax.ShapeDtypeStruct((b, GRP * KVH, HD, t), BF16)] + [kv_shape] * 6
        + [rows_shape(16, F32)] + [rows_shape(n, BF16) for n in aug],
        input_output_aliases={n_in + k: 1 + k for k in range(len(prev))},
        compiler_params=_params("parallel", "parallel"),
        name="nsa_proj",
    )(x, gn, wt, gq, gk, bg, cos, sin, *prev)


def _nsa_out_kernel(ot_ref, x_ref, w_ref, y_ref):
    ot = ot_ref[...].reshape(GRP * KVH * HD, ot_ref.shape[-1])
    y_ref[...] = x_ref[...] + _dot_tn(ot, w_ref[...])


def _nsa_out(ot, x, w, tm):
    b, t, d = x.shape
    return pl.pallas_call(
        _nsa_out_kernel,
        grid=(b, t // tm),
        in_specs=[pl.BlockSpec((None, GRP * KVH, HD, tm), lambda i, j: (i, 0, 0, j)),
                  pl.BlockSpec((None, tm, d), lambda i, j: (i, j, 0)),
                  pl.BlockSpec(w.shape, lambda i, j: (0, 0))],
        out_specs=pl.BlockSpec((None, tm, d), lambda i, j: (i, j, 0)),
        out_shape=jax.ShapeDtypeStruct((b, t, d), F32),
        compiler_params=_params("parallel", "parallel"),
        name="nsa_out",
    )(ot, x, w)


def _compress_kernel(*refs, n_pages, paged, is_key):
    refs = refs[1:] if paged else refs
    pages = refs[:n_pages]
    perm_ref, w_ref, b1_ref, pe_ref, w1_ref, w2t_ref, gk_ref, o_ref, lhs_scr = refs[n_pages:]
    p = pl.program_id(1)
    half = CMP_STRIDE
    n_half = PAGE // half
    low = lax.broadcasted_iota(jnp.int32, (n_half, LANES), 1) < HD
    odd_slot = (lax.broadcasted_iota(jnp.int32, (2 * HD, PAGE), 1) // n_half) % 2 == 1
    for pair in range(KVH // 2):
        for j in range(n_pages):
            y = _dot(pages[j][2 * pair:2 * pair + 2].reshape(2 * HD, PAGE).astype(BF16), perm_ref[...])
            x = jnp.where(odd_slot, jnp.concatenate([y[HD:], y[:HD]], axis=0), y).T
            rows = pl.ds(pl.multiple_of((p * n_pages + j) * n_half, n_half), n_half)
            for t in range(half // 2):
                ev = x[2 * t * n_half:(2 * t + 1) * n_half]
                od = x[(2 * t + 1) * n_half:(2 * t + 2) * n_half]
                lanes = slice(t * LANES, (t + 1) * LANES)
                lhs_scr[2 * pair, rows, lanes] = jnp.where(low, ev, od)
                lhs_scr[2 * pair + 1, rows, lanes] = jnp.where(low, od, ev)

    @pl.when(p == pl.num_programs(1) - 1)
    def _():
        nh = lhs_scr.shape[1]
        hidden = b1_ref.shape[1]
        c = _dot(pe_ref[0].astype(BF16), w1_ref[0]) + _dot(pe_ref[1].astype(BF16), w1_ref[1])
        bias = c[0:1, :] + b1_ref[...]
        for kvh in range(KVH):
            fs = _dot(lhs_scr[kvh].astype(BF16), w_ref[kvh % 2])
            hid = jax.nn.gelu(fs[:, :hidden] + pltpu.roll(fs[:, hidden:], nh - 1, 0) + bias)
            yt = _dot_nt(w2t_ref[...], hid.astype(BF16))
            if is_key:
                yt = yt * lax.rsqrt(jnp.mean(yt * yt, axis=0, keepdims=True) + EPS) * gk_ref[...]
            o_ref[kvh] = yt.astype(BF16)


def _compress(src, weights, n_pages_total, pages_per_step, is_key, page_table=None, layer=0):
    paged = page_table is not None
    b = page_table.shape[0] if paged else src.shape[1]
    n_half = PAGE // CMP_STRIDE
    nh = n_pages_total * n_half
    steps = n_pages_total // pages_per_step
    tok = jnp.arange(PAGE)
    perm = (((tok % CMP_STRIDE) * n_half + tok // CMP_STRIDE)[:, None] == tok[None, :]).astype(BF16)
    if paged:
        page_spec = lambda j: pl.BlockSpec(
            (None, None, KVH, HD, PAGE), lambda i, p, pt: (layer, pt[i, p * pages_per_step + j], 0, 0, 0))
        full = lambda a: pl.BlockSpec(a.shape, lambda i, p, pt: (0,) * a.ndim)
        o_spec = pl.BlockSpec((None, KVH, HD, nh), lambda i, p, pt: (i, 0, 0, 0))
    else:
        page_spec = lambda j: pl.BlockSpec((None, None, KVH, HD, PAGE),
                                           lambda i, p: (layer, i, 0, 0, p * pages_per_step + j))
        full = lambda a: pl.BlockSpec(a.shape, lambda i, p: (0,) * a.ndim)
        o_spec = pl.BlockSpec((None, KVH, HD, nh), lambda i, p: (i, 0, 0, 0))
    consts = [perm] + list(weights)
    grid_spec = pltpu.PrefetchScalarGridSpec(
        num_scalar_prefetch=1 if paged else 0,
        grid=(b, steps),
        in_specs=[page_spec(j) for j in range(pages_per_step)] + [full(a) for a in consts],
        out_specs=o_spec,
        scratch_shapes=[pltpu.VMEM((KVH, nh, CMP_STRIDE * HD), F32)],
    )
    args = ([page_table] if paged else []) + [src] * pages_per_step + consts
    return pl.pallas_call(
        functools.partial(_compress_kernel, n_pages=pages_per_step, paged=paged, is_key=is_key),
        grid_spec=grid_spec,
        out_shape=jax.ShapeDtypeStruct((b, KVH, HD, nh), BF16),
        compiler_params=_params("parallel", "arbitrary"),
        name="nsa_compress",
    )(*args)


def _cmp_weights(pe, w1, b1, w2, g_kcmp, nh):
    hidden = w1.shape[-1]
    wfs = jnp.concatenate([w1[:CMP_STRIDE], w1[CMP_STRIDE:]], axis=-1)
    swapped = wfs.reshape(CMP_STRIDE // 2, 2, HD, 2 * hidden)[:, ::-1]
    w_pair = jnp.stack([wfs, swapped.reshape(wfs.shape)]).reshape(2, CMP_STRIDE * HD, 2 * hidden).astype(BF16)
    pe2 = jnp.zeros((2, 8, CMP_STRIDE * HD), F32).at[:, 0, :].set(pe.reshape(2, CMP_STRIDE * HD))
    w1f = w1.reshape(2, CMP_STRIDE * HD, hidden).astype(BF16)
    gk = jnp.broadcast_to(g_kcmp.reshape(HD, 1), (HD, nh))
    return w_pair, b1.reshape(1, hidden), pe2, w1f, w2.T.astype(BF16), gk


def _softmax_update(s, mask, vt, m, l, acc):
    m_new = jnp.maximum(m, jnp.max(jnp.where(mask, s, NEG), axis=0, keepdims=True))
    alpha = jnp.exp2(m - m_new)
    p = jnp.where(mask, jnp.exp2(s - m_new), 0.0)
    l_new = l * alpha + jnp.sum(p, axis=0, keepdims=True)
    acc_new = acc * alpha + _dot(vt, p.astype(BF16))
    return m_new, l_new, acc_new, p


def _softmax_init(lanes):
    return jnp.full((1, lanes), NEG, F32), jnp.zeros((1, lanes), F32), jnp.zeros((HD, lanes), F32)


def _finish(l, acc):
    return acc / jnp.maximum(l, 1e-30)


def _split_dot(mt, x):
    hi = x.astype(BF16)
    lo = (x - hi.astype(F32)).astype(BF16)
    return _dot(mt, hi) + _dot(mt, lo)


def _attn_prompt_kernel(q_ref, kc_ref, vc_ref, ksa_ref, vsa_ref, kwb_ref, vwa_ref, gt_ref, mt_ref, o_ref,
                        s_scr, s2_scr, *, tq, n_cmp, n_sel):
    i = pl.program_id(2)
    lanes = GRP * tq
    ncp = kc_ref.shape[1]
    nsp = mt_ref.shape[0]
    qt = jnp.concatenate([q_ref[h] for h in range(GRP)], axis=1)

    def col_max(mx, s):
        return jnp.maximum(mx, jnp.max(s.reshape(s.shape[0] // 8, 8, lanes), axis=0))

    def finish(acc):
        return acc[0:HD] / jnp.maximum(acc[HD:HD + 1], 1e-30)

    tpos1 = i * tq + lax.broadcasted_iota(jnp.int32, (1, tq), 1)
    tpos = jnp.concatenate([tpos1] * GRP, axis=1)

    s = _dot_tn(kc_ref[...], qt)
    n_idx = lax.broadcasted_iota(jnp.int32, (ncp, lanes), 0)
    mask = (n_idx * CMP_STRIDE + (L_CMP - 1) <= tpos) & (n_idx < n_cmp)
    m, l, acc, p = _softmax_update(s, mask, vc_ref[...], *_softmax_init(lanes))
    o_c = _finish(l, acc)
    p = p / jnp.maximum(l, 1e-30)
    imp = p[:, 0:tq]
    for h in range(1, GRP):
        imp = imp + p[:, h * tq:(h + 1) * tq]
    p_slc = _split_dot(mt_ref[...], imp)

    j_idx = lax.broadcasted_iota(jnp.int32, (nsp, tq), 0)
    cur = tpos1 // L_SLC
    valid = j_idx <= cur
    forced = (j_idx == 0) | (j_idx == cur) | (j_idx == cur - 1)
    score = jnp.where(valid & forced, jnp.inf, jnp.where(valid, p_slc, -jnp.inf))
    n_grp = nsp // 8
    keys = [score[8 * r:8 * r + 8] for r in range(n_grp)]
    sub = lax.broadcasted_iota(jnp.int32, (8, tq), 0)

    def count_group(grp, ranks):
        ranks = list(ranks)
        for u in range(8):
            row = jnp.broadcast_to(keys[grp][u:u + 1, :], (8, tq))
            for r in range(n_grp):
                gt = jnp.where(row > keys[r], 1.0, 0.0)
                ge = jnp.where(row >= keys[r], 1.0, 0.0)
                ranks[r] = ranks[r] + (gt if r < grp else ge if r > grp else jnp.where(sub > u, ge, gt))
        return tuple(ranks)

    ranks = tuple(jnp.zeros((8, tq), F32) for _ in range(n_grp))
    last_valid = (i * tq + tq - 1) // L_SLC
    for grp in range(n_grp):
        ranks = lax.cond(8 * grp <= last_valid, functools.partial(count_group, grp), lambda r: r, ranks)
    bias = jnp.where(jnp.concatenate(ranks, axis=0) < n_sel, 0.0, NEG)
    qa = jnp.concatenate([qt, jnp.concatenate([bias] * GRP, axis=1).astype(BF16)], axis=0)

    row_k = lax.broadcasted_iota(jnp.int32, (KT, lanes), 0)

    def slc_scores(st, buf):
        start = pl.multiple_of(st * KT, KT)
        buf[0:KT, :] = _dot_tn(ksa_ref[:, pl.ds(start, KT)], qa)

    def slc_softmax(st, buf, carry, diagonal=False):
        m, acc = carry
        s = buf[0:KT, :]
        if diagonal:
            s = jnp.where(st * KT + row_k <= tpos, s, NEG)
        m_new = jnp.maximum(m, jnp.max(col_max(jnp.full((8, lanes), NEG, F32), s), axis=0, keepdims=True))
        p = jnp.exp2(s - m_new).astype(BF16)
        pv = _dot(vsa_ref[:, pl.ds(pl.multiple_of(st * KT, KT), KT)], p)
        return m_new, acc * jnp.exp2(m - m_new) + pv

    def slc_pair(u, carry):
        slc_scores(2 * u + 1, s2_scr)
        carry = slc_softmax(2 * u, s_scr, carry)
        slc_scores(2 * u + 2, s_scr)
        return slc_softmax(2 * u + 1, s2_scr, carry)

    def slc_tail_odd(carry):
        slc_scores(n_full, s2_scr)
        carry = slc_softmax(n_full - 1, s_scr, carry)
        return slc_softmax(n_full, s2_scr, carry, diagonal=True)

    def slc_tail_even(carry):
        return slc_softmax(n_full, s_scr, carry, diagonal=True)

    n_full = (i * tq) // KT
    slc_scores(0, s_scr)
    init = (jnp.full((1, lanes), NEG, F32), jnp.zeros((HD + V_PAD, lanes), F32))
    carry = lax.fori_loop(0, n_full // 2, slc_pair, init)
    o_s = finish(lax.cond(n_full % 2 == 1, slc_tail_odd, slc_tail_even, carry)[1])

    n_wt = WINDOW // tq + 1
    w0 = jnp.maximum(i - (n_wt - 1), 0) * tq
    row_w = lax.broadcasted_iota(jnp.int32, (tq, lanes), 0)
    col_w = jnp.concatenate([lax.broadcasted_iota(jnp.int32, (tq, tq), 1)] * GRP, axis=1)

    def window(steady):
        mx = jnp.full((8, lanes), NEG, F32)
        for r in range(n_wt):
            start = pl.multiple_of(w0 + r * tq, tq)
            s = _dot_tn(kwb_ref[:, pl.ds(start, tq)], qt)
            if not steady:
                rel = tpos - (start + row_w)
                s = jnp.where((rel >= 0) & (rel <= WINDOW), s, NEG)
            elif r == 0:
                s = jnp.where(row_w >= col_w, s, NEG)
            elif r == n_wt - 1:
                s = jnp.where(row_w <= col_w, s, NEG)
            s_scr[r * tq:(r + 1) * tq, :] = s
            mx = col_max(mx, s)
        m = jnp.max(mx, axis=0, keepdims=True)
        acc = jnp.zeros((HD + V_PAD, lanes), F32)
        for r in range(n_wt):
            start = pl.multiple_of(w0 + r * tq, tq)
            p = jnp.exp2(s_scr[r * tq:(r + 1) * tq, :] - m).astype(BF16)
            acc = acc + _dot(vwa_ref[:, pl.ds(start, tq)], p)
        return finish(acc)

    o_w = lax.cond(i >= n_wt - 1, functools.partial(window, True), functools.partial(window, False))

    for h in range(GRP):
        ls = slice(h * tq, (h + 1) * tq)
        o = (gt_ref[3 * h:3 * h + 1, :] * o_c[:, ls] + gt_ref[3 * h + 1:3 * h + 2, :] * o_s[:, ls]
             + gt_ref[3 * h + 2:3 * h + 3, :] * o_w[:, ls])
        o_ref[h] = o.astype(BF16)


def _slc_matrix(nsp, ncp, n_cmp):
    ratio = L_SLC // CMP_STRIDE
    j = jnp.arange(nsp)[:, None]
    n = jnp.arange(ncp)[None, :]
    m = ((n >= ratio * j) & (n <= ratio * j + ratio - 1)).astype(F32)
    m = m + ((n >= ratio * j - 1) & (n <= ratio * j + ratio - 2)).astype(F32)
    return jnp.where(n < n_cmp, m, 0.0).astype(BF16)


def _attn_prompt(qt, kct, vct, ksa, vsa, kwb, vwa, gt, tq=4 * LANES):
    b, _, _, t = qt.shape
    assert t % KT == 0 and WINDOW % tq == 0 and t >= WINDOW + tq
    ncp = kct.shape[-1]
    n_cmp = t // CMP_STRIDE - 1
    ns = -(-t // L_SLC)
    mt = _slc_matrix(_sel_rows(t), ncp, n_cmp)
    res = lambda a: pl.BlockSpec((None, None) + a.shape[2:], lambda i, g, j: (i, g, 0, 0))
    return pl.pallas_call(
        functools.partial(_attn_prompt_kernel, tq=tq, n_cmp=n_cmp, n_sel=min(N_SEL, ns)),
        grid=(b, KVH, t // tq),
        in_specs=[pl.BlockSpec((None, GRP, HD, tq), lambda i, g, j: (i, g, 0, j)),
                  res(kct), res(vct), res(ksa), res(vsa), res(kwb), res(vwa),
                  pl.BlockSpec((None, None, 16, tq), lambda i, g, j: (i, g, 0, j)),
                  pl.BlockSpec(mt.shape, lambda i, g, j: (0, 0))],
        out_specs=pl.BlockSpec((None, GRP, HD, tq), lambda i, g, j: (i, g, 0, j)),
        out_shape=jax.ShapeDtypeStruct(qt.shape, BF16),
        scratch_shapes=[pltpu.VMEM((max(KT, WINDOW + tq), GRP * tq), F32), pltpu.VMEM((KT, GRP * tq), F32)],
        compiler_params=_params("parallel", "parallel", "arbitrary"),
        name="nsa_attn_prompt",
    )(qt, kct, vct, ksa, vsa, kwb, vwa, gt, mt)


def _split_dot_rows(x, mt):
    hi = x.astype(BF16)
    lo = (x - hi.astype(F32)).astype(BF16)
    return _dot(hi, mt) + _dot(lo, mt)


def _attn_sample_rows_kernel(*refs, n_pages, n_cmp, n_blocks, n_sel):
    refs = refs[1:]
    k_pages = refs[:n_pages]
    v_pages = refs[n_pages:2 * n_pages]
    (q_ref, kc_ref, vc_ref, kw_ref, vw_ref, kns_ref, vns_ref, knw_ref, vnw_ref, gt_ref, mtt_ref, exp_ref,
     o_ref, bias_scr, oc_scr, ow_scr, m_scr, l_scr, acc_scr) = refs[2 * n_pages:]
    p = pl.program_id(1)
    ncp = kc_ref.shape[-1]
    nsp = mtt_ref.shape[1]
    n_q = KVH * GRP
    rows_all = KVH * HD
    q = q_ref[...]

    def stacked(ref):
        return ref[...].reshape(rows_all, ref.shape[-1]).astype(BF16)

    def update(s, valid, vt, m, l, acc):
        if valid is not None:
            s = jnp.where(valid, s, NEG)
        m_new = jnp.maximum(m, jnp.max(s, axis=1, keepdims=True))
        alpha = jnp.exp2(m - m_new)
        pr = jnp.exp2(s - m_new)
        if valid is not None:
            pr = jnp.where(valid, pr, 0.0)
        l_new = l * alpha + jnp.sum(pr, axis=1, keepdims=True)
        return m_new, l_new, acc * alpha + _dot_nt(pr.astype(BF16), vt), pr

    def init():
        return jnp.full((n_q, 1), NEG, F32), jnp.zeros((n_q, 1), F32), jnp.zeros((n_q, rows_all), F32)

    def save(m, l, acc):
        m_scr[...] = jnp.broadcast_to(m, m_scr.shape)
        l_scr[...] = jnp.broadcast_to(l, l_scr.shape)
        acc_scr[...] = acc

    @pl.when(p == 0)
    def _():
        cur = n_blocks - 1
        row_q = lax.broadcasted_iota(jnp.int32, (n_q, ncp), 0)
        row_b = lax.broadcasted_iota(jnp.int32, (n_q, nsp), 0)
        j_row = lax.broadcasted_iota(jnp.int32, (1, nsp), 1)
        jp_idx = lax.broadcasted_iota(jnp.int32, (nsp, nsp), 0)
        j_idx = lax.broadcasted_iota(jnp.int32, (nsp, nsp), 1)
        s = _dot(q, stacked(kc_ref))
        valid = lax.broadcasted_iota(jnp.int32, (n_q, ncp), 1) < n_cmp
        m, l, acc, pr = update(s, valid, stacked(vc_ref), *init())
        oc_scr[...] = acc / jnp.maximum(l, 1e-30)
        pr = pr / jnp.maximum(l, 1e-30)
        bias_blocks = jnp.zeros((n_q, nsp), F32)
        for g in range(KVH):
            imp = jnp.sum(jnp.where(row_q // GRP == g, pr, 0.0), axis=0, keepdims=True)
            p_slc = _split_dot_rows(jnp.broadcast_to(imp, (8, ncp)), mtt_ref[...])[0:1]
            ok = j_row <= cur
            forced = (j_row == 0) | (j_row == cur) | (j_row == cur - 1)
            score = jnp.where(ok & forced, jnp.inf, jnp.where(ok, p_slc, -jnp.inf))
            row = jnp.broadcast_to(score, (nsp, nsp))
            col = row.T
            ahead = (col > row) | ((col == row) & (jp_idx < j_idx))
            rank = jnp.sum(jnp.where(ahead, 1.0, 0.0), axis=0, keepdims=True)
            bias = jnp.broadcast_to(jnp.where(rank < n_sel, 0.0, NEG), (n_q, nsp))
            bias_blocks = jnp.where(row_b // GRP == g, bias, bias_blocks)
        bias_scr[...] = _dot(bias_blocks.astype(BF16), exp_ref[...])
        n_win = kw_ref.shape[-1]
        kt = jnp.concatenate([stacked(kw_ref), stacked(knw_ref)], axis=1)
        vt = jnp.concatenate([stacked(vw_ref), stacked(vnw_ref)], axis=1)
        valid = lax.broadcasted_iota(jnp.int32, (n_q, n_win + LANES), 1) <= n_win
        _, l, acc, _ = update(_dot(q, kt), valid, vt, *init())
        ow_scr[...] = acc / jnp.maximum(l, 1e-30)
        save(*init())

    n_keys = n_pages * PAGE
    kt = jnp.concatenate([k_pages[j][...].reshape(rows_all, PAGE) for j in range(n_pages)], axis=1).astype(BF16)
    vt = jnp.concatenate([v_pages[j][...].reshape(rows_all, PAGE) for j in range(n_pages)], axis=1).astype(BF16)
    s = _dot(q, kt) + bias_scr[:, pl.ds(pl.multiple_of(p * n_keys, n_keys), n_keys)]
    save(*update(s, None, vt, m_scr[:, 0:1], l_scr[:, 0:1], acc_scr[...])[:3])

    @pl.when(p == pl.num_programs(1) - 1)
    def _():
        valid = lax.broadcasted_iota(jnp.int32, (n_q, LANES), 1) == 0
        _, l, acc, _ = update(_dot(q, stacked(kns_ref)), valid, stacked(vns_ref),
                              m_scr[:, 0:1], l_scr[:, 0:1], acc_scr[...])
        o_ref[...] = (gt_ref[:, 0:1] * oc_scr[...] + gt_ref[:, 1:2] * (acc / jnp.maximum(l, 1e-30))
                      + gt_ref[:, 2:3] * ow_scr[...])


def _attn_sample(qs, kct, vct, pool_k, pool_v, page_table, layer, kwt, vwt, kns, vns, knw, vnw, gs,
                 pages_per_step=64):
    b, n_pages_total = page_table.shape
    pages_per_step = min(pages_per_step, n_pages_total)
    past = n_pages_total * PAGE
    n_cmp = past // CMP_STRIDE - 1
    ncp = kct.shape[-1]
    n_blocks = past // L_SLC + 1
    nsp = -(-n_blocks // LANES) * LANES
    mtt = _slc_matrix(nsp, ncp, n_cmp).T
    expand = (jnp.arange(nsp)[:, None] == jnp.arange(past)[None, :] // L_SLC).astype(BF16)
    n_q, rows_all = KVH * GRP, KVH * HD
    steps = n_pages_total // pages_per_step
    page_spec = lambda j: pl.BlockSpec((None, None, KVH, HD, PAGE),
                                       lambda i, p, pt: (layer, pt[i, p * pages_per_step + j], 0, 0, 0))
    def per_b(a):
        if a.ndim == 5:
            return pl.BlockSpec((None, None) + a.shape[2:], lambda i, p, pt: (layer, i, 0, 0, 0))
        return pl.BlockSpec((None,) + a.shape[1:], lambda i, p, pt: (i,) + (0,) * (a.ndim - 1))

    small = [qs, kct, vct, kwt, vwt, kns, vns, knw, vnw, gs]
    grid_spec = pltpu.PrefetchScalarGridSpec(
        num_scalar_prefetch=1,
        grid=(b, steps),
        in_specs=[page_spec(j) for j in range(pages_per_step)] * 2 + [per_b(a) for a in small]
        + [pl.BlockSpec(a.shape, lambda i, p, pt: (0, 0)) for a in (mtt, expand)],
        out_specs=pl.BlockSpec((None, n_q, rows_all), lambda i, p, pt: (i, 0, 0)),
        scratch_shapes=[pltpu.VMEM((n_q, past), F32), pltpu.VMEM((n_q, rows_all), F32),
                        pltpu.VMEM((n_q, rows_all), F32), pltpu.VMEM((n_q, LANES), F32),
                        pltpu.VMEM((n_q, LANES), F32), pltpu.VMEM((n_q, rows_all), F32)],
    )
    return pl.pallas_call(
        functools.partial(_attn_sample_rows_kernel, n_pages=pages_per_step, n_cmp=n_cmp, n_blocks=n_blocks,
                          n_sel=min(N_SEL, n_blocks)),
        grid_spec=grid_spec,
        out_shape=jax.ShapeDtypeStruct((b, n_q, rows_all), F32),
        compiler_params=_params("parallel", "arbitrary"),
        name="nsa_attn_sample",
    )(page_table, *([pool_k] * pages_per_step), *([pool_v] * pages_per_step), *small, mtt, expand)


def _rope_tables(pos):
    inv_freq = ROPE_THETA ** (-jnp.arange(ROT_HALF, dtype=F32) * 2.0 / (2 * ROT_HALF))
    ang = pos.astype(F32)[:, None] * inv_freq[None, :]
    return jnp.cos(ang).T, jnp.sin(ang).T


def _nsa_layer(xp, xs, pools, win_bufs, page_table, layer, n_layers, prev_rows, norm_g, w_in, b_gate, g_q, g_k,
               g_kcmp, cmp_pe, cmp_w1, cmp_b1, cmp_w2, w_out):
    b, t, d = xp.shape
    bs = xs.shape[0]
    past = page_table.shape[1] * PAGE
    nq, nkv = GRP * KVH * HD, KVH * HD
    n_gate = 3 * GRP * KVH

    wt_g = jnp.pad(w_in[:, nq + 6 * nkv:].T.reshape(KVH, n_gate // KVH, d), ((0, 0), (0, 16 - n_gate // KVH), (0, 0)))
    wt = jnp.concatenate([w_in[:, :nq + 6 * nkv].T, wt_g.reshape(KVH * 16, d)], axis=0).astype(BF16)
    bg = jnp.pad(b_gate.reshape(KVH, n_gate // KVH), ((0, 0), (0, 16 - n_gate // KVH))).reshape(KVH * 16, 1)
    gn = norm_g.reshape(1, d)
    wo = w_out.astype(BF16)

    def project(x, pos, tm, **stacking):
        cos, sin = _rope_tables(pos)
        col = lambda v: jnp.broadcast_to(v[..., None], v.shape + (tm,))
        return _nsa_proj(x, gn, wt, col(g_q), col(g_k), col(bg[:, 0]), cos, sin, tm, **stacking)

    tm = min(512, t)
    outs = project(xp, jnp.arange(t), tm, layer=layer, n_layers=n_layers, prev=prev_rows)
    qt, rows_p, (gt, ksa, vsa, kwb, vwa) = outs[0], list(outs[1:7]), outs[7:]
    pages = t // PAGE
    nh = pages * (PAGE // CMP_STRIDE)
    cmp_w = [_cmp_weights(cmp_pe[i], cmp_w1[i], cmp_b1[i], cmp_w2[i], g_kcmp, nh) for i in range(2)]
    pps = min(32, pages)
    kct = _compress(rows_p[0], cmp_w[0], pages, pps, True, layer=layer)
    vct = _compress(rows_p[1], cmp_w[1], pages, pps, False, layer=layer)
    ot = _attn_prompt(qt, kct, vct, ksa, vsa, kwb, vwa, gt)
    yp = _nsa_out(ot, xp, wo, min(1024, t))

    xs_pad = jnp.pad(xs.reshape(1, bs, d), ((0, 0), (0, LANES - bs), (0, 0)))
    outs = project(xs_pad, jnp.full((LANES,), past), LANES)
    qt_s, rows_s, gt_s = outs[0], outs[1:7], outs[7]
    pages_s = page_table.shape[1]
    nh_s = pages_s * (PAGE // CMP_STRIDE)
    cmp_ws = [_cmp_weights(cmp_pe[i], cmp_w1[i], cmp_b1[i], cmp_w2[i], g_kcmp, nh_s) for i in range(2)]
    pool5 = [jnp.transpose(pl_, (0, 1, 3, 4, 2)) for pl_ in pools]
    pps = min(64, pages_s)
    kct_s = _compress(pool5[0], cmp_ws[0], pages_s, pps, True, page_table, layer)
    vct_s = _compress(pool5[1], cmp_ws[1], pages_s, pps, False, page_table, layer)
    qs = jnp.transpose(qt_s[0, :, :, :bs].reshape(KVH, GRP, HD, bs), (3, 0, 2, 1))
    qs = jnp.einsum("bgdh,gk->bghkd", qs, jnp.eye(KVH, dtype=qs.dtype))
    qs = qs.reshape(bs, KVH * GRP, KVH * HD)
    new = lambda a: jnp.pad(jnp.transpose(a[0, 0, :, :, :bs], (2, 0, 1))[..., None],
                            ((0, 0), (0, 0), (0, 0), (0, LANES - 1)))
    gs = jnp.transpose(gt_s[0, :, :n_gate // KVH, :bs].reshape(KVH, GRP, 3, bs), (3, 0, 1, 2))
    gs = jnp.pad(gs.reshape(bs, KVH * GRP, 3), ((0, 0), (0, 0), (0, LANES - 3)))
    win5 = [jnp.transpose(wb, (0, 1, 3, 4, 2)) for wb in win_bufs]
    ot_s = _attn_sample(qs, kct_s, vct_s, pool5[2], pool5[3], page_table, layer, win5[0], win5[1],
                        new(rows_s[2]), new(rows_s[3]), new(rows_s[4]), new(rows_s[5]), gs)
    ot_s = jnp.stack([ot_s[:, g * GRP:(g + 1) * GRP, g * HD:(g + 1) * HD] for g in range(KVH)])
    ot_s = jnp.transpose(ot_s, (0, 2, 3, 1)).reshape(1, GRP * KVH, HD, bs)
    ot_s = jnp.pad(ot_s, ((0, 0), (0, 0), (0, 0), (0, LANES - bs))).astype(BF16)
    ys = _nsa_out(ot_s, xs_pad, wo, LANES)[0, :bs].reshape(bs, 1, d)

    rows_s = [jnp.transpose(a[0, 0, :, :, :bs], (2, 0, 1)).reshape(bs, 1, KVH, HD) for a in rows_s]
    return yp, ys, rows_p, rows_s


def kernel(x_prompt, x_sample, state_gla, cache_k_cmp, cache_v_cmp, cache_k_slc, cache_v_slc, cache_k_win, cache_v_win, page_table, norm_mix, norm_mlp, mlp_up, mlp_down, gla_w_in, gla_w_gate2, gla_b_gate, gla_g_out, gla_w_out, nsa_w_in, nsa_b_gate, nsa_g_q, nsa_g_k, nsa_g_kcmp, nsa_cmp_pe, nsa_cmp_w1, nsa_cmp_b1, nsa_cmp_w2, nsa_w_out):
    depth = norm_mix.shape[0]
    b, t, d = x_prompt.shape
    bs = x_sample.shape[0]
    xp, xs = x_prompt, x_sample
    gla_p, nsa_s = [], []
    gla_s = None
    nsa_p = None
    pools = (cache_k_cmp, cache_v_cmp, cache_k_slc, cache_v_slc)
    wu, wd = mlp_up.astype(BF16), mlp_down.astype(BF16)
    for i in range(depth):
        j = i // 2
        if i % 2 == 0:
            xp, xs, sp, gla_s = _gla_layer(xp, xs, state_gla, j, gla_s, norm_mix[i], gla_w_in[j], gla_w_gate2[j],
                                           gla_b_gate[j], gla_g_out[j], gla_w_out[j])
            gla_p.append(sp)
        else:
            xp, xs, nsa_p, rs = _nsa_layer(xp, xs, pools, (cache_k_win, cache_v_win), page_table, j, depth // 2,
                                           nsa_p, norm_mix[i], nsa_w_in[j], nsa_b_gate[j], nsa_g_q[j], nsa_g_k[j],
                                           nsa_g_kcmp[j], nsa_cmp_pe[j], nsa_cmp_w1[j], nsa_cmp_b1[j],
                                           nsa_cmp_w2[j], nsa_w_out[j])
            nsa_s.append(rs)
        g = norm_mlp[i].reshape(1, d)
        xp = _mlp(xp.reshape(b * t, d), g, wu, wd, i, tm=min(1024, b * t)).reshape(b, t, d)
        xs = _mlp(xs.reshape(bs, d), g, wu, wd, i, tm=bs).reshape(bs, 1, d)
    stack = lambda lst, r: jnp.stack([e[r] for e in lst])
    n_win = min(WINDOW, t)
    rows = lambda r, n: jnp.transpose(nsa_p[r][..., t - n:], (0, 1, 4, 2, 3))
    return (xp, xs, jnp.stack(gla_p), gla_s,
            rows(0, t), stack(nsa_s, 0), rows(1, t), stack(nsa_s, 1),
            rows(2, t), stack(nsa_s, 2), rows(3, t), stack(nsa_s, 3),
            rows(4, n_win), stack(nsa_s, 4), rows(5, n_win), stack(nsa_s, 5))
```
